```python
import math
import jax
import jax.numpy as jnp
from jax import lax
import numpy as np

D_MODEL = 2048
BATCH = 16
SEQ = 256
DEPTH = 2
DEC_BATCH = 2
DEC_SEQ = 2048
PAST_LEN = 512

GRID_W = 64
HEAD_DIM = 128
A_HEADS = 8
A_KV = 2
C_HEADS = 4
C_KV = 2
WINDOW = 128
Q_BLOCK = 128
S5_GROUP_CH = 16
S5_CH = D_MODEL // 4
S5_GROUPS = S5_CH // S5_GROUP_CH
S5_STATE = 64
MIX_WIDTH = A_HEADS * HEAD_DIM + C_HEADS * HEAD_DIM + S5_CH
IN_SIZES = (A_HEADS * HEAD_DIM, A_KV * HEAD_DIM, A_KV * HEAD_DIM,
            C_HEADS * HEAD_DIM, C_KV * HEAD_DIM, C_KV * HEAD_DIM, S5_CH)
IN_WIDTH = sum(IN_SIZES)
D_FF = 5632
ROPE_BASE = 10000.0
EPS = 1e-6
N_MOD = 9
HALF_STEP = 0.5
NEG_INF = -1e30

kernel_name = 'hybrid_diffusion_prefix_trunk_step'


def rms_norm(x, g):
    x32 = x.astype(jnp.float32)
    y = x32 * lax.rsqrt(jnp.mean(x32 * x32, axis=-1, keepdims=True) + EPS)
    return (y * g.astype(jnp.float32)).astype(x.dtype)


def swiglu(h, w_gate, w_up, w_down):
    return (jax.nn.silu(h @ w_gate) * (h @ w_up)) @ w_down


def axial_rope_angles(n_tokens):
    rows = n_tokens // GRID_W
    row = jnp.repeat(jnp.arange(rows, dtype=jnp.float32), GRID_W)
    col = jnp.tile(jnp.arange(GRID_W, dtype=jnp.float32), rows)
    axis_dim = HEAD_DIM // 2
    inv_freq = ROPE_BASE ** (-jnp.arange(0, axis_dim, 2, dtype=jnp.float32) / axis_dim)
    return row[:, None] * inv_freq, col[:, None] * inv_freq


def _rotate(x, ang):
    cos = jnp.cos(ang)[None, :, None, :].astype(x.dtype)
    sin = jnp.sin(ang)[None, :, None, :].astype(x.dtype)
    x1, x2 = jnp.split(x, 2, axis=-1)
    return jnp.concatenate([x1 * cos - x2 * sin, x2 * cos + x1 * sin], axis=-1)


def axial_rope(x, ang_row, ang_col):
    axis_dim = HEAD_DIM // 2
    return jnp.concatenate([_rotate(x[..., :axis_dim], ang_row),
                            _rotate(x[..., axis_dim:], ang_col)], axis=-1)


def block_attention(q, k, v, sink=None):
    b, lq, h, hd = q.shape
    kv = k.shape[2]
    g = h // kv
    nblk = lq // Q_BLOCK
    qb = q.reshape(b, nblk, Q_BLOCK, kv, g, hd).transpose(1, 0, 2, 3, 4, 5)
    scale = hd ** -0.5

    def one_block(qi):
        s = jnp.einsum('bqkgd,bskd->bkgqs', qi, k, preferred_element_type=jnp.float32) * scale
        if sink is None:
            p = jax.nn.softmax(s, axis=-1)
        else:
            sk = jnp.broadcast_to(sink.astype(jnp.float32).reshape(kv, g)[None, :, :, None, None],
                                  s.shape[:-1] + (1,))
            p = jax.nn.softmax(jnp.concatenate([s, sk], axis=-1), axis=-1)[..., :-1]
        return jnp.einsum('bkgqs,bskd->bqkgd', p.astype(v.dtype), v)

    out = lax.map(one_block, qb)
    return out.transpose(1, 0, 2, 3, 4, 5).reshape(b, lq, h * hd)


def window_attention(q, k, v, k_ctx, v_ctx, sink):
    b, l, h, hd = q.shape
    kv = k.shape[2]
    g = h // kv
    nblk = l // Q_BLOCK
    span = Q_BLOCK + 2 * WINDOW
    pad = ((0, 0), (WINDOW, WINDOW), (0, 0), (0, 0))
    kp = jnp.pad(k, pad)
    vp = jnp.pad(v, pad)
    idx = jnp.arange(nblk)[:, None] * Q_BLOCK + jnp.arange(span)[None, :]
    kb = kp[:, idx]
    vb = vp[:, idx]
    qb = q.reshape(b, nblk, Q_BLOCK, kv, g, hd)
    scale = hd ** -0.5
    s_win = jnp.einsum('bnqkgd,bnskd->bnkgqs', qb, kb, preferred_element_type=jnp.float32) * scale
    qpos = jnp.arange(nblk)[:, None] * Q_BLOCK + jnp.arange(Q_BLOCK)[None, :]
    kpos = idx - WINDOW
    valid = ((jnp.abs(qpos[:, :, None] - kpos[:, None, :]) <= WINDOW)
             & (kpos[:, None, :] >= 0) & (kpos[:, None, :] < l))
    s_win = jnp.where(valid[None, :, None, None], s_win, NEG_INF)
    s_ctx = jnp.einsum('bnqkgd,bskd->bnkgqs', qb, k_ctx, preferred_element_type=jnp.float32) * scale
    s_sink = jnp.broadcast_to(sink.astype(jnp.float32).reshape(kv, g)[None, None, :, :, None, None],
                              s_win.shape[:-1] + (1,))
    p = jax.nn.softmax(jnp.concatenate([s_win, s_ctx, s_sink], axis=-1), axis=-1)
    lc = k_ctx.shape[1]
    p_win = p[..., :span].astype(v.dtype)
    p_ctx = p[..., span:span + lc].astype(v.dtype)
    o = (jnp.einsum('bnkgqs,bnskd->bnqkgd', p_win, vb)
         + jnp.einsum('bnkgqs,bskd->bnqkgd', p_ctx, v_ctx))
    return o.reshape(b, l, h * hd)


def s5_discretize(lam_re, lam_im, log_step, b_re, b_im):
    lam_re = lam_re.astype(jnp.float32)
    lam_im = lam_im.astype(jnp.float32)
    dt = jnp.exp(log_step.astype(jnp.float32))[:, None]
    mag = jnp.exp(lam_re * dt)
    ab_re = mag * jnp.cos(lam_im * dt)
    ab_im = mag * jnp.sin(lam_im * dt)
    den = lam_re * lam_re + lam_im * lam_im
    n_re = ab_re - 1.0
    n_im = ab_im
    f_re = (n_re * lam_re + n_im * lam_im) / den
    f_im = (n_im * lam_re - n_re * lam_im) / den
    b_re = b_re.astype(jnp.float32)
    b_im = b_im.astype(jnp.float32)
    bb_re = f_re[..., None] * b_re - f_im[..., None] * b_im
    bb_im = f_re[..., None] * b_im + f_im[..., None] * b_re
    return ab_re, ab_im, bb_re, bb_im


def _complex_linear_combine(e1, e2):
    a1r, a1i, b1r, b1i = e1
    a2r, a2i, b2r, b2i = e2
    return (a2r * a1r - a2i * a1i,
            a2r * a1i + a2i * a1r,
            a2r * b1r - a2i * b1i + b2r,
            a2r * b1i + a2i * b1r + b2i)


def s5_scan(u, h0_re, h0_im, lam_re, lam_im, log_step, b_re, b_im, c_re, c_im):
    ab_re, ab_im, bb_re, bb_im = s5_discretize(lam_re, lam_im, log_step, b_re, b_im)
    x_re = jnp.einsum('blgc,gpc->blgp', u, bb_re)
    x_im = jnp.einsum('blgc,gpc->blgp', u, bb_im)
    h0_re = h0_re.astype(jnp.float32)
    h0_im = h0_im.astype(jnp.float32)
    x_re = x_re.at[:, 0].add(ab_re * h0_re - ab_im * h0_im)
    x_im = x_im.at[:, 0].add(ab_re * h0_im + ab_im * h0_re)
    a_re = jnp.broadcast_to(ab_re, x_re.shape)
    a_im = jnp.broadcast_to(ab_im, x_im.shape)
    _, _, h_re, h_im = lax.associative_scan(_complex_linear_combine, (a_re, a_im, x_re, x_im), axis=1)
    y = (jnp.einsum('blgp,gcp->blgc', h_re, c_re.astype(jnp.float32))
         - jnp.einsum('blgp,gcp->blgc', h_im, c_im.astype(jnp.float32)))
    return y, h_re[:, -1], h_im[:, -1]


def s5_mixer(u, h0_re, h0_im, lp):
    b, l, _ = u.shape
    ug = u.astype(jnp.float32).reshape(b, l, S5_GROUPS, S5_GROUP_CH)
    y_f, hf_re, hf_im = s5_scan(ug, h0_re[:, 0], h0_im[:, 0],
                                lp['s5_lam_re'][0], lp['s5_lam_im'][0], lp['s5_log_step'][0],
                                lp['s5_b_re'][0], lp['s5_b_im'][0], lp['s5_c_re'][0], lp['s5_c_im'][0])
    y_b, hb_re, hb_im = s5_scan(ug[:, ::-1], h0_re[:, 1], h0_im[:, 1],
                                lp['s5_lam_re'][1], lp['s5_lam_im'][1], lp['s5_log_step'][1],
                                lp['s5_b_re'][1], lp['s5_b_im'][1], lp['s5_c_re'][1], lp['s5_c_im'][1])
    d_skip = lp['s5_d'].astype(jnp.float32).reshape(S5_GROUPS, S5_GROUP_CH)
    y = (y_f + y_b[:, ::-1] + d_skip * ug).reshape(b, l, S5_CH)
    z = jax.nn.gelu(y)
    out = z * jax.nn.sigmoid(z @ lp['w_glu'].astype(jnp.float32) + lp['b_glu'].astype(jnp.float32))
    h_re = jnp.stack([hf_re, hb_re], axis=1)
    h_im = jnp.stack([hf_im, hb_im], axis=1)
    return out.astype(u.dtype), h_re, h_im


def split_projection(h, w_in):
    proj = h @ w_in
    points = []
    acc = 0
    for size in IN_SIZES[:-1]:
        acc += size
        points.append(acc)
    return jnp.split(proj, points, axis=-1)


def context_mixer(h, lp):
    b, l, _ = h.shape
    aq, ak, av, cq, ck, cv, u = split_projection(h, lp['w_in'])
    aq = rms_norm(aq.reshape(b, l, A_HEADS, HEAD_DIM), lp['q_norm'])
    ak = rms_norm(ak.reshape(b, l, A_KV, HEAD_DIM), lp['k_norm'])
    av = av.reshape(b, l, A_KV, HEAD_DIM)
    cq = cq.reshape(b, l, C_HEADS, HEAD_DIM)
    ck = ck.reshape(b, l, C_KV, HEAD_DIM)
    cv = cv.reshape(b, l, C_KV, HEAD_DIM)
    a_out = block_attention(aq, ak, av)
    c_out = block_attention(cq, ck, cv, lp['sink'])
    zeros = jnp.zeros((b, 2, S5_GROUPS, S5_STATE), jnp.float32)
    s_out, h_re, h_im = s5_mixer(u, zeros, zeros, lp)
    y = jnp.concatenate([a_out, c_out, s_out], axis=-1) @ lp['w_out']
    return y, (ak, av, ck, cv, h_re, h_im)


def latent_mixer(h, lp, ak_ctx, av_ctx, ck_ctx, cv_ctx, h0_re, h0_im):
    b, l, _ = h.shape
    ang_row, ang_col = axial_rope_angles(l)
    aq, ak, av, cq, ck, cv, u = split_projection(h, lp['w_in'])
    aq = axial_rope(rms_norm(aq.reshape(b, l, A_HEADS, HEAD_DIM), lp['q_norm']), ang_row, ang_col)
    ak = axial_rope(rms_norm(ak.reshape(b, l, A_KV, HEAD_DIM), lp['k_norm']), ang_row, ang_col)
    av = av.reshape(b, l, A_KV, HEAD_DIM)
    keys_a = jnp.concatenate([ak, ak_ctx.astype(ak.dtype)], axis=1)
    vals_a = jnp.concatenate([av, av_ctx.astype(av.dtype)], axis=1)
    a_out = block_attention(aq, keys_a, vals_a)
    cq = axial_rope(cq.reshape(b, l, C_HEADS, HEAD_DIM), ang_row, ang_col)
    ck = axial_rope(ck.reshape(b, l, C_KV, HEAD_DIM), ang_row, ang_col)
    cv = cv.reshape(b, l, C_KV, HEAD_DIM)
    c_out = window_attention(cq, ck, cv, ck_ctx.astype(ck.dtype), cv_ctx.astype(cv.dtype), lp['sink'])
    s_out, _, _ = s5_mixer(u, h0_re, h0_im, lp)
    y = jnp.concatenate([a_out, c_out, s_out], axis=-1) @ lp['w_out']
    return y, None


def modulate(hn, shift, scale):
    return hn * (1.0 + scale) + shift


def trunk_layer(x, lp, cvec, mixer):
    mods = jnp.split(jax.nn.silu(cvec) @ lp['w_mod'] + lp['b_mod'], N_MOD, axis=-1)
    h = modulate(rms_norm(x, lp['norm_pre'][0]), mods[0], mods[1])
    y = swiglu(h, lp['ffn_gate'][0], lp['ffn_up'][0], lp['ffn_down'][0])
    x = x + HALF_STEP * mods[2] * rms_norm(y, lp['norm_post'][0])
    h = modulate(rms_norm(x, lp['norm_pre'][1]), mods[3], mods[4])
    y, aux = mixer(h, lp)
    x = x + mods[5] * rms_norm(y, lp['norm_post'][1])
    h = modulate(rms_norm(x, lp['norm_pre'][2]), mods[6], mods[7])
    y = swiglu(h, lp['ffn_gate'][1], lp['ffn_up'][1], lp['ffn_down'][1])
    x = x + HALF_STEP * mods[8] * rms_norm(y, lp['norm_post'][2])
    return x, aux


def setup_inputs(seed: int = 0) -> dict:
    key = jax.random.key(seed)
    ks = jax.random.split(key, 32)
    f32 = jnp.float32

    def nrm(i, shape, std):
        return std * jax.random.normal(ks[i], shape, f32)

    d, f = D_MODEL, D_FF
    g, p, gc = S5_GROUPS, S5_STATE, S5_GROUP_CH
    lam_im_base = math.pi * jnp.arange(p, dtype=f32)
    return {
        'x_prompt': nrm(0, (BATCH, SEQ, d), 1.0),
        'x_sample': nrm(1, (DEC_BATCH, DEC_SEQ, d), 1.0),
        'cache_a_k': nrm(2, (DEC_BATCH, DEPTH, PAST_LEN, A_KV, HEAD_DIM), 1.0),
        'cache_a_v': nrm(3, (DEC_BATCH, DEPTH, PAST_LEN, A_KV, HEAD_DIM), 1.0),
        'cache_c_k': nrm(4, (DEC_BATCH, DEPTH, PAST_LEN, C_KV, HEAD_DIM), 1.0),
        'cache_c_v': nrm(5, (DEC_BATCH, DEPTH, PAST_LEN, C_KV, HEAD_DIM), 1.0),
        'state_ssm_re': nrm(6, (DEC_BATCH, DEPTH, 2, g, p), 0.1),
        'state_ssm_im': nrm(7, (DEC_BATCH, DEPTH, 2, g, p), 0.1),
        'c': nrm(8, (DEC_BATCH, d), 1.0),
        'c_ctx': nrm(9, (d,), 1.0),
        'w_mod': nrm(10, (DEPTH, d, N_MOD * d), 0.5 * d ** -0.5),
        'b_mod': nrm(11, (DEPTH, N_MOD * d), 0.01),
        'norm_pre': 1.0 + nrm(12, (DEPTH, 3, d), 0.01),
        'norm_post': 1.0 + nrm(13, (DEPTH, 3, d), 0.01),
        'ffn_gate': nrm(14, (DEPTH, 2, d, f), d ** -0.5),
        'ffn_up': nrm(15, (DEPTH, 2, d, f), d ** -0.5),
        'ffn_down': nrm(16, (DEPTH, 2, f, d), f ** -0.5),
        'w_in': nrm(17, (DEPTH, d, IN_WIDTH), d ** -0.5),
        'w_out': nrm(18, (DEPTH, MIX_WIDTH, d), MIX_WIDTH ** -0.5),
        'q_norm': 1.0 + nrm(19, (DEPTH, HEAD_DIM), 0.01),
        'k_norm': 1.0 + nrm(20, (DEPTH, HEAD_DIM), 0.01),
        'sink': nrm(21, (DEPTH, C_HEADS), 0.5),
        's5_lam_re': -0.5 + nrm(22, (DEPTH, 2, g, p), 0.01),
        's5_lam_im': lam_im_base + nrm(23, (DEPTH, 2, g, p), 0.01),
        's5_log_step': jax.random.uniform(ks[24], (DEPTH, 2, g), f32, math.log(1e-3), math.log(1e-1)),
        's5_b_re': nrm(25, (DEPTH, 2, g, p, gc), gc ** -0.5),
        's5_b_im': nrm(26, (DEPTH, 2, g, p, gc), gc ** -0.5),
        's5_c_re': nrm(27, (DEPTH, 2, g, gc, p), p ** -0.5),
        's5_c_im': nrm(28, (DEPTH, 2, g, gc, p), p ** -0.5),
        's5_d': nrm(29, (DEPTH, S5_CH), 0.5),
        'w_glu': nrm(30, (DEPTH, S5_CH, S5_CH), S5_CH ** -0.5),
        'b_glu': nrm(31, (DEPTH, S5_CH), 0.01),
    }


def reference(x_prompt, x_sample, cache_a_k, cache_a_v, cache_c_k, cache_c_v, state_ssm_re, state_ssm_im,
              c, c_ctx, w_mod, b_mod, norm_pre, norm_post, ffn_gate, ffn_up, ffn_down, w_in, w_out,
              q_norm, k_norm, sink, s5_lam_re, s5_lam_im, s5_log_step, s5_b_re, s5_b_im, s5_c_re, s5_c_im,
              s5_d, w_glu, b_glu):
    layers = []
    for l in range(DEPTH):
        layers.append({
            'w_mod': w_mod[l], 'b_mod': b_mod[l], 'norm_pre': norm_pre[l], 'norm_post': norm_post[l],
            'ffn_gate': ffn_gate[l], 'ffn_up': ffn_up[l], 'ffn_down': ffn_down[l],
            'w_in': w_in[l], 'w_out': w_out[l], 'q_norm': q_norm[l], 'k_norm': k_norm[l], 'sink': sink[l],
            's5_lam_re': s5_lam_re[l], 's5_lam_im': s5_lam_im[l], 's5_log_step': s5_log_step[l],
            's5_b_re': s5_b_re[l], 's5_b_im': s5_b_im[l], 's5_c_re': s5_c_re[l], 's5_c_im': s5_c_im[l],
            's5_d': s5_d[l], 'w_glu': w_glu[l], 'b_glu': b_glu[l],
        })

    xp = x_prompt
    cvec_ctx = c_ctx[None, None, :]
    ctx_tensors = []
    for l in range(DEPTH):
        xp, ctx = trunk_layer(xp, layers[l], cvec_ctx, context_mixer)
        ctx_tensors.append(ctx)
    y_prompt = xp
    new_cache_a_k = jnp.stack([t[0] for t in ctx_tensors], axis=1)
    new_cache_a_v = jnp.stack([t[1] for t in ctx_tensors], axis=1)
    new_cache_c_k = jnp.stack([t[2] for t in ctx_tensors], axis=1)
    new_cache_c_v = jnp.stack([t[3] for t in ctx_tensors], axis=1)
    new_state_ssm_re = jnp.stack([t[4] for t in ctx_tensors], axis=1)
    new_state_ssm_im = jnp.stack([t[5] for t in ctx_tensors], axis=1)

    xs = x_sample
    cvec = c[:, None, :]
    for l in range(DEPTH):
        def mixer(h, lp, l=l):
            return latent_mixer(h, lp, cache_a_k[:, l], cache_a_v[:, l], cache_c_k[:, l], cache_c_v[:, l],
                                state_ssm_re[:, l], state_ssm_im[:, l])
        xs, _ = trunk_layer(xs, layers[l], cvec, mixer)
    y_sample = xs

    return (y_prompt, y_sample, new_cache_a_k, new_cache_a_v, new_cache_c_k, new_cache_c_v,
            new_state_ssm_re, new_state_ssm_im)
```

```python
import functools
import math

import jax
import jax.numpy as jnp
from jax import lax
from jax.experimental import pallas as pl
from jax.experimental.pallas import tpu as pltpu

F32 = jnp.float32
BF16 = jnp.bfloat16

D_MODEL = 2048
CTX_B, CTX_L = 16, 256
LAT_B, LAT_L = 2, 2048
DEPTH = 2
PAST = 512
GRID_W = 64
HD = 128
A_HEADS, A_KV = 8, 2
C_HEADS, C_KV = 4, 2
WINDOW = 128
S5_GC = 16
S5_CH = 512
S5_G = 32
S5_P = 64
D_FF = 5632
N_MOD = 9
IN_WIDTH = 3072
ROPE_BASE = 10000.0
EPS = 1e-6
HALF_STEP = 0.5
NEG_INF = -1e30
SCALE = HD ** -0.5

T_CTX = CTX_B * CTX_L
T_LAT = LAT_B * LAT_L
T_ALL = T_CTX + T_LAT

COL_AQ, COL_AK, COL_AV = 0, 1024, 1280
COL_CQ, COL_CK, COL_CV = 1536, 2048, 2304
COL_U = 2560

S5_T = 16
S5_ROW = S5_T * S5_GC
CTX_NC = CTX_L // S5_T
LAT_NC = LAT_L // S5_T
S5_ROWS_CTX = CTX_NC * CTX_B
S5_ROWS_LAT = LAT_NC * LAT_B
S5_ROWS = S5_ROWS_CTX + S5_ROWS_LAT
S5_W1 = S5_ROW + 4 * 128
S5_W2 = 4 * 128

V7X_VMEM_BYTES = 64 * 1024 * 1024
VMEM_BIG = 56 * 1024 * 1024
VMEM_MID = 40 * 1024 * 1024

BM = 1024
BM_OUT = 512
BF = 256
RC = 512
BN_MOD = 1024
BN_IN = 512
BQ_A = 512
BQ_C = 256
S5_GS = 2


def _dot(a, b):
    return jnp.dot(a, b, preferred_element_type=F32)


def _dot_nt(a, b):
    return lax.dot_general(a, b, (((1,), (1,)), ((), ())), preferred_element_type=F32)


def _dot_hi(a, b):
    return jnp.dot(a, b, preferred_element_type=F32, precision=lax.Precision.HIGHEST)


def _rms(x, g):
    return x * lax.rsqrt(jnp.mean(x * x, axis=-1, keepdims=True) + EPS) * g


def _mod_index(i, bm):
    nct = T_CTX // bm
    return jnp.where(i < nct, 0, 1 + (i - nct) // (LAT_L // bm))


def _mod_kernel(c_ref, w_ref, b_ref, o_ref):
    c = c_ref[...]
    s = (c * jax.nn.sigmoid(c)).astype(BF16)
    o_ref[...] = _dot(s, w_ref[...].astype(BF16)) + b_ref[...]


def _modulation(cvec8, w_mod, b_mod):
    n = N_MOD * D_MODEL
    return pl.pallas_call(
        _mod_kernel,
        grid=(DEPTH, n // BN_MOD),
        in_specs=[
            pl.BlockSpec((8, D_MODEL), lambda l, j: (0, 0)),
            pl.BlockSpec((None, D_MODEL, BN_MOD), lambda l, j: (l, 0, j)),
            pl.BlockSpec((None, 1, BN_MOD), lambda l, j: (l, 0, j)),
        ],
        out_specs=pl.BlockSpec((None, 8, BN_MOD), lambda l, j: (l, 0, j)),
        out_shape=jax.ShapeDtypeStruct((DEPTH, 8, n), F32),
        compiler_params=pltpu.CompilerParams(
            dimension_semantics=("arbitrary", "arbitrary"), vmem_limit_bytes=VMEM_MID),
        name="modulation",
    )(cvec8, w_mod, b_mod.reshape(DEPTH, 1, n))


def _ffn_kernel(x_ref, mod_ref, npre_ref, npost_ref, wg_ref, wu_ref, wd_ref, o_ref, h_ref, *, mo):
    j = pl.program_id(1)
    nrc = BM // RC

    @pl.when(j == 0)
    def _():
        shift = mod_ref[:, mo * D_MODEL:(mo + 1) * D_MODEL]
        scale = mod_ref[:, (mo + 1) * D_MODEL:(mo + 2) * D_MODEL]
        g = npre_ref[...]
        for r in range(nrc):
            rows = slice(r * RC, (r + 1) * RC)
            hn = _rms(x_ref[rows, :], g)
            h_ref[rows, :] = (hn * (1.0 + scale) + shift).astype(BF16)
        o_ref[...] = jnp.zeros_like(o_ref)

    wg = wg_ref[...].astype(BF16)
    wu = wu_ref[...].astype(BF16)
    wd = wd_ref[...].astype(BF16)
    for r in range(nrc):
        rows = slice(r * RC, (r + 1) * RC)
        h = h_ref[rows, :]
        g = _dot(h, wg)
        u = _dot(h, wu)
        a = (g * jax.nn.sigmoid(g) * u).astype(BF16)
        o_ref[rows, :] += _dot(a, wd)

    @pl.when(j == pl.num_programs(1) - 1)
    def _():
        gate = mod_ref[:, (mo + 2) * D_MODEL:(mo + 3) * D_MODEL]
        gp = npost_ref[...]
        for r in range(nrc):
            rows = slice(r * RC, (r + 1) * RC)
            yn = _rms(o_ref[rows, :], gp)
            o_ref[rows, :] = x_ref[rows, :] + (HALF_STEP * gate) * yn


def _ffn(x, mods4, norm_pre4, norm_post4, ffn_gate, ffn_up, ffn_down, l, s):
    mo = 6 * s
    ni = 2 * s
    return pl.pallas_call(
        functools.partial(_ffn_kernel, mo=mo),
        grid=(T_ALL // BM, D_FF // BF),
        in_specs=[
            pl.BlockSpec((BM, D_MODEL), lambda i, j: (i, 0), pipeline_mode=pl.Buffered(1)),
            pl.BlockSpec((None, None, 1, N_MOD * D_MODEL), lambda i, j: (l, _mod_index(i, BM), 0, 0)),
            pl.BlockSpec((None, None, 1, D_MODEL), lambda i, j: (l, ni, 0, 0)),
            pl.BlockSpec((None, None, 1, D_MODEL), lambda i, j: (l, ni, 0, 0)),
            pl.BlockSpec((None, None, D_MODEL, BF), lambda i, j: (l, s, 0, j)),
            pl.BlockSpec((None, None, D_MODEL, BF), lambda i, j: (l, s, 0, j)),
            pl.BlockSpec((None, None, BF, D_MODEL), lambda i, j: (l, s, j, 0)),
        ],
        out_specs=pl.BlockSpec((BM, D_MODEL), lambda i, j: (i, 0)),
        out_shape=jax.ShapeDtypeStruct((T_ALL, D_MODEL), F32),
        scratch_shapes=[pltpu.VMEM((BM, D_MODEL), BF16)],
        compiler_params=pltpu.CompilerParams(
            dimension_semantics=("arbitrary", "arbitrary"), vmem_limit_bytes=VMEM_BIG),
        name=f"ffn_l{l}_s{s}",
    )(x, mods4, norm_pre4, norm_post4, ffn_gate, ffn_up, ffn_down)


def _rope(y, cos, sins):
    lane = lax.broadcasted_iota(jnp.int32, y.shape, 1)
    first = (lane & 63) < 32
    partner = jnp.where(first, pltpu.roll(y, 96, 1), pltpu.roll(y, 32, 1))
    return y * cos + partner * sins


_IN_TILES = (
    (("q", True, None),) * 4,
    (("q", True, None),) * 4,
    (("k", True, "kv"), ("k", True, "kv"), (None, False, "kv"), (None, False, "kv")),
    ((None, True, None),) * 4,
    ((None, True, "kv"), (None, True, "kv"), (None, False, "kv"), (None, False, "kv")),
    ((None, False, "u"),) * 4,
)


def _inproj_kernel(x_ref, mod_ref, npre_ref, qn_ref, kn_ref, cos_ref, sin_ref, w_ref,
                   qkvu_ref, kvf_ref, uf_ref, h_ref, p_ref):
    i = pl.program_id(0)
    j = pl.program_id(1)
    is_lat = i >= T_CTX // BM

    @pl.when(j == 0)
    def _():
        shift = mod_ref[:, 3 * D_MODEL:4 * D_MODEL]
        scale = mod_ref[:, 4 * D_MODEL:5 * D_MODEL]
        g = npre_ref[...]
        for r in range(BM // RC):
            rows = slice(r * RC, (r + 1) * RC)
            hn = _rms(x_ref[rows, :], g)
            h_ref[rows, :] = (hn * (1.0 + scale) + shift).astype(BF16)

    p_ref[...] = _dot(h_ref[...], w_ref[...].astype(BF16))

    def epilogue(tile, lat):
        for k, (norm, rot, copy) in enumerate(tile):
            cols = slice(k * HD, (k + 1) * HD)
            y = p_ref[:, cols]
            if norm == "q":
                y = _rms(y, qn_ref[...])
            elif norm == "k":
                y = _rms(y, kn_ref[...])
            if rot and lat:
                y = _rope(y, cos_ref[...], sin_ref[...])
            qkvu_ref[:, cols] = y.astype(BF16)
            if copy == "kv":
                kvf_ref[:, cols] = y
            elif copy == "u":
                uf_ref[:, cols] = y

    for jj, tile in enumerate(_IN_TILES):
        has_rot = any(t[1] for t in tile)
        if has_rot:
            pl.when(jnp.logical_and(j == jj, is_lat))(functools.partial(epilogue, tile, True))
            pl.when(jnp.logical_and(j == jj, jnp.logical_not(is_lat)))(functools.partial(epilogue, tile, False))
        else:
            pl.when(j == jj)(functools.partial(epilogue, tile, False))


def _inproj(x, mods4, norm_pre4, q_norm3, k_norm3, cos_t, sin_t, w_in, l):
    nct = T_CTX // BM
    tab_idx = lambda i, j: (jnp.maximum(i - nct, 0) % (LAT_L // BM), 0)
    return pl.pallas_call(
        _inproj_kernel,
        grid=(T_ALL // BM, IN_WIDTH // BN_IN),
        in_specs=[
            pl.BlockSpec((BM, D_MODEL), lambda i, j: (i, 0)),
            pl.BlockSpec((None, None, 1, N_MOD * D_MODEL), lambda i, j: (l, _mod_index(i, BM), 0, 0)),
            pl.BlockSpec((None, None, 1, D_MODEL), lambda i, j: (l, 1, 0, 0)),
            pl.BlockSpec((None, 1, HD), lambda i, j: (l, 0, 0)),
            pl.BlockSpec((None, 1, HD), lambda i, j: (l, 0, 0)),
            pl.BlockSpec((BM, HD), tab_idx),
            pl.BlockSpec((BM, HD), tab_idx),
            pl.BlockSpec((None, D_MODEL, BN_IN), lambda i, j: (l, 0, j)),
        ],
        out_specs=[
            pl.BlockSpec((BM, BN_IN), lambda i, j: (i, j)),
            pl.BlockSpec((BM, BN_IN), lambda i, j: (i, jnp.where(j >= 4, 1, 0))),
            pl.BlockSpec((BM, BN_IN), lambda i, j: (i, 0)),
        ],
        out_shape=[
            jax.ShapeDtypeStruct((T_ALL, IN_WIDTH), BF16),
            jax.ShapeDtypeStruct((T_ALL, 2 * BN_IN), F32),
            jax.ShapeDtypeStruct((T_ALL, S5_CH), F32),
        ],
        scratch_shapes=[pltpu.VMEM((BM, D_MODEL), BF16), pltpu.VMEM((BM, BN_IN), F32)],
        compiler_params=pltpu.CompilerParams(
            dimension_semantics=("arbitrary", "arbitrary"), vmem_limit_bytes=VMEM_BIG),
        name=f"inproj_l{l}",
    )(x, mods4, norm_pre4, q_norm3, k_norm3, cos_t, sin_t, w_in)


def _softmax_pv(scores, values, sink=None):
    m = functools.reduce(jnp.maximum, [jnp.max(s, axis=-1, keepdims=True) for s in scores])
    if sink is not None:
        m = jnp.maximum(m, sink)
    ps = [jnp.exp(s - m) for s in scores]
    den = functools.reduce(jnp.add, [jnp.sum(p, axis=-1, keepdims=True) for p in ps])
    if sink is not None:
        den = den + jnp.exp(sink - m)
    o = functools.reduce(jnp.add, [_dot(p.astype(BF16), v) for p, v in zip(ps, values)])
    return o / den


def _attn_ctx_kernel(sink_ref, aq_ref, ak_ref, av_ref, cq_ref, ck_ref, cv_ref, ao_ref, co_ref):
    kv = pl.program_id(1)
    k = ak_ref[...]
    v = av_ref[...]
    for h in range(A_HEADS // A_KV):
        cols = slice(h * HD, (h + 1) * HD)
        s = _dot_nt(aq_ref[:, cols], k) * SCALE
        ao_ref[:, cols] = _softmax_pv([s], [v]).astype(BF16)
    k = ck_ref[...]
    v = cv_ref[...]
    g = C_HEADS // C_KV
    for h in range(g):
        cols = slice(h * HD, (h + 1) * HD)
        s = _dot_nt(cq_ref[:, cols], k) * SCALE
        co_ref[:, cols] = _softmax_pv([s], [v], sink_ref[kv * g + h]).astype(BF16)


def _attn_ctx(qkvu, sink_l, l):
    ga = A_HEADS // A_KV * HD
    gc = C_HEADS // C_KV * HD
    blk = lambda width, col0: pl.BlockSpec((CTX_L, width), lambda b, k: (b, col0 // width + k))
    return pl.pallas_call(
        _attn_ctx_kernel,
        grid=(CTX_B, A_KV),
        in_specs=[
            pl.BlockSpec(memory_space=pltpu.SMEM),
            blk(ga, COL_AQ), blk(HD, COL_AK), blk(HD, COL_AV),
            blk(gc, COL_CQ), blk(HD, COL_CK), blk(HD, COL_CV),
        ],
        out_specs=[
            pl.BlockSpec((CTX_L, ga), lambda b, k: (b, k)),
            pl.BlockSpec((CTX_L, gc), lambda b, k: (b, k)),
        ],
        out_shape=[
            jax.ShapeDtypeStruct((T_ALL, A_HEADS * HD), BF16),
            jax.ShapeDtypeStruct((T_ALL, C_HEADS * HD), BF16),
        ],
        compiler_params=pltpu.CompilerParams(dimension_semantics=("arbitrary", "arbitrary")),
        name=f"attn_ctx_l{l}",
    )(sink_l, qkvu, qkvu, qkvu, qkvu, qkvu, qkvu)


def _attn_lat_a_kernel(q_ref, k_ref, v_ref, kc_ref, vc_ref, prev_ref, o_ref):
    del prev_ref
    k = k_ref[...]
    v = v_ref[...]
    kc = kc_ref[...].astype(BF16)
    vc = vc_ref[...].astype(BF16)
    for h in range(A_HEADS // A_KV):
        cols = slice(h * HD, (h + 1) * HD)
        q = q_ref[:, cols]
        s1 = _dot_nt(q, k) * SCALE
        s2 = _dot_nt(q, kc) * SCALE
        o_ref[:, cols] = _softmax_pv([s1, s2], [v, vc]).astype(BF16)


def _attn_lat_a(qkvu, cache_k4, cache_v4, a_out, l):
    ga = A_HEADS // A_KV * HD
    nq = LAT_L // BQ_A
    row0 = T_CTX // BQ_A
    lat_blk = T_CTX // LAT_L
    return pl.pallas_call(
        _attn_lat_a_kernel,
        grid=(LAT_B, A_KV, nq),
        in_specs=[
            pl.BlockSpec((BQ_A, ga), lambda b, k, q: (row0 + b * nq + q, COL_AQ // ga + k)),
            pl.BlockSpec((LAT_L, HD), lambda b, k, q: (lat_blk + b, COL_AK // HD + k)),
            pl.BlockSpec((LAT_L, HD), lambda b, k, q: (lat_blk + b, COL_AV // HD + k)),
            pl.BlockSpec((None, None, PAST, HD), lambda b, k, q: (b, l, 0, k)),
            pl.BlockSpec((None, None, PAST, HD), lambda b, k, q: (b, l, 0, k)),
            pl.BlockSpec(memory_space=pl.ANY),
        ],
        out_specs=pl.BlockSpec((BQ_A, ga), lambda b, k, q: (row0 + b * nq + q, k)),
        out_shape=jax.ShapeDtypeStruct((T_ALL, A_HEADS * HD), BF16),
        input_output_aliases={5: 0},
        compiler_params=pltpu.CompilerParams(
            dimension_semantics=("arbitrary", "arbitrary", "arbitrary"), vmem_limit_bytes=VMEM_BIG),
        name=f"attn_lat_a_l{l}",
    )(qkvu, qkvu, qkvu, cache_k4, cache_v4, a_out)


def _attn_lat_c_kernel(sink_ref, q_ref, k_ref, v_ref, kc_ref, vc_ref, prev_ref, o_ref):
    del prev_ref
    kv = pl.program_id(1)
    n = pl.program_id(2)
    span = BQ_C + 2 * WINDOW
    start = pl.multiple_of(jnp.clip(n * BQ_C - WINDOW, 0, LAT_L - span), WINDOW)
    kw = k_ref[pl.ds(start, span), :]
    vw = v_ref[pl.ds(start, span), :]
    kc = kc_ref[...].astype(BF16)
    vc = vc_ref[...].astype(BF16)
    qpos = n * BQ_C + lax.broadcasted_iota(jnp.int32, (BQ_C, span), 0)
    kpos = start + lax.broadcasted_iota(jnp.int32, (BQ_C, span), 1)
    valid = jnp.abs(qpos - kpos) <= WINDOW
    g = C_HEADS // C_KV
    for h in range(g):
        cols = slice(h * HD, (h + 1) * HD)
        q = q_ref[:, cols]
        s1 = jnp.where(valid, _dot_nt(q, kw) * SCALE, NEG_INF)
        s2 = _dot_nt(q, kc) * SCALE
        o_ref[:, cols] = _softmax_pv([s1, s2], [vw, vc], sink_ref[kv * g + h]).astype(BF16)


def _attn_lat_c(qkvu, sink_l, cache_k4, cache_v4, c_out, l):
    gc = C_HEADS // C_KV * HD
    nq = LAT_L // BQ_C
    row0 = T_CTX // BQ_C
    lat_blk = T_CTX // LAT_L
    return pl.pallas_call(
        _attn_lat_c_kernel,
        grid=(LAT_B, C_KV, nq),
        in_specs=[
            pl.BlockSpec(memory_space=pltpu.SMEM),
            pl.BlockSpec((BQ_C, gc), lambda b, k, q: (row0 + b * nq + q, COL_CQ // gc + k)),
            pl.BlockSpec((LAT_L, HD), lambda b, k, q: (lat_blk + b, COL_CK // HD + k)),
            pl.BlockSpec((LAT_L, HD), lambda b, k, q: (lat_blk + b, COL_CV // HD + k)),
            pl.BlockSpec((None, None, PAST, HD), lambda b, k, q: (b, l, 0, k)),
            pl.BlockSpec((None, None, PAST, HD), lambda b, k, q: (b, l, 0, k)),
            pl.BlockSpec(memory_space=pl.ANY),
        ],
        out_specs=pl.BlockSpec((BQ_C, gc), lambda b, k, q: (row0 + b * nq + q, k)),
        out_shape=jax.ShapeDtypeStruct((T_ALL, C_HEADS * HD), BF16),
        input_output_aliases={6: 0},
        compiler_params=pltpu.CompilerParams(
            dimension_semantics=("arbitrary", "arbitrary", "arbitrary")),
        name=f"attn_lat_c_l{l}",
    )(sink_l, qkvu, qkvu, qkvu, cache_k4, cache_v4, c_out)


def _s5_param_kernel(lrc_ref, lic_ref, lrr_ref, lir_ref, ls_ref, btr_ref, bti_ref, ctr_ref, cti_ref,
                     w1_ref, w2_ref, a16_ref):
    w1_ref[...] = jnp.zeros_like(w1_ref)
    w2_ref[...] = jnp.zeros_like(w2_ref)
    a16_ref[...] = jnp.zeros_like(a16_ref)

    lane = lax.broadcasted_iota(jnp.int32, (1, S5_ROW), 1)
    blk = (lane >> 4).astype(F32)
    lane16 = lax.broadcasted_iota(jnp.int32, (S5_GC, S5_ROW), 1)
    rowj = (lax.broadcasted_iota(jnp.int32, (S5_ROW, 1), 0) >> 4).astype(F32)

    def table(d, e):
        dt = jnp.exp(ls_ref[d])
        et = e * dt
        mag = jnp.exp(lrc_ref[d] * et)
        ang = lic_ref[d] * et
        pr = mag * jnp.cos(ang)
        pi = mag * jnp.sin(ang)
        cr = ctr_ref[d]
        ci = cti_ref[d]
        return pr * cr - pi * ci, pr * ci + pi * cr

    def pow_rows(d, e):
        dt = jnp.exp(ls_ref[d])
        et = e * dt
        mag = jnp.exp(lrr_ref[d] * et)
        ang = lir_ref[d] * et
        return mag * jnp.cos(ang), mag * jnp.sin(ang)

    def bbar_t(d):
        lr = lrr_ref[d]
        li = lir_ref[d]
        ar, ai = pow_rows(d, 1.0)
        den = lr * lr + li * li
        n_re = ar - 1.0
        f_re = (n_re * lr + ai * li) / den
        f_im = (ai * lr - n_re * li) / den
        br = btr_ref[d]
        bi = bti_ref[d]
        return f_re * br - f_im * bi, f_re * bi + f_im * br

    def kernel_taps(d, e):
        br, bi = bbar_t(d)
        tr, ti = table(d, e)
        return _dot_hi(br, tr) - _dot_hi(bi, ti)

    taps_f = kernel_taps(0, blk)
    taps_b = kernel_taps(1, 15.0 - blk)
    row_blocks = []
    for jb in range(S5_T):
        mf = taps_f if jb == 0 else jnp.where(lane16 >= S5_GC * jb, pltpu.roll(taps_f, S5_GC * jb, 1), 0.0)
        sh = (S5_GC * (jb + 1)) % S5_ROW
        mb = taps_b if sh == 0 else jnp.where(lane16 < S5_GC * (jb + 1), pltpu.roll(taps_b, sh, 1), 0.0)
        row_blocks.append(mf + mb)
    w1_ref[:, 0:S5_ROW] = jnp.concatenate(row_blocks, axis=0).astype(BF16)

    for d, e in ((0, 15.0 - rowj), (1, rowj)):
        br, bi = bbar_t(d)
        br = jnp.concatenate([br] * S5_T, axis=0)
        bi = jnp.concatenate([bi] * S5_T, axis=0)
        pr, pi = pow_rows(d, e)
        c0 = S5_ROW + 256 * d
        w1_ref[:, c0:c0 + S5_P] = (pr * br - pi * bi).astype(BF16)
        w1_ref[:, c0 + 128:c0 + 128 + S5_P] = (pr * bi + pi * br).astype(BF16)

    for d, e in ((0, blk + 1.0), (1, 16.0 - blk)):
        tr, ti = table(d, e)
        r0 = 256 * d
        w2_ref[r0:r0 + S5_P, :] = tr.astype(BF16)
        w2_ref[r0 + 128:r0 + 128 + S5_P, :] = (-ti).astype(BF16)
        ar, ai = pow_rows(d, float(S5_T))
        a16_ref[2 * d:2 * d + 1, 0:S5_P] = ar
        a16_ref[2 * d + 1:2 * d + 2, 0:S5_P] = ai


def _s5_params(s5_lam_re, s5_lam_im, s5_log_step, s5_b_re, s5_b_im, s5_c_re, s5_c_im):
    col = lambda a: a.reshape(DEPTH, 2, S5_G, S5_P, 1)
    row = lambda a: a.reshape(DEPTH, 2, S5_G, 1, S5_P)
    bt = lambda a: jnp.swapaxes(a, -1, -2)
    ct = lambda a: jnp.tile(jnp.swapaxes(a, -1, -2), (1, 1, 1, 1, S5_T))
    spec = lambda r, c: pl.BlockSpec((None, 2, None, r, c), lambda l, g: (l, 0, g, 0, 0))
    return pl.pallas_call(
        _s5_param_kernel,
        grid=(DEPTH, S5_G),
        in_specs=[
            spec(S5_P, 1), spec(S5_P, 1), spec(1, S5_P), spec(1, S5_P), spec(1, 1),
            spec(S5_GC, S5_P), spec(S5_GC, S5_P), spec(S5_P, S5_ROW), spec(S5_P, S5_ROW),
        ],
        out_specs=[
            pl.BlockSpec((None, None, S5_ROW, S5_W1), lambda l, g: (l, g, 0, 0)),
            pl.BlockSpec((None, None, S5_W2, S5_ROW), lambda l, g: (l, g, 0, 0)),
            pl.BlockSpec((None, None, 8, 128), lambda l, g: (l, g, 0, 0)),
        ],
        out_shape=[
            jax.ShapeDtypeStruct((DEPTH, S5_G, S5_ROW, S5_W1), BF16),
            jax.ShapeDtypeStruct((DEPTH, S5_G, S5_W2, S5_ROW), BF16),
            jax.ShapeDtypeStruct((DEPTH, S5_G, 8, 128), F32),
        ],
        compiler_params=pltpu.CompilerParams(dimension_semantics=("arbitrary", "arbitrary")),
        name="s5_params",
    )(col(s5_lam_re), col(s5_lam_im), row(s5_lam_re), row(s5_lam_im),
      s5_log_step.reshape(DEPTH, 2, S5_G, 1, 1), bt(s5_b_re), bt(s5_b_im), ct(s5_c_re), ct(s5_c_im))


def _s5_scan_kernel(uc_ref, w1_ref, w2_ref, a16_ref, h0_ref, yc_ref, hfin_ref, a_ref, hp_ref):
    for g in range(S5_GS):
        a_ref[g] = _dot(uc_ref[g], w1_ref[g])

    streams = ((0, CTX_B, CTX_NC), (S5_ROWS_CTX, LAT_B, LAT_NC))
    coef = [[a16_ref[g, k:k + 1, :] for k in range(4)] for g in range(S5_GS)]
    state = {}
    for g in range(S5_GS):
        for si, (off, nb, nc) in enumerate(streams):
            for d in range(2):
                if si == 0:
                    state[g, si, d] = (jnp.zeros((nb, 128), F32), jnp.zeros((nb, 128), F32))
                else:
                    state[g, si, d] = (h0_ref[g, 2 * d], h0_ref[g, 2 * d + 1])
    for step in range(LAT_NC):
        for g in range(S5_GS):
            for si, (off, nb, nc) in enumerate(streams):
                if step >= nc:
                    continue
                for d in range(2):
                    c = step if d == 0 else nc - 1 - step
                    rows = slice(off + c * nb, off + (c + 1) * nb)
                    hr, hi = state[g, si, d]
                    hp_ref[g, rows, 256 * d:256 * d + 128] = hr
                    hp_ref[g, rows, 256 * d + 128:256 * d + 256] = hi
                    c0 = S5_ROW + 256 * d
                    gr = a_ref[g, rows, c0:c0 + 128]
                    gi = a_ref[g, rows, c0 + 128:c0 + 256]
                    ar, ai = coef[g][2 * d], coef[g][2 * d + 1]
                    state[g, si, d] = (ar * hr - ai * hi + gr, ar * hi + ai * hr + gi)
    for g in range(S5_GS):
        for d in range(2):
            hr, hi = state[g, 0, d]
            hfin_ref[g, 2 * d] = hr
            hfin_ref[g, 2 * d + 1] = hi
        yc_ref[g] = a_ref[g, :, 0:S5_ROW] + _dot(hp_ref[g].astype(BF16), w2_ref[g])


def _s5_scan(uc, w1, w2, a16, h0, l):
    return pl.pallas_call(
        _s5_scan_kernel,
        grid=(S5_G // S5_GS,),
        in_specs=[
            pl.BlockSpec((S5_GS, S5_ROWS, S5_ROW), lambda g: (g, 0, 0)),
            pl.BlockSpec((None, S5_GS, S5_ROW, S5_W1), lambda g: (l, g, 0, 0)),
            pl.BlockSpec((None, S5_GS, S5_W2, S5_ROW), lambda g: (l, g, 0, 0)),
            pl.BlockSpec((None, S5_GS, 8, 128), lambda g: (l, g, 0, 0)),
            pl.BlockSpec((S5_GS, 4, LAT_B, 128), lambda g: (g, 0, 0, 0)),
        ],
        out_specs=[
            pl.BlockSpec((S5_GS, S5_ROWS, S5_ROW), lambda g: (g, 0, 0)),
            pl.BlockSpec((S5_GS, 4, CTX_B, 128), lambda g: (g, 0, 0, 0)),
        ],
        out_shape=[
            jax.ShapeDtypeStruct((S5_G, S5_ROWS, S5_ROW), F32),
            jax.ShapeDtypeStruct((S5_G, 4, CTX_B, 128), F32),
        ],
        scratch_shapes=[
            pltpu.VMEM((S5_GS, S5_ROWS, S5_W1), F32),
            pltpu.VMEM((S5_GS, S5_ROWS, S5_W2), F32),
        ],
        compiler_params=pltpu.CompilerParams(dimension_semantics=("arbitrary",)),
        name=f"s5_scan_l{l}",
    )(uc, w1, w2, a16, h0)


def _glu_kernel(y_ref, u_ref, d_ref, w_ref, b_ref, o_ref):
    y = y_ref[...] + d_ref[...] * u_ref[...]
    z = y * (0.5 * (1.0 + jnp.tanh(math.sqrt(2.0 / math.pi) * (y + 0.044715 * (y * y * y)))))
    t = _dot(z.astype(BF16), w_ref[...].astype(BF16)) + b_ref[...]
    o_ref[...] = (z * jax.nn.sigmoid(t)).astype(BF16)


def _glu(y, uf, s5_d3, w_glu, b_glu3, l):
    return pl.pallas_call(
        _glu_kernel,
        grid=(T_ALL // BM,),
        in_specs=[
            pl.BlockSpec((BM, S5_CH), lambda i: (i, 0)),
            pl.BlockSpec((BM, S5_CH), lambda i: (i, 0)),
            pl.BlockSpec((None, 1, S5_CH), lambda i: (l, 0, 0)),
            pl.BlockSpec((None, S5_CH, S5_CH), lambda i: (l, 0, 0)),
            pl.BlockSpec((None, 1, S5_CH), lambda i: (l, 0, 0)),
        ],
        out_specs=pl.BlockSpec((BM, S5_CH), lambda i: (i, 0)),
        out_shape=jax.ShapeDtypeStruct((T_ALL, S5_CH), BF16),
        compiler_params=pltpu.CompilerParams(dimension_semantics=("arbitrary",)),
        name=f"s5_glu_l{l}",
    )(y, uf, s5_d3, w_glu, b_glu3)


def _outproj_kernel(x_ref, a_ref, c_ref, s_ref, mod_ref, npost_ref, w_ref, o_ref, wb_ref):
    @pl.when(pl.program_id(0) == 0)
    def _():
        nk = 4
        rk = D_MODEL // nk
        for r in range(nk):
            wb_ref[r * rk:(r + 1) * rk, :] = w_ref[r * rk:(r + 1) * rk, :].astype(BF16)

    na = A_HEADS * HD
    nc = na + C_HEADS * HD
    y = (_dot(a_ref[...], wb_ref[0:na, :]) + _dot(c_ref[...], wb_ref[na:nc, :])
         + _dot(s_ref[...], wb_ref[nc:, :]))
    gate = mod_ref[:, 5 * D_MODEL:6 * D_MODEL]
    o_ref[...] = x_ref[...] + gate * _rms(y, npost_ref[...])


def _outproj(x, a_out, c_out, s_out, mods4, norm_post4, w_out, l):
    bm = BM_OUT
    return pl.pallas_call(
        _outproj_kernel,
        grid=(T_ALL // bm,),
        in_specs=[
            pl.BlockSpec((bm, D_MODEL), lambda i: (i, 0)),
            pl.BlockSpec((bm, A_HEADS * HD), lambda i: (i, 0)),
            pl.BlockSpec((bm, C_HEADS * HD), lambda i: (i, 0)),
            pl.BlockSpec((bm, S5_CH), lambda i: (i, 0)),
            pl.BlockSpec((None, None, 1, N_MOD * D_MODEL), lambda i: (l, _mod_index(i, bm), 0, 0)),
            pl.BlockSpec((None, None, 1, D_MODEL), lambda i: (l, 1, 0, 0)),
            pl.BlockSpec((None, D_MODEL, D_MODEL), lambda i: (l, 0, 0), pipeline_mode=pl.Buffered(1)),
        ],
        out_specs=pl.BlockSpec((bm, D_MODEL), lambda i: (i, 0)),
        out_shape=jax.ShapeDtypeStruct((T_ALL, D_MODEL), F32),
        scratch_shapes=[pltpu.VMEM((D_MODEL, D_MODEL), BF16)],
        compiler_params=pltpu.CompilerParams(
            dimension_semantics=("arbitrary",), vmem_limit_bytes=VMEM_BIG),
        name=f"outproj_l{l}",
    )(x, a_out, c_out, s_out, mods4, norm_post4, w_out)


def _rope_tables():
    rows = LAT_L // GRID_W
    row = jnp.repeat(jnp.arange(rows, dtype=F32), GRID_W)
    col = jnp.tile(jnp.arange(GRID_W, dtype=F32), rows)
    axis_dim = HD // 2
    inv_freq = ROPE_BASE ** (-jnp.arange(0, axis_dim, 2, dtype=F32) / axis_dim)
    ang_row = row[:, None] * inv_freq
    ang_col = col[:, None] * inv_freq
    cr, sr = jnp.cos(ang_row), jnp.sin(ang_row)
    cc, sc = jnp.cos(ang_col), jnp.sin(ang_col)
    cos_t = jnp.concatenate([cr, cr, cc, cc], axis=-1)
    sin_t = jnp.concatenate([-sr, sr, -sc, sc], axis=-1)
    return cos_t, sin_t


def _to_chunks(u, nb, nc):
    return u.reshape(nb, nc, S5_T, S5_G, S5_GC).transpose(3, 1, 0, 2, 4).reshape(S5_G, nc * nb, S5_ROW)


def _from_chunks(y, nb, nc):
    return y.reshape(S5_G, nc, nb, S5_T, S5_GC).transpose(2, 1, 3, 0, 4).reshape(nb * nc * S5_T, S5_CH)


def kernel(x_prompt, x_sample, cache_a_k, cache_a_v, cache_c_k, cache_c_v, state_ssm_re, state_ssm_im,
           c, c_ctx, w_mod, b_mod, norm_pre, norm_post, ffn_gate, ffn_up, ffn_down, w_in, w_out,
           q_norm, k_norm, sink, s5_lam_re, s5_lam_im, s5_log_step, s5_b_re, s5_b_im, s5_c_re, s5_c_im,
           s5_d, w_glu, b_glu):
    cvec8 = jnp.concatenate([c_ctx[None, :], c, jnp.zeros((8 - 1 - LAT_B, D_MODEL), F32)], axis=0)
    mods4 = _modulation(cvec8, w_mod, b_mod).reshape(DEPTH, 8, 1, N_MOD * D_MODEL)
    norm_pre4 = norm_pre.reshape(DEPTH, 3, 1, D_MODEL)
    norm_post4 = norm_post.reshape(DEPTH, 3, 1, D_MODEL)
    q_norm3 = q_norm.reshape(DEPTH, 1, HD)
    k_norm3 = k_norm.reshape(DEPTH, 1, HD)
    s5_d3 = s5_d.reshape(DEPTH, 1, S5_CH)
    b_glu3 = b_glu.reshape(DEPTH, 1, S5_CH)
    cos_t, sin_t = _rope_tables()
    w1, w2, a16 = _s5_params(s5_lam_re, s5_lam_im, s5_log_step, s5_b_re, s5_b_im, s5_c_re, s5_c_im)
    kv4 = lambda a: a.reshape(LAT_B, DEPTH, PAST, A_KV * HD)
    cak, cav, cck, ccv = kv4(cache_a_k), kv4(cache_a_v), kv4(cache_c_k), kv4(cache_c_v)

    x = jnp.concatenate([x_prompt.reshape(T_CTX, D_MODEL), x_sample.reshape(T_LAT, D_MODEL)], axis=0)
    new_kv = []
    new_state = []
    for l in range(DEPTH):
        x = _ffn(x, mods4, norm_pre4, norm_post4, ffn_gate, ffn_up, ffn_down, l, 0)

        qkvu, kvf, uf = _inproj(x, mods4, norm_pre4, q_norm3, k_norm3, cos_t, sin_t, w_in, l)
        a_out, c_out = _attn_ctx(qkvu, sink[l], l)
        a_out = _attn_lat_a(qkvu, cak, cav, a_out, l)
        c_out = _attn_lat_c(qkvu, sink[l], cck, ccv, c_out, l)

        ub = qkvu[:, COL_U:]
        uc = jnp.concatenate([_to_chunks(ub[:T_CTX], CTX_B, CTX_NC), _to_chunks(ub[T_CTX:], LAT_B, LAT_NC)], axis=1)
        h0 = jnp.stack([state_ssm_re[:, l, 0], state_ssm_im[:, l, 0],
                        state_ssm_re[:, l, 1], state_ssm_im[:, l, 1]], axis=0)
        h0 = jnp.pad(h0.transpose(2, 0, 1, 3), ((0, 0), (0, 0), (0, 0), (0, 128 - S5_P)))
        yc, hfin = _s5_scan(uc, w1, w2, a16, h0, l)
        y = jnp.concatenate([_from_chunks(yc[:, :S5_ROWS_CTX], CTX_B, CTX_NC),
                             _from_chunks(yc[:, S5_ROWS_CTX:], LAT_B, LAT_NC)], axis=0)
        s_out = _glu(y, uf, s5_d3, w_glu, b_glu3, l)

        x = _outproj(x, a_out, c_out, s_out, mods4, norm_post4, w_out, l)
        x = _ffn(x, mods4, norm_pre4, norm_post4, ffn_gate, ffn_up, ffn_down, l, 1)

        kvc = kvf[:T_CTX].reshape(CTX_B, CTX_L, 4, A_KV, HD)
        new_kv.append([kvc[:, :, k] for k in range(4)])
        hf = hfin[..., :S5_P].transpose(2, 1, 0, 3)
        new_state.append((jnp.stack([hf[:, 0], hf[:, 2]], axis=1), jnp.stack([hf[:, 1], hf[:, 3]], axis=1)))

    y_prompt = x[:T_CTX].reshape(CTX_B, CTX_L, D_MODEL)
    y_sample = x[T_CTX:].reshape(LAT_B, LAT_L, D_MODEL)
    caches = [jnp.stack([new_kv[l][k] for l in range(DEPTH)], axis=1) for k in range(4)]
    st_re = jnp.stack([new_state[l][0] for l in range(DEPTH)], axis=1)
    st_im = jnp.stack([new_state[l][1] for l in range(DEPTH)], axis=1)
    return (y_prompt, y_sample, caches[0], caches[1], caches[2], caches[3], st_re, st_im)
```

```python
import functools
import math

import jax
import jax.numpy as jnp
from jax import lax
from jax.experimental import pallas as pl
from jax.experimental.pallas import tpu as pltpu

F32 = jnp.float32
BF16 = jnp.bfloat16

D_MODEL = 2048
CTX_B, CTX_L = 16, 256
LAT_B, LAT_L = 2, 2048
DEPTH = 2
PAST = 512
GRID_W = 64
HD = 128
A_HEADS, A_KV = 8, 2
C_HEADS, C_KV = 4, 2
WINDOW = 128
S5_GC = 16
S5_CH = 512
S5_G = 32
S5_P = 64
D_FF = 5632
N_MOD = 9
IN_WIDTH = 3072
ROPE_BASE = 10000.0
EPS = 1e-6
HALF_STEP = 0.5
NEG_INF = -1e30
SCALE = HD ** -0.5

T_CTX = CTX_B * CTX_L
T_LAT = LAT_B * LAT_L
T_ALL = T_CTX + T_LAT

COL_AQ, COL_AK, COL_AV = 0, 1024, 1280
COL_CQ, COL_CK, COL_CV = 1536, 2048, 2304
COL_U = 2560

S5_T = 8
S5_OCT = S5_CH // 128
S5_OG = S5_G // S5_OCT
S5_ROW = S5_T * 128
S5_ST = S5_OG * S5_P
S5_W1 = S5_ROW + 4 * S5_ST
CTX_NC = CTX_L // S5_T
LAT_NC = LAT_L // S5_T
S5_ROWS = CTX_NC * CTX_B
assert S5_ROWS == LAT_NC * LAT_B and T_CTX == T_LAT

V7X_VMEM_BYTES = 64 * 1024 * 1024
VMEM_BIG = 56 * 1024 * 1024
VMEM_MID = 40 * 1024 * 1024

BM = 1024
BM_OUT = 512
BF = 256
RC = 512
BN_MOD = 1024
BN_IN = 512
BQ_A = 512
BQ_C = 256
S5_UNROLL = 4


def _dot(a, b):
    return jnp.dot(a, b, preferred_element_type=F32)


def _dot_nt(a, b, precision=None):
    return lax.dot_general(a, b, (((1,), (1,)), ((), ())), preferred_element_type=F32, precision=precision)


def _rms(x, g):
    return x * lax.rsqrt(jnp.mean(x * x, axis=-1, keepdims=True) + EPS) * g


def _mod_index(i, bm):
    nct = T_CTX // bm
    return jnp.where(i < nct, 0, 1 + (i - nct) // (LAT_L // bm))


def _mod_kernel(c_ref, w_ref, b_ref, o_ref):
    c = c_ref[...]
    s = (c * jax.nn.sigmoid(c)).astype(BF16)
    o_ref[...] = _dot(s, w_ref[...].astype(BF16)) + b_ref[...]


def _modulation(cvec8, w_mod, b_mod):
    n = N_MOD * D_MODEL
    return pl.pallas_call(
        _mod_kernel,
        grid=(DEPTH, n // BN_MOD),
        in_specs=[
            pl.BlockSpec((8, D_MODEL), lambda l, j: (0, 0)),
            pl.BlockSpec((None, D_MODEL, BN_MOD), lambda l, j: (l, 0, j)),
            pl.BlockSpec((None, 1, BN_MOD), lambda l, j: (l, 0, j)),
        ],
        out_specs=pl.BlockSpec((None, 8, BN_MOD), lambda l, j: (l, 0, j)),
        out_shape=jax.ShapeDtypeStruct((DEPTH, 8, n), F32),
        compiler_params=pltpu.CompilerParams(
            dimension_semantics=("arbitrary", "arbitrary"), vmem_limit_bytes=VMEM_MID),
        name="modulation",
    )(cvec8, w_mod, b_mod.reshape(DEPTH, 1, n))


def _ffn_kernel(x_ref, mod_ref, npre_ref, npost_ref, wg_ref, wu_ref, wd_ref, o_ref, h_ref, *, mo):
    j = pl.program_id(1)
    last = pl.num_programs(1) - 1

    def step(first, final):
        wg = wg_ref[...].astype(BF16)
        wu = wu_ref[...].astype(BF16)
        wd = wd_ref[...].astype(BF16)
        for r in range(BM // RC):
            rows = slice(r * RC, (r + 1) * RC)
            if first:
                shift = mod_ref[:, mo * D_MODEL:(mo + 1) * D_MODEL]
                scale = mod_ref[:, (mo + 1) * D_MODEL:(mo + 2) * D_MODEL]
                hn = _rms(x_ref[rows, :], npre_ref[...])
                h = (hn * (1.0 + scale) + shift).astype(BF16)
                h_ref[rows, :] = h
            else:
                h = h_ref[rows, :]
            g = _dot(h, wg)
            u = _dot(h, wu)
            a = (g * jax.nn.sigmoid(g) * u).astype(BF16)
            acc = _dot(a, wd)
            if not first:
                acc = o_ref[rows, :] + acc
            if final:
                gate = mod_ref[:, (mo + 2) * D_MODEL:(mo + 3) * D_MODEL]
                acc = x_ref[rows, :] + (HALF_STEP * gate) * _rms(acc, npost_ref[...])
            o_ref[rows, :] = acc

    pl.when(j == 0)(functools.partial(step, True, False))
    pl.when(jnp.logical_and(j > 0, j < last))(functools.partial(step, False, False))
    pl.when(j == last)(functools.partial(step, False, True))


def _ffn(x, mods4, norm_pre4, norm_post4, ffn_gate, ffn_up, ffn_down, l, s):
    mo = 6 * s
    ni = 2 * s
    return pl.pallas_call(
        functools.partial(_ffn_kernel, mo=mo),
        grid=(T_ALL // BM, D_FF // BF),
        in_specs=[
            pl.BlockSpec((BM, D_MODEL), lambda i, j: (i, 0), pipeline_mode=pl.Buffered(1)),
            pl.BlockSpec((None, None, 1, N_MOD * D_MODEL), lambda i, j: (l, _mod_index(i, BM), 0, 0)),
            pl.BlockSpec((None, None, 1, D_MODEL), lambda i, j: (l, ni, 0, 0)),
            pl.BlockSpec((None, None, 1, D_MODEL), lambda i, j: (l, ni, 0, 0)),
            pl.BlockSpec((None, None, D_MODEL, BF), lambda i, j: (l, s, 0, j)),
            pl.BlockSpec((None, None, D_MODEL, BF), lambda i, j: (l, s, 0, j)),
            pl.BlockSpec((None, None, BF, D_MODEL), lambda i, j: (l, s, j, 0)),
        ],
        out_specs=pl.BlockSpec((BM, D_MODEL), lambda i, j: (i, 0)),
        out_shape=jax.ShapeDtypeStruct((T_ALL, D_MODEL), F32),
        scratch_shapes=[pltpu.VMEM((BM, D_MODEL), BF16)],
        compiler_params=pltpu.CompilerParams(
            dimension_semantics=("arbitrary", "arbitrary"), vmem_limit_bytes=VMEM_BIG),
        name=f"ffn_l{l}_s{s}",
    )(x, mods4, norm_pre4, norm_post4, ffn_gate, ffn_up, ffn_down)


def _rope(y, cos, sins):
    lane = lax.broadcasted_iota(jnp.int32, y.shape, 1)
    first = (lane & 63) < 32
    partner = jnp.where(first, pltpu.roll(y, 96, 1), pltpu.roll(y, 32, 1))
    return y * cos + partner * sins


_IN_TILES = (
    (("q", True, None),) * 4,
    (("q", True, None),) * 4,
    (("k", True, "kv"), ("k", True, "kv"), (None, False, "kv"), (None, False, "kv")),
    ((None, True, None),) * 4,
    ((None, True, "kv"), (None, True, "kv"), (None, False, "kv"), (None, False, "kv")),
    ((None, False, "u"),) * 4,
)


def _inproj_kernel(x_ref, mod_ref, npre_ref, qn_ref, kn_ref, cos_ref, sin_ref, w_ref,
                   qkvu_ref, kvf_ref, uf_ref, h_ref, p_ref):
    i = pl.program_id(0)
    j = pl.program_id(1)
    is_lat = i >= T_CTX // BM

    def project(first):
        w = w_ref[...].astype(BF16)
        for r in range(BM // RC):
            rows = slice(r * RC, (r + 1) * RC)
            if first:
                shift = mod_ref[:, 3 * D_MODEL:4 * D_MODEL]
                scale = mod_ref[:, 4 * D_MODEL:5 * D_MODEL]
                hn = _rms(x_ref[rows, :], npre_ref[...])
                h = (hn * (1.0 + scale) + shift).astype(BF16)
                h_ref[rows, :] = h
            else:
                h = h_ref[rows, :]
            p_ref[rows, :] = _dot(h, w)

    pl.when(j == 0)(functools.partial(project, True))
    pl.when(j > 0)(functools.partial(project, False))

    def epilogue(tile, lat):
        for k, (norm, rot, copy) in enumerate(tile):
            cols = slice(k * HD, (k + 1) * HD)
            y = p_ref[:, cols]
            if norm == "q":
                y = _rms(y, qn_ref[...])
            elif norm == "k":
                y = _rms(y, kn_ref[...])
            if rot and lat:
                y = _rope(y, cos_ref[...], sin_ref[...])
            qkvu_ref[:, cols] = y.astype(BF16)
            if copy == "kv":
                kvf_ref[:, cols] = y
            elif copy == "u":
                uf_ref[:, cols] = y

    for jj, tile in enumerate(_IN_TILES):
        has_rot = any(t[1] for t in tile)
        if has_rot:
            pl.when(jnp.logical_and(j == jj, is_lat))(functools.partial(epilogue, tile, True))
            pl.when(jnp.logical_and(j == jj, jnp.logical_not(is_lat)))(functools.partial(epilogue, tile, False))
        else:
            pl.when(j == jj)(functools.partial(epilogue, tile, False))


def _inproj(x, mods4, norm_pre4, q_norm3, k_norm3, cos_t, sin_t, w_in, l):
    nct = T_CTX // BM
    tab_idx = lambda i, j: (jnp.maximum(i - nct, 0) % (LAT_L // BM), 0)
    return pl.pallas_call(
        _inproj_kernel,
        grid=(T_ALL // BM, IN_WIDTH // BN_IN),
        in_specs=[
            pl.BlockSpec((BM, D_MODEL), lambda i, j: (i, 0)),
            pl.BlockSpec((None, None, 1, N_MOD * D_MODEL), lambda i, j: (l, _mod_index(i, BM), 0, 0)),
            pl.BlockSpec((None, None, 1, D_MODEL), lambda i, j: (l, 1, 0, 0)),
            pl.BlockSpec((None, 1, HD), lambda i, j: (l, 0, 0)),
            pl.BlockSpec((None, 1, HD), lambda i, j: (l, 0, 0)),
            pl.BlockSpec((BM, HD), tab_idx),
            pl.BlockSpec((BM, HD), tab_idx),
            pl.BlockSpec((None, D_MODEL, BN_IN), lambda i, j: (l, 0, j)),
        ],
        out_specs=[
            pl.BlockSpec((BM, BN_IN), lambda i, j: (i, j)),
            pl.BlockSpec((BM, BN_IN), lambda i, j: (i, jnp.where(j >= 4, 1, 0))),
            pl.BlockSpec((BM, BN_IN), lambda i, j: (i, 0)),
        ],
        out_shape=[
            jax.ShapeDtypeStruct((T_ALL, IN_WIDTH), BF16),
            jax.ShapeDtypeStruct((T_ALL, 2 * BN_IN), F32),
            jax.ShapeDtypeStruct((T_ALL, S5_CH), F32),
        ],
        scratch_shapes=[pltpu.VMEM((BM, D_MODEL), BF16), pltpu.VMEM((BM, BN_IN), F32)],
        compiler_params=pltpu.CompilerParams(
            dimension_semantics=("arbitrary", "arbitrary"), vmem_limit_bytes=VMEM_BIG),
        name=f"inproj_l{l}",
    )(x, mods4, norm_pre4, q_norm3, k_norm3, cos_t, sin_t, w_in)


def _softmax_pv(scores, values, sink=None):
    m = functools.reduce(jnp.maximum, [jnp.max(s, axis=-1, keepdims=True) for s in scores])
    if sink is not None:
        m = jnp.maximum(m, sink)
    ps = [jnp.exp(s - m) for s in scores]
    den = functools.reduce(jnp.add, [jnp.sum(p, axis=-1, keepdims=True) for p in ps])
    if sink is not None:
        den = den + jnp.exp(sink - m)
    o = functools.reduce(jnp.add, [_dot(p.astype(BF16), v) for p, v in zip(ps, values)])
    return o / den


def _attn_ctx_kernel(sink_ref, aq_ref, ak_ref, av_ref, cq_ref, ck_ref, cv_ref, ao_ref, co_ref):
    kv = pl.program_id(1)
    k = ak_ref[...]
    v = av_ref[...]
    for h in range(A_HEADS // A_KV):
        cols = slice(h * HD, (h + 1) * HD)
        s = _dot_nt(aq_ref[:, cols], k) * SCALE
        ao_ref[:, cols] = _softmax_pv([s], [v]).astype(BF16)
    k = ck_ref[...]
    v = cv_ref[...]
    g = C_HEADS // C_KV
    for h in range(g):
        cols = slice(h * HD, (h + 1) * HD)
        s = _dot_nt(cq_ref[:, cols], k) * SCALE
        co_ref[:, cols] = _softmax_pv([s], [v], sink_ref[kv * g + h]).astype(BF16)


def _attn_ctx(qkvu, sink_l, l):
    ga = A_HEADS // A_KV * HD
    gc = C_HEADS // C_KV * HD
    blk = lambda width, col0: pl.BlockSpec((CTX_L, width), lambda b, k: (b, col0 // width + k))
    return pl.pallas_call(
        _attn_ctx_kernel,
        grid=(CTX_B, A_KV),
        in_specs=[
            pl.BlockSpec(memory_space=pltpu.SMEM),
            blk(ga, COL_AQ), blk(HD, COL_AK), blk(HD, COL_AV),
            blk(gc, COL_CQ), blk(HD, COL_CK), blk(HD, COL_CV),
        ],
        out_specs=[
            pl.BlockSpec((CTX_L, ga), lambda b, k: (b, k)),
            pl.BlockSpec((CTX_L, gc), lambda b, k: (b, k)),
        ],
        out_shape=[
            jax.ShapeDtypeStruct((T_ALL, A_HEADS * HD), BF16),
            jax.ShapeDtypeStruct((T_ALL, C_HEADS * HD), BF16),
        ],
        compiler_params=pltpu.CompilerParams(dimension_semantics=("arbitrary", "arbitrary")),
        name=f"attn_ctx_l{l}",
    )(sink_l, qkvu, qkvu, qkvu, qkvu, qkvu, qkvu)


def _attn_lat_a_kernel(q_ref, k_ref, v_ref, kc_ref, vc_ref, prev_ref, o_ref):
    del prev_ref
    k = k_ref[...]
    v = v_ref[...]
    kc = kc_ref[...].astype(BF16)
    vc = vc_ref[...].astype(BF16)
    for h in range(A_HEADS // A_KV):
        cols = slice(h * HD, (h + 1) * HD)
        q = q_ref[:, cols]
        s1 = _dot_nt(q, k) * SCALE
        s2 = _dot_nt(q, kc) * SCALE
        o_ref[:, cols] = _softmax_pv([s1, s2], [v, vc]).astype(BF16)


def _attn_lat_a(qkvu, cache_k4, cache_v4, a_out, l):
    ga = A_HEADS // A_KV * HD
    nq = LAT_L // BQ_A
    row0 = T_CTX // BQ_A
    lat_blk = T_CTX // LAT_L
    return pl.pallas_call(
        _attn_lat_a_kernel,
        grid=(LAT_B, A_KV, nq),
        in_specs=[
            pl.BlockSpec((BQ_A, ga), lambda b, k, q: (row0 + b * nq + q, COL_AQ // ga + k)),
            pl.BlockSpec((LAT_L, HD), lambda b, k, q: (lat_blk + b, COL_AK // HD + k)),
            pl.BlockSpec((LAT_L, HD), lambda b, k, q: (lat_blk + b, COL_AV // HD + k)),
            pl.BlockSpec((None, None, PAST, HD), lambda b, k, q: (b, l, 0, k)),
            pl.BlockSpec((None, None, PAST, HD), lambda b, k, q: (b, l, 0, k)),
            pl.BlockSpec(memory_space=pl.ANY),
        ],
        out_specs=pl.BlockSpec((BQ_A, ga), lambda b, k, q: (row0 + b * nq + q, k)),
        out_shape=jax.ShapeDtypeStruct((T_ALL, A_HEADS * HD), BF16),
        input_output_aliases={5: 0},
        compiler_params=pltpu.CompilerParams(
            dimension_semantics=("arbitrary", "arbitrary", "arbitrary"), vmem_limit_bytes=VMEM_BIG),
        name=f"attn_lat_a_l{l}",
    )(qkvu, qkvu, qkvu, cache_k4, cache_v4, a_out)


def _attn_lat_c_kernel(sink_ref, q_ref, k_ref, v_ref, kc_ref, vc_ref, prev_ref, o_ref):
    del prev_ref
    kv = pl.program_id(1)
    n = pl.program_id(2)
    span = BQ_C + 2 * WINDOW
    start = pl.multiple_of(jnp.clip(n * BQ_C - WINDOW, 0, LAT_L - span), WINDOW)
    kw = k_ref[pl.ds(start, span), :]
    vw = v_ref[pl.ds(start, span), :]
    kc = kc_ref[...].astype(BF16)
    vc = vc_ref[...].astype(BF16)
    qpos = n * BQ_C + lax.broadcasted_iota(jnp.int32, (BQ_C, span), 0)
    kpos = start + lax.broadcasted_iota(jnp.int32, (BQ_C, span), 1)
    valid = jnp.abs(qpos - kpos) <= WINDOW
    g = C_HEADS // C_KV
    for h in range(g):
        cols = slice(h * HD, (h + 1) * HD)
        q = q_ref[:, cols]
        s1 = jnp.where(valid, _dot_nt(q, kw) * SCALE, NEG_INF)
        s2 = _dot_nt(q, kc) * SCALE
        o_ref[:, cols] = _softmax_pv([s1, s2], [vw, vc], sink_ref[kv * g + h]).astype(BF16)


def _attn_lat_c(qkvu, sink_l, cache_k4, cache_v4, c_out, l):
    gc = C_HEADS // C_KV * HD
    nq = LAT_L // BQ_C
    row0 = T_CTX // BQ_C
    lat_blk = T_CTX // LAT_L
    return pl.pallas_call(
        _attn_lat_c_kernel,
        grid=(LAT_B, C_KV, nq),
        in_specs=[
            pl.BlockSpec(memory_space=pltpu.SMEM),
            pl.BlockSpec((BQ_C, gc), lambda b, k, q: (row0 + b * nq + q, COL_CQ // gc + k)),
            pl.BlockSpec((LAT_L, HD), lambda b, k, q: (lat_blk + b, COL_CK // HD + k)),
            pl.BlockSpec((LAT_L, HD), lambda b, k, q: (lat_blk + b, COL_CV // HD + k)),
            pl.BlockSpec((None, None, PAST, HD), lambda b, k, q: (b, l, 0, k)),
            pl.BlockSpec((None, None, PAST, HD), lambda b, k, q: (b, l, 0, k)),
            pl.BlockSpec(memory_space=pl.ANY),
        ],
        out_specs=pl.BlockSpec((BQ_C, gc), lambda b, k, q: (row0 + b * nq + q, k)),
        out_shape=jax.ShapeDtypeStruct((T_ALL, C_HEADS * HD), BF16),
        input_output_aliases={6: 0},
        compiler_params=pltpu.CompilerParams(
            dimension_semantics=("arbitrary", "arbitrary", "arbitrary")),
        name=f"attn_lat_c_l{l}",
    )(sink_l, qkvu, qkvu, qkvu, cache_k4, cache_v4, c_out)


def _s5_param_kernel(lr_ref, li_ref, ls_ref, btr_ref, bti_ref, cbr_ref, cbi_ref, w1_ref, w2_ref, a8_ref):
    a8_ref[...] = jnp.zeros_like(a8_ref)
    taps = []
    for d in range(2):
        lr = lr_ref[d]
        li = li_ref[d]
        dt = jnp.exp(ls_ref[d])
        mag = jnp.exp(lr * dt)
        ar = mag * jnp.cos(li * dt)
        ai = mag * jnp.sin(li * dt)
        den = lr * lr + li * li
        n_re = ar - 1.0
        f_re = (n_re * lr + ai * li) / den
        f_im = (ai * lr - n_re * li) / den
        pw = [(jnp.ones_like(ar), jnp.zeros_like(ar))]
        for _ in range(S5_T):
            pr, pi = pw[-1]
            pw.append((pr * ar - pi * ai, pr * ai + pi * ar))
        br = btr_ref[d]
        bi = bti_ref[d]
        bbr = f_re * br - f_im * bi
        bbi = f_re * bi + f_im * br
        cr = cbr_ref[d]
        ci = cbi_ref[d]
        xr_blocks, xi_blocks = [], []
        for t in range(S5_T):
            rows = slice(t * 128, (t + 1) * 128)
            pr, pi = pw[S5_T - 1 - t if d == 0 else t]
            xr = pr * bbr - pi * bbi
            xi = pr * bbi + pi * bbr
            c0 = S5_ROW + 2 * S5_ST * d
            w1_ref[rows, c0:c0 + S5_ST] = xr.astype(BF16)
            w1_ref[rows, c0 + S5_ST:c0 + 2 * S5_ST] = xi.astype(BF16)
            xr_blocks.append(xr)
            xi_blocks.append(xi)
            pr, pi = pw[t + 1 if d == 0 else S5_T - t]
            c0 = 2 * S5_ST * d
            w2_ref[rows, c0:c0 + S5_ST] = (cr * pr - ci * pi).astype(BF16)
            w2_ref[rows, c0 + S5_ST:c0 + 2 * S5_ST] = (-(cr * pi + ci * pr)).astype(BF16)
        a8_ref[2 * d:2 * d + 1, :] = pw[S5_T][0]
        a8_ref[2 * d + 1:2 * d + 2, :] = pw[S5_T][1]
        xr_all = jnp.concatenate(xr_blocks, axis=0)
        xi_all = jnp.concatenate(xi_blocks, axis=0)
        hi = lax.Precision.HIGHEST
        taps.append(_dot_nt(xr_all, cr, hi) - _dot_nt(xi_all, ci, hi))
    tf, tb = taps
    for t in range(S5_T):
        for t2 in range(S5_T):
            blk = None
            if t2 >= t:
                r0 = (S5_T - 1 - (t2 - t)) * 128
                blk = tf[r0:r0 + 128, :]
            if t2 <= t:
                r0 = (t - t2) * 128
                b2 = tb[r0:r0 + 128, :]
                blk = b2 if blk is None else blk + b2
            w1_ref[t * 128:(t + 1) * 128, t2 * 128:(t2 + 1) * 128] = blk.astype(BF16)


def _s5_params(s5_lam_re, s5_lam_im, s5_log_step, s5_b_re, s5_b_im, s5_c_re, s5_c_im):
    row = lambda a: a.reshape(DEPTH, 2, S5_OCT, 1, S5_ST)
    eye = jnp.eye(S5_OG, dtype=bool)[:, None, :, None]

    def expand(a):
        a = a.reshape(DEPTH, 2, S5_OCT, S5_OG, S5_GC, 1, S5_P)
        return jnp.where(eye, a, 0.0).reshape(DEPTH, 2, S5_OCT, 128, S5_ST)

    bt = lambda a: expand(jnp.swapaxes(a, -1, -2))
    spec = lambda r: pl.BlockSpec((None, 2, None, r, S5_ST), lambda l, s: (l, 0, s, 0, 0))
    return pl.pallas_call(
        _s5_param_kernel,
        grid=(DEPTH, S5_OCT),
        in_specs=[spec(1), spec(1), spec(1), spec(128), spec(128), spec(128), spec(128)],
        out_specs=[
            pl.BlockSpec((None, None, S5_ROW, S5_W1), lambda l, s: (l, s, 0, 0)),
            pl.BlockSpec((None, None, S5_ROW, 4 * S5_ST), lambda l, s: (l, s, 0, 0)),
            pl.BlockSpec((None, None, 8, S5_ST), lambda l, s: (l, s, 0, 0)),
        ],
        out_shape=[
            jax.ShapeDtypeStruct((DEPTH, S5_OCT, S5_ROW, S5_W1), BF16),
            jax.ShapeDtypeStruct((DEPTH, S5_OCT, S5_ROW, 4 * S5_ST), BF16),
            jax.ShapeDtypeStruct((DEPTH, S5_OCT, 8, S5_ST), F32),
        ],
        compiler_params=pltpu.CompilerParams(
            dimension_semantics=("arbitrary", "arbitrary"), vmem_limit_bytes=VMEM_MID),
        name="s5_params",
    )(row(s5_lam_re), row(s5_lam_im), row(jnp.repeat(s5_log_step, S5_P, axis=-1)),
      bt(s5_b_re), bt(s5_b_im), expand(s5_c_re), expand(s5_c_im))


def _s5_mix_kernel(u_ref, w1_ref, w2_ref, a8_ref, h0_ref, y_ref, hfin_ref, lhs_ref, a_ref, hp_ref, y8_ref):
    def stream(nb, nc, is_ctx):
        for b in range(nb):
            for t in range(S5_T):
                lhs_ref[t, pl.ds(b, nc, stride=nb), :] = u_ref[pl.ds(b * nc * S5_T + t, nc, stride=S5_T), :]
        lhs = jnp.concatenate([lhs_ref[t] for t in range(S5_T)], axis=1).astype(BF16)
        a_ref[...] = _dot(lhs, w1_ref[...])

        coef = [a8_ref[k:k + 1, :] for k in range(4)]
        if is_ctx:
            init = tuple(jnp.zeros((nb, S5_ST), F32) for _ in range(4))
        else:
            init = tuple(h0_ref[k] for k in range(4))

        def step(c, st):
            new = []
            for d in range(2):
                cc = c if d == 0 else nc - 1 - c
                rows = pl.ds(cc * nb, nb) if isinstance(c, int) else pl.ds(pl.multiple_of(cc * nb, nb), nb)
                hr, hi = st[2 * d], st[2 * d + 1]
                hp_ref[rows, 2 * S5_ST * d:2 * S5_ST * d + S5_ST] = hr
                hp_ref[rows, 2 * S5_ST * d + S5_ST:2 * S5_ST * (d + 1)] = hi
                c0 = S5_ROW + 2 * S5_ST * d
                gr = a_ref[rows, c0:c0 + S5_ST]
                gi = a_ref[rows, c0 + S5_ST:c0 + 2 * S5_ST]
                ar, ai = coef[2 * d], coef[2 * d + 1]
                new += [ar * hr - ai * hi + gr, ar * hi + ai * hr + gi]
            return tuple(new)

        if nb % 8 == 0:
            fin = lax.fori_loop(0, nc, step, init, unroll=S5_UNROLL)
        else:
            fin = init
            for c in range(nc):
                fin = step(c, fin)
        if is_ctx:
            for k in range(4):
                hfin_ref[k] = fin[k]

        y8 = a_ref[:, 0:S5_ROW] + _dot_nt(hp_ref[...].astype(BF16), w2_ref[...])
        for t in range(S5_T):
            y8_ref[t] = y8[:, t * 128:(t + 1) * 128]
        for b in range(nb):
            for t in range(S5_T):
                y_ref[pl.ds(b * nc * S5_T + t, nc, stride=S5_T), :] = y8_ref[t, pl.ds(b, nc, stride=nb), :]

    pl.when(pl.program_id(1) == 0)(functools.partial(stream, CTX_B, CTX_NC, True))
    pl.when(pl.program_id(1) == 1)(functools.partial(stream, LAT_B, LAT_NC, False))


def _s5_mix(uf, w1, w2, a8, h0, l):
    return pl.pallas_call(
        _s5_mix_kernel,
        grid=(S5_OCT, 2),
        in_specs=[
            pl.BlockSpec((T_CTX, 128), lambda s, k: (k, s)),
            pl.BlockSpec((None, None, S5_ROW, S5_W1), lambda s, k: (l, s, 0, 0)),
            pl.BlockSpec((None, None, S5_ROW, 4 * S5_ST), lambda s, k: (l, s, 0, 0)),
            pl.BlockSpec((None, None, 8, S5_ST), lambda s, k: (l, s, 0, 0)),
            pl.BlockSpec((None, 4, LAT_B, S5_ST), lambda s, k: (s, 0, 0, 0)),
        ],
        out_specs=[
            pl.BlockSpec((T_CTX, 128), lambda s, k: (k, s)),
            pl.BlockSpec((None, 4, CTX_B, S5_ST), lambda s, k: (s, 0, 0, 0)),
        ],
        out_shape=[
            jax.ShapeDtypeStruct((T_ALL, S5_CH), F32),
            jax.ShapeDtypeStruct((S5_OCT, 4, CTX_B, S5_ST), F32),
        ],
        scratch_shapes=[
            pltpu.VMEM((S5_T, S5_ROWS, 128), F32),
            pltpu.VMEM((S5_ROWS, S5_W1), F32),
            pltpu.VMEM((S5_ROWS, 4 * S5_ST), F32),
            pltpu.VMEM((S5_T, S5_ROWS, 128), F32),
        ],
        compiler_params=pltpu.CompilerParams(
            dimension_semantics=("arbitrary", "arbitrary"), vmem_limit_bytes=VMEM_BIG),
        name=f"s5_mix_l{l}",
    )(uf, w1, w2, a8, h0)


def _glu_kernel(y_ref, u_ref, d_ref, w_ref, b_ref, o_ref):
    y = y_ref[...] + d_ref[...] * u_ref[...]
    z = y * (0.5 * (1.0 + jnp.tanh(math.sqrt(2.0 / math.pi) * (y + 0.044715 * (y * y * y)))))
    t = _dot(z.astype(BF16), w_ref[...].astype(BF16)) + b_ref[...]
    o_ref[...] = (z * jax.nn.sigmoid(t)).astype(BF16)


def _glu(y, uf, s5_d3, w_glu, b_glu3, l):
    return pl.pallas_call(
        _glu_kernel,
        grid=(T_ALL // BM,),
        in_specs=[
            pl.BlockSpec((BM, S5_CH), lambda i: (i, 0)),
            pl.BlockSpec((BM, S5_CH), lambda i: (i, 0)),
            pl.BlockSpec((None, 1, S5_CH), lambda i: (l, 0, 0)),
            pl.BlockSpec((None, S5_CH, S5_CH), lambda i: (l, 0, 0)),
            pl.BlockSpec((None, 1, S5_CH), lambda i: (l, 0, 0)),
        ],
        out_specs=pl.BlockSpec((BM, S5_CH), lambda i: (i, 0)),
        out_shape=jax.ShapeDtypeStruct((T_ALL, S5_CH), BF16),
        compiler_params=pltpu.CompilerParams(dimension_semantics=("arbitrary",)),
        name=f"s5_glu_l{l}",
    )(y, uf, s5_d3, w_glu, b_glu3)


def _outproj_kernel(x_ref, a_ref, c_ref, s_ref, mod_ref, npost_ref, w_ref, o_ref, wb_ref):
    @pl.when(pl.program_id(0) == 0)
    def _():
        nk = 4
        rk = D_MODEL // nk
        for r in range(nk):
            wb_ref[r * rk:(r + 1) * rk, :] = w_ref[r * rk:(r + 1) * rk, :].astype(BF16)

    na = A_HEADS * HD
    nc = na + C_HEADS * HD
    y = (_dot(a_ref[...], wb_ref[0:na, :]) + _dot(c_ref[...], wb_ref[na:nc, :])
         + _dot(s_ref[...], wb_ref[nc:, :]))
    gate = mod_ref[:, 5 * D_MODEL:6 * D_MODEL]
    o_ref[...] = x_ref[...] + gate * _rms(y, npost_ref[...])


def _outproj(x, a_out, c_out, s_out, mods4, norm_post4, w_out, l):
    bm = BM_OUT
    return pl.pallas_call(
        _outproj_kernel,
        grid=(T_ALL // bm,),
        in_specs=[
            pl.BlockSpec((bm, D_MODEL), lambda i: (i, 0)),
            pl.BlockSpec((bm, A_HEADS * HD), lambda i: (i, 0)),
            pl.BlockSpec((bm, C_HEADS * HD), lambda i: (i, 0)),
            pl.BlockSpec((bm, S5_CH), lambda i: (i, 0)),
            pl.BlockSpec((None, None, 1, N_MOD * D_MODEL), lambda i: (l, _mod_index(i, bm), 0, 0)),
            pl.BlockSpec((None, None, 1, D_MODEL), lambda i: (l, 1, 0, 0)),
            pl.BlockSpec((None, D_MODEL, D_MODEL), lambda i: (l, 0, 0), pipeline_mode=pl.Buffered(1)),
        ],
        out_specs=pl.BlockSpec((bm, D_MODEL), lambda i: (i, 0)),
        out_shape=jax.ShapeDtypeStruct((T_ALL, D_MODEL), F32),
        scratch_shapes=[pltpu.VMEM((D_MODEL, D_MODEL), BF16)],
        compiler_params=pltpu.CompilerParams(
            dimension_semantics=("arbitrary",), vmem_limit_bytes=VMEM_BIG),
        name=f"outproj_l{l}",
    )(x, a_out, c_out, s_out, mods4, norm_post4, w_out)


def _rope_tables():
    rows = LAT_L // GRID_W
    row = jnp.repeat(jnp.arange(rows, dtype=F32), GRID_W)
    col = jnp.tile(jnp.arange(GRID_W, dtype=F32), rows)
    axis_dim = HD // 2
    inv_freq = ROPE_BASE ** (-jnp.arange(0, axis_dim, 2, dtype=F32) / axis_dim)
    ang_row = row[:, None] * inv_freq
    ang_col = col[:, None] * inv_freq
    cr, sr = jnp.cos(ang_row), jnp.sin(ang_row)
    cc, sc = jnp.cos(ang_col), jnp.sin(ang_col)
    cos_t = jnp.concatenate([cr, cr, cc, cc], axis=-1)
    sin_t = jnp.concatenate([-sr, sr, -sc, sc], axis=-1)
    return cos_t, sin_t


def kernel(x_prompt, x_sample, cache_a_k, cache_a_v, cache_c_k, cache_c_v, state_ssm_re, state_ssm_im,
           c, c_ctx, w_mod, b_mod, norm_pre, norm_post, ffn_gate, ffn_up, ffn_down, w_in, w_out,
           q_norm, k_norm, sink, s5_lam_re, s5_lam_im, s5_log_step, s5_b_re, s5_b_im, s5_c_re, s5_c_im,
           s5_d, w_glu, b_glu):
    cvec8 = jnp.concatenate([c_ctx[None, :], c, jnp.zeros((8 - 1 - LAT_B, D_MODEL), F32)], axis=0)
    mods4 = _modulation(cvec8, w_mod, b_mod).reshape(DEPTH, 8, 1, N_MOD * D_MODEL)
    norm_pre4 = norm_pre.reshape(DEPTH, 3, 1, D_MODEL)
    norm_post4 = norm_post.reshape(DEPTH, 3, 1, D_MODEL)
    q_norm3 = q_norm.reshape(DEPTH, 1, HD)
    k_norm3 = k_norm.reshape(DEPTH, 1, HD)
    s5_d3 = s5_d.reshape(DEPTH, 1, S5_CH)
    b_glu3 = b_glu.reshape(DEPTH, 1, S5_CH)
    cos_t, sin_t = _rope_tables()
    w1, w2, a8 = _s5_params(s5_lam_re, s5_lam_im, s5_log_step, s5_b_re, s5_b_im, s5_c_re, s5_c_im)
    kv4 = lambda a: a.reshape(LAT_B, DEPTH, PAST, A_KV * HD)
    cak, cav, cck, ccv = kv4(cache_a_k), kv4(cache_a_v), kv4(cache_c_k), kv4(cache_c_v)
    h0_all = jnp.stack([state_ssm_re[:, :, 0], state_ssm_im[:, :, 0],
                        state_ssm_re[:, :, 1], state_ssm_im[:, :, 1]], axis=0)
    h0_all = h0_all.reshape(4, LAT_B, DEPTH, S5_OCT, S5_ST).transpose(2, 3, 0, 1, 4)

    x = jnp.concatenate([x_prompt.reshape(T_CTX, D_MODEL), x_sample.reshape(T_LAT, D_MODEL)], axis=0)
    new_kv = []
    new_state = []
    for l in range(DEPTH):
        x = _ffn(x, mods4, norm_pre4, norm_post4, ffn_gate, ffn_up, ffn_down, l, 0)

        qkvu, kvf, uf = _inproj(x, mods4, norm_pre4, q_norm3, k_norm3, cos_t, sin_t, w_in, l)
        a_out, c_out = _attn_ctx(qkvu, sink[l], l)
        a_out = _attn_lat_a(qkvu, cak, cav, a_out, l)
        c_out = _attn_lat_c(qkvu, sink[l], cck, ccv, c_out, l)
        y, hfin = _s5_mix(uf, w1, w2, a8, h0_all[l], l)
        s_out = _glu(y, uf, s5_d3, w_glu, b_glu3, l)

        x = _outproj(x, a_out, c_out, s_out, mods4, norm_post4, w_out, l)
        x = _ffn(x, mods4, norm_pre4, norm_post4, ffn_gate, ffn_up, ffn_down, l, 1)

        kvc = kvf[:T_CTX].reshape(CTX_B, CTX_L, 4, A_KV, HD)
        new_kv.append([kvc[:, :, k] for k in range(4)])
        hf = hfin.reshape(S5_OCT, 4, CTX_B, S5_OG, S5_P).transpose(1, 2, 0, 3, 4).reshape(4, CTX_B, S5_G, S5_P)
        new_state.append((jnp.stack([hf[0], hf[2]], axis=1), jnp.stack([hf[1], hf[3]], axis=1)))

    y_prompt = x[:T_CTX].reshape(CTX_B, CTX_L, D_MODEL)
    y_sample = x[T_CTX:].reshape(LAT_B, LAT_L, D_MODEL)
    caches = [jnp.stack([new_kv[l][k] for l in range(DEPTH)], axis=1) for k in range(4)]
    st_re = jnp.stack([new_state[l][0] for l in range(DEPTH)], axis=1)
    st_im = jnp.stack([new_state[l][1] for l in range(DEPTH)], axis=1)
    return (y_prompt, y_sample, caches[0], caches[1], caches[2], caches[3], st_re, st_im)
```

```python
import functools
import math

import jax
import jax.numpy as jnp
from jax import lax
from jax.experimental import pallas as pl
from jax.experimental.pallas import tpu as pltpu

F32 = jnp.float32
BF16 = jnp.bfloat16

D_MODEL = 2048
CTX_B, CTX_L = 16, 256
LAT_B, LAT_L = 2, 2048
DEPTH = 2
PAST = 512
GRID_W = 64
HD = 128
A_HEADS, A_KV = 8, 2
C_HEADS, C_KV = 4, 2
WINDOW = 128
S5_GC = 16
S5_CH = 512
S5_G = 32
S5_P = 64
D_FF = 5632
N_MOD = 9
IN_WIDTH = 3072
ROPE_BASE = 10000.0
EPS = 1e-6
HALF_STEP = 0.5
NEG_INF = -1e30
SCALE = HD ** -0.5

T_CTX = CTX_B * CTX_L
T_LAT = LAT_B * LAT_L
T_ALL = T_CTX + T_LAT

COL_AQ, COL_AK, COL_AV = 0, 1024, 1280
COL_CQ, COL_CK, COL_CV = 1536, 2048, 2304
COL_U = 2560

S5_T = 8
S5_OCT = S5_CH // 128
S5_OG = S5_G // S5_OCT
S5_ROW = S5_T * 128
S5_ST = S5_OG * S5_P
S5_W1 = S5_ROW + 4 * S5_ST
CTX_NC = CTX_L // S5_T
LAT_NC = LAT_L // S5_T
S5_ROWS = CTX_NC * CTX_B
assert S5_ROWS == LAT_NC * LAT_B and T_CTX == T_LAT

V7X_VMEM_BYTES = 64 * 1024 * 1024
VMEM_BIG = 56 * 1024 * 1024
VMEM_MID = 40 * 1024 * 1024

BM = 1024
BM_OUT = 512
BF = 256
RC = 512
BN_MOD = 1024
BM_IN = 512
BQ_A = 512
BQ_C = 256
S5_UNROLL = 4


def _dot(a, b):
    return jnp.dot(a, b, preferred_element_type=F32)


def _dot_nt(a, b, precision=None):
    return lax.dot_general(a, b, (((1,), (1,)), ((), ())), preferred_element_type=F32, precision=precision)


def _rms(x, g):
    return x * lax.rsqrt(jnp.mean(x * x, axis=-1, keepdims=True) + EPS) * g


def _mod_index(i, bm):
    nct = T_CTX // bm
    return jnp.where(i < nct, 0, 1 + (i - nct) // (LAT_L // bm))


def _mod_kernel(c_ref, w_ref, b_ref, o_ref):
    c = c_ref[...]
    s = (c * jax.nn.sigmoid(c)).astype(BF16)
    o_ref[...] = _dot(s, w_ref[...].astype(BF16)) + b_ref[...]


def _modulation(cvec8, w_mod, b_mod):
    n = N_MOD * D_MODEL
    return pl.pallas_call(
        _mod_kernel,
        grid=(DEPTH, n // BN_MOD),
        in_specs=[
            pl.BlockSpec((8, D_MODEL), lambda l, j: (0, 0)),
            pl.BlockSpec((None, D_MODEL, BN_MOD), lambda l, j: (l, 0, j)),
            pl.BlockSpec((None, 1, BN_MOD), lambda l, j: (l, 0, j)),
        ],
        out_specs=pl.BlockSpec((None, 8, BN_MOD), lambda l, j: (l, 0, j)),
        out_shape=jax.ShapeDtypeStruct((DEPTH, 8, n), F32),
        compiler_params=pltpu.CompilerParams(
            dimension_semantics=("arbitrary", "arbitrary"), vmem_limit_bytes=VMEM_MID),
        name="modulation",
    )(cvec8, w_mod, b_mod.reshape(DEPTH, 1, n))


def _ffn_kernel(x_ref, mod_ref, npre_ref, npost_ref, wg_ref, wu_ref, wd_ref, o_ref, h_ref, *, mo):
    j = pl.program_id(1)
    last = pl.num_programs(1) - 1

    def step(first, final):
        wg = wg_ref[...].astype(BF16)
        wu = wu_ref[...].astype(BF16)
        wd = wd_ref[...].astype(BF16)
        for r in range(BM // RC):
            rows = slice(r * RC, (r + 1) * RC)
            if first:
                shift = mod_ref[:, mo * D_MODEL:(mo + 1) * D_MODEL]
                scale = mod_ref[:, (mo + 1) * D_MODEL:(mo + 2) * D_MODEL]
                hn = _rms(x_ref[rows, :], npre_ref[...])
                h = (hn * (1.0 + scale) + shift).astype(BF16)
                h_ref[rows, :] = h
            else:
                h = h_ref[rows, :]
            g = _dot(h, wg)
            u = _dot(h, wu)
            a = (g * jax.nn.sigmoid(g) * u).astype(BF16)
            acc = _dot(a, wd)
            if not first:
                acc = o_ref[rows, :] + acc
            if final:
                gate = mod_ref[:, (mo + 2) * D_MODEL:(mo + 3) * D_MODEL]
                acc = x_ref[rows, :] + (HALF_STEP * gate) * _rms(acc, npost_ref[...])
            o_ref[rows, :] = acc

    pl.when(j == 0)(functools.partial(step, True, False))
    pl.when(jnp.logical_and(j > 0, j < last))(functools.partial(step, False, False))
    pl.when(j == last)(functools.partial(step, False, True))


def _ffn(x, mods4, norm_pre4, norm_post4, ffn_gate, ffn_up, ffn_down, l, s, *,
         slab_tile0=0, n_tiles=T_ALL // BM, in_tile0=0, out_tile0=0, out_rows=T_ALL, prev=None, tag=""):
    mo = 6 * s
    ni = 2 * s
    in_specs = [
        pl.BlockSpec((BM, D_MODEL), lambda i, j: (in_tile0 + i, 0)),
        pl.BlockSpec((None, None, 1, N_MOD * D_MODEL), lambda i, j: (l, _mod_index(slab_tile0 + i, BM), 0, 0)),
        pl.BlockSpec((None, None, 1, D_MODEL), lambda i, j: (l, ni, 0, 0)),
        pl.BlockSpec((None, None, 1, D_MODEL), lambda i, j: (l, ni, 0, 0)),
        pl.BlockSpec((None, None, D_MODEL, BF), lambda i, j: (l, s, 0, j)),
        pl.BlockSpec((None, None, D_MODEL, BF), lambda i, j: (l, s, 0, j)),
        pl.BlockSpec((None, None, BF, D_MODEL), lambda i, j: (l, s, j, 0)),
    ]
    args = [x, mods4, norm_pre4, norm_post4, ffn_gate, ffn_up, ffn_down]
    kern = functools.partial(_ffn_kernel, mo=mo)
    aliases = {}
    if prev is not None:
        in_specs.append(pl.BlockSpec(memory_space=pl.ANY))
        args.append(prev)
        aliases = {len(args) - 1: 0}
        kern = functools.partial(_ffn_kernel_keep, mo=mo)
    return pl.pallas_call(
        kern,
        grid=(n_tiles, D_FF // BF),
        in_specs=in_specs,
        out_specs=pl.BlockSpec((BM, D_MODEL), lambda i, j: (out_tile0 + i, 0)),
        out_shape=jax.ShapeDtypeStruct((out_rows, D_MODEL), F32),
        input_output_aliases=aliases,
        scratch_shapes=[pltpu.VMEM((BM, D_MODEL), BF16)],
        compiler_params=pltpu.CompilerParams(
            dimension_semantics=("arbitrary", "arbitrary"), vmem_limit_bytes=VMEM_BIG),
        name=f"ffn_l{l}_s{s}{tag}",
    )(*args)


def _ffn_kernel_keep(x_ref, mod_ref, npre_ref, npost_ref, wg_ref, wu_ref, wd_ref, prev_ref, o_ref, h_ref, *, mo):
    del prev_ref
    _ffn_kernel(x_ref, mod_ref, npre_ref, npost_ref, wg_ref, wu_ref, wd_ref, o_ref, h_ref, mo=mo)


def _rope(y, cos, sins):
    lane = lax.broadcasted_iota(jnp.int32, y.shape, 1)
    first = (lane & 63) < 32
    partner = jnp.where(first, pltpu.roll(y, 96, 1), pltpu.roll(y, 32, 1))
    return y * cos + partner * sins


_IN_SEGMENTS = (
    (COL_AQ, A_HEADS, "q", True, None),
    (COL_AK, A_KV, "k", True, 0),
    (COL_AV, A_KV, None, False, 1),
    (COL_CQ, C_HEADS, None, True, None),
    (COL_CK, C_KV, None, True, 2),
    (COL_CV, C_KV, None, False, 3),
)


def _inproj_kernel(x_ref, mod_ref, npre_ref, qn_ref, kn_ref, cos_ref, sin_ref, w_ref, *rest):
    qkvu_ref, uf_ref = rest[-6], rest[-5]
    cache_refs = rest[-4:]
    nb = BM_IN // CTX_L

    def body(lat):
        shift = mod_ref[:, 3 * D_MODEL:4 * D_MODEL]
        scale = mod_ref[:, 4 * D_MODEL:5 * D_MODEL]
        h = (_rms(x_ref[...], npre_ref[...]) * (1.0 + scale) + shift).astype(BF16)
        for col0, heads, norm, rot, cache in _IN_SEGMENTS:
            p = _dot(h, w_ref[:, col0:col0 + heads * HD])
            for k in range(heads):
                y = p[:, k * HD:(k + 1) * HD]
                if norm == "q":
                    y = _rms(y, qn_ref[...])
                elif norm == "k":
                    y = _rms(y, kn_ref[...])
                if rot and lat:
                    y = _rope(y, cos_ref[...], sin_ref[...])
                qkvu_ref[:, col0 + k * HD:col0 + (k + 1) * HD] = y.astype(BF16)
                if cache is not None and not lat:
                    cache_refs[cache][:, :, k * HD:(k + 1) * HD] = y.reshape(nb, CTX_L, HD)
        u = _dot(h, w_ref[:, COL_U:])
        uf_ref[...] = u
        qkvu_ref[:, COL_U:] = u.astype(BF16)

    is_lat = pl.program_id(0) >= T_CTX // BM_IN
    pl.when(is_lat)(functools.partial(body, True))
    pl.when(jnp.logical_not(is_lat))(functools.partial(body, False))


def _inproj(x, mods4, norm_pre4, q_norm3, k_norm3, cos_t, sin_t, w_in_bf, l, prev_caches):
    bm = BM_IN
    nct = T_CTX // bm
    nb = bm // CTX_L
    tab_idx = lambda i: (jnp.maximum(i - nct, 0) % (LAT_L // bm), 0)
    cache_shape = jax.ShapeDtypeStruct((CTX_B, DEPTH, CTX_L, A_KV * HD), F32)
    cache_spec = pl.BlockSpec((nb, None, CTX_L, A_KV * HD), lambda i: (jnp.minimum(i, nct - 1), l, 0, 0))
    n_in = 8
    return pl.pallas_call(
        _inproj_kernel,
        grid=(T_ALL // bm,),
        in_specs=[
            pl.BlockSpec((bm, D_MODEL), lambda i: (i, 0)),
            pl.BlockSpec((None, None, 1, N_MOD * D_MODEL), lambda i: (l, _mod_index(i, bm), 0, 0)),
            pl.BlockSpec((None, None, 1, D_MODEL), lambda i: (l, 1, 0, 0)),
            pl.BlockSpec((None, 1, HD), lambda i: (l, 0, 0)),
            pl.BlockSpec((None, 1, HD), lambda i: (l, 0, 0)),
            pl.BlockSpec((bm, HD), tab_idx),
            pl.BlockSpec((bm, HD), tab_idx),
            pl.BlockSpec((None, D_MODEL, IN_WIDTH), lambda i: (l, 0, 0), pipeline_mode=pl.Buffered(1)),
        ] + [pl.BlockSpec(memory_space=pl.ANY)] * len(prev_caches),
        out_specs=[
            pl.BlockSpec((bm, IN_WIDTH), lambda i: (i, 0)),
            pl.BlockSpec((bm, S5_CH), lambda i: (i, 0)),
        ] + [cache_spec] * 4,
        out_shape=[
            jax.ShapeDtypeStruct((T_ALL, IN_WIDTH), BF16),
            jax.ShapeDtypeStruct((T_ALL, S5_CH), F32),
        ] + [cache_shape] * 4,
        input_output_aliases={n_in + k: 2 + k for k in range(len(prev_caches))},
        compiler_params=pltpu.CompilerParams(
            dimension_semantics=("arbitrary",), vmem_limit_bytes=VMEM_BIG),
        name=f"inproj_l{l}",
    )(x, mods4, norm_pre4, q_norm3, k_norm3, cos_t, sin_t, w_in_bf, *prev_caches)


def _softmax_pv(scores, values, sink=None):
    m = functools.reduce(jnp.maximum, [jnp.max(s, axis=-1, keepdims=True) for s in scores])
    if sink is not None:
        m = jnp.maximum(m, sink)
    ps = [jnp.exp(s - m) for s in scores]
    den = functools.reduce(jnp.add, [jnp.sum(p, axis=-1, keepdims=True) for p in ps])
    if sink is not None:
        den = den + jnp.exp(sink - m)
    o = functools.reduce(jnp.add, [_dot(p.astype(BF16), v) for p, v in zip(ps, values)])
    return o / den


def _attn_ctx_kernel(sink_ref, aq_ref, ak_ref, av_ref, cq_ref, ck_ref, cv_ref, ao_ref, co_ref):
    kv = pl.program_id(1)
    k = ak_ref[...]
    v = av_ref[...]
    for h in range(A_HEADS // A_KV):
        cols = slice(h * HD, (h + 1) * HD)
        s = _dot_nt(aq_ref[:, cols], k) * SCALE
        ao_ref[:, cols] = _softmax_pv([s], [v]).astype(BF16)
    k = ck_ref[...]
    v = cv_ref[...]
    g = C_HEADS // C_KV
    for h in range(g):
        cols = slice(h * HD, (h + 1) * HD)
        s = _dot_nt(cq_ref[:, cols], k) * SCALE
        co_ref[:, cols] = _softmax_pv([s], [v], sink_ref[kv * g + h]).astype(BF16)


def _attn_ctx(qkvu, sink_l, l):
    ga = A_HEADS // A_KV * HD
    gc = C_HEADS // C_KV * HD
    blk = lambda width, col0: pl.BlockSpec((CTX_L, width), lambda b, k: (b, col0 // width + k))
    return pl.pallas_call(
        _attn_ctx_kernel,
        grid=(CTX_B, A_KV),
        in_specs=[
            pl.BlockSpec(memory_space=pltpu.SMEM),
            blk(ga, COL_AQ), blk(HD, COL_AK), blk(HD, COL_AV),
            blk(gc, COL_CQ), blk(HD, COL_CK), blk(HD, COL_CV),
        ],
        out_specs=[
            pl.BlockSpec((CTX_L, ga), lambda b, k: (b, k)),
            pl.BlockSpec((CTX_L, gc), lambda b, k: (b, k)),
        ],
        out_shape=[
            jax.ShapeDtypeStruct((T_ALL, A_HEADS * HD), BF16),
            jax.ShapeDtypeStruct((T_ALL, C_HEADS * HD), BF16),
        ],
        compiler_params=pltpu.CompilerParams(dimension_semantics=("arbitrary", "arbitrary")),
        name=f"attn_ctx_l{l}",
    )(sink_l, qkvu, qkvu, qkvu, qkvu, qkvu, qkvu)


def _attn_lat_a_kernel(q_ref, k_ref, v_ref, kc_ref, vc_ref, prev_ref, o_ref):
    del prev_ref
    k = k_ref[...]
    v = v_ref[...]
    kc = kc_ref[...].astype(BF16)
    vc = vc_ref[...].astype(BF16)
    for h in range(A_HEADS // A_KV):
        cols = slice(h * HD, (h + 1) * HD)
        q = q_ref[:, cols]
        s1 = _dot_nt(q, k) * SCALE
        s2 = _dot_nt(q, kc) * SCALE
        o_ref[:, cols] = _softmax_pv([s1, s2], [v, vc]).astype(BF16)


def _attn_lat_a(qkvu, cache_k4, cache_v4, a_out, l):
    ga = A_HEADS // A_KV * HD
    nq = LAT_L // BQ_A
    row0 = T_CTX // BQ_A
    lat_blk = T_CTX // LAT_L
    return pl.pallas_call(
        _attn_lat_a_kernel,
        grid=(LAT_B, A_KV, nq),
        in_specs=[
            pl.BlockSpec((BQ_A, ga), lambda b, k, q: (row0 + b * nq + q, COL_AQ // ga + k)),
            pl.BlockSpec((LAT_L, HD), lambda b, k, q: (lat_blk + b, COL_AK // HD + k)),
            pl.BlockSpec((LAT_L, HD), lambda b, k, q: (lat_blk + b, COL_AV // HD + k)),
            pl.BlockSpec((None, None, PAST, HD), lambda b, k, q: (b, l, 0, k)),
            pl.BlockSpec((None, None, PAST, HD), lambda b, k, q: (b, l, 0, k)),
            pl.BlockSpec(memory_space=pl.ANY),
        ],
        out_specs=pl.BlockSpec((BQ_A, ga), lambda b, k, q: (row0 + b * nq + q, k)),
        out_shape=jax.ShapeDtypeStruct((T_ALL, A_HEADS * HD), BF16),
        input_output_aliases={5: 0},
        compiler_params=pltpu.CompilerParams(
            dimension_semantics=("arbitrary", "arbitrary", "arbitrary"), vmem_limit_bytes=VMEM_BIG),
        name=f"attn_lat_a_l{l}",
    )(qkvu, qkvu, qkvu, cache_k4, cache_v4, a_out)


def _attn_lat_c_kernel(sink_ref, q_ref, k_ref, v_ref, kc_ref, vc_ref, prev_ref, o_ref):
    del prev_ref
    kv = pl.program_id(1)
    n = pl.program_id(2)
    span = BQ_C + 2 * WINDOW
    start = pl.multiple_of(jnp.clip(n * BQ_C - WINDOW, 0, LAT_L - span), WINDOW)
    kw = k_ref[pl.ds(start, span), :]
    vw = v_ref[pl.ds(start, span), :]
    kc = kc_ref[...].astype(BF16)
    vc = vc_ref[...].astype(BF16)
    qpos = n * BQ_C + lax.broadcasted_iota(jnp.int32, (BQ_C, span), 0)
    kpos = start + lax.broadcasted_iota(jnp.int32, (BQ_C, span), 1)
    valid = jnp.abs(qpos - kpos) <= WINDOW
    g = C_HEADS // C_KV
    for h in range(g):
        cols = slice(h * HD, (h + 1) * HD)
        q = q_ref[:, cols]
        s1 = jnp.where(valid, _dot_nt(q, kw) * SCALE, NEG_INF)
        s2 = _dot_nt(q, kc) * SCALE
        o_ref[:, cols] = _softmax_pv([s1, s2], [vw, vc], sink_ref[kv * g + h]).astype(BF16)


def _attn_lat_c(qkvu, sink_l, cache_k4, cache_v4, c_out, l):
    gc = C_HEADS // C_KV * HD
    nq = LAT_L // BQ_C
    row0 = T_CTX // BQ_C
    lat_blk = T_CTX // LAT_L
    return pl.pallas_call(
        _attn_lat_c_kernel,
        grid=(LAT_B, C_KV, nq),
        in_specs=[
            pl.BlockSpec(memory_space=pltpu.SMEM),
            pl.BlockSpec((BQ_C, gc), lambda b, k, q: (row0 + b * nq + q, COL_CQ // gc + k)),
            pl.BlockSpec((LAT_L, HD), lambda b, k, q: (lat_blk + b, COL_CK // HD + k)),
            pl.BlockSpec((LAT_L, HD), lambda b, k, q: (lat_blk + b, COL_CV // HD + k)),
            pl.BlockSpec((None, None, PAST, HD), lambda b, k, q: (b, l, 0, k)),
            pl.BlockSpec((None, None, PAST, HD), lambda b, k, q: (b, l, 0, k)),
            pl.BlockSpec(memory_space=pl.ANY),
        ],
        out_specs=pl.BlockSpec((BQ_C, gc), lambda b, k, q: (row0 + b * nq + q, k)),
        out_shape=jax.ShapeDtypeStruct((T_ALL, C_HEADS * HD), BF16),
        input_output_aliases={6: 0},
        compiler_params=pltpu.CompilerParams(
            dimension_semantics=("arbitrary", "arbitrary", "arbitrary")),
        name=f"attn_lat_c_l{l}",
    )(sink_l, qkvu, qkvu, qkvu, cache_k4, cache_v4, c_out)


def _s5_param_kernel(lr_ref, li_ref, ls_ref, btr_ref, bti_ref, cbr_ref, cbi_ref, w1_ref, w2_ref, a8_ref):
    a8_ref[...] = jnp.zeros_like(a8_ref)
    taps = []
    for d in range(2):
        lr = lr_ref[d]
        li = li_ref[d]
        dt = jnp.exp(ls_ref[d])
        mag = jnp.exp(lr * dt)
        ar = mag * jnp.cos(li * dt)
        ai = mag * jnp.sin(li * dt)
        den = lr * lr + li * li
        n_re = ar - 1.0
        f_re = (n_re * lr + ai * li) / den
        f_im = (ai * lr - n_re * li) / den
        pw = [(jnp.ones_like(ar), jnp.zeros_like(ar))]
        for _ in range(S5_T):
            pr, pi = pw[-1]
            pw.append((pr * ar - pi * ai, pr * ai + pi * ar))
        br = btr_ref[d]
        bi = bti_ref[d]
        bbr = f_re * br - f_im * bi
        bbi = f_re * bi + f_im * br
        cr = cbr_ref[d]
        ci = cbi_ref[d]
        xr_blocks, xi_blocks = [], []
        for t in range(S5_T):
            rows = slice(t * 128, (t + 1) * 128)
            pr, pi = pw[S5_T - 1 - t if d == 0 else t]
            xr = pr * bbr - pi * bbi
            xi = pr * bbi + pi * bbr
            c0 = S5_ROW + 2 * S5_ST * d
            w1_ref[rows, c0:c0 + S5_ST] = xr.astype(BF16)
            w1_ref[rows, c0 + S5_ST:c0 + 2 * S5_ST] = xi.astype(BF16)
            xr_blocks.append(xr)
            xi_blocks.append(xi)
            pr, pi = pw[t + 1 if d == 0 else S5_T - t]
            c0 = 2 * S5_ST * d
            w2_ref[rows, c0:c0 + S5_ST] = (cr * pr - ci * pi).astype(BF16)
            w2_ref[rows, c0 + S5_ST:c0 + 2 * S5_ST] = (-(cr * pi + ci * pr)).astype(BF16)
        a8_ref[2 * d:2 * d + 1, :] = pw[S5_T][0]
        a8_ref[2 * d + 1:2 * d + 2, :] = pw[S5_T][1]
        xr_all = jnp.concatenate(xr_blocks, axis=0)
        xi_all = jnp.concatenate(xi_blocks, axis=0)
        hi = lax.Precision.HIGHEST
        taps.append(_dot_nt(xr_all, cr, hi) - _dot_nt(xi_all, ci, hi))
    tf, tb = taps
    for t in range(S5_T):
        for t2 in range(S5_T):
            blk = None
            if t2 >= t:
                r0 = (S5_T - 1 - (t2 - t)) * 128
                blk = tf[r0:r0 + 128, :]
            if t2 <= t:
                r0 = (t - t2) * 128
                b2 = tb[r0:r0 + 128, :]
                blk = b2 if blk is None else blk + b2
            w1_ref[t * 128:(t + 1) * 128, t2 * 128:(t2 + 1) * 128] = blk.astype(BF16)


def _s5_params(s5_lam_re, s5_lam_im, s5_log_step, s5_b_re, s5_b_im, s5_c_re, s5_c_im):
    row = lambda a: a.reshape(DEPTH, 2, S5_OCT, 1, S5_ST)
    eye = jnp.eye(S5_OG, dtype=bool)[:, None, :, None]

    def expand(a):
        a = a.reshape(DEPTH, 2, S5_OCT, S5_OG, S5_GC, 1, S5_P)
        return jnp.where(eye, a, 0.0).reshape(DEPTH, 2, S5_OCT, 128, S5_ST)

    bt = lambda a: expand(jnp.swapaxes(a, -1, -2))
    spec = lambda r: pl.BlockSpec((None, 2, None, r, S5_ST), lambda l, s: (l, 0, s, 0, 0))
    return pl.pallas_call(
        _s5_param_kernel,
        grid=(DEPTH, S5_OCT),
        in_specs=[spec(1), spec(1), spec(1), spec(128), spec(128), spec(128), spec(128)],
        out_specs=[
            pl.BlockSpec((None, None, S5_ROW, S5_W1), lambda l, s: (l, s, 0, 0)),
            pl.BlockSpec((None, None, S5_ROW, 4 * S5_ST), lambda l, s: (l, s, 0, 0)),
            pl.BlockSpec((None, None, 8, S5_ST), lambda l, s: (l, s, 0, 0)),
        ],
        out_shape=[
            jax.ShapeDtypeStruct((DEPTH, S5_OCT, S5_ROW, S5_W1), BF16),
            jax.ShapeDtypeStruct((DEPTH, S5_OCT, S5_ROW, 4 * S5_ST), BF16),
            jax.ShapeDtypeStruct((DEPTH, S5_OCT, 8, S5_ST), F32),
        ],
        compiler_params=pltpu.CompilerParams(
            dimension_semantics=("arbitrary", "arbitrary"), vmem_limit_bytes=VMEM_MID),
        name="s5_params",
    )(row(s5_lam_re), row(s5_lam_im), row(jnp.repeat(s5_log_step, S5_P, axis=-1)),
      bt(s5_b_re), bt(s5_b_im), expand(s5_c_re), expand(s5_c_im))


def _s5_mix_kernel(u_ref, w1_ref, w2_ref, a8_ref, h0_ref, y_ref, hfin_ref, lhs_ref, a_ref, hp_ref, y8_ref):
    def stream(nb, nc, is_ctx):
        for b in range(nb):
            for t in range(S5_T):
                lhs_ref[t, pl.ds(b, nc, stride=nb), :] = u_ref[pl.ds(b * nc * S5_T + t, nc, stride=S5_T), :]
        lhs = jnp.concatenate([lhs_ref[t] for t in range(S5_T)], axis=1).astype(BF16)
        a_ref[...] = _dot(lhs, w1_ref[...])

        coef = [a8_ref[k:k + 1, :] for k in range(4)]
        if is_ctx:
            init = tuple(jnp.zeros((nb, S5_ST), F32) for _ in range(4))
        else:
            init = tuple(h0_ref[k] for k in range(4))

        def step(c, st):
            new = []
            for d in range(2):
                cc = c if d == 0 else nc - 1 - c
                rows = pl.ds(cc * nb, nb) if isinstance(c, int) else pl.ds(pl.multiple_of(cc * nb, nb), nb)
                hr, hi = st[2 * d], st[2 * d + 1]
                hp_ref[rows, 2 * S5_ST * d:2 * S5_ST * d + S5_ST] = hr
                hp_ref[rows, 2 * S5_ST * d + S5_ST:2 * S5_ST * (d + 1)] = hi
                c0 = S5_ROW + 2 * S5_ST * d
                gr = a_ref[rows, c0:c0 + S5_ST]
                gi = a_ref[rows, c0 + S5_ST:c0 + 2 * S5_ST]
                ar, ai = coef[2 * d], coef[2 * d + 1]
                new += [ar * hr - ai * hi + gr, ar * hi + ai * hr + gi]
            return tuple(new)

        if nb % 8 == 0:
            fin = lax.fori_loop(0, nc, step, init, unroll=S5_UNROLL)
        else:
            fin = init
            for c in range(nc):
                fin = step(c, fin)
        if is_ctx:
            for k in range(4):
                hfin_ref[k] = fin[k]

        y8 = a_ref[:, 0:S5_ROW] + _dot_nt(hp_ref[...].astype(BF16), w2_ref[...])
        for t in range(S5_T):
            y8_ref[t] = y8[:, t * 128:(t + 1) * 128]
        for b in range(nb):
            for t in range(S5_T):
                y_ref[pl.ds(b * nc * S5_T + t, nc, stride=S5_T), :] = y8_ref[t, pl.ds(b, nc, stride=nb), :]

    pl.when(pl.program_id(1) == 0)(functools.partial(stream, CTX_B, CTX_NC, True))
    pl.when(pl.program_id(1) == 1)(functools.partial(stream, LAT_B, LAT_NC, False))


def _s5_mix(uf, w1, w2, a8, h0, l):
    return pl.pallas_call(
        _s5_mix_kernel,
        grid=(S5_OCT, 2),
        in_specs=[
            pl.BlockSpec((T_CTX, 128), lambda s, k: (k, s)),
            pl.BlockSpec((None, None, S5_ROW, S5_W1), lambda s, k: (l, s, 0, 0)),
            pl.BlockSpec((None, None, S5_ROW, 4 * S5_ST), lambda s, k: (l, s, 0, 0)),
            pl.BlockSpec((None, None, 8, S5_ST), lambda s, k: (l, s, 0, 0)),
            pl.BlockSpec((None, 4, LAT_B, S5_ST), lambda s, k: (s, 0, 0, 0)),
        ],
        out_specs=[
            pl.BlockSpec((T_CTX, 128), lambda s, k: (k, s)),
            pl.BlockSpec((None, 4, CTX_B, S5_ST), lambda s, k: (s, 0, 0, 0)),
        ],
        out_shape=[
            jax.ShapeDtypeStruct((T_ALL, S5_CH), F32),
            jax.ShapeDtypeStruct((S5_OCT, 4, CTX_B, S5_ST), F32),
        ],
        scratch_shapes=[
            pltpu.VMEM((S5_T, S5_ROWS, 128), F32),
            pltpu.VMEM((S5_ROWS, S5_W1), F32),
            pltpu.VMEM((S5_ROWS, 4 * S5_ST), F32),
            pltpu.VMEM((S5_T, S5_ROWS, 128), F32),
        ],
        compiler_params=pltpu.CompilerParams(
            dimension_semantics=("arbitrary", "arbitrary"), vmem_limit_bytes=VMEM_BIG),
        name=f"s5_mix_l{l}",
    )(uf, w1, w2, a8, h0)


def _glu_kernel(y_ref, u_ref, d_ref, w_ref, b_ref, o_ref):
    y = y_ref[...] + d_ref[...] * u_ref[...]
    z = y * (0.5 * (1.0 + jnp.tanh(math.sqrt(2.0 / math.pi) * (y + 0.044715 * (y * y * y)))))
    t = _dot(z.astype(BF16), w_ref[...].astype(BF16)) + b_ref[...]
    o_ref[...] = (z * jax.nn.sigmoid(t)).astype(BF16)


def _glu(y, uf, s5_d3, w_glu, b_glu3, l):
    return pl.pallas_call(
        _glu_kernel,
        grid=(T_ALL // BM,),
        in_specs=[
            pl.BlockSpec((BM, S5_CH), lambda i: (i, 0)),
            pl.BlockSpec((BM, S5_CH), lambda i: (i, 0)),
            pl.BlockSpec((None, 1, S5_CH), lambda i: (l, 0, 0)),
            pl.BlockSpec((None, S5_CH, S5_CH), lambda i: (l, 0, 0)),
            pl.BlockSpec((None, 1, S5_CH), lambda i: (l, 0, 0)),
        ],
        out_specs=pl.BlockSpec((BM, S5_CH), lambda i: (i, 0)),
        out_shape=jax.ShapeDtypeStruct((T_ALL, S5_CH), BF16),
        compiler_params=pltpu.CompilerParams(dimension_semantics=("arbitrary",)),
        name=f"s5_glu_l{l}",
    )(y, uf, s5_d3, w_glu, b_glu3)


def _outproj_kernel(x_ref, a_ref, c_ref, s_ref, mod_ref, npost_ref, w_ref, o_ref, wb_ref):
    @pl.when(pl.program_id(0) == 0)
    def _():
        nk = 4
        rk = D_MODEL // nk
        for r in range(nk):
            wb_ref[r * rk:(r + 1) * rk, :] = w_ref[r * rk:(r + 1) * rk, :].astype(BF16)

    na = A_HEADS * HD
    nc = na + C_HEADS * HD
    y = (_dot(a_ref[...], wb_ref[0:na, :]) + _dot(c_ref[...], wb_ref[na:nc, :])
         + _dot(s_ref[...], wb_ref[nc:, :]))
    gate = mod_ref[:, 5 * D_MODEL:6 * D_MODEL]
    o_ref[...] = x_ref[...] + gate * _rms(y, npost_ref[...])


def _outproj(x, a_out, c_out, s_out, mods4, norm_post4, w_out, l):
    bm = BM_OUT
    return pl.pallas_call(
        _outproj_kernel,
        grid=(T_ALL // bm,),
        in_specs=[
            pl.BlockSpec((bm, D_MODEL), lambda i: (i, 0)),
            pl.BlockSpec((bm, A_HEADS * HD), lambda i: (i, 0)),
            pl.BlockSpec((bm, C_HEADS * HD), lambda i: (i, 0)),
            pl.BlockSpec((bm, S5_CH), lambda i: (i, 0)),
            pl.BlockSpec((None, None, 1, N_MOD * D_MODEL), lambda i: (l, _mod_index(i, bm), 0, 0)),
            pl.BlockSpec((None, None, 1, D_MODEL), lambda i: (l, 1, 0, 0)),
            pl.BlockSpec((None, D_MODEL, D_MODEL), lambda i: (l, 0, 0), pipeline_mode=pl.Buffered(1)),
        ],
        out_specs=pl.BlockSpec((bm, D_MODEL), lambda i: (i, 0)),
        out_shape=jax.ShapeDtypeStruct((T_ALL, D_MODEL), F32),
        scratch_shapes=[pltpu.VMEM((D_MODEL, D_MODEL), BF16)],
        compiler_params=pltpu.CompilerParams(
            dimension_semantics=("arbitrary",), vmem_limit_bytes=VMEM_BIG),
        name=f"outproj_l{l}",
    )(x, a_out, c_out, s_out, mods4, norm_post4, w_out)


def _rope_tables():
    rows = LAT_L // GRID_W
    row = jnp.repeat(jnp.arange(rows, dtype=F32), GRID_W)
    col = jnp.tile(jnp.arange(GRID_W, dtype=F32), rows)
    axis_dim = HD // 2
    inv_freq = ROPE_BASE ** (-jnp.arange(0, axis_dim, 2, dtype=F32) / axis_dim)
    ang_row = row[:, None] * inv_freq
    ang_col = col[:, None] * inv_freq
    cr, sr = jnp.cos(ang_row), jnp.sin(ang_row)
    cc, sc = jnp.cos(ang_col), jnp.sin(ang_col)
    cos_t = jnp.concatenate([cr, cr, cc, cc], axis=-1)
    sin_t = jnp.concatenate([-sr, sr, -sc, sc], axis=-1)
    return cos_t, sin_t


def kernel(x_prompt, x_sample, cache_a_k, cache_a_v, cache_c_k, cache_c_v, state_ssm_re, state_ssm_im,
           c, c_ctx, w_mod, b_mod, norm_pre, norm_post, ffn_gate, ffn_up, ffn_down, w_in, w_out,
           q_norm, k_norm, sink, s5_lam_re, s5_lam_im, s5_log_step, s5_b_re, s5_b_im, s5_c_re, s5_c_im,
           s5_d, w_glu, b_glu):
    cvec8 = jnp.concatenate([c_ctx[None, :], c, jnp.zeros((8 - 1 - LAT_B, D_MODEL), F32)], axis=0)
    mods4 = _modulation(cvec8, w_mod, b_mod).reshape(DEPTH, 8, 1, N_MOD * D_MODEL)
    norm_pre4 = norm_pre.reshape(DEPTH, 3, 1, D_MODEL)
    norm_post4 = norm_post.reshape(DEPTH, 3, 1, D_MODEL)
    q_norm3 = q_norm.reshape(DEPTH, 1, HD)
    k_norm3 = k_norm.reshape(DEPTH, 1, HD)
    s5_d3 = s5_d.reshape(DEPTH, 1, S5_CH)
    b_glu3 = b_glu.reshape(DEPTH, 1, S5_CH)
    cos_t, sin_t = _rope_tables()
    w1, w2, a8 = _s5_params(s5_lam_re, s5_lam_im, s5_log_step, s5_b_re, s5_b_im, s5_c_re, s5_c_im)
    kv4 = lambda a: a.reshape(LAT_B, DEPTH, PAST, A_KV * HD)
    cak, cav, cck, ccv = kv4(cache_a_k), kv4(cache_a_v), kv4(cache_c_k), kv4(cache_c_v)
    h0_all = jnp.stack([state_ssm_re[:, :, 0], state_ssm_im[:, :, 0],
                        state_ssm_re[:, :, 1], state_ssm_im[:, :, 1]], axis=0)
    h0_all = h0_all.reshape(4, LAT_B, DEPTH, S5_OCT, S5_ST).transpose(2, 3, 0, 1, 4)

    w_in_bf = w_in.astype(BF16)
    ffn_w = (mods4, norm_pre4, norm_post4, ffn_gate, ffn_up, ffn_down)
    nct = T_CTX // BM
    nlt = T_LAT // BM
    new_caches = ()
    new_state = []
    for l in range(DEPTH):
        if l == 0:
            x = _ffn(x_prompt.reshape(T_CTX, D_MODEL), *ffn_w, l, 0, n_tiles=nct, tag="_ctx")
            x = _ffn(x_sample.reshape(T_LAT, D_MODEL), *ffn_w, l, 0, slab_tile0=nct, n_tiles=nlt,
                     out_tile0=nct, prev=x, tag="_lat")
        else:
            x = _ffn(x, *ffn_w, l, 0)

        qkvu, uf, *new_caches = _inproj(x, mods4, norm_pre4, q_norm3, k_norm3, cos_t, sin_t, w_in_bf, l,
                                        new_caches)
        a_out, c_out = _attn_ctx(qkvu, sink[l], l)
        a_out = _attn_lat_a(qkvu, cak, cav, a_out, l)
        c_out = _attn_lat_c(qkvu, sink[l], cck, ccv, c_out, l)
        y, hfin = _s5_mix(uf, w1, w2, a8, h0_all[l], l)
        s_out = _glu(y, uf, s5_d3, w_glu, b_glu3, l)

        x = _outproj(x, a_out, c_out, s_out, mods4, norm_post4, w_out, l)
        if l < DEPTH - 1:
            x = _ffn(x, *ffn_w, l, 1)
        else:
            y_prompt = _ffn(x, *ffn_w, l, 1, n_tiles=nct, out_rows=T_CTX, tag="_ctx")
            y_sample = _ffn(x, *ffn_w, l, 1, slab_tile0=nct, n_tiles=nlt, in_tile0=nct, out_rows=T_LAT,
                            tag="_lat")

        hf = hfin.reshape(S5_OCT, 4, CTX_B, S5_OG, S5_P).transpose(1, 2, 0, 3, 4).reshape(4, CTX_B, S5_G, S5_P)
        new_state.append((jnp.stack([hf[0], hf[2]], axis=1), jnp.stack([hf[1], hf[3]], axis=1)))

    y_prompt = y_prompt.reshape(CTX_B, CTX_L, D_MODEL)
    y_sample = y_sample.reshape(LAT_B, LAT_L, D_MODEL)
    caches = [a.reshape(CTX_B, DEPTH, CTX_L, A_KV, HD) for a in new_caches]
    st_re = jnp.stack([new_state[l][0] for l in range(DEPTH)], axis=1)
    st_im = jnp.stack([new_state[l][1] for l in range(DEPTH)], axis=1)
    return (y_prompt, y_sample, caches[0], caches[1], caches[2], caches[3], st_re, st_im)
```

```python
import functools
import math

import jax
import jax.numpy as jnp
from jax import lax
from jax.experimental import pallas as pl
from jax.experimental.pallas import tpu as pltpu

F32 = jnp.float32
BF16 = jnp.bfloat16

D_MODEL = 2048
CTX_B, CTX_L = 16, 256
LAT_B, LAT_L = 2, 2048
DEPTH = 2
PAST = 512
GRID_W = 64
HD = 128
A_HEADS, A_KV = 8, 2
C_HEADS, C_KV = 4, 2
WINDOW = 128
S5_GC = 16
S5_CH = 512
S5_G = 32
S5_P = 64
D_FF = 5632
N_MOD = 9
IN_WIDTH = 3072
ROPE_BASE = 10000.0
EPS = 1e-6
HALF_STEP = 0.5
NEG_INF = -1e30
SCALE = HD ** -0.5
LOG2E = math.log2(math.e)

T_CTX = CTX_B * CTX_L
T_LAT = LAT_B * LAT_L
T_ALL = T_CTX + T_LAT

COL_AQ, COL_AK, COL_AV = 0, 1024, 1280
COL_CQ, COL_CK, COL_CV = 1536, 2048, 2304
COL_U = 2560

S5_T = 8
S5_OCT = S5_CH // 128
S5_OG = S5_G // S5_OCT
S5_ROW = S5_T * 128
S5_ST = S5_OG * S5_P
S5_W1 = S5_ROW + 4 * S5_ST
CTX_NC = CTX_L // S5_T
LAT_NC = LAT_L // S5_T
S5_ROWS = CTX_NC * CTX_B
assert S5_ROWS == LAT_NC * LAT_B and T_CTX == T_LAT

V7X_VMEM_BYTES = 64 * 1024 * 1024
VMEM_BIG = 56 * 1024 * 1024
VMEM_MID = 40 * 1024 * 1024

BM = 1024
BM_OUT = 512
BF = 256
RC = 512
BN_MOD = 1024
BM_IN = 512
BQ_A = 512
BQ_C = 256
S5_UNROLL = 4


def _dot(a, b):
    return jnp.dot(a, b, preferred_element_type=F32)


def _dot_nt(a, b, precision=None):
    return lax.dot_general(a, b, (((1,), (1,)), ((), ())), preferred_element_type=F32, precision=precision)


def _rms(x, g):
    return x * lax.rsqrt(jnp.mean(x * x, axis=-1, keepdims=True) + EPS) * g


def _mod_index(i, bm):
    nct = T_CTX // bm
    return jnp.where(i < nct, 0, 1 + (i - nct) // (LAT_L // bm))


def _mod_kernel(c_ref, w_ref, b_ref, o_ref):
    c = c_ref[...]
    s = (c * jax.nn.sigmoid(c)).astype(BF16)
    o_ref[...] = _dot(s, w_ref[...].astype(BF16)) + b_ref[...]


def _modulation(cvec8, w_mod, b_mod):
    n = N_MOD * D_MODEL
    return pl.pallas_call(
        _mod_kernel,
        grid=(DEPTH, n // BN_MOD),
        in_specs=[
            pl.BlockSpec((8, D_MODEL), lambda l, j: (0, 0)),
            pl.BlockSpec((None, D_MODEL, BN_MOD), lambda l, j: (l, 0, j)),
            pl.BlockSpec((None, 1, BN_MOD), lambda l, j: (l, 0, j)),
        ],
        out_specs=pl.BlockSpec((None, 8, BN_MOD), lambda l, j: (l, 0, j)),
        out_shape=jax.ShapeDtypeStruct((DEPTH, 8, n), F32),
        compiler_params=pltpu.CompilerParams(
            dimension_semantics=("arbitrary", "arbitrary"), vmem_limit_bytes=VMEM_MID),
        name="modulation",
    )(cvec8, w_mod, b_mod.reshape(DEPTH, 1, n))


def _ffn_kernel(x_ref, mod_ref, npre_ref, npost_ref, wg_ref, wu_ref, wd_ref, o_ref, h_ref, *, mo):
    j = pl.program_id(1)
    last = pl.num_programs(1) - 1

    def step(first, final):
        wg = wg_ref[...].astype(BF16)
        wu = wu_ref[...].astype(BF16)
        wd = wd_ref[...].astype(BF16)
        for r in range(BM // RC):
            rows = slice(r * RC, (r + 1) * RC)
            if first:
                shift = mod_ref[:, mo * D_MODEL:(mo + 1) * D_MODEL]
                scale = mod_ref[:, (mo + 1) * D_MODEL:(mo + 2) * D_MODEL]
                hn = _rms(x_ref[rows, :], npre_ref[...])
                h = (hn * (1.0 + scale) + shift).astype(BF16)
                h_ref[rows, :] = h
            else:
                h = h_ref[rows, :]
            g = _dot(h, wg)
            u = _dot(h, wu)
            a = (g * jax.nn.sigmoid(g) * u).astype(BF16)
            acc = _dot(a, wd)
            if not first:
                acc = o_ref[rows, :] + acc
            if final:
                gate = mod_ref[:, (mo + 2) * D_MODEL:(mo + 3) * D_MODEL]
                acc = x_ref[rows, :] + (HALF_STEP * gate) * _rms(acc, npost_ref[...])
            o_ref[rows, :] = acc

    pl.when(j == 0)(functools.partial(step, True, False))
    pl.when(jnp.logical_and(j > 0, j < last))(functools.partial(step, False, False))
    pl.when(j == last)(functools.partial(step, False, True))


def _ffn(x, mods4, norm_pre4, norm_post4, ffn_gate, ffn_up, ffn_down, l, s, *,
         slab_tile0=0, n_tiles=T_ALL // BM, in_tile0=0, out_tile0=0, out_rows=T_ALL, prev=None, tag=""):
    mo = 6 * s
    ni = 2 * s
    in_specs = [
        pl.BlockSpec((BM, D_MODEL), lambda i, j: (in_tile0 + i, 0)),
        pl.BlockSpec((None, None, 1, N_MOD * D_MODEL), lambda i, j: (l, _mod_index(slab_tile0 + i, BM), 0, 0)),
        pl.BlockSpec((None, None, 1, D_MODEL), lambda i, j: (l, ni, 0, 0)),
        pl.BlockSpec((None, None, 1, D_MODEL), lambda i, j: (l, ni, 0, 0)),
        pl.BlockSpec((None, None, D_MODEL, BF), lambda i, j: (l, s, 0, j)),
        pl.BlockSpec((None, None, D_MODEL, BF), lambda i, j: (l, s, 0, j)),
        pl.BlockSpec((None, None, BF, D_MODEL), lambda i, j: (l, s, j, 0)),
    ]
    args = [x, mods4, norm_pre4, norm_post4, ffn_gate, ffn_up, ffn_down]
    kern = functools.partial(_ffn_kernel, mo=mo)
    aliases = {}
    if prev is not None:
        in_specs.append(pl.BlockSpec(memory_space=pl.ANY))
        args.append(prev)
        aliases = {len(args) - 1: 0}
        kern = functools.partial(_ffn_kernel_keep, mo=mo)
    return pl.pallas_call(
        kern,
        grid=(n_tiles, D_FF // BF),
        in_specs=in_specs,
        out_specs=pl.BlockSpec((BM, D_MODEL), lambda i, j: (out_tile0 + i, 0)),
        out_shape=jax.ShapeDtypeStruct((out_rows, D_MODEL), F32),
        input_output_aliases=aliases,
        scratch_shapes=[pltpu.VMEM((BM, D_MODEL), BF16)],
        compiler_params=pltpu.CompilerParams(
            dimension_semantics=("arbitrary", "arbitrary"), vmem_limit_bytes=VMEM_BIG),
        name=f"ffn_l{l}_s{s}{tag}",
    )(*args)


def _ffn_kernel_keep(x_ref, mod_ref, npre_ref, npost_ref, wg_ref, wu_ref, wd_ref, prev_ref, o_ref, h_ref, *, mo):
    del prev_ref
    _ffn_kernel(x_ref, mod_ref, npre_ref, npost_ref, wg_ref, wu_ref, wd_ref, o_ref, h_ref, mo=mo)


def _rope(y, cos, sins):
    lane = lax.broadcasted_iota(jnp.int32, y.shape, 1)
    first = (lane & 63) < 32
    partner = jnp.where(first, pltpu.roll(y, 96, 1), pltpu.roll(y, 32, 1))
    return y * cos + partner * sins


_IN_SEGMENTS = (
    (COL_AQ, A_HEADS, "q", True, None),
    (COL_AK, A_KV, "k", True, 0),
    (COL_AV, A_KV, None, False, 1),
    (COL_CQ, C_HEADS, None, True, None),
    (COL_CK, C_KV, None, True, 2),
    (COL_CV, C_KV, None, False, 3),
)


def _inproj_kernel(x_ref, mod_ref, npre_ref, qn_ref, kn_ref, cos_ref, sin_ref, w_ref, *rest):
    qkvu_ref, uf_ref = rest[-6], rest[-5]
    cache_refs = rest[-4:]
    nb = BM_IN // CTX_L

    def body(lat):
        shift = mod_ref[:, 3 * D_MODEL:4 * D_MODEL]
        scale = mod_ref[:, 4 * D_MODEL:5 * D_MODEL]
        h = (_rms(x_ref[...], npre_ref[...]) * (1.0 + scale) + shift).astype(BF16)
        for col0, heads, norm, rot, cache in _IN_SEGMENTS:
            p = _dot(h, w_ref[:, col0:col0 + heads * HD])
            for k in range(heads):
                y = p[:, k * HD:(k + 1) * HD]
                if norm == "q":
                    y = _rms(y, qn_ref[...])
                elif norm == "k":
                    y = _rms(y, kn_ref[...])
                if rot and lat:
                    y = _rope(y, cos_ref[...], sin_ref[...])
                qkvu_ref[:, col0 + k * HD:col0 + (k + 1) * HD] = y.astype(BF16)
                if cache is not None and not lat:
                    cache_refs[cache][:, :, k, :] = y.reshape(nb, CTX_L, HD)
        u = _dot(h, w_ref[:, COL_U:])
        uf_ref[...] = u
        qkvu_ref[:, COL_U:] = u.astype(BF16)

    is_lat = pl.program_id(0) >= T_CTX // BM_IN
    pl.when(is_lat)(functools.partial(body, True))
    pl.when(jnp.logical_not(is_lat))(functools.partial(body, False))


def _inproj(x, mods4, norm_pre4, q_norm3, k_norm3, cos_t, sin_t, w_in_bf, l, prev_caches):
    bm = BM_IN
    nct = T_CTX // bm
    nb = bm // CTX_L
    tab_idx = lambda i: (jnp.maximum(i - nct, 0) % (LAT_L // bm), 0)
    cache_shape = jax.ShapeDtypeStruct((CTX_B, DEPTH, CTX_L, A_KV, HD), F32)
    cache_spec = pl.BlockSpec((nb, None, CTX_L, A_KV, HD), lambda i: (jnp.minimum(i, nct - 1), l, 0, 0, 0))
    n_in = 8
    return pl.pallas_call(
        _inproj_kernel,
        grid=(T_ALL // bm,),
        in_specs=[
            pl.BlockSpec((bm, D_MODEL), lambda i: (i, 0)),
            pl.BlockSpec((None, None, 1, N_MOD * D_MODEL), lambda i: (l, _mod_index(i, bm), 0, 0)),
            pl.BlockSpec((None, None, 1, D_MODEL), lambda i: (l, 1, 0, 0)),
            pl.BlockSpec((None, 1, HD), lambda i: (l, 0, 0)),
            pl.BlockSpec((None, 1, HD), lambda i: (l, 0, 0)),
            pl.BlockSpec((bm, HD), tab_idx),
            pl.BlockSpec((bm, HD), tab_idx),
            pl.BlockSpec((None, D_MODEL, IN_WIDTH), lambda i: (l, 0, 0), pipeline_mode=pl.Buffered(1)),
        ] + [pl.BlockSpec(memory_space=pl.ANY)] * len(prev_caches),
        out_specs=[
            pl.BlockSpec((bm, IN_WIDTH), lambda i: (i, 0)),
            pl.BlockSpec((bm, S5_CH), lambda i: (i, 0)),
        ] + [cache_spec] * 4,
        out_shape=[
            jax.ShapeDtypeStruct((T_ALL, IN_WIDTH), BF16),
            jax.ShapeDtypeStruct((T_ALL, S5_CH), F32),
        ] + [cache_shape] * 4,
        input_output_aliases={n_in + k: 2 + k for k in range(len(prev_caches))},
        compiler_params=pltpu.CompilerParams(
            dimension_semantics=("arbitrary",), vmem_limit_bytes=VMEM_BIG),
        name=f"inproj_l{l}",
    )(x, mods4, norm_pre4, q_norm3, k_norm3, cos_t, sin_t, w_in_bf, *prev_caches)


def _softmax_pv(dots, values, sink=None):
    m = functools.reduce(jnp.maximum, [jnp.max(d, axis=-1, keepdims=True) for d in dots]) * SCALE
    if sink is not None:
        m = jnp.maximum(m, sink)
    m2 = m * LOG2E
    ps = [jnp.exp2(d * (SCALE * LOG2E) - m2) for d in dots]
    den = functools.reduce(jnp.add, [jnp.sum(p, axis=-1, keepdims=True) for p in ps])
    if sink is not None:
        den = den + jnp.exp2(sink * LOG2E - m2)
    o = functools.reduce(jnp.add, [_dot(p.astype(BF16), v) for p, v in zip(ps, values)])
    return o / den


def _attn_ctx_kernel(sink_ref, aq_ref, ak_ref, av_ref, cq_ref, ck_ref, cv_ref, ao_ref, co_ref):
    kv = pl.program_id(1)
    k = ak_ref[...]
    v = av_ref[...]
    for h in range(A_HEADS // A_KV):
        cols = slice(h * HD, (h + 1) * HD)
        s = _dot_nt(aq_ref[:, cols], k)
        ao_ref[:, cols] = _softmax_pv([s], [v]).astype(BF16)
    k = ck_ref[...]
    v = cv_ref[...]
    g = C_HEADS // C_KV
    for h in range(g):
        cols = slice(h * HD, (h + 1) * HD)
        s = _dot_nt(cq_ref[:, cols], k)
        co_ref[:, cols] = _softmax_pv([s], [v], sink_ref[kv * g + h]).astype(BF16)


def _attn_ctx(qkvu, sink_l, l):
    ga = A_HEADS // A_KV * HD
    gc = C_HEADS // C_KV * HD
    blk = lambda width, col0: pl.BlockSpec((CTX_L, width), lambda b, k: (b, col0 // width + k))
    return pl.pallas_call(
        _attn_ctx_kernel,
        grid=(CTX_B, A_KV),
        in_specs=[
            pl.BlockSpec(memory_space=pltpu.SMEM),
            blk(ga, COL_AQ), blk(HD, COL_AK), blk(HD, COL_AV),
            blk(gc, COL_CQ), blk(HD, COL_CK), blk(HD, COL_CV),
        ],
        out_specs=[
            pl.BlockSpec((CTX_L, ga), lambda b, k: (b, k)),
            pl.BlockSpec((CTX_L, gc), lambda b, k: (b, k)),
        ],
        out_shape=[
            jax.ShapeDtypeStruct((T_ALL, A_HEADS * HD), BF16),
            jax.ShapeDtypeStruct((T_ALL, C_HEADS * HD), BF16),
        ],
        compiler_params=pltpu.CompilerParams(dimension_semantics=("arbitrary", "arbitrary")),
        name=f"attn_ctx_l{l}",
    )(sink_l, qkvu, qkvu, qkvu, qkvu, qkvu, qkvu)


def _attn_lat_a_kernel(q_ref, k_ref, v_ref, kc_ref, vc_ref, prev_ref, o_ref):
    del prev_ref
    k = k_ref[...]
    v = v_ref[...]
    kc = kc_ref[...].astype(BF16)
    vc = vc_ref[...].astype(BF16)
    for h in range(A_HEADS // A_KV):
        cols = slice(h * HD, (h + 1) * HD)
        q = q_ref[:, cols]
        s1 = _dot_nt(q, k)
        s2 = _dot_nt(q, kc)
        o_ref[:, cols] = _softmax_pv([s1, s2], [v, vc]).astype(BF16)


def _attn_lat_a(qkvu, cache_k4, cache_v4, a_out, l):
    ga = A_HEADS // A_KV * HD
    nq = LAT_L // BQ_A
    row0 = T_CTX // BQ_A
    lat_blk = T_CTX // LAT_L
    return pl.pallas_call(
        _attn_lat_a_kernel,
        grid=(LAT_B, A_KV, nq),
        in_specs=[
            pl.BlockSpec((BQ_A, ga), lambda b, k, q: (row0 + b * nq + q, COL_AQ // ga + k)),
            pl.BlockSpec((LAT_L, HD), lambda b, k, q: (lat_blk + b, COL_AK // HD + k)),
            pl.BlockSpec((LAT_L, HD), lambda b, k, q: (lat_blk + b, COL_AV // HD + k)),
            pl.BlockSpec((None, None, PAST, HD), lambda b, k, q: (b, l, 0, k)),
            pl.BlockSpec((None, None, PAST, HD), lambda b, k, q: (b, l, 0, k)),
            pl.BlockSpec(memory_space=pl.ANY),
        ],
        out_specs=pl.BlockSpec((BQ_A, ga), lambda b, k, q: (row0 + b * nq + q, k)),
        out_shape=jax.ShapeDtypeStruct((T_ALL, A_HEADS * HD), BF16),
        input_output_aliases={5: 0},
        compiler_params=pltpu.CompilerParams(
            dimension_semantics=("arbitrary", "arbitrary", "arbitrary"), vmem_limit_bytes=VMEM_BIG),
        name=f"attn_lat_a_l{l}",
    )(qkvu, qkvu, qkvu, cache_k4, cache_v4, a_out)


def _attn_lat_c_kernel(sink_ref, q_ref, k_ref, v_ref, kc_ref, vc_ref, prev_ref, o_ref):
    del prev_ref
    kv = pl.program_id(1)
    n = pl.program_id(2)
    span = BQ_C + 2 * WINDOW
    start = pl.multiple_of(jnp.clip(n * BQ_C - WINDOW, 0, LAT_L - span), WINDOW)
    kw = k_ref[pl.ds(start, span), :]
    vw = v_ref[pl.ds(start, span), :]
    kc = kc_ref[...].astype(BF16)
    vc = vc_ref[...].astype(BF16)
    qpos = n * BQ_C + lax.broadcasted_iota(jnp.int32, (BQ_C, span), 0)
    kpos = start + lax.broadcasted_iota(jnp.int32, (BQ_C, span), 1)
    valid = jnp.abs(qpos - kpos) <= WINDOW
    g = C_HEADS // C_KV
    for h in range(g):
        cols = slice(h * HD, (h + 1) * HD)
        q = q_ref[:, cols]
        s1 = jnp.where(valid, _dot_nt(q, kw), NEG_INF)
        s2 = _dot_nt(q, kc)
        o_ref[:, cols] = _softmax_pv([s1, s2], [vw, vc], sink_ref[kv * g + h]).astype(BF16)


def _attn_lat_c(qkvu, sink_l, cache_k4, cache_v4, c_out, l):
    gc = C_HEADS // C_KV * HD
    nq = LAT_L // BQ_C
    row0 = T_CTX // BQ_C
    lat_blk = T_CTX // LAT_L
    return pl.pallas_call(
        _attn_lat_c_kernel,
        grid=(LAT_B, C_KV, nq),
        in_specs=[
            pl.BlockSpec(memory_space=pltpu.SMEM),
            pl.BlockSpec((BQ_C, gc), lambda b, k, q: (row0 + b * nq + q, COL_CQ // gc + k)),
            pl.BlockSpec((LAT_L, HD), lambda b, k, q: (lat_blk + b, COL_CK // HD + k)),
            pl.BlockSpec((LAT_L, HD), lambda b, k, q: (lat_blk + b, COL_CV // HD + k)),
            pl.BlockSpec((None, None, PAST, HD), lambda b, k, q: (b, l, 0, k)),
            pl.BlockSpec((None, None, PAST, HD), lambda b, k, q: (b, l, 0, k)),
            pl.BlockSpec(memory_space=pl.ANY),
        ],
        out_specs=pl.BlockSpec((BQ_C, gc), lambda b, k, q: (row0 + b * nq + q, k)),
        out_shape=jax.ShapeDtypeStruct((T_ALL, C_HEADS * HD), BF16),
        input_output_aliases={6: 0},
        compiler_params=pltpu.CompilerParams(
            dimension_semantics=("arbitrary", "arbitrary", "arbitrary")),
        name=f"attn_lat_c_l{l}",
    )(sink_l, qkvu, qkvu, qkvu, cache_k4, cache_v4, c_out)


def _s5_param_kernel(rows_ref, bc_ref, w1_ref, w2_ref, a8_ref):
    a8_ref[...] = jnp.zeros_like(a8_ref)
    taps = []
    for d in range(2):
        lr = rows_ref[0, d]
        li = rows_ref[1, d]
        dt = jnp.exp(rows_ref[2, d])
        mag = jnp.exp(lr * dt)
        ar = mag * jnp.cos(li * dt)
        ai = mag * jnp.sin(li * dt)
        den = lr * lr + li * li
        n_re = ar - 1.0
        f_re = (n_re * lr + ai * li) / den
        f_im = (ai * lr - n_re * li) / den
        pw = [(jnp.ones_like(ar), jnp.zeros_like(ar))]
        for _ in range(S5_T):
            pr, pi = pw[-1]
            pw.append((pr * ar - pi * ai, pr * ai + pi * ar))
        br = bc_ref[0, d]
        bi = bc_ref[1, d]
        bbr = f_re * br - f_im * bi
        bbi = f_re * bi + f_im * br
        cr = bc_ref[2, d]
        ci = bc_ref[3, d]
        xr_blocks, xi_blocks = [], []
        for t in range(S5_T):
            rows = slice(t * 128, (t + 1) * 128)
            pr, pi = pw[S5_T - 1 - t if d == 0 else t]
            xr = pr * bbr - pi * bbi
            xi = pr * bbi + pi * bbr
            c0 = S5_ROW + 2 * S5_ST * d
            w1_ref[rows, c0:c0 + S5_ST] = xr.astype(BF16)
            w1_ref[rows, c0 + S5_ST:c0 + 2 * S5_ST] = xi.astype(BF16)
            xr_blocks.append(xr)
            xi_blocks.append(xi)
            pr, pi = pw[t + 1 if d == 0 else S5_T - t]
            c0 = 2 * S5_ST * d
            w2_ref[rows, c0:c0 + S5_ST] = (cr * pr - ci * pi).astype(BF16)
            w2_ref[rows, c0 + S5_ST:c0 + 2 * S5_ST] = (-(cr * pi + ci * pr)).astype(BF16)
        a8_ref[2 * d:2 * d + 1, :] = pw[S5_T][0]
        a8_ref[2 * d + 1:2 * d + 2, :] = pw[S5_T][1]
        xr_all = jnp.concatenate(xr_blocks, axis=0)
        xi_all = jnp.concatenate(xi_blocks, axis=0)
        hi = lax.Precision.HIGHEST
        taps.append(_dot_nt(xr_all, cr, hi) - _dot_nt(xi_all, ci, hi))
    tf, tb = taps
    for t in range(S5_T):
        for t2 in range(S5_T):
            blk = None
            if t2 >= t:
                r0 = (S5_T - 1 - (t2 - t)) * 128
                blk = tf[r0:r0 + 128, :]
            if t2 <= t:
                r0 = (t - t2) * 128
                b2 = tb[r0:r0 + 128, :]
                blk = b2 if blk is None else blk + b2
            w1_ref[t * 128:(t + 1) * 128, t2 * 128:(t2 + 1) * 128] = blk.astype(BF16)


def _s5_params(s5_lam_re, s5_lam_im, s5_log_step, s5_b_re, s5_b_im, s5_c_re, s5_c_im):
    rows = jnp.stack([s5_lam_re, s5_lam_im, jnp.repeat(s5_log_step[..., None], S5_P, axis=-1)], axis=0)
    rows = rows.reshape(3, DEPTH, 2, S5_OCT, 1, S5_ST)
    bc = jnp.stack([jnp.swapaxes(s5_b_re, -1, -2), jnp.swapaxes(s5_b_im, -1, -2), s5_c_re, s5_c_im], axis=0)
    eye = jnp.eye(S5_OG, dtype=bool)[:, None, :, None]
    bc = jnp.where(eye, bc.reshape(4, DEPTH, 2, S5_OCT, S5_OG, S5_GC, 1, S5_P), 0.0)
    bc = bc.reshape(4, DEPTH, 2, S5_OCT, 128, S5_ST)
    spec = lambda n, r: pl.BlockSpec((n, None, 2, None, r, S5_ST), lambda l, s: (0, l, 0, s, 0, 0))
    return pl.pallas_call(
        _s5_param_kernel,
        grid=(DEPTH, S5_OCT),
        in_specs=[spec(3, 1), spec(4, 128)],
        out_specs=[
            pl.BlockSpec((None, None, S5_ROW, S5_W1), lambda l, s: (l, s, 0, 0)),
            pl.BlockSpec((None, None, S5_ROW, 4 * S5_ST), lambda l, s: (l, s, 0, 0)),
            pl.BlockSpec((None, None, 8, S5_ST), lambda l, s: (l, s, 0, 0)),
        ],
        out_shape=[
            jax.ShapeDtypeStruct((DEPTH, S5_OCT, S5_ROW, S5_W1), BF16),
            jax.ShapeDtypeStruct((DEPTH, S5_OCT, S5_ROW, 4 * S5_ST), BF16),
            jax.ShapeDtypeStruct((DEPTH, S5_OCT, 8, S5_ST), F32),
        ],
        compiler_params=pltpu.CompilerParams(
            dimension_semantics=("arbitrary", "arbitrary"), vmem_limit_bytes=VMEM_MID),
        name="s5_params",
    )(rows, bc)


def _s5_mix_kernel(u_ref, w1_ref, w2_ref, a8_ref, h0_ref, y_ref, hfin_ref, lhs_ref, a_ref, hp_ref, y8_ref):
    def stream(nb, nc, is_ctx):
        for b in range(nb):
            for t in range(S5_T):
                lhs_ref[t, pl.ds(b, nc, stride=nb), :] = u_ref[pl.ds(b * nc * S5_T + t, nc, stride=S5_T), :]
        lhs = jnp.concatenate([lhs_ref[t] for t in range(S5_T)], axis=1).astype(BF16)
        a_ref[...] = _dot(lhs, w1_ref[...])

        coef = [a8_ref[k:k + 1, :] for k in range(4)]
        if is_ctx:
            init = tuple(jnp.zeros((nb, S5_ST), F32) for _ in range(4))
        else:
            init = tuple(h0_ref[k] for k in range(4))

        def step(c, st):
            new = []
            for d in range(2):
                cc = c if d == 0 else nc - 1 - c
                rows = pl.ds(cc * nb, nb) if isinstance(c, int) else pl.ds(pl.multiple_of(cc * nb, nb), nb)
                hr, hi = st[2 * d], st[2 * d + 1]
                hp_ref[rows, 2 * S5_ST * d:2 * S5_ST * d + S5_ST] = hr
                hp_ref[rows, 2 * S5_ST * d + S5_ST:2 * S5_ST * (d + 1)] = hi
                c0 = S5_ROW + 2 * S5_ST * d
                gr = a_ref[rows, c0:c0 + S5_ST]
                gi = a_ref[rows, c0 + S5_ST:c0 + 2 * S5_ST]
                ar, ai = coef[2 * d], coef[2 * d + 1]
                new += [ar * hr - ai * hi + gr, ar * hi + ai * hr + gi]
            return tuple(new)

        if nb % 8 == 0:
            fin = lax.fori_loop(0, nc, step, init, unroll=S5_UNROLL)
        else:
            fin = init
            for c in range(nc):
                fin = step(c, fin)
        if is_ctx:
            for k in range(4):
                hfin_ref[k] = fin[k]

        y8 = a_ref[:, 0:S5_ROW] + _dot_nt(hp_ref[...].astype(BF16), w2_ref[...])
        for t in range(S5_T):
            y8_ref[t] = y8[:, t * 128:(t + 1) * 128]
        for b in range(nb):
            for t in range(S5_T):
                y_ref[pl.ds(b * nc * S5_T + t, nc, stride=S5_T), :] = y8_ref[t, pl.ds(b, nc, stride=nb), :]

    pl.when(pl.program_id(1) == 0)(functools.partial(stream, CTX_B, CTX_NC, True))
    pl.when(pl.program_id(1) == 1)(functools.partial(stream, LAT_B, LAT_NC, False))


def _s5_mix(uf, w1, w2, a8, h0, l):
    return pl.pallas_call(
        _s5_mix_kernel,
        grid=(S5_OCT, 2),
        in_specs=[
            pl.BlockSpec((T_CTX, 128), lambda s, k: (k, s)),
            pl.BlockSpec((None, None, S5_ROW, S5_W1), lambda s, k: (l, s, 0, 0)),
            pl.BlockSpec((None, None, S5_ROW, 4 * S5_ST), lambda s, k: (l, s, 0, 0)),
            pl.BlockSpec((None, None, 8, S5_ST), lambda s, k: (l, s, 0, 0)),
            pl.BlockSpec((None, 4, LAT_B, S5_ST), lambda s, k: (s, 0, 0, 0)),
        ],
        out_specs=[
            pl.BlockSpec((T_CTX, 128), lambda s, k: (k, s)),
            pl.BlockSpec((None, 4, CTX_B, S5_ST), lambda s, k: (s, 0, 0, 0)),
        ],
        out_shape=[
            jax.ShapeDtypeStruct((T_ALL, S5_CH), F32),
            jax.ShapeDtypeStruct((S5_OCT, 4, CTX_B, S5_ST), F32),
        ],
        scratch_shapes=[
            pltpu.VMEM((S5_T, S5_ROWS, 128), F32),
            pltpu.VMEM((S5_ROWS, S5_W1), F32),
            pltpu.VMEM((S5_ROWS, 4 * S5_ST), F32),
            pltpu.VMEM((S5_T, S5_ROWS, 128), F32),
        ],
        compiler_params=pltpu.CompilerParams(
            dimension_semantics=("arbitrary", "arbitrary"), vmem_limit_bytes=VMEM_BIG),
        name=f"s5_mix_l{l}",
    )(uf, w1, w2, a8, h0)


def _glu_kernel(y_ref, u_ref, d_ref, w_ref, b_ref, o_ref):
    y = y_ref[...] + d_ref[...] * u_ref[...]
    z = y * (0.5 * (1.0 + jnp.tanh(math.sqrt(2.0 / math.pi) * (y + 0.044715 * (y * y * y)))))
    t = _dot(z.astype(BF16), w_ref[...].astype(BF16)) + b_ref[...]
    o_ref[...] = (z * jax.nn.sigmoid(t)).astype(BF16)


def _glu(y, uf, s5_d3, w_glu, b_glu3, l):
    return pl.pallas_call(
        _glu_kernel,
        grid=(T_ALL // BM,),
        in_specs=[
            pl.BlockSpec((BM, S5_CH), lambda i: (i, 0)),
            pl.BlockSpec((BM, S5_CH), lambda i: (i, 0)),
            pl.BlockSpec((None, 1, S5_CH), lambda i: (l, 0, 0)),
            pl.BlockSpec((None, S5_CH, S5_CH), lambda i: (l, 0, 0)),
            pl.BlockSpec((None, 1, S5_CH), lambda i: (l, 0, 0)),
        ],
        out_specs=pl.BlockSpec((BM, S5_CH), lambda i: (i, 0)),
        out_shape=jax.ShapeDtypeStruct((T_ALL, S5_CH), BF16),
        compiler_params=pltpu.CompilerParams(dimension_semantics=("arbitrary",)),
        name=f"s5_glu_l{l}",
    )(y, uf, s5_d3, w_glu, b_glu3)


def _outproj_kernel(x_ref, a_ref, c_ref, s_ref, mod_ref, npost_ref, w_ref, o_ref, wb_ref):
    @pl.when(pl.program_id(0) == 0)
    def _():
        nk = 4
        rk = D_MODEL // nk
        for r in range(nk):
            wb_ref[r * rk:(r + 1) * rk, :] = w_ref[r * rk:(r + 1) * rk, :].astype(BF16)

    na = A_HEADS * HD
    nc = na + C_HEADS * HD
    y = (_dot(a_ref[...], wb_ref[0:na, :]) + _dot(c_ref[...], wb_ref[na:nc, :])
         + _dot(s_ref[...], wb_ref[nc:, :]))
    gate = mod_ref[:, 5 * D_MODEL:6 * D_MODEL]
    o_ref[...] = x_ref[...] + gate * _rms(y, npost_ref[...])


def _outproj(x, a_out, c_out, s_out, mods4, norm_post4, w_out, l):
    bm = BM_OUT
    return pl.pallas_call(
        _outproj_kernel,
        grid=(T_ALL // bm,),
        in_specs=[
            pl.BlockSpec((bm, D_MODEL), lambda i: (i, 0)),
            pl.BlockSpec((bm, A_HEADS * HD), lambda i: (i, 0)),
            pl.BlockSpec((bm, C_HEADS * HD), lambda i: (i, 0)),
            pl.BlockSpec((bm, S5_CH), lambda i: (i, 0)),
            pl.BlockSpec((None, None, 1, N_MOD * D_MODEL), lambda i: (l, _mod_index(i, bm), 0, 0)),
            pl.BlockSpec((None, None, 1, D_MODEL), lambda i: (l, 1, 0, 0)),
            pl.BlockSpec((None, D_MODEL, D_MODEL), lambda i: (l, 0, 0), pipeline_mode=pl.Buffered(1)),
        ],
        out_specs=pl.BlockSpec((bm, D_MODEL), lambda i: (i, 0)),
        out_shape=jax.ShapeDtypeStruct((T_ALL, D_MODEL), F32),
        scratch_shapes=[pltpu.VMEM((D_MODEL, D_MODEL), BF16)],
        compiler_params=pltpu.CompilerParams(
            dimension_semantics=("arbitrary",), vmem_limit_bytes=VMEM_BIG),
        name=f"outproj_l{l}",
    )(x, a_out, c_out, s_out, mods4, norm_post4, w_out)


def _rope_tables():
    rows = LAT_L // GRID_W
    row = jnp.repeat(jnp.arange(rows, dtype=F32), GRID_W)
    col = jnp.tile(jnp.arange(GRID_W, dtype=F32), rows)
    axis_dim = HD // 2
    inv_freq = ROPE_BASE ** (-jnp.arange(0, axis_dim, 2, dtype=F32) / axis_dim)
    ang_row = row[:, None] * inv_freq
    ang_col = col[:, None] * inv_freq
    cr, sr = jnp.cos(ang_row), jnp.sin(ang_row)
    cc, sc = jnp.cos(ang_col), jnp.sin(ang_col)
    cos_t = jnp.concatenate([cr, cr, cc, cc], axis=-1)
    sin_t = jnp.concatenate([-sr, sr, -sc, sc], axis=-1)
    return cos_t, sin_t


def kernel(x_prompt, x_sample, cache_a_k, cache_a_v, cache_c_k, cache_c_v, state_ssm_re, state_ssm_im,
           c, c_ctx, w_mod, b_mod, norm_pre, norm_post, ffn_gate, ffn_up, ffn_down, w_in, w_out,
           q_norm, k_norm, sink, s5_lam_re, s5_lam_im, s5_log_step, s5_b_re, s5_b_im, s5_c_re, s5_c_im,
           s5_d, w_glu, b_glu):
    cvec8 = jnp.concatenate([c_ctx[None, :], c, jnp.zeros((8 - 1 - LAT_B, D_MODEL), F32)], axis=0)
    mods4 = _modulation(cvec8, w_mod, b_mod).reshape(DEPTH, 8, 1, N_MOD * D_MODEL)
    norm_pre4 = norm_pre.reshape(DEPTH, 3, 1, D_MODEL)
    norm_post4 = norm_post.reshape(DEPTH, 3, 1, D_MODEL)
    q_norm3 = q_norm.reshape(DEPTH, 1, HD)
    k_norm3 = k_norm.reshape(DEPTH, 1, HD)
    s5_d3 = s5_d.reshape(DEPTH, 1, S5_CH)
    b_glu3 = b_glu.reshape(DEPTH, 1, S5_CH)
    cos_t, sin_t = _rope_tables()
    w1, w2, a8 = _s5_params(s5_lam_re, s5_lam_im, s5_log_step, s5_b_re, s5_b_im, s5_c_re, s5_c_im)
    kv4 = lambda a: a.reshape(LAT_B, DEPTH, PAST, A_KV * HD)
    cak, cav, cck, ccv = kv4(cache_a_k), kv4(cache_a_v), kv4(cache_c_k), kv4(cache_c_v)
    h0_all = jnp.stack([state_ssm_re[:, :, 0], state_ssm_im[:, :, 0],
                        state_ssm_re[:, :, 1], state_ssm_im[:, :, 1]], axis=0)
    h0_all = h0_all.reshape(4, LAT_B, DEPTH, S5_OCT, S5_ST).transpose(2, 3, 0, 1, 4)

    w_in_bf = w_in.astype(BF16)
    ffn_w = (mods4, norm_pre4, norm_post4, ffn_gate, ffn_up, ffn_down)
    nct = T_CTX // BM
    nlt = T_LAT // BM
    new_caches = ()
    new_state = []
    for l in range(DEPTH):
        if l == 0:
            x = _ffn(x_prompt.reshape(T_CTX, D_MODEL), *ffn_w, l, 0, n_tiles=nct, tag="_ctx")
            x = _ffn(x_sample.reshape(T_LAT, D_MODEL), *ffn_w, l, 0, slab_tile0=nct, n_tiles=nlt,
                     out_tile0=nct, prev=x, tag="_lat")
        else:
            x = _ffn(x, *ffn_w, l, 0)

        qkvu, uf, *new_caches = _inproj(x, mods4, norm_pre4, q_norm3, k_norm3, cos_t, sin_t, w_in_bf, l,
                                        new_caches)
        a_out, c_out = _attn_ctx(qkvu, sink[l], l)
        a_out = _attn_lat_a(qkvu, cak, cav, a_out, l)
        c_out = _attn_lat_c(qkvu, sink[l], cck, ccv, c_out, l)
        y, hfin = _s5_mix(uf, w1, w2, a8, h0_all[l], l)
        s_out = _glu(y, uf, s5_d3, w_glu, b_glu3, l)

        x = _outproj(x, a_out, c_out, s_out, mods4, norm_post4, w_out, l)
        if l < DEPTH - 1:
            x = _ffn(x, *ffn_w, l, 1)
        else:
            y_prompt = _ffn(x, *ffn_w, l, 1, n_tiles=nct, out_rows=T_CTX, tag="_ctx")
            y_sample = _ffn(x, *ffn_w, l, 1, slab_tile0=nct, n_tiles=nlt, in_tile0=nct, out_rows=T_LAT,
                            tag="_lat")

        hf = hfin.reshape(S5_OCT, 4, CTX_B, S5_OG, S5_P).transpose(1, 2, 0, 3, 4).reshape(4, CTX_B, S5_G, S5_P)
        new_state.append((jnp.stack([hf[0], hf[2]], axis=1), jnp.stack([hf[1], hf[3]], axis=1)))

    y_prompt = y_prompt.reshape(CTX_B, CTX_L, D_MODEL)
    y_sample = y_sample.reshape(LAT_B, LAT_L, D_MODEL)
    caches = list(new_caches)
    st_re = jnp.stack([new_state[l][0] for l in range(DEPTH)], axis=1)
    st_im = jnp.stack([new_state[l][1] for l in range(DEPTH)], axis=1)
    return (y_prompt, y_sample, caches[0], caches[1], caches[2], caches[3], st_re, st_im)
```

```python
import functools
import math

import jax
import jax.numpy as jnp
from jax import lax
from jax.experimental import pallas as pl
from jax.experimental.pallas import tpu as pltpu

F32 = jnp.float32
BF16 = jnp.bfloat16

D_MODEL = 2048
CTX_B, CTX_L = 16, 256
LAT_B, LAT_L = 2, 2048
DEPTH = 2
PAST = 512
GRID_W = 64
HD = 128
A_HEADS, A_KV = 8, 2
C_HEADS, C_KV = 4, 2
WINDOW = 128
S5_GC = 16
S5_CH = 512
S5_G = 32
S5_P = 64
D_FF = 5632
N_MOD = 9
IN_WIDTH = 3072
ROPE_BASE = 10000.0
EPS = 1e-6
HALF_STEP = 0.5
NEG_INF = -1e30
SCALE = HD ** -0.5
LOG2E = math.log2(math.e)

T_CTX = CTX_B * CTX_L
T_LAT = LAT_B * LAT_L
T_ALL = T_CTX + T_LAT

COL_AQ, COL_AK, COL_AV = 0, 1024, 1280
COL_CQ, COL_CK, COL_CV = 1536, 2048, 2304
COL_U = 2560

S5_T = 8
S5_OCT = S5_CH // 128
S5_OG = S5_G // S5_OCT
S5_ROW = S5_T * 128
S5_ST = S5_OG * S5_P
S5_W1 = S5_ROW + 4 * S5_ST
CTX_NC = CTX_L // S5_T
LAT_NC = LAT_L // S5_T
S5_ROWS = CTX_NC * CTX_B
assert S5_ROWS == LAT_NC * LAT_B and T_CTX == T_LAT

V7X_VMEM_BYTES = 64 * 1024 * 1024
VMEM_BIG = 56 * 1024 * 1024
VMEM_MID = 40 * 1024 * 1024

BM = 1024
BM_OUT = 512
RC_OUT = 256
BF = 256
RC = 512
BN_MOD = 1024
BM_IN = 512
BQ_A = 512
BQ_C = 256
S5_UNROLL = 4


def _dot(a, b):
    return jnp.dot(a, b, preferred_element_type=F32)


def _dot_nt(a, b, precision=None):
    return lax.dot_general(a, b, (((1,), (1,)), ((), ())), preferred_element_type=F32, precision=precision)


def _dot_nt_split(a, b):
    ah = a.astype(BF16)
    bh = b.astype(BF16)
    al = (a - ah.astype(F32)).astype(BF16)
    bl = (b - bh.astype(F32)).astype(BF16)
    return _dot_nt(ah, bh) + (_dot_nt(ah, bl) + _dot_nt(al, bh))


def _rms(x, g):
    return x * lax.rsqrt(jnp.mean(x * x, axis=-1, keepdims=True) + EPS) * g


def _mod_index(i, bm):
    nct = T_CTX // bm
    return jnp.where(i < nct, 0, 1 + (i - nct) // (LAT_L // bm))


def _mod_kernel(c_ref, w_ref, b_ref, o_ref):
    c = c_ref[...]
    s = (c * jax.nn.sigmoid(c)).astype(BF16)
    o_ref[...] = _dot(s, w_ref[...].astype(BF16)) + b_ref[...]


def _modulation(cvec8, w_mod, b_mod):
    n = N_MOD * D_MODEL
    return pl.pallas_call(
        _mod_kernel,
        grid=(DEPTH, n // BN_MOD),
        in_specs=[
            pl.BlockSpec((8, D_MODEL), lambda l, j: (0, 0)),
            pl.BlockSpec((None, D_MODEL, BN_MOD), lambda l, j: (l, 0, j)),
            pl.BlockSpec((None, 1, BN_MOD), lambda l, j: (l, 0, j)),
        ],
        out_specs=pl.BlockSpec((None, 8, BN_MOD), lambda l, j: (l, 0, j)),
        out_shape=jax.ShapeDtypeStruct((DEPTH, 8, n), F32),
        compiler_params=pltpu.CompilerParams(
            dimension_semantics=("arbitrary", "arbitrary"), vmem_limit_bytes=VMEM_MID),
        name="modulation",
    )(cvec8, w_mod, b_mod.reshape(DEPTH, 1, n))


def _ffn_kernel(x_ref, mod_ref, npre_ref, npost_ref, wg_ref, wu_ref, wd_ref, o_ref, h_ref, *, mo):
    j = pl.program_id(1)
    last = pl.num_programs(1) - 1

    def step(first, final):
        wg = wg_ref[...].astype(BF16)
        wu = wu_ref[...].astype(BF16)
        wd = wd_ref[...].astype(BF16)
        for r in range(BM // RC):
            rows = slice(r * RC, (r + 1) * RC)
            if first:
                shift = mod_ref[:, mo * D_MODEL:(mo + 1) * D_MODEL]
                scale = mod_ref[:, (mo + 1) * D_MODEL:(mo + 2) * D_MODEL]
                hn = _rms(x_ref[rows, :], npre_ref[...])
                h = (hn * (1.0 + scale) + shift).astype(BF16)
                h_ref[rows, :] = h
            else:
                h = h_ref[rows, :]
            g = _dot(h, wg)
            u = _dot(h, wu)
            a = (g * jax.nn.sigmoid(g) * u).astype(BF16)
            acc = _dot(a, wd)
            if not first:
                acc = o_ref[rows, :] + acc
            if final:
                gate = mod_ref[:, (mo + 2) * D_MODEL:(mo + 3) * D_MODEL]
                acc = x_ref[rows, :] + (HALF_STEP * gate) * _rms(acc, npost_ref[...])
            o_ref[rows, :] = acc

    pl.when(j == 0)(functools.partial(step, True, False))
    pl.when(jnp.logical_and(j > 0, j < last))(functools.partial(step, False, False))
    pl.when(j == last)(functools.partial(step, False, True))


def _ffn(x, mods4, norm_pre4, norm_post4, ffn_gate, ffn_up, ffn_down, l, s, *,
         slab_tile0=0, n_tiles=T_ALL // BM, in_tile0=0, out_tile0=0, out_rows=T_ALL, prev=None, tag=""):
    mo = 6 * s
    ni = 2 * s
    in_specs = [
        pl.BlockSpec((BM, D_MODEL), lambda i, j: (in_tile0 + i, 0)),
        pl.BlockSpec((None, None, 1, N_MOD * D_MODEL), lambda i, j: (l, _mod_index(slab_tile0 + i, BM), 0, 0)),
        pl.BlockSpec((None, None, 1, D_MODEL), lambda i, j: (l, ni, 0, 0)),
        pl.BlockSpec((None, None, 1, D_MODEL), lambda i, j: (l, ni, 0, 0)),
        pl.BlockSpec((None, None, D_MODEL, BF), lambda i, j: (l, s, 0, j)),
        pl.BlockSpec((None, None, D_MODEL, BF), lambda i, j: (l, s, 0, j)),
        pl.BlockSpec((None, None, BF, D_MODEL), lambda i, j: (l, s, j, 0)),
    ]
    args = [x, mods4, norm_pre4, norm_post4, ffn_gate, ffn_up, ffn_down]
    kern = functools.partial(_ffn_kernel, mo=mo)
    aliases = {}
    if prev is not None:
        in_specs.append(pl.BlockSpec(memory_space=pl.ANY))
        args.append(prev)
        aliases = {len(args) - 1: 0}
        kern = functools.partial(_ffn_kernel_keep, mo=mo)
    return pl.pallas_call(
        kern,
        grid=(n_tiles, D_FF // BF),
        in_specs=in_specs,
        out_specs=pl.BlockSpec((BM, D_MODEL), lambda i, j: (out_tile0 + i, 0)),
        out_shape=jax.ShapeDtypeStruct((out_rows, D_MODEL), F32),
        input_output_aliases=aliases,
        scratch_shapes=[pltpu.VMEM((BM, D_MODEL), BF16)],
        compiler_params=pltpu.CompilerParams(
            dimension_semantics=("arbitrary", "arbitrary"), vmem_limit_bytes=VMEM_BIG),
        name=f"ffn_l{l}_s{s}{tag}",
    )(*args)


def _ffn_kernel_keep(x_ref, mod_ref, npre_ref, npost_ref, wg_ref, wu_ref, wd_ref, prev_ref, o_ref, h_ref, *, mo):
    del prev_ref
    _ffn_kernel(x_ref, mod_ref, npre_ref, npost_ref, wg_ref, wu_ref, wd_ref, o_ref, h_ref, mo=mo)


def _rope(y, cos, sins):
    lane = lax.broadcasted_iota(jnp.int32, y.shape, 1)
    first = (lane & 63) < 32
    partner = jnp.where(first, pltpu.roll(y, 96, 1), pltpu.roll(y, 32, 1))
    return y * cos + partner * sins


_IN_SEGMENTS = (
    (COL_AQ, A_HEADS, "q", True, None),
    (COL_AK, A_KV, "k", True, 0),
    (COL_AV, A_KV, None, False, 1),
    (COL_CQ, C_HEADS, None, True, None),
    (COL_CK, C_KV, None, True, 2),
    (COL_CV, C_KV, None, False, 3),
)


def _inproj_kernel(x_ref, mod_ref, npre_ref, qn_ref, kn_ref, cos_ref, sin_ref, w_ref, *rest):
    qkvu_ref, uf_ref = rest[-6], rest[-5]
    cache_refs = rest[-4:]
    nb = BM_IN // CTX_L

    def body(lat):
        shift = mod_ref[:, 3 * D_MODEL:4 * D_MODEL]
        scale = mod_ref[:, 4 * D_MODEL:5 * D_MODEL]
        h = (_rms(x_ref[...], npre_ref[...]) * (1.0 + scale) + shift).astype(BF16)
        for col0, heads, norm, rot, cache in _IN_SEGMENTS:
            p = _dot(h, w_ref[:, col0:col0 + heads * HD])
            for k in range(heads):
                y = p[:, k * HD:(k + 1) * HD]
                if norm == "q":
                    y = _rms(y, qn_ref[...])
                elif norm == "k":
                    y = _rms(y, kn_ref[...])
                if rot and lat:
                    y = _rope(y, cos_ref[...], sin_ref[...])
                qkvu_ref[:, col0 + k * HD:col0 + (k + 1) * HD] = y.astype(BF16)
                if cache is not None and not lat:
                    cache_refs[cache][:, :, k, :] = y.reshape(nb, CTX_L, HD)
        u = _dot(h, w_ref[:, COL_U:])
        uf_ref[...] = u
        qkvu_ref[:, COL_U:] = u.astype(BF16)

    is_lat = pl.program_id(0) >= T_CTX // BM_IN
    pl.when(is_lat)(functools.partial(body, True))
    pl.when(jnp.logical_not(is_lat))(functools.partial(body, False))


def _inproj(x, mods4, norm_pre4, q_norm3, k_norm3, cos_t, sin_t, w_in_bf, l, prev_caches):
    bm = BM_IN
    nct = T_CTX // bm
    nb = bm // CTX_L
    tab_idx = lambda i: (jnp.maximum(i - nct, 0) % (LAT_L // bm), 0)
    cache_shape = jax.ShapeDtypeStruct((CTX_B, DEPTH, CTX_L, A_KV, HD), F32)
    cache_spec = pl.BlockSpec((nb, None, CTX_L, A_KV, HD), lambda i: (jnp.minimum(i, nct - 1), l, 0, 0, 0))
    n_in = 8
    return pl.pallas_call(
        _inproj_kernel,
        grid=(T_ALL // bm,),
        in_specs=[
            pl.BlockSpec((bm, D_MODEL), lambda i: (i, 0)),
            pl.BlockSpec((None, None, 1, N_MOD * D_MODEL), lambda i: (l, _mod_index(i, bm), 0, 0)),
            pl.BlockSpec((None, None, 1, D_MODEL), lambda i: (l, 1, 0, 0)),
            pl.BlockSpec((None, 1, HD), lambda i: (l, 0, 0)),
            pl.BlockSpec((None, 1, HD), lambda i: (l, 0, 0)),
            pl.BlockSpec((bm, HD), tab_idx),
            pl.BlockSpec((bm, HD), tab_idx),
            pl.BlockSpec((None, D_MODEL, IN_WIDTH), lambda i: (l, 0, 0), pipeline_mode=pl.Buffered(1)),
        ] + [pl.BlockSpec(memory_space=pl.ANY)] * len(prev_caches),
        out_specs=[
            pl.BlockSpec((bm, IN_WIDTH), lambda i: (i, 0)),
            pl.BlockSpec((bm, S5_CH), lambda i: (i, 0)),
        ] + [cache_spec] * 4,
        out_shape=[
            jax.ShapeDtypeStruct((T_ALL, IN_WIDTH), BF16),
            jax.ShapeDtypeStruct((T_ALL, S5_CH), F32),
        ] + [cache_shape] * 4,
        input_output_aliases={n_in + k: 2 + k for k in range(len(prev_caches))},
        compiler_params=pltpu.CompilerParams(
            dimension_semantics=("arbitrary",), vmem_limit_bytes=VMEM_BIG),
        name=f"inproj_l{l}",
    )(x, mods4, norm_pre4, q_norm3, k_norm3, cos_t, sin_t, w_in_bf, *prev_caches)


def _softmax_pv(dots, values, sink=None):
    m = functools.reduce(jnp.maximum, [jnp.max(d, axis=-1, keepdims=True) for d in dots]) * SCALE
    if sink is not None:
        m = jnp.maximum(m, sink)
    m2 = m * LOG2E
    ps = [jnp.exp2(d * (SCALE * LOG2E) - m2) for d in dots]
    den = functools.reduce(jnp.add, [jnp.sum(p, axis=-1, keepdims=True) for p in ps])
    if sink is not None:
        den = den + jnp.exp2(sink * LOG2E - m2)
    o = functools.reduce(jnp.add, [_dot(p.astype(BF16), v) for p, v in zip(ps, values)])
    return o / den


def _attn_ctx_kernel(sink_ref, aq_ref, ak_ref, av_ref, cq_ref, ck_ref, cv_ref, ao_ref, co_ref):
    kv = pl.program_id(1)
    k = ak_ref[...]
    v = av_ref[...]
    for h in range(A_HEADS // A_KV):
        cols = slice(h * HD, (h + 1) * HD)
        s = _dot_nt(aq_ref[:, cols], k)
        ao_ref[:, cols] = _softmax_pv([s], [v]).astype(BF16)
    k = ck_ref[...]
    v = cv_ref[...]
    g = C_HEADS // C_KV
    for h in range(g):
        cols = slice(h * HD, (h + 1) * HD)
        s = _dot_nt(cq_ref[:, cols], k)
        co_ref[:, cols] = _softmax_pv([s], [v], sink_ref[kv * g + h]).astype(BF16)


def _attn_ctx(qkvu, sink_l, l):
    ga = A_HEADS // A_KV * HD
    gc = C_HEADS // C_KV * HD
    blk = lambda width, col0: pl.BlockSpec((CTX_L, width), lambda b, k: (b, col0 // width + k))
    return pl.pallas_call(
        _attn_ctx_kernel,
        grid=(CTX_B, A_KV),
        in_specs=[
            pl.BlockSpec(memory_space=pltpu.SMEM),
            blk(ga, COL_AQ), blk(HD, COL_AK), blk(HD, COL_AV),
            blk(gc, COL_CQ), blk(HD, COL_CK), blk(HD, COL_CV),
        ],
        out_specs=[
            pl.BlockSpec((CTX_L, ga), lambda b, k: (b, k)),
            pl.BlockSpec((CTX_L, gc), lambda b, k: (b, k)),
        ],
        out_shape=[
            jax.ShapeDtypeStruct((T_ALL, A_HEADS * HD), BF16),
            jax.ShapeDtypeStruct((T_ALL, C_HEADS * HD), BF16),
        ],
        compiler_params=pltpu.CompilerParams(dimension_semantics=("arbitrary", "arbitrary")),
        name=f"attn_ctx_l{l}",
    )(sink_l, qkvu, qkvu, qkvu, qkvu, qkvu, qkvu)


def _attn_lat_a_kernel(q_ref, k_ref, v_ref, kc_ref, vc_ref, prev_ref, o_ref):
    del prev_ref
    k = k_ref[...]
    v = v_ref[...]
    kc = kc_ref[...].astype(BF16)
    vc = vc_ref[...].astype(BF16)
    for h in range(A_HEADS // A_KV):
        cols = slice(h * HD, (h + 1) * HD)
        q = q_ref[:, cols]
        s1 = _dot_nt(q, k)
        s2 = _dot_nt(q, kc)
        o_ref[:, cols] = _softmax_pv([s1, s2], [v, vc]).astype(BF16)


def _attn_lat_a(qkvu, cache_k4, cache_v4, a_out, l):
    ga = A_HEADS // A_KV * HD
    nq = LAT_L // BQ_A
    row0 = T_CTX // BQ_A
    lat_blk = T_CTX // LAT_L
    return pl.pallas_call(
        _attn_lat_a_kernel,
        grid=(LAT_B, A_KV, nq),
        in_specs=[
            pl.BlockSpec((BQ_A, ga), lambda b, k, q: (row0 + b * nq + q, COL_AQ // ga + k)),
            pl.BlockSpec((LAT_L, HD), lambda b, k, q: (lat_blk + b, COL_AK // HD + k)),
            pl.BlockSpec((LAT_L, HD), lambda b, k, q: (lat_blk + b, COL_AV // HD + k)),
            pl.BlockSpec((None, None, PAST, HD), lambda b, k, q: (b, l, 0, k)),
            pl.BlockSpec((None, None, PAST, HD), lambda b, k, q: (b, l, 0, k)),
            pl.BlockSpec(memory_space=pl.ANY),
        ],
        out_specs=pl.BlockSpec((BQ_A, ga), lambda b, k, q: (row0 + b * nq + q, k)),
        out_shape=jax.ShapeDtypeStruct((T_ALL, A_HEADS * HD), BF16),
        input_output_aliases={5: 0},
        compiler_params=pltpu.CompilerParams(
            dimension_semantics=("arbitrary", "arbitrary", "arbitrary"), vmem_limit_bytes=VMEM_BIG),
        name=f"attn_lat_a_l{l}",
    )(qkvu, qkvu, qkvu, cache_k4, cache_v4, a_out)


def _attn_lat_c_kernel(sink_ref, q_ref, k_ref, v_ref, kc_ref, vc_ref, prev_ref, o_ref):
    del prev_ref
    kv = pl.program_id(1)
    n = pl.program_id(2)
    span = BQ_C + 2 * WINDOW
    start = pl.multiple_of(jnp.clip(n * BQ_C - WINDOW, 0, LAT_L - span), WINDOW)
    kw = k_ref[pl.ds(start, span), :]
    vw = v_ref[pl.ds(start, span), :]
    kc = kc_ref[...].astype(BF16)
    vc = vc_ref[...].astype(BF16)
    qpos = n * BQ_C + lax.broadcasted_iota(jnp.int32, (BQ_C, span), 0)
    kpos = start + lax.broadcasted_iota(jnp.int32, (BQ_C, span), 1)
    valid = jnp.abs(qpos - kpos) <= WINDOW
    g = C_HEADS // C_KV
    for h in range(g):
        cols = slice(h * HD, (h + 1) * HD)
        q = q_ref[:, cols]
        s1 = jnp.where(valid, _dot_nt(q, kw), NEG_INF)
        s2 = _dot_nt(q, kc)
        o_ref[:, cols] = _softmax_pv([s1, s2], [vw, vc], sink_ref[kv * g + h]).astype(BF16)


def _attn_lat_c(qkvu, sink_l, cache_k4, cache_v4, c_out, l):
    gc = C_HEADS // C_KV * HD
    nq = LAT_L // BQ_C
    row0 = T_CTX // BQ_C
    lat_blk = T_CTX // LAT_L
    return pl.pallas_call(
        _attn_lat_c_kernel,
        grid=(LAT_B, C_KV, nq),
        in_specs=[
            pl.BlockSpec(memory_space=pltpu.SMEM),
            pl.BlockSpec((BQ_C, gc), lambda b, k, q: (row0 + b * nq + q, COL_CQ // gc + k)),
            pl.BlockSpec((LAT_L, HD), lambda b, k, q: (lat_blk + b, COL_CK // HD + k)),
            pl.BlockSpec((LAT_L, HD), lambda b, k, q: (lat_blk + b, COL_CV // HD + k)),
            pl.BlockSpec((None, None, PAST, HD), lambda b, k, q: (b, l, 0, k)),
            pl.BlockSpec((None, None, PAST, HD), lambda b, k, q: (b, l, 0, k)),
            pl.BlockSpec(memory_space=pl.ANY),
        ],
        out_specs=pl.BlockSpec((BQ_C, gc), lambda b, k, q: (row0 + b * nq + q, k)),
        out_shape=jax.ShapeDtypeStruct((T_ALL, C_HEADS * HD), BF16),
        input_output_aliases={6: 0},
        compiler_params=pltpu.CompilerParams(
            dimension_semantics=("arbitrary", "arbitrary", "arbitrary")),
        name=f"attn_lat_c_l{l}",
    )(sink_l, qkvu, qkvu, qkvu, cache_k4, cache_v4, c_out)


def _s5_param_kernel(rows_ref, bc_ref, w1_ref, w2_ref, a8_ref):
    a8_ref[...] = jnp.zeros_like(a8_ref)
    taps = []
    for d in range(2):
        lr = rows_ref[0, d]
        li = rows_ref[1, d]
        dt = jnp.exp(rows_ref[2, d])
        mag = jnp.exp(lr * dt)
        ar = mag * jnp.cos(li * dt)
        ai = mag * jnp.sin(li * dt)
        den = lr * lr + li * li
        n_re = ar - 1.0
        f_re = (n_re * lr + ai * li) / den
        f_im = (ai * lr - n_re * li) / den
        pw = [(jnp.ones_like(ar), jnp.zeros_like(ar))]
        for _ in range(S5_T):
            pr, pi = pw[-1]
            pw.append((pr * ar - pi * ai, pr * ai + pi * ar))
        br = bc_ref[0, d]
        bi = bc_ref[1, d]
        bbr = f_re * br - f_im * bi
        bbi = f_re * bi + f_im * br
        cr = bc_ref[2, d]
        ci = bc_ref[3, d]
        xr_blocks, xi_blocks = [], []
        for t in range(S5_T):
            rows = slice(t * 128, (t + 1) * 128)
            pr, pi = pw[S5_T - 1 - t if d == 0 else t]
            xr = pr * bbr - pi * bbi
            xi = pr * bbi + pi * bbr
            c0 = S5_ROW + 2 * S5_ST * d
            w1_ref[rows, c0:c0 + S5_ST] = xr.astype(BF16)
            w1_ref[rows, c0 + S5_ST:c0 + 2 * S5_ST] = xi.astype(BF16)
            xr_blocks.append(xr)
            xi_blocks.append(xi)
            pr, pi = pw[t + 1 if d == 0 else S5_T - t]
            c0 = 2 * S5_ST * d
            w2_ref[rows, c0:c0 + S5_ST] = (cr * pr - ci * pi).astype(BF16)
            w2_ref[rows, c0 + S5_ST:c0 + 2 * S5_ST] = (-(cr * pi + ci * pr)).astype(BF16)
        a8_ref[2 * d:2 * d + 1, :] = pw[S5_T][0]
        a8_ref[2 * d + 1:2 * d + 2, :] = pw[S5_T][1]
        xr_all = jnp.concatenate(xr_blocks, axis=0)
        xi_all = jnp.concatenate(xi_blocks, axis=0)
        taps.append(_dot_nt_split(xr_all, cr) - _dot_nt_split(xi_all, ci))
    tf, tb = taps
    for t in range(S5_T):
        for t2 in range(S5_T):
            blk = None
            if t2 >= t:
                r0 = (S5_T - 1 - (t2 - t)) * 128
                blk = tf[r0:r0 + 128, :]
            if t2 <= t:
                r0 = (t - t2) * 128
                b2 = tb[r0:r0 + 128, :]
                blk = b2 if blk is None else blk + b2
            w1_ref[t * 128:(t + 1) * 128, t2 * 128:(t2 + 1) * 128] = blk.astype(BF16)


def _s5_params(s5_lam_re, s5_lam_im, s5_log_step, s5_b_re, s5_b_im, s5_c_re, s5_c_im):
    rows = jnp.stack([s5_lam_re, s5_lam_im, jnp.repeat(s5_log_step[..., None], S5_P, axis=-1)], axis=0)
    rows = rows.reshape(3, DEPTH, 2, S5_OCT, 1, S5_ST)
    bc = jnp.stack([jnp.swapaxes(s5_b_re, -1, -2), jnp.swapaxes(s5_b_im, -1, -2), s5_c_re, s5_c_im], axis=0)
    bc = bc.reshape(4, DEPTH, 2, S5_OCT, S5_OG, S5_GC, S5_P)
    grp = lax.broadcasted_iota(jnp.int32, bc.shape, 4)
    bc = jnp.concatenate([jnp.where(grp == g, bc, 0.0) for g in range(S5_OG)], axis=-1)
    bc = bc.reshape(4, DEPTH, 2, S5_OCT, 128, S5_ST)
    spec = lambda n, r: pl.BlockSpec((n, None, 2, None, r, S5_ST), lambda l, s: (0, l, 0, s, 0, 0))
    return pl.pallas_call(
        _s5_param_kernel,
        grid=(DEPTH, S5_OCT),
        in_specs=[spec(3, 1), spec(4, 128)],
        out_specs=[
            pl.BlockSpec((None, None, S5_ROW, S5_W1), lambda l, s: (l, s, 0, 0)),
            pl.BlockSpec((None, None, S5_ROW, 4 * S5_ST), lambda l, s: (l, s, 0, 0)),
            pl.BlockSpec((None, None, 8, S5_ST), lambda l, s: (l, s, 0, 0)),
        ],
        out_shape=[
            jax.ShapeDtypeStruct((DEPTH, S5_OCT, S5_ROW, S5_W1), BF16),
            jax.ShapeDtypeStruct((DEPTH, S5_OCT, S5_ROW, 4 * S5_ST), BF16),
            jax.ShapeDtypeStruct((DEPTH, S5_OCT, 8, S5_ST), F32),
        ],
        compiler_params=pltpu.CompilerParams(
            dimension_semantics=("arbitrary", "arbitrary"), vmem_limit_bytes=VMEM_MID),
        name="s5_params",
    )(rows, bc)


def _s5_mix_kernel(u_ref, w1_ref, w2_ref, a8_ref, h0_ref, y_ref, hfin_ref, lhs_ref, a_ref, hp_ref, y8_ref):
    def stream(nb, nc, is_ctx):
        for b in range(nb):
            for t in range(S5_T):
                lhs_ref[t, pl.ds(b, nc, stride=nb), :] = u_ref[pl.ds(b * nc * S5_T + t, nc, stride=S5_T), :]
        lhs = jnp.concatenate([lhs_ref[t] for t in range(S5_T)], axis=1).astype(BF16)
        a_ref[...] = _dot(lhs, w1_ref[...])

        coef = [a8_ref[k:k + 1, :] for k in range(4)]
        if is_ctx:
            init = tuple(jnp.zeros((nb, S5_ST), F32) for _ in range(4))
        else:
            init = tuple(h0_ref[k] for k in range(4))

        def step(c, st):
            new = []
            for d in range(2):
                cc = c if d == 0 else nc - 1 - c
                rows = pl.ds(cc * nb, nb) if isinstance(c, int) else pl.ds(pl.multiple_of(cc * nb, nb), nb)
                hr, hi = st[2 * d], st[2 * d + 1]
                hp_ref[rows, 2 * S5_ST * d:2 * S5_ST * d + S5_ST] = hr
                hp_ref[rows, 2 * S5_ST * d + S5_ST:2 * S5_ST * (d + 1)] = hi
                c0 = S5_ROW + 2 * S5_ST * d
                gr = a_ref[rows, c0:c0 + S5_ST]
                gi = a_ref[rows, c0 + S5_ST:c0 + 2 * S5_ST]
                ar, ai = coef[2 * d], coef[2 * d + 1]
                new += [ar * hr - ai * hi + gr, ar * hi + ai * hr + gi]
            return tuple(new)

        if nb % 8 == 0:
            fin = lax.fori_loop(0, nc, step, init, unroll=S5_UNROLL)
        else:
            fin = init
            for c in range(nc):
                fin = step(c, fin)
        if is_ctx:
            for k in range(4):
                hfin_ref[k] = fin[k]

        y8 = a_ref[:, 0:S5_ROW] + _dot_nt(hp_ref[...].astype(BF16), w2_ref[...])
        for t in range(S5_T):
            y8_ref[t] = y8[:, t * 128:(t + 1) * 128]
        for b in range(nb):
            for t in range(S5_T):
                y_ref[pl.ds(b * nc * S5_T + t, nc, stride=S5_T), :] = y8_ref[t, pl.ds(b, nc, stride=nb), :]

    pl.when(pl.program_id(1) == 0)(functools.partial(stream, CTX_B, CTX_NC, True))
    pl.when(pl.program_id(1) == 1)(functools.partial(stream, LAT_B, LAT_NC, False))


def _s5_mix(uf, w1, w2, a8, h0, l):
    return pl.pallas_call(
        _s5_mix_kernel,
        grid=(S5_OCT, 2),
        in_specs=[
            pl.BlockSpec((T_CTX, 128), lambda s, k: (k, s)),
            pl.BlockSpec((None, None, S5_ROW, S5_W1), lambda s, k: (l, s, 0, 0)),
            pl.BlockSpec((None, None, S5_ROW, 4 * S5_ST), lambda s, k: (l, s, 0, 0)),
            pl.BlockSpec((None, None, 8, S5_ST), lambda s, k: (l, s, 0, 0)),
            pl.BlockSpec((None, 4, LAT_B, S5_ST), lambda s, k: (s, 0, 0, 0)),
        ],
        out_specs=[
            pl.BlockSpec((T_CTX, 128), lambda s, k: (k, s)),
            pl.BlockSpec((None, 4, CTX_B, S5_ST), lambda s, k: (s, 0, 0, 0)),
        ],
        out_shape=[
            jax.ShapeDtypeStruct((T_ALL, S5_CH), F32),
            jax.ShapeDtypeStruct((S5_OCT, 4, CTX_B, S5_ST), F32),
        ],
        scratch_shapes=[
            pltpu.VMEM((S5_T, S5_ROWS, 128), F32),
            pltpu.VMEM((S5_ROWS, S5_W1), F32),
            pltpu.VMEM((S5_ROWS, 4 * S5_ST), F32),
            pltpu.VMEM((S5_T, S5_ROWS, 128), F32),
        ],
        compiler_params=pltpu.CompilerParams(
            dimension_semantics=("arbitrary", "arbitrary"), vmem_limit_bytes=VMEM_BIG),
        name=f"s5_mix_l{l}",
    )(uf, w1, w2, a8, h0)


def _glu_kernel(y_ref, u_ref, d_ref, w_ref, b_ref, o_ref):
    y = y_ref[...] + d_ref[...] * u_ref[...]
    z = y * (0.5 * (1.0 + jnp.tanh(math.sqrt(2.0 / math.pi) * (y + 0.044715 * (y * y * y)))))
    t = _dot(z.astype(BF16), w_ref[...].astype(BF16)) + b_ref[...]
    o_ref[...] = (z * jax.nn.sigmoid(t)).astype(BF16)


def _glu(y, uf, s5_d3, w_glu, b_glu3, l):
    return pl.pallas_call(
        _glu_kernel,
        grid=(T_ALL // BM,),
        in_specs=[
            pl.BlockSpec((BM, S5_CH), lambda i: (i, 0)),
            pl.BlockSpec((BM, S5_CH), lambda i: (i, 0)),
            pl.BlockSpec((None, 1, S5_CH), lambda i: (l, 0, 0)),
            pl.BlockSpec((None, S5_CH, S5_CH), lambda i: (l, 0, 0)),
            pl.BlockSpec((None, 1, S5_CH), lambda i: (l, 0, 0)),
        ],
        out_specs=pl.BlockSpec((BM, S5_CH), lambda i: (i, 0)),
        out_shape=jax.ShapeDtypeStruct((T_ALL, S5_CH), BF16),
        compiler_params=pltpu.CompilerParams(dimension_semantics=("arbitrary",)),
        name=f"s5_glu_l{l}",
    )(y, uf, s5_d3, w_glu, b_glu3)


def _outproj_kernel(x_ref, a_ref, c_ref, s_ref, mod_ref, npost_ref, w_ref, o_ref, wb_ref):
    @pl.when(pl.program_id(0) == 0)
    def _():
        nk = 4
        rk = D_MODEL // nk
        for r in range(nk):
            wb_ref[r * rk:(r + 1) * rk, :] = w_ref[r * rk:(r + 1) * rk, :].astype(BF16)

    na = A_HEADS * HD
    nc = na + C_HEADS * HD
    gate = mod_ref[:, 5 * D_MODEL:6 * D_MODEL]
    for r in range(BM_OUT // RC_OUT):
        rows = slice(r * RC_OUT, (r + 1) * RC_OUT)
        y = (_dot(a_ref[rows, :], wb_ref[0:na, :]) + _dot(c_ref[rows, :], wb_ref[na:nc, :])
             + _dot(s_ref[rows, :], wb_ref[nc:, :]))
        o_ref[rows, :] = x_ref[rows, :] + gate * _rms(y, npost_ref[...])


def _outproj(x, a_out, c_out, s_out, mods4, norm_post4, w_out, l):
    bm = BM_OUT
    return pl.pallas_call(
        _outproj_kernel,
        grid=(T_ALL // bm,),
        in_specs=[
            pl.BlockSpec((bm, D_MODEL), lambda i: (i, 0)),
            pl.BlockSpec((bm, A_HEADS * HD), lambda i: (i, 0)),
            pl.BlockSpec((bm, C_HEADS * HD), lambda i: (i, 0)),
            pl.BlockSpec((bm, S5_CH), lambda i: (i, 0)),
            pl.BlockSpec((None, None, 1, N_MOD * D_MODEL), lambda i: (l, _mod_index(i, bm), 0, 0)),
            pl.BlockSpec((None, None, 1, D_MODEL), lambda i: (l, 1, 0, 0)),
            pl.BlockSpec((None, D_MODEL, D_MODEL), lambda i: (l, 0, 0), pipeline_mode=pl.Buffered(1)),
        ],
        out_specs=pl.BlockSpec((bm, D_MODEL), lambda i: (i, 0)),
        out_shape=jax.ShapeDtypeStruct((T_ALL, D_MODEL), F32),
        scratch_shapes=[pltpu.VMEM((D_MODEL, D_MODEL), BF16)],
        compiler_params=pltpu.CompilerParams(
            dimension_semantics=("arbitrary",), vmem_limit_bytes=VMEM_BIG),
        name=f"outproj_l{l}",
    )(x, a_out, c_out, s_out, mods4, norm_post4, w_out)


def _rope_tables():
    rows = LAT_L // GRID_W
    row = jnp.repeat(jnp.arange(rows, dtype=F32), GRID_W)
    col = jnp.tile(jnp.arange(GRID_W, dtype=F32), rows)
    axis_dim = HD // 2
    inv_freq = ROPE_BASE ** (-jnp.arange(0, axis_dim, 2, dtype=F32) / axis_dim)
    ang_row = row[:, None] * inv_freq
    ang_col = col[:, None] * inv_freq
    cr, sr = jnp.cos(ang_row), jnp.sin(ang_row)
    cc, sc = jnp.cos(ang_col), jnp.sin(ang_col)
    cos_t = jnp.concatenate([cr, cr, cc, cc], axis=-1)
    sin_t = jnp.concatenate([-sr, sr, -sc, sc], axis=-1)
    return cos_t, sin_t


def kernel(x_prompt, x_sample, cache_a_k, cache_a_v, cache_c_k, cache_c_v, state_ssm_re, state_ssm_im,
           c, c_ctx, w_mod, b_mod, norm_pre, norm_post, ffn_gate, ffn_up, ffn_down, w_in, w_out,
           q_norm, k_norm, sink, s5_lam_re, s5_lam_im, s5_log_step, s5_b_re, s5_b_im, s5_c_re, s5_c_im,
           s5_d, w_glu, b_glu):
    cvec8 = jnp.concatenate([c_ctx[None, :], c, jnp.zeros((8 - 1 - LAT_B, D_MODEL), F32)], axis=0)
    mods4 = _modulation(cvec8, w_mod, b_mod).reshape(DEPTH, 8, 1, N_MOD * D_MODEL)
    norm_pre4 = norm_pre.reshape(DEPTH, 3, 1, D_MODEL)
    norm_post4 = norm_post.reshape(DEPTH, 3, 1, D_MODEL)
    q_norm3 = q_norm.reshape(DEPTH, 1, HD)
    k_norm3 = k_norm.reshape(DEPTH, 1, HD)
    s5_d3 = s5_d.reshape(DEPTH, 1, S5_CH)
    b_glu3 = b_glu.reshape(DEPTH, 1, S5_CH)
    cos_t, sin_t = _rope_tables()
    w1, w2, a8 = _s5_params(s5_lam_re, s5_lam_im, s5_log_step, s5_b_re, s5_b_im, s5_c_re, s5_c_im)
    kv4 = lambda a: a.reshape(LAT_B, DEPTH, PAST, A_KV * HD)
    cak, cav, cck, ccv = kv4(cache_a_k), kv4(cache_a_v), kv4(cache_c_k), kv4(cache_c_v)
    h0_all = jnp.stack([state_ssm_re[:, :, 0], state_ssm_im[:, :, 0],
                        state_ssm_re[:, :, 1], state_ssm_im[:, :, 1]], axis=0)
    h0_all = h0_all.reshape(4, LAT_B, DEPTH, S5_OCT, S5_ST).transpose(2, 3, 0, 1, 4)

    w_in_bf = w_in.astype(BF16)
    ffn_w = (mods4, norm_pre4, norm_post4, ffn_gate, ffn_up, ffn_down)
    nct = T_CTX // BM
    nlt = T_LAT // BM
    new_caches = ()
    new_state = []
    for l in range(DEPTH):
        if l == 0:
            x = _ffn(x_prompt.reshape(T_CTX, D_MODEL), *ffn_w, l, 0, n_tiles=nct, tag="_ctx")
            x = _ffn(x_sample.reshape(T_LAT, D_MODEL), *ffn_w, l, 0, slab_tile0=nct, n_tiles=nlt,
                     out_tile0=nct, prev=x, tag="_lat")
        else:
            x = _ffn(x, *ffn_w, l, 0)

        qkvu, uf, *new_caches = _inproj(x, mods4, norm_pre4, q_norm3, k_norm3, cos_t, sin_t, w_in_bf, l,
                                        new_caches)
        a_out, c_out = _attn_ctx(qkvu, sink[l], l)
        a_out = _attn_lat_a(qkvu, cak, cav, a_out, l)
        c_out = _attn_lat_c(qkvu, sink[l], cck, ccv, c_out, l)
        y, hfin = _s5_mix(uf, w1, w2, a8, h0_all[l], l)
        s_out = _glu(y, uf, s5_d3, w_glu, b_glu3, l)

        x = _outproj(x, a_out, c_out, s_out, mods4, norm_post4, w_out, l)
        if l < DEPTH - 1:
            x = _ffn(x, *ffn_w, l, 1)
        else:
            y_prompt = _ffn(x, *ffn_w, l, 1, n_tiles=nct, out_rows=T_CTX, tag="_ctx")
            y_sample = _ffn(x, *ffn_w, l, 1, slab_tile0=nct, n_tiles=nlt, in_tile0=nct, out_rows=T_LAT,
                            tag="_lat")

        hf = hfin.reshape(S5_OCT, 4, CTX_B, S5_OG, S5_P).transpose(1, 2, 0, 3, 4).reshape(4, CTX_B, S5_G, S5_P)
        new_state.append((jnp.stack([hf[0], hf[2]], axis=1), jnp.stack([hf[1], hf[3]], axis=1)))

    y_prompt = y_prompt.reshape(CTX_B, CTX_L, D_MODEL)
    y_sample = y_sample.reshape(LAT_B, LAT_L, D_MODEL)
    caches = list(new_caches)
    st_re = jnp.stack([new_state[l][0] for l in range(DEPTH)], axis=1)
    st_im = jnp.stack([new_state[l][1] for l in range(DEPTH)], axis=1)
    return (y_prompt, y_sample, caches[0], caches[1], caches[2], caches[3], st_re, st_im)
```

```python
import functools
import math

import jax
import jax.numpy as jnp
from jax import lax
from jax.experimental import pallas as pl
from jax.experimental.pallas import tpu as pltpu

F32 = jnp.float32
BF16 = jnp.bfloat16

D_MODEL = 2048
CTX_B, CTX_L = 16, 256
LAT_B, LAT_L = 2, 2048
DEPTH = 2
PAST = 512
GRID_W = 64
HD = 128
A_HEADS, A_KV = 8, 2
C_HEADS, C_KV = 4, 2
WINDOW = 128
S5_GC = 16
S5_CH = 512
S5_G = 32
S5_P = 64
D_FF = 5632
N_MOD = 9
IN_WIDTH = 3072
ROPE_BASE = 10000.0
EPS = 1e-6
HALF_STEP = 0.5
NEG_INF = -1e30
SCALE = HD ** -0.5
LOG2E = math.log2(math.e)

T_CTX = CTX_B * CTX_L
T_LAT = LAT_B * LAT_L
T_ALL = T_CTX + T_LAT

COL_AQ, COL_AK, COL_AV = 0, 1024, 1280
COL_CQ, COL_CK, COL_CV = 1536, 2048, 2304
COL_U = 2560

S5_T = 8
S5_OCT = S5_CH // 128
S5_OG = S5_G // S5_OCT
S5_ROW = S5_T * 128
S5_ST = S5_OG * S5_P
S5_W1 = S5_ROW + 4 * S5_ST
CTX_NC = CTX_L // S5_T
LAT_NC = LAT_L // S5_T
S5_ROWS = CTX_NC * CTX_B
assert S5_ROWS == LAT_NC * LAT_B and T_CTX == T_LAT

V7X_VMEM_BYTES = 64 * 1024 * 1024
VMEM_FFN = 60 * 1024 * 1024
VMEM_BIG = 56 * 1024 * 1024
VMEM_MID = 40 * 1024 * 1024

BM = 1024
BM_OUT = 512
RC_OUT = 256
BF = 512
RC = 512
RCX = 256
FFN_PREFETCH_STEP = 2
BN_MOD = 1024
BM_IN = 512
BQ_A = 512
BQ_C = 256
S5_UNROLL = 4


def _dot(a, b):
    return jnp.dot(a, b, preferred_element_type=F32)


def _dot_nt(a, b, precision=None):
    return lax.dot_general(a, b, (((1,), (1,)), ((), ())), preferred_element_type=F32, precision=precision)


def _dot_nt_split(a, b):
    ah = a.astype(BF16)
    bh = b.astype(BF16)
    al = (a - ah.astype(F32)).astype(BF16)
    bl = (b - bh.astype(F32)).astype(BF16)
    return _dot_nt(ah, bh) + (_dot_nt(ah, bl) + _dot_nt(al, bh))


def _rms(x, g):
    return x * lax.rsqrt(jnp.mean(x * x, axis=-1, keepdims=True) + EPS) * g


def _mod_index(i, bm):
    nct = T_CTX // bm
    return jnp.where(i < nct, 0, 1 + (i - nct) // (LAT_L // bm))


def _mod_kernel(c_ref, w_ref, b_ref, o_ref):
    c = c_ref[...]
    s = (c * jax.nn.sigmoid(c)).astype(BF16)
    o_ref[...] = _dot(s, w_ref[...].astype(BF16)) + b_ref[...]


def _modulation(cvec8, w_mod, b_mod):
    n = N_MOD * D_MODEL
    return pl.pallas_call(
        _mod_kernel,
        grid=(DEPTH, n // BN_MOD),
        in_specs=[
            pl.BlockSpec((8, D_MODEL), lambda l, j: (0, 0)),
            pl.BlockSpec((None, D_MODEL, BN_MOD), lambda l, j: (l, 0, j)),
            pl.BlockSpec((None, 1, BN_MOD), lambda l, j: (l, 0, j)),
        ],
        out_specs=pl.BlockSpec((None, 8, BN_MOD), lambda l, j: (l, 0, j)),
        out_shape=jax.ShapeDtypeStruct((DEPTH, 8, n), F32),
        compiler_params=pltpu.CompilerParams(
            dimension_semantics=("arbitrary", "arbitrary"), vmem_limit_bytes=VMEM_MID),
        name="modulation",
    )(cvec8, w_mod, b_mod.reshape(DEPTH, 1, n))


def _ffn_kernel(*refs, mo, n_x, n_out):
    x_hbms = refs[:n_x]
    mod_ref, npre_ref, npost_ref, wg_ref, wu_ref, wd_ref = refs[n_x:n_x + 6]
    out_hbms = refs[n_x + 6:n_x + 6 + n_out]
    acc_ref, xc_ref, h_ref, sem_x, sem_c, sem_o = refs[n_x + 6 + n_out:]
    i = pl.program_id(0)
    j = pl.program_id(1)
    n_tiles = pl.num_programs(0)
    last = pl.num_programs(1) - 1
    slot = i % 2
    nq = RC // RCX
    nct = T_CTX // BM

    def per_stream(arrays, tile, fn):
        if len(arrays) == 1:
            fn(arrays[0], pl.multiple_of(tile * BM, BM))
        else:
            pl.when(tile < nct)(lambda: fn(arrays[0], pl.multiple_of(tile * BM, BM)))
            pl.when(tile >= nct)(lambda: fn(arrays[1], pl.multiple_of((tile - nct) * BM, BM)))

    def x_tile_copy(arr, row0, sl):
        return pltpu.make_async_copy(arr.at[pl.ds(row0, BM), :], acc_ref.at[sl], sem_x)

    def x_chunk_copy(arr, row0, cs):
        return pltpu.make_async_copy(arr.at[pl.ds(row0, RCX), :], xc_ref.at[cs], sem_c.at[cs])

    def out_copy(arr, row0, sl):
        return pltpu.make_async_copy(acc_ref.at[sl], arr.at[pl.ds(row0, BM), :], sem_o.at[sl])

    wait_x_tile = lambda sl: x_tile_copy(x_hbms[0], 0, sl).wait()
    wait_x_chunk = lambda cs: x_chunk_copy(x_hbms[0], 0, cs).wait()
    wait_out = lambda sl: out_copy(out_hbms[0], 0, sl).wait()

    def start_x_chunk(q):
        per_stream(x_hbms, i, lambda arr, row0: x_chunk_copy(
            arr, pl.multiple_of(row0 + q * RCX, RCX), q % 2).start())

    @pl.when(jnp.logical_and(i == 0, j == 0))
    def _():
        x_tile_copy(x_hbms[0], 0, 0).start()

    @pl.when(j == 0)
    def _():
        wait_x_tile(slot)

    @pl.when(j == FFN_PREFETCH_STEP)
    def _():
        pl.when(i >= 1)(lambda: wait_out(1 - slot))

        @pl.when(i + 1 < n_tiles)
        def _():
            per_stream(x_hbms, i + 1, lambda arr, row0: x_tile_copy(arr, row0, 1 - slot).start())

    @pl.when(j == last - 1)
    def _():
        start_x_chunk(0)

    def step(first, final):
        acc_slot = acc_ref.at[slot]
        wg = wg_ref[...].astype(BF16)
        wu = wu_ref[...].astype(BF16)
        wd = wd_ref[...].astype(BF16)
        for r in range(BM // RC):
            rows = slice(r * RC, (r + 1) * RC)
            if first:
                shift = mod_ref[:, mo * D_MODEL:(mo + 1) * D_MODEL]
                scale = mod_ref[:, (mo + 1) * D_MODEL:(mo + 2) * D_MODEL]
                hn = _rms(acc_slot[rows, :], npre_ref[...])
                h = (hn * (1.0 + scale) + shift).astype(BF16)
                h_ref[rows, :] = h
            else:
                h = h_ref[rows, :]
            g = _dot(h, wg)
            u = _dot(h, wu)
            a = (g * jax.nn.sigmoid(g) * u).astype(BF16)
            acc = _dot(a, wd)
            if not first:
                acc = acc_slot[rows, :] + acc
            if not final:
                acc_slot[rows, :] = acc
                continue
            gate = mod_ref[:, (mo + 2) * D_MODEL:(mo + 3) * D_MODEL]
            for qq in range(nq):
                q = r * nq + qq
                wait_x_chunk(q % 2)
                if q + 1 < BM // RCX:
                    start_x_chunk(q + 1)
                sub = slice(qq * RCX, (qq + 1) * RCX)
                y = xc_ref[q % 2] + (HALF_STEP * gate) * _rms(acc[sub, :], npost_ref[...])
                acc_slot[r * RC + qq * RCX:r * RC + (qq + 1) * RCX, :] = y
        if final:
            per_stream(out_hbms, i, lambda arr, row0: out_copy(arr, row0, slot).start())

    pl.when(j == 0)(functools.partial(step, True, False))
    pl.when(jnp.logical_and(j > 0, j < last))(functools.partial(step, False, False))
    pl.when(j == last)(functools.partial(step, False, True))

    @pl.when(jnp.logical_and(i == n_tiles - 1, j == last))
    def _():
        wait_out(slot)


def _ffn(xs, mods4, norm_pre4, norm_post4, ffn_gate, ffn_up, ffn_down, l, s, *, split_out=False):
    mo = 6 * s
    ni = 2 * s
    in_specs = [pl.BlockSpec(memory_space=pl.ANY)] * len(xs) + [
        pl.BlockSpec((None, None, 1, N_MOD * D_MODEL), lambda i, j: (l, _mod_index(i, BM), 0, 0)),
        pl.BlockSpec((None, None, 1, D_MODEL), lambda i, j: (l, ni, 0, 0)),
        pl.BlockSpec((None, None, 1, D_MODEL), lambda i, j: (l, ni, 0, 0)),
        pl.BlockSpec((None, None, D_MODEL, BF), lambda i, j: (l, s, 0, j)),
        pl.BlockSpec((None, None, D_MODEL, BF), lambda i, j: (l, s, 0, j)),
        pl.BlockSpec((None, None, BF, D_MODEL), lambda i, j: (l, s, j, 0)),
    ]
    out_rows = (T_CTX, T_LAT) if split_out else (T_ALL,)
    outs = pl.pallas_call(
        functools.partial(_ffn_kernel, mo=mo, n_x=len(xs), n_out=len(out_rows)),
        grid=(T_ALL // BM, D_FF // BF),
        in_specs=in_specs,
        out_specs=[pl.BlockSpec(memory_space=pl.ANY)] * len(out_rows),
        out_shape=[jax.ShapeDtypeStruct((r, D_MODEL), F32) for r in out_rows],
        scratch_shapes=[
            pltpu.VMEM((2, BM, D_MODEL), F32),
            pltpu.VMEM((2, RCX, D_MODEL), F32),
            pltpu.VMEM((BM, D_MODEL), BF16),
            pltpu.SemaphoreType.DMA(()),
            pltpu.SemaphoreType.DMA((2,)),
            pltpu.SemaphoreType.DMA((2,)),
        ],
        compiler_params=pltpu.CompilerParams(
            dimension_semantics=("arbitrary", "arbitrary"), vmem_limit_bytes=VMEM_FFN),
        name=f"ffn_l{l}_s{s}",
    )(*xs, mods4, norm_pre4, norm_post4, ffn_gate, ffn_up, ffn_down)
    return outs if split_out else outs[0]


def _rope(y, cos, sins):
    lane = lax.broadcasted_iota(jnp.int32, y.shape, 1)
    first = (lane & 63) < 32
    partner = jnp.where(first, pltpu.roll(y, 96, 1), pltpu.roll(y, 32, 1))
    return y * cos + partner * sins


_IN_SEGMENTS = (
    (COL_AQ, A_HEADS, "q", True, None),
    (COL_AK, A_KV, "k", True, 0),
    (COL_AV, A_KV, None, False, 1),
    (COL_CQ, C_HEADS, None, True, None),
    (COL_CK, C_KV, None, True, 2),
    (COL_CV, C_KV, None, False, 3),
)


def _inproj_kernel(x_ref, mod_ref, npre_ref, qn_ref, kn_ref, cos_ref, sin_ref, w_ref, *rest, first_layer):
    qkvu_ref, uf_ref = rest[-6], rest[-5]
    cache_refs = rest[-4:]
    nb = BM_IN // CTX_L
    if first_layer:
        cache_refs = [c.at[:, 0] for c in rest[-4:]]

    def body(lat):
        shift = mod_ref[:, 3 * D_MODEL:4 * D_MODEL]
        scale = mod_ref[:, 4 * D_MODEL:5 * D_MODEL]
        h = (_rms(x_ref[...], npre_ref[...]) * (1.0 + scale) + shift).astype(BF16)
        for col0, heads, norm, rot, cache in _IN_SEGMENTS:
            p = _dot(h, w_ref[:, col0:col0 + heads * HD])
            for k in range(heads):
                y = p[:, k * HD:(k + 1) * HD]
                if norm == "q":
                    y = _rms(y, qn_ref[...])
                elif norm == "k":
                    y = _rms(y, kn_ref[...])
                if rot and lat:
                    y = _rope(y, cos_ref[...], sin_ref[...])
                qkvu_ref[:, col0 + k * HD:col0 + (k + 1) * HD] = y.astype(BF16)
                if cache is not None and not lat:
                    cache_refs[cache][:, :, k, :] = y.reshape(nb, CTX_L, HD)
        u = _dot(h, w_ref[:, COL_U:])
        uf_ref[...] = u
        qkvu_ref[:, COL_U:] = u.astype(BF16)
        if first_layer and not lat:
            for c in rest[-4:]:
                c[:, 1:] = jnp.zeros((nb, DEPTH - 1, CTX_L, A_KV, HD), F32)

    is_lat = pl.program_id(0) >= T_CTX // BM_IN
    pl.when(is_lat)(functools.partial(body, True))
    pl.when(jnp.logical_not(is_lat))(functools.partial(body, False))


def _inproj(x, mods4, norm_pre4, q_norm3, k_norm3, cos_t, sin_t, w_in_bf, l, prev_caches):
    bm = BM_IN
    nct = T_CTX // bm
    nb = bm // CTX_L
    tab_idx = lambda i: (jnp.maximum(i - nct, 0) % (LAT_L // bm), 0)
    cache_shape = jax.ShapeDtypeStruct((CTX_B, DEPTH, CTX_L, A_KV, HD), F32)
    first_layer = not prev_caches
    if first_layer:
        cache_spec = pl.BlockSpec((nb, DEPTH, CTX_L, A_KV, HD), lambda i: (jnp.minimum(i, nct - 1), 0, 0, 0, 0))
    else:
        cache_spec = pl.BlockSpec((nb, None, CTX_L, A_KV, HD), lambda i: (jnp.minimum(i, nct - 1), l, 0, 0, 0))
    n_in = 8
    return pl.pallas_call(
        functools.partial(_inproj_kernel, first_layer=first_layer),
        grid=(T_ALL // bm,),
        in_specs=[
            pl.BlockSpec((bm, D_MODEL), lambda i: (i, 0)),
            pl.BlockSpec((None, None, 1, N_MOD * D_MODEL), lambda i: (l, _mod_index(i, bm), 0, 0)),
            pl.BlockSpec((None, None, 1, D_MODEL), lambda i: (l, 1, 0, 0)),
            pl.BlockSpec((None, 1, HD), lambda i: (l, 0, 0)),
            pl.BlockSpec((None, 1, HD), lambda i: (l, 0, 0)),
            pl.BlockSpec((bm, HD), tab_idx),
            pl.BlockSpec((bm, HD), tab_idx),
            pl.BlockSpec((None, D_MODEL, IN_WIDTH), lambda i: (l, 0, 0), pipeline_mode=pl.Buffered(1)),
        ] + [pl.BlockSpec(memory_space=pl.ANY)] * len(prev_caches),
        out_specs=[
            pl.BlockSpec((bm, IN_WIDTH), lambda i: (i, 0)),
            pl.BlockSpec((bm, S5_CH), lambda i: (i, 0)),
        ] + [cache_spec] * 4,
        out_shape=[
            jax.ShapeDtypeStruct((T_ALL, IN_WIDTH), BF16),
            jax.ShapeDtypeStruct((T_ALL, S5_CH), F32),
        ] + [cache_shape] * 4,
        input_output_aliases={n_in + k: 2 + k for k in range(len(prev_caches))},
        compiler_params=pltpu.CompilerParams(
            dimension_semantics=("arbitrary",), vmem_limit_bytes=VMEM_BIG),
        name=f"inproj_l{l}",
    )(x, mods4, norm_pre4, q_norm3, k_norm3, cos_t, sin_t, w_in_bf, *prev_caches)


def _softmax_pv(dots, values, sink=None):
    m = functools.reduce(jnp.maximum, [jnp.max(d, axis=-1, keepdims=True) for d in dots]) * SCALE
    if sink is not None:
        m = jnp.maximum(m, sink)
    m2 = m * LOG2E
    ps = [jnp.exp2(d * (SCALE * LOG2E) - m2) for d in dots]
    den = functools.reduce(jnp.add, [jnp.sum(p, axis=-1, keepdims=True) for p in ps])
    if sink is not None:
        den = den + jnp.exp2(sink * LOG2E - m2)
    o = functools.reduce(jnp.add, [_dot(p.astype(BF16), v) for p, v in zip(ps, values)])
    return o / den


def _attn_ctx_kernel(sink_ref, aq_ref, ak_ref, av_ref, cq_ref, ck_ref, cv_ref, ao_ref, co_ref):
    kv = pl.program_id(1)
    k = ak_ref[...]
    v = av_ref[...]
    for h in range(A_HEADS // A_KV):
        cols = slice(h * HD, (h + 1) * HD)
        s = _dot_nt(aq_ref[:, cols], k)
        ao_ref[:, cols] = _softmax_pv([s], [v]).astype(BF16)
    k = ck_ref[...]
    v = cv_ref[...]
    g = C_HEADS // C_KV
    for h in range(g):
        cols = slice(h * HD, (h + 1) * HD)
        s = _dot_nt(cq_ref[:, cols], k)
        co_ref[:, cols] = _softmax_pv([s], [v], sink_ref[kv * g + h]).astype(BF16)


def _attn_ctx(qkvu, sink_l, l):
    ga = A_HEADS // A_KV * HD
    gc = C_HEADS // C_KV * HD
    blk = lambda width, col0: pl.BlockSpec((CTX_L, width), lambda b, k: (b, col0 // width + k))
    return pl.pallas_call(
        _attn_ctx_kernel,
        grid=(CTX_B, A_KV),
        in_specs=[
            pl.BlockSpec(memory_space=pltpu.SMEM),
            blk(ga, COL_AQ), blk(HD, COL_AK), blk(HD, COL_AV),
            blk(gc, COL_CQ), blk(HD, COL_CK), blk(HD, COL_CV),
        ],
        out_specs=[
            pl.BlockSpec((CTX_L, ga), lambda b, k: (b, k)),
            pl.BlockSpec((CTX_L, gc), lambda b, k: (b, k)),
        ],
        out_shape=[
            jax.ShapeDtypeStruct((T_CTX, A_HEADS * HD), BF16),
            jax.ShapeDtypeStruct((T_CTX, C_HEADS * HD), BF16),
        ],
        compiler_params=pltpu.CompilerParams(dimension_semantics=("arbitrary", "arbitrary")),
        name=f"attn_ctx_l{l}",
    )(sink_l, qkvu, qkvu, qkvu, qkvu, qkvu, qkvu)


def _attn_lat_a_kernel(q_ref, k_ref, v_ref, kc_ref, vc_ref, o_ref):
    k = k_ref[...]
    v = v_ref[...]
    kc = kc_ref[...].astype(BF16)
    vc = vc_ref[...].astype(BF16)
    for h in range(A_HEADS // A_KV):
        cols = slice(h * HD, (h + 1) * HD)
        q = q_ref[:, cols]
        s1 = _dot_nt(q, k)
        s2 = _dot_nt(q, kc)
        o_ref[:, cols] = _softmax_pv([s1, s2], [v, vc]).astype(BF16)


def _attn_lat_a(qkvu, cache_k4, cache_v4, l):
    ga = A_HEADS // A_KV * HD
    nq = LAT_L // BQ_A
    row0 = T_CTX // BQ_A
    lat_blk = T_CTX // LAT_L
    return pl.pallas_call(
        _attn_lat_a_kernel,
        grid=(LAT_B, A_KV, nq),
        in_specs=[
            pl.BlockSpec((BQ_A, ga), lambda b, k, q: (row0 + b * nq + q, COL_AQ // ga + k)),
            pl.BlockSpec((LAT_L, HD), lambda b, k, q: (lat_blk + b, COL_AK // HD + k)),
            pl.BlockSpec((LAT_L, HD), lambda b, k, q: (lat_blk + b, COL_AV // HD + k)),
            pl.BlockSpec((None, None, PAST, HD), lambda b, k, q: (b, l, 0, k)),
            pl.BlockSpec((None, None, PAST, HD), lambda b, k, q: (b, l, 0, k)),
        ],
        out_specs=pl.BlockSpec((BQ_A, ga), lambda b, k, q: (b * nq + q, k)),
        out_shape=jax.ShapeDtypeStruct((T_LAT, A_HEADS * HD), BF16),
        compiler_params=pltpu.CompilerParams(
            dimension_semantics=("arbitrary", "arbitrary", "arbitrary"), vmem_limit_bytes=VMEM_BIG),
        name=f"attn_lat_a_l{l}",
    )(qkvu, qkvu, qkvu, cache_k4, cache_v4)


def _attn_lat_c_kernel(sink_ref, q_ref, k_ref, v_ref, kc_ref, vc_ref, o_ref):
    kv = pl.program_id(1)
    n = pl.program_id(2)
    span = BQ_C + 2 * WINDOW
    start = pl.multiple_of(jnp.clip(n * BQ_C - WINDOW, 0, LAT_L - span), WINDOW)
    kw = k_ref[pl.ds(start, span), :]
    vw = v_ref[pl.ds(start, span), :]
    kc = kc_ref[...].astype(BF16)
    vc = vc_ref[...].astype(BF16)
    qpos = n * BQ_C + lax.broadcasted_iota(jnp.int32, (BQ_C, span), 0)
    kpos = start + lax.broadcasted_iota(jnp.int32, (BQ_C, span), 1)
    valid = jnp.abs(qpos - kpos) <= WINDOW
    g = C_HEADS // C_KV
    for h in range(g):
        cols = slice(h * HD, (h + 1) * HD)
        q = q_ref[:, cols]
        s1 = jnp.where(valid, _dot_nt(q, kw), NEG_INF)
        s2 = _dot_nt(q, kc)
        o_ref[:, cols] = _softmax_pv([s1, s2], [vw, vc], sink_ref[kv * g + h]).astype(BF16)


def _attn_lat_c(qkvu, sink_l, cache_k4, cache_v4, l):
    gc = C_HEADS // C_KV * HD
    nq = LAT_L // BQ_C
    row0 = T_CTX // BQ_C
    lat_blk = T_CTX // LAT_L
    return pl.pallas_call(
        _attn_lat_c_kernel,
        grid=(LAT_B, C_KV, nq),
        in_specs=[
            pl.BlockSpec(memory_space=pltpu.SMEM),
            pl.BlockSpec((BQ_C, gc), lambda b, k, q: (row0 + b * nq + q, COL_CQ // gc + k)),
            pl.BlockSpec((LAT_L, HD), lambda b, k, q: (lat_blk + b, COL_CK // HD + k)),
            pl.BlockSpec((LAT_L, HD), lambda b, k, q: (lat_blk + b, COL_CV // HD + k)),
            pl.BlockSpec((None, None, PAST, HD), lambda b, k, q: (b, l, 0, k)),
            pl.BlockSpec((None, None, PAST, HD), lambda b, k, q: (b, l, 0, k)),
        ],
        out_specs=pl.BlockSpec((BQ_C, gc), lambda b, k, q: (b * nq + q, k)),
        out_shape=jax.ShapeDtypeStruct((T_LAT, C_HEADS * HD), BF16),
        compiler_params=pltpu.CompilerParams(
            dimension_semantics=("arbitrary", "arbitrary", "arbitrary")),
        name=f"attn_lat_c_l{l}",
    )(sink_l, qkvu, qkvu, qkvu, cache_k4, cache_v4)


def _s5_param_kernel(rows_ref, bc_ref, w1_ref, w2_ref, a8_ref):
    a8_ref[...] = jnp.zeros_like(a8_ref)
    row_grp = lax.broadcasted_iota(jnp.int32, (128, S5_ST), 0) // S5_GC
    lane_grp = lax.broadcasted_iota(jnp.int32, (128, S5_ST), 1) // S5_P
    on_diag = row_grp == lane_grp

    def expand(a):
        return jnp.where(on_diag, jnp.concatenate([a] * S5_OG, axis=1), 0.0)

    taps = []
    for d in range(2):
        lr = rows_ref[0, d]
        li = rows_ref[1, d]
        dt = jnp.exp(rows_ref[2, d])
        mag = jnp.exp(lr * dt)
        ar = mag * jnp.cos(li * dt)
        ai = mag * jnp.sin(li * dt)
        den = lr * lr + li * li
        n_re = ar - 1.0
        f_re = (n_re * lr + ai * li) / den
        f_im = (ai * lr - n_re * li) / den
        pw = [(jnp.ones_like(ar), jnp.zeros_like(ar))]
        for _ in range(S5_T):
            pr, pi = pw[-1]
            pw.append((pr * ar - pi * ai, pr * ai + pi * ar))
        br, bi, cr, ci = (expand(bc_ref[k, d]) for k in range(4))
        bbr = f_re * br - f_im * bi
        bbi = f_re * bi + f_im * br
        xr_blocks, xi_blocks = [], []
        for t in range(S5_T):
            rows = slice(t * 128, (t + 1) * 128)
            pr, pi = pw[S5_T - 1 - t if d == 0 else t]
            xr = pr * bbr - pi * bbi
            xi = pr * bbi + pi * bbr
            c0 = S5_ROW + 2 * S5_ST * d
            w1_ref[rows, c0:c0 + S5_ST] = xr.astype(BF16)
            w1_ref[rows, c0 + S5_ST:c0 + 2 * S5_ST] = xi.astype(BF16)
            xr_blocks.append(xr)
            xi_blocks.append(xi)
            pr, pi = pw[t + 1 if d == 0 else S5_T - t]
            c0 = 2 * S5_ST * d
            w2_ref[rows, c0:c0 + S5_ST] = (cr * pr - ci * pi).astype(BF16)
            w2_ref[rows, c0 + S5_ST:c0 + 2 * S5_ST] = (-(cr * pi + ci * pr)).astype(BF16)
        a8_ref[2 * d:2 * d + 1, :] = pw[S5_T][0]
        a8_ref[2 * d + 1:2 * d + 2, :] = pw[S5_T][1]
        xr_all = jnp.concatenate(xr_blocks, axis=0)
        xi_all = jnp.concatenate(xi_blocks, axis=0)
        taps.append(_dot_nt_split(xr_all, cr) - _dot_nt_split(xi_all, ci))
    tf, tb = taps
    for t in range(S5_T):
        for t2 in range(S5_T):
            blk = None
            if t2 >= t:
                r0 = (S5_T - 1 - (t2 - t)) * 128
                blk = tf[r0:r0 + 128, :]
            if t2 <= t:
                r0 = (t - t2) * 128
                b2 = tb[r0:r0 + 128, :]
                blk = b2 if blk is None else blk + b2
            w1_ref[t * 128:(t + 1) * 128, t2 * 128:(t2 + 1) * 128] = blk.astype(BF16)


def _s5_params(s5_lam_re, s5_lam_im, s5_log_step, s5_b_re, s5_b_im, s5_c_re, s5_c_im):
    rows = jnp.stack([s5_lam_re, s5_lam_im, jnp.repeat(s5_log_step[..., None], S5_P, axis=-1)], axis=0)
    rows = rows.reshape(3, DEPTH, 2, S5_OCT, 1, S5_ST)
    bc = jnp.stack([jnp.swapaxes(s5_b_re, -1, -2), jnp.swapaxes(s5_b_im, -1, -2), s5_c_re, s5_c_im], axis=0)
    bc = bc.reshape(4, DEPTH, 2, S5_OCT, 128, S5_P)
    spec = lambda n, r, c: pl.BlockSpec((n, None, 2, None, r, c), lambda l, s: (0, l, 0, s, 0, 0))
    return pl.pallas_call(
        _s5_param_kernel,
        grid=(DEPTH, S5_OCT),
        in_specs=[spec(3, 1, S5_ST), spec(4, 128, S5_P)],
        out_specs=[
            pl.BlockSpec((None, None, S5_ROW, S5_W1), lambda l, s: (l, s, 0, 0)),
            pl.BlockSpec((None, None, S5_ROW, 4 * S5_ST), lambda l, s: (l, s, 0, 0)),
            pl.BlockSpec((None, None, 8, S5_ST), lambda l, s: (l, s, 0, 0)),
        ],
        out_shape=[
            jax.ShapeDtypeStruct((DEPTH, S5_OCT, S5_ROW, S5_W1), BF16),
            jax.ShapeDtypeStruct((DEPTH, S5_OCT, S5_ROW, 4 * S5_ST), BF16),
            jax.ShapeDtypeStruct((DEPTH, S5_OCT, 8, S5_ST), F32),
        ],
        compiler_params=pltpu.CompilerParams(
            dimension_semantics=("arbitrary", "arbitrary"), vmem_limit_bytes=VMEM_MID),
        name="s5_params",
    )(rows, bc)


def _s5_mix_kernel(u_ref, w1_ref, w2_ref, a8_ref, h0_ref, y_ref, hfin_ref, lhs_ref, a_ref, hp_ref, y8_ref):
    def stream(nb, nc, is_ctx):
        for b in range(nb):
            for t in range(S5_T):
                lhs_ref[t, pl.ds(b, nc, stride=nb), :] = u_ref[pl.ds(b * nc * S5_T + t, nc, stride=S5_T), :]
        lhs = jnp.concatenate([lhs_ref[t] for t in range(S5_T)], axis=1).astype(BF16)
        a_ref[...] = _dot(lhs, w1_ref[...])

        coef = [a8_ref[k:k + 1, :] for k in range(4)]
        if is_ctx:
            init = tuple(jnp.zeros((nb, S5_ST), F32) for _ in range(4))
        else:
            init = tuple(h0_ref[k] for k in range(4))

        def step(c, st):
            new = []
            for d in range(2):
                cc = c if d == 0 else nc - 1 - c
                rows = pl.ds(cc * nb, nb) if isinstance(c, int) else pl.ds(pl.multiple_of(cc * nb, nb), nb)
                hr, hi = st[2 * d], st[2 * d + 1]
                hp_ref[rows, 2 * S5_ST * d:2 * S5_ST * d + S5_ST] = hr
                hp_ref[rows, 2 * S5_ST * d + S5_ST:2 * S5_ST * (d + 1)] = hi
                c0 = S5_ROW + 2 * S5_ST * d
                gr = a_ref[rows, c0:c0 + S5_ST]
                gi = a_ref[rows, c0 + S5_ST:c0 + 2 * S5_ST]
                ar, ai = coef[2 * d], coef[2 * d + 1]
                new += [ar * hr - ai * hi + gr, ar * hi + ai * hr + gi]
            return tuple(new)

        if nb % 8 == 0:
            fin = lax.fori_loop(0, nc, step, init, unroll=S5_UNROLL)
        else:
            fin = init
            for c in range(nc):
                fin = step(c, fin)
        if is_ctx:
            for k in range(4):
                hfin_ref[k] = fin[k]

        y8 = a_ref[:, 0:S5_ROW] + _dot_nt(hp_ref[...].astype(BF16), w2_ref[...])
        for t in range(S5_T):
            y8_ref[t] = y8[:, t * 128:(t + 1) * 128]
        for b in range(nb):
            for t in range(S5_T):
                y_ref[pl.ds(b * nc * S5_T + t, nc, stride=S5_T), :] = y8_ref[t, pl.ds(b, nc, stride=nb), :]

    pl.when(pl.program_id(1) == 0)(functools.partial(stream, CTX_B, CTX_NC, True))
    pl.when(pl.program_id(1) == 1)(functools.partial(stream, LAT_B, LAT_NC, False))


def _s5_mix(uf, w1, w2, a8, h0, l):
    return pl.pallas_call(
        _s5_mix_kernel,
        grid=(S5_OCT, 2),
        in_specs=[
            pl.BlockSpec((T_CTX, 128), lambda s, k: (k, s)),
            pl.BlockSpec((None, None, S5_ROW, S5_W1), lambda s, k: (l, s, 0, 0)),
            pl.BlockSpec((None, None, S5_ROW, 4 * S5_ST), lambda s, k: (l, s, 0, 0)),
            pl.BlockSpec((None, None, 8, S5_ST), lambda s, k: (l, s, 0, 0)),
            pl.BlockSpec((None, 4, LAT_B, S5_ST), lambda s, k: (s, 0, 0, 0)),
        ],
        out_specs=[
            pl.BlockSpec((T_CTX, 128), lambda s, k: (k, s)),
            pl.BlockSpec((None, 4, CTX_B, S5_ST), lambda s, k: (s, 0, 0, 0)),
        ],
        out_shape=[
            jax.ShapeDtypeStruct((T_ALL, S5_CH), F32),
            jax.ShapeDtypeStruct((S5_OCT, 4, CTX_B, S5_ST), F32),
        ],
        scratch_shapes=[
            pltpu.VMEM((S5_T, S5_ROWS, 128), F32),
            pltpu.VMEM((S5_ROWS, S5_W1), F32),
            pltpu.VMEM((S5_ROWS, 4 * S5_ST), F32),
            pltpu.VMEM((S5_T, S5_ROWS, 128), F32),
        ],
        compiler_params=pltpu.CompilerParams(
            dimension_semantics=("arbitrary", "arbitrary"), vmem_limit_bytes=VMEM_BIG),
        name=f"s5_mix_l{l}",
    )(uf, w1, w2, a8, h0)


def _glu_kernel(y_ref, u_ref, d_ref, w_ref, b_ref, o_ref):
    y = y_ref[...] + d_ref[...] * u_ref[...]
    z = y * (0.5 * (1.0 + jnp.tanh(math.sqrt(2.0 / math.pi) * (y + 0.044715 * (y * y * y)))))
    t = _dot(z.astype(BF16), w_ref[...].astype(BF16)) + b_ref[...]
    o_ref[...] = (z * jax.nn.sigmoid(t)).astype(BF16)


def _glu(y, uf, s5_d3, w_glu, b_glu3, l):
    return pl.pallas_call(
        _glu_kernel,
        grid=(T_ALL // BM,),
        in_specs=[
            pl.BlockSpec((BM, S5_CH), lambda i: (i, 0)),
            pl.BlockSpec((BM, S5_CH), lambda i: (i, 0)),
            pl.BlockSpec((None, 1, S5_CH), lambda i: (l, 0, 0)),
            pl.BlockSpec((None, S5_CH, S5_CH), lambda i: (l, 0, 0)),
            pl.BlockSpec((None, 1, S5_CH), lambda i: (l, 0, 0)),
        ],
        out_specs=pl.BlockSpec((BM, S5_CH), lambda i: (i, 0)),
        out_shape=jax.ShapeDtypeStruct((T_ALL, S5_CH), BF16),
        compiler_params=pltpu.CompilerParams(dimension_semantics=("arbitrary",)),
        name=f"s5_glu_l{l}",
    )(y, uf, s5_d3, w_glu, b_glu3)


def _outproj_kernel(x_ref, a_ctx_ref, a_lat_ref, c_ctx_ref, c_lat_ref, s_ref, mod_ref, npost_ref, w_ref,
                    o_ref, wb_ref):
    @pl.when(pl.program_id(0) == 0)
    def _():
        nk = 4
        rk = D_MODEL // nk
        for r in range(nk):
            wb_ref[r * rk:(r + 1) * rk, :] = w_ref[r * rk:(r + 1) * rk, :].astype(BF16)

    na = A_HEADS * HD
    nc = na + C_HEADS * HD

    def body(a_ref, c_ref):
        gate = mod_ref[:, 5 * D_MODEL:6 * D_MODEL]
        for r in range(BM_OUT // RC_OUT):
            rows = slice(r * RC_OUT, (r + 1) * RC_OUT)
            y = (_dot(a_ref[rows, :], wb_ref[0:na, :]) + _dot(c_ref[rows, :], wb_ref[na:nc, :])
                 + _dot(s_ref[rows, :], wb_ref[nc:, :]))
            o_ref[rows, :] = x_ref[rows, :] + gate * _rms(y, npost_ref[...])

    is_lat = pl.program_id(0) >= T_CTX // BM_OUT
    pl.when(jnp.logical_not(is_lat))(functools.partial(body, a_ctx_ref, c_ctx_ref))
    pl.when(is_lat)(functools.partial(body, a_lat_ref, c_lat_ref))


def _outproj(x, a_ctx, a_lat, c_ctx, c_lat, s_out, mods4, norm_post4, w_out, l):
    bm = BM_OUT
    nct = T_CTX // bm
    ctx_idx = lambda i: (jnp.minimum(i, nct - 1), 0)
    lat_idx = lambda i: (jnp.maximum(i - nct, 0), 0)
    return pl.pallas_call(
        _outproj_kernel,
        grid=(T_ALL // bm,),
        in_specs=[
            pl.BlockSpec((bm, D_MODEL), lambda i: (i, 0)),
            pl.BlockSpec((bm, A_HEADS * HD), ctx_idx),
            pl.BlockSpec((bm, A_HEADS * HD), lat_idx),
            pl.BlockSpec((bm, C_HEADS * HD), ctx_idx),
            pl.BlockSpec((bm, C_HEADS * HD), lat_idx),
            pl.BlockSpec((bm, S5_CH), lambda i: (i, 0)),
            pl.BlockSpec((None, None, 1, N_MOD * D_MODEL), lambda i: (l, _mod_index(i, bm), 0, 0)),
            pl.BlockSpec((None, None, 1, D_MODEL), lambda i: (l, 1, 0, 0)),
            pl.BlockSpec((None, D_MODEL, D_MODEL), lambda i: (l, 0, 0), pipeline_mode=pl.Buffered(1)),
        ],
        out_specs=pl.BlockSpec((bm, D_MODEL), lambda i: (i, 0)),
        out_shape=jax.ShapeDtypeStruct((T_ALL, D_MODEL), F32),
        scratch_shapes=[pltpu.VMEM((D_MODEL, D_MODEL), BF16)],
        compiler_params=pltpu.CompilerParams(
            dimension_semantics=("arbitrary",), vmem_limit_bytes=VMEM_BIG),
        name=f"outproj_l{l}",
    )(x, a_ctx, a_lat, c_ctx, c_lat, s_out, mods4, norm_post4, w_out)


def _rope_tables():
    rows = LAT_L // GRID_W
    row = jnp.repeat(jnp.arange(rows, dtype=F32), GRID_W)
    col = jnp.tile(jnp.arange(GRID_W, dtype=F32), rows)
    axis_dim = HD // 2
    inv_freq = ROPE_BASE ** (-jnp.arange(0, axis_dim, 2, dtype=F32) / axis_dim)
    ang_row = row[:, None] * inv_freq
    ang_col = col[:, None] * inv_freq
    cr, sr = jnp.cos(ang_row), jnp.sin(ang_row)
    cc, sc = jnp.cos(ang_col), jnp.sin(ang_col)
    cos_t = jnp.concatenate([cr, cr, cc, cc], axis=-1)
    sin_t = jnp.concatenate([-sr, sr, -sc, sc], axis=-1)
    return cos_t, sin_t


def kernel(x_prompt, x_sample, cache_a_k, cache_a_v, cache_c_k, cache_c_v, state_ssm_re, state_ssm_im,
           c, c_ctx, w_mod, b_mod, norm_pre, norm_post, ffn_gate, ffn_up, ffn_down, w_in, w_out,
           q_norm, k_norm, sink, s5_lam_re, s5_lam_im, s5_log_step, s5_b_re, s5_b_im, s5_c_re, s5_c_im,
           s5_d, w_glu, b_glu):
    cvec8 = jnp.concatenate([c_ctx[None, :], c, jnp.zeros((8 - 1 - LAT_B, D_MODEL), F32)], axis=0)
    mods4 = _modulation(cvec8, w_mod, b_mod).reshape(DEPTH, 8, 1, N_MOD * D_MODEL)
    norm_pre4 = norm_pre.reshape(DEPTH, 3, 1, D_MODEL)
    norm_post4 = norm_post.reshape(DEPTH, 3, 1, D_MODEL)
    q_norm3 = q_norm.reshape(DEPTH, 1, HD)
    k_norm3 = k_norm.reshape(DEPTH, 1, HD)
    s5_d3 = s5_d.reshape(DEPTH, 1, S5_CH)
    b_glu3 = b_glu.reshape(DEPTH, 1, S5_CH)
    cos_t, sin_t = _rope_tables()
    w1, w2, a8 = _s5_params(s5_lam_re, s5_lam_im, s5_log_step, s5_b_re, s5_b_im, s5_c_re, s5_c_im)
    kv4 = lambda a: a.reshape(LAT_B, DEPTH, PAST, A_KV * HD)
    cak, cav, cck, ccv = kv4(cache_a_k), kv4(cache_a_v), kv4(cache_c_k), kv4(cache_c_v)
    h0_all = jnp.stack([state_ssm_re[:, :, 0], state_ssm_im[:, :, 0],
                        state_ssm_re[:, :, 1], state_ssm_im[:, :, 1]], axis=0)
    h0_all = h0_all.reshape(4, LAT_B, DEPTH, S5_OCT, S5_ST).transpose(2, 3, 0, 1, 4)

    w_in_bf = w_in.astype(BF16)
    ffn_w = (mods4, norm_pre4, norm_post4, ffn_gate, ffn_up, ffn_down)
    new_caches = ()
    new_state = []
    xs = [x_prompt.reshape(T_CTX, D_MODEL), x_sample.reshape(T_LAT, D_MODEL)]
    for l in range(DEPTH):
        x = _ffn(xs, *ffn_w, l, 0)

        qkvu, uf, *new_caches = _inproj(x, mods4, norm_pre4, q_norm3, k_norm3, cos_t, sin_t, w_in_bf, l,
                                        new_caches)
        a_ctx, c_ctx = _attn_ctx(qkvu, sink[l], l)
        a_lat = _attn_lat_a(qkvu, cak, cav, l)
        c_lat = _attn_lat_c(qkvu, sink[l], cck, ccv, l)
        y, hfin = _s5_mix(uf, w1, w2, a8, h0_all[l], l)
        s_out = _glu(y, uf, s5_d3, w_glu, b_glu3, l)

        x = _outproj(x, a_ctx, a_lat, c_ctx, c_lat, s_out, mods4, norm_post4, w_out, l)
        if l < DEPTH - 1:
            xs = [_ffn([x], *ffn_w, l, 1)]
        else:
            y_prompt, y_sample = _ffn([x], *ffn_w, l, 1, split_out=True)

        hf = hfin.reshape(S5_OCT, 4, CTX_B, S5_OG, S5_P).transpose(1, 2, 0, 3, 4).reshape(4, CTX_B, S5_G, S5_P)
        new_state.append((jnp.stack([hf[0], hf[2]], axis=1), jnp.stack([hf[1], hf[3]], axis=1)))

    y_prompt = y_prompt.reshape(CTX_B, CTX_L, D_MODEL)
    y_sample = y_sample.reshape(LAT_B, LAT_L, D_MODEL)
    caches = list(new_caches)
    st_re = jnp.stack([new_state[l][0] for l in range(DEPTH)], axis=1)
    st_im = jnp.stack([new_state[l][1] for l in range(DEPTH)], axis=1)
    return (y_prompt, y_sample, caches[0], caches[1], caches[2], caches[3], st_re, st_im)
```

```python
import functools
import math

import jax
import jax.numpy as jnp
from jax import lax
from jax.experimental import pallas as pl
from jax.experimental.pallas import tpu as pltpu

F32 = jnp.float32
BF16 = jnp.bfloat16

D_MODEL = 2048
CTX_B, CTX_L = 16, 256
LAT_B, LAT_L = 2, 2048
DEPTH = 2
PAST = 512
GRID_W = 64
HD = 128
A_HEADS, A_KV = 8, 2
C_HEADS, C_KV = 4, 2
WINDOW = 128
S5_GC = 16
S5_CH = 512
S5_G = 32
S5_P = 64
D_FF = 5632
N_MOD = 9
IN_WIDTH = 3072
ROPE_BASE = 10000.0
EPS = 1e-6
HALF_STEP = 0.5
NEG_INF = -1e30
SCALE = HD ** -0.5
LOG2E = math.log2(math.e)

T_CTX = CTX_B * CTX_L
T_LAT = LAT_B * LAT_L
T_ALL = T_CTX + T_LAT

COL_AQ, COL_AK, COL_AV = 0, 1024, 1280
COL_CQ, COL_CK, COL_CV = 1536, 2048, 2304
COL_U = 2560

S5_T = 8
S5_OCT = S5_CH // 128
S5_OG = S5_G // S5_OCT
S5_ROW = S5_T * 128
S5_ST = S5_OG * S5_P
S5_W1 = S5_ROW + 4 * S5_ST
CTX_NC = CTX_L // S5_T
LAT_NC = LAT_L // S5_T
S5_ROWS = CTX_NC * CTX_B
assert S5_ROWS == LAT_NC * LAT_B and T_CTX == T_LAT

V7X_VMEM_BYTES = 64 * 1024 * 1024
VMEM_FFN = 60 * 1024 * 1024
VMEM_BIG = 56 * 1024 * 1024
VMEM_MID = 40 * 1024 * 1024

BM = 1024
BM_OUT = 512
RC_OUT = 256
BF = 512
RC = 512
RCX = 256
FFN_PREFETCH_STEP = 2
BN_MOD = 1024
BM_IN = 512
BQ_A = 512
BQ_C = 256
NQ_C = 4
NB_CTX = 4
S5_UNROLL = 4


def _dot(a, b):
    return jnp.dot(a, b, preferred_element_type=F32)


def _dot_nt(a, b, precision=None):
    return lax.dot_general(a, b, (((1,), (1,)), ((), ())), preferred_element_type=F32, precision=precision)


def _dot_nt_split(a, b):
    ah = a.astype(BF16)
    bh = b.astype(BF16)
    al = (a - ah.astype(F32)).astype(BF16)
    bl = (b - bh.astype(F32)).astype(BF16)
    return _dot_nt(ah, bh) + (_dot_nt(ah, bl) + _dot_nt(al, bh))


def _rms(x, g):
    return x * lax.rsqrt(jnp.mean(x * x, axis=-1, keepdims=True) + EPS) * g


def _mod_index(i, bm):
    nct = T_CTX // bm
    return jnp.where(i < nct, 0, 1 + (i - nct) // (LAT_L // bm))


def _mod_kernel(c_ref, w_ref, b_ref, o_ref):
    c = c_ref[...]
    s = (c * jax.nn.sigmoid(c)).astype(BF16)
    o_ref[...] = _dot(s, w_ref[...].astype(BF16)) + b_ref[...]


def _modulation(cvec8, w_mod, b_mod):
    n = N_MOD * D_MODEL
    return pl.pallas_call(
        _mod_kernel,
        grid=(DEPTH, n // BN_MOD),
        in_specs=[
            pl.BlockSpec((8, D_MODEL), lambda l, j: (0, 0)),
            pl.BlockSpec((None, D_MODEL, BN_MOD), lambda l, j: (l, 0, j)),
            pl.BlockSpec((None, 1, BN_MOD), lambda l, j: (l, 0, j)),
        ],
        out_specs=pl.BlockSpec((None, 8, BN_MOD), lambda l, j: (l, 0, j)),
        out_shape=jax.ShapeDtypeStruct((DEPTH, 8, n), F32),
        compiler_params=pltpu.CompilerParams(
            dimension_semantics=("arbitrary", "arbitrary"), vmem_limit_bytes=VMEM_MID),
        name="modulation",
    )(cvec8, w_mod, b_mod.reshape(DEPTH, 1, n))


def _ffn_kernel(*refs, mo, n_x, n_out):
    x_hbms = refs[:n_x]
    mod_ref, npre_ref, npost_ref, wg_ref, wu_ref, wd_ref = refs[n_x:n_x + 6]
    out_hbms = refs[n_x + 6:n_x + 6 + n_out]
    acc_ref, xc_ref, h_ref, sem_x, sem_c, sem_o = refs[n_x + 6 + n_out:]
    i = pl.program_id(0)
    j = pl.program_id(1)
    n_tiles = pl.num_programs(0)
    last = pl.num_programs(1) - 1
    slot = i % 2
    nq = RC // RCX
    nct = T_CTX // BM

    def per_stream(arrays, tile, fn):
        if len(arrays) == 1:
            fn(arrays[0], pl.multiple_of(tile * BM, BM))
        else:
            pl.when(tile < nct)(lambda: fn(arrays[0], pl.multiple_of(tile * BM, BM)))
            pl.when(tile >= nct)(lambda: fn(arrays[1], pl.multiple_of((tile - nct) * BM, BM)))

    def x_tile_copy(arr, row0, sl):
        return pltpu.make_async_copy(arr.at[pl.ds(row0, BM), :], acc_ref.at[sl], sem_x)

    def x_chunk_copy(arr, row0, cs):
        return pltpu.make_async_copy(arr.at[pl.ds(row0, RCX), :], xc_ref.at[cs], sem_c.at[cs])

    def out_copy(arr, row0, sl):
        return pltpu.make_async_copy(acc_ref.at[sl], arr.at[pl.ds(row0, BM), :], sem_o.at[sl])

    wait_x_tile = lambda sl: x_tile_copy(x_hbms[0], 0, sl).wait()
    wait_x_chunk = lambda cs: x_chunk_copy(x_hbms[0], 0, cs).wait()
    wait_out = lambda sl: out_copy(out_hbms[0], 0, sl).wait()

    def start_x_chunk(q):
        per_stream(x_hbms, i, lambda arr, row0: x_chunk_copy(
            arr, pl.multiple_of(row0 + q * RCX, RCX), q % 2).start())

    @pl.when(jnp.logical_and(i == 0, j == 0))
    def _():
        x_tile_copy(x_hbms[0], 0, 0).start()

    @pl.when(j == 0)
    def _():
        wait_x_tile(slot)

    @pl.when(j == FFN_PREFETCH_STEP)
    def _():
        pl.when(i >= 1)(lambda: wait_out(1 - slot))

        @pl.when(i + 1 < n_tiles)
        def _():
            per_stream(x_hbms, i + 1, lambda arr, row0: x_tile_copy(arr, row0, 1 - slot).start())

    @pl.when(j == last - 1)
    def _():
        start_x_chunk(0)

    def step(first, final):
        acc_slot = acc_ref.at[slot]
        wg = wg_ref[...].astype(BF16)
        wu = wu_ref[...].astype(BF16)
        wd = wd_ref[...].astype(BF16)
        for r in range(BM // RC):
            rows = slice(r * RC, (r + 1) * RC)
            if first:
                shift = mod_ref[:, mo * D_MODEL:(mo + 1) * D_MODEL]
                scale = mod_ref[:, (mo + 1) * D_MODEL:(mo + 2) * D_MODEL]
                hn = _rms(acc_slot[rows, :], npre_ref[...])
                h = (hn * (1.0 + scale) + shift).astype(BF16)
                h_ref[rows, :] = h
            else:
                h = h_ref[rows, :]
            g = _dot(h, wg)
            u = _dot(h, wu)
            a = (g * jax.nn.sigmoid(g) * u).astype(BF16)
            acc = _dot(a, wd)
            if not first:
                acc = acc_slot[rows, :] + acc
            if not final:
                acc_slot[rows, :] = acc
                continue
            gate = mod_ref[:, (mo + 2) * D_MODEL:(mo + 3) * D_MODEL]
            for qq in range(nq):
                q = r * nq + qq
                wait_x_chunk(q % 2)
                if q + 1 < BM // RCX:
                    start_x_chunk(q + 1)
                sub = slice(qq * RCX, (qq + 1) * RCX)
                y = xc_ref[q % 2] + (HALF_STEP * gate) * _rms(acc[sub, :], npost_ref[...])
                acc_slot[r * RC + qq * RCX:r * RC + (qq + 1) * RCX, :] = y
        if final:
            per_stream(out_hbms, i, lambda arr, row0: out_copy(arr, row0, slot).start())

    pl.when(j == 0)(functools.partial(step, True, False))
    pl.when(jnp.logical_and(j > 0, j < last))(functools.partial(step, False, False))
    pl.when(j == last)(functools.partial(step, False, True))

    @pl.when(jnp.logical_and(i == n_tiles - 1, j == last))
    def _():
        wait_out(slot)


def _ffn(xs, mods4, norm_pre4, norm_post4, ffn_gate, ffn_up, ffn_down, l, s, *, split_out=False):
    mo = 6 * s
    ni = 2 * s
    in_specs = [pl.BlockSpec(memory_space=pl.ANY)] * len(xs) + [
        pl.BlockSpec((None, None, 1, N_MOD * D_MODEL), lambda i, j: (l, _mod_index(i, BM), 0, 0)),
        pl.BlockSpec((None, None, 1, D_MODEL), lambda i, j: (l, ni, 0, 0)),
        pl.BlockSpec((None, None, 1, D_MODEL), lambda i, j: (l, ni, 0, 0)),
        pl.BlockSpec((None, None, D_MODEL, BF), lambda i, j: (l, s, 0, j)),
        pl.BlockSpec((None, None, D_MODEL, BF), lambda i, j: (l, s, 0, j)),
        pl.BlockSpec((None, None, BF, D_MODEL), lambda i, j: (l, s, j, 0)),
    ]
    out_rows = (T_CTX, T_LAT) if split_out else (T_ALL,)
    outs = pl.pallas_call(
        functools.partial(_ffn_kernel, mo=mo, n_x=len(xs), n_out=len(out_rows)),
        grid=(T_ALL // BM, D_FF // BF),
        in_specs=in_specs,
        out_specs=[pl.BlockSpec(memory_space=pl.ANY)] * len(out_rows),
        out_shape=[jax.ShapeDtypeStruct((r, D_MODEL), F32) for r in out_rows],
        scratch_shapes=[
            pltpu.VMEM((2, BM, D_MODEL), F32),
            pltpu.VMEM((2, RCX, D_MODEL), F32),
            pltpu.VMEM((BM, D_MODEL), BF16),
            pltpu.SemaphoreType.DMA(()),
            pltpu.SemaphoreType.DMA((2,)),
            pltpu.SemaphoreType.DMA((2,)),
        ],
        compiler_params=pltpu.CompilerParams(
            dimension_semantics=("arbitrary", "arbitrary"), vmem_limit_bytes=VMEM_FFN),
        name=f"ffn_l{l}_s{s}",
    )(*xs, mods4, norm_pre4, norm_post4, ffn_gate, ffn_up, ffn_down)
    return outs if split_out else outs[0]


def _rope(y, cos, sins):
    lane = lax.broadcasted_iota(jnp.int32, y.shape, 1)
    first = (lane & 63) < 32
    partner = jnp.where(first, pltpu.roll(y, 96, 1), pltpu.roll(y, 32, 1))
    return y * cos + partner * sins


_IN_SEGMENTS = (
    (COL_AQ, A_HEADS, "q", True, None),
    (COL_AK, A_KV, "k", True, 0),
    (COL_AV, A_KV, None, False, 1),
    (COL_CQ, C_HEADS, None, True, None),
    (COL_CK, C_KV, None, True, 2),
    (COL_CV, C_KV, None, False, 3),
)


def _inproj_kernel(x_ref, mod_ref, npre_ref, qn_ref, kn_ref, cos_ref, sin_ref, w_ref, *rest, first_layer):
    qkvu_ref, uf_ref = rest[-6], rest[-5]
    cache_refs = rest[-4:]
    nb = BM_IN // CTX_L
    if first_layer:
        cache_refs = [c.at[:, 0] for c in rest[-4:]]

    def body(lat):
        shift = mod_ref[:, 3 * D_MODEL:4 * D_MODEL]
        scale = mod_ref[:, 4 * D_MODEL:5 * D_MODEL]
        h = (_rms(x_ref[...], npre_ref[...]) * (1.0 + scale) + shift).astype(BF16)
        for col0, heads, norm, rot, cache in _IN_SEGMENTS:
            p = _dot(h, w_ref[:, col0:col0 + heads * HD])
            for k in range(heads):
                y = p[:, k * HD:(k + 1) * HD]
                if norm == "q":
                    y = _rms(y, qn_ref[...])
                elif norm == "k":
                    y = _rms(y, kn_ref[...])
                if rot and lat:
                    y = _rope(y, cos_ref[...], sin_ref[...])
                qkvu_ref[:, col0 + k * HD:col0 + (k + 1) * HD] = y.astype(BF16)
                if cache is not None and not lat:
                    cache_refs[cache][:, :, k, :] = y.reshape(nb, CTX_L, HD)
        u = _dot(h, w_ref[:, COL_U:])
        uf_ref[...] = u
        qkvu_ref[:, COL_U:] = u.astype(BF16)
        if first_layer and not lat:
            for c in rest[-4:]:
                c[:, 1:] = jnp.zeros((nb, DEPTH - 1, CTX_L, A_KV, HD), F32)

    is_lat = pl.program_id(0) >= T_CTX // BM_IN
    pl.when(is_lat)(functools.partial(body, True))
    pl.when(jnp.logical_not(is_lat))(functools.partial(body, False))


def _inproj(x, mods4, norm_pre4, q_norm3, k_norm3, cos_t, sin_t, w_in_bf, l, prev_caches):
    bm = BM_IN
    nct = T_CTX // bm
    nb = bm // CTX_L
    tab_idx = lambda i: (jnp.maximum(i - nct, 0) % (LAT_L // bm), 0)
    cache_shape = jax.ShapeDtypeStruct((CTX_B, DEPTH, CTX_L, A_KV, HD), F32)
    first_layer = not prev_caches
    if first_layer:
        cache_spec = pl.BlockSpec((nb, DEPTH, CTX_L, A_KV, HD), lambda i: (jnp.minimum(i, nct - 1), 0, 0, 0, 0))
    else:
        cache_spec = pl.BlockSpec((nb, None, CTX_L, A_KV, HD), lambda i: (jnp.minimum(i, nct - 1), l, 0, 0, 0))
    n_in = 8
    return pl.pallas_call(
        functools.partial(_inproj_kernel, first_layer=first_layer),
        grid=(T_ALL // bm,),
        in_specs=[
            pl.BlockSpec((bm, D_MODEL), lambda i: (i, 0)),
            pl.BlockSpec((None, None, 1, N_MOD * D_MODEL), lambda i: (l, _mod_index(i, bm), 0, 0)),
            pl.BlockSpec((None, None, 1, D_MODEL), lambda i: (l, 1, 0, 0)),
            pl.BlockSpec((None, 1, HD), lambda i: (l, 0, 0)),
            pl.BlockSpec((None, 1, HD), lambda i: (l, 0, 0)),
            pl.BlockSpec((bm, HD), tab_idx),
            pl.BlockSpec((bm, HD), tab_idx),
            pl.BlockSpec((None, D_MODEL, IN_WIDTH), lambda i: (l, 0, 0), pipeline_mode=pl.Buffered(1)),
        ] + [pl.BlockSpec(memory_space=pl.ANY)] * len(prev_caches),
        out_specs=[
            pl.BlockSpec((bm, IN_WIDTH), lambda i: (i, 0)),
            pl.BlockSpec((bm, S5_CH), lambda i: (i, 0)),
        ] + [cache_spec] * 4,
        out_shape=[
            jax.ShapeDtypeStruct((T_ALL, IN_WIDTH), BF16),
            jax.ShapeDtypeStruct((T_ALL, S5_CH), F32),
        ] + [cache_shape] * 4,
        input_output_aliases={n_in + k: 2 + k for k in range(len(prev_caches))},
        compiler_params=pltpu.CompilerParams(
            dimension_semantics=("arbitrary",), vmem_limit_bytes=VMEM_BIG),
        name=f"inproj_l{l}",
    )(x, mods4, norm_pre4, q_norm3, k_norm3, cos_t, sin_t, w_in_bf, *prev_caches)


def _softmax_pv(dots, values, sink=None):
    m = functools.reduce(jnp.maximum, [jnp.max(d, axis=-1, keepdims=True) for d in dots]) * SCALE
    if sink is not None:
        m = jnp.maximum(m, sink)
    m2 = m * LOG2E
    ps = [jnp.exp2(d * (SCALE * LOG2E) - m2) for d in dots]
    den = functools.reduce(jnp.add, [jnp.sum(p, axis=-1, keepdims=True) for p in ps])
    if sink is not None:
        den = den + jnp.exp2(sink * LOG2E - m2)
    o = functools.reduce(jnp.add, [_dot(p.astype(BF16), v) for p, v in zip(ps, values)])
    return o / den


def _attn_ctx_kernel(sink_ref, aq_ref, ak_ref, av_ref, cq_ref, ck_ref, cv_ref, ao_ref, co_ref):
    kv = pl.program_id(1)
    g = C_HEADS // C_KV
    for b in range(NB_CTX):
        rows = slice(b * CTX_L, (b + 1) * CTX_L)
        k = ak_ref[rows, :]
        v = av_ref[rows, :]
        for h in range(A_HEADS // A_KV):
            cols = slice(h * HD, (h + 1) * HD)
            s = _dot_nt(aq_ref[rows, cols], k)
            ao_ref[rows, cols] = _softmax_pv([s], [v]).astype(BF16)
        k = ck_ref[rows, :]
        v = cv_ref[rows, :]
        for h in range(g):
            cols = slice(h * HD, (h + 1) * HD)
            s = _dot_nt(cq_ref[rows, cols], k)
            co_ref[rows, cols] = _softmax_pv([s], [v], sink_ref[kv * g + h]).astype(BF16)


def _attn_ctx(qkvu, sink_l, l):
    ga = A_HEADS // A_KV * HD
    gc = C_HEADS // C_KV * HD
    rows = NB_CTX * CTX_L
    blk = lambda width, col0: pl.BlockSpec((rows, width), lambda b, k: (b, col0 // width + k))
    return pl.pallas_call(
        _attn_ctx_kernel,
        grid=(CTX_B // NB_CTX, A_KV),
        in_specs=[
            pl.BlockSpec(memory_space=pltpu.SMEM),
            blk(ga, COL_AQ), blk(HD, COL_AK), blk(HD, COL_AV),
            blk(gc, COL_CQ), blk(HD, COL_CK), blk(HD, COL_CV),
        ],
        out_specs=[
            pl.BlockSpec((rows, ga), lambda b, k: (b, k)),
            pl.BlockSpec((rows, gc), lambda b, k: (b, k)),
        ],
        out_shape=[
            jax.ShapeDtypeStruct((T_CTX, A_HEADS * HD), BF16),
            jax.ShapeDtypeStruct((T_CTX, C_HEADS * HD), BF16),
        ],
        compiler_params=pltpu.CompilerParams(dimension_semantics=("arbitrary", "arbitrary")),
        name=f"attn_ctx_l{l}",
    )(sink_l, qkvu, qkvu, qkvu, qkvu, qkvu, qkvu)


def _attn_lat_a_kernel(q_ref, k_ref, v_ref, kc_ref, vc_ref, o_ref):
    k = k_ref[...]
    v = v_ref[...]
    kc = kc_ref[...].astype(BF16)
    vc = vc_ref[...].astype(BF16)
    for h in range(A_HEADS // A_KV):
        cols = slice(h * HD, (h + 1) * HD)
        q = q_ref[:, cols]
        s1 = _dot_nt(q, k)
        s2 = _dot_nt(q, kc)
        o_ref[:, cols] = _softmax_pv([s1, s2], [v, vc]).astype(BF16)


def _attn_lat_a(qkvu, cache_k4, cache_v4, l):
    ga = A_HEADS // A_KV * HD
    nq = LAT_L // BQ_A
    row0 = T_CTX // BQ_A
    lat_blk = T_CTX // LAT_L
    return pl.pallas_call(
        _attn_lat_a_kernel,
        grid=(LAT_B, A_KV, nq),
        in_specs=[
            pl.BlockSpec((BQ_A, ga), lambda b, k, q: (row0 + b * nq + q, COL_AQ // ga + k)),
            pl.BlockSpec((LAT_L, HD), lambda b, k, q: (lat_blk + b, COL_AK // HD + k)),
            pl.BlockSpec((LAT_L, HD), lambda b, k, q: (lat_blk + b, COL_AV // HD + k)),
            pl.BlockSpec((None, None, PAST, HD), lambda b, k, q: (b, l, 0, k)),
            pl.BlockSpec((None, None, PAST, HD), lambda b, k, q: (b, l, 0, k)),
        ],
        out_specs=pl.BlockSpec((BQ_A, ga), lambda b, k, q: (b * nq + q, k)),
        out_shape=jax.ShapeDtypeStruct((T_LAT, A_HEADS * HD), BF16),
        compiler_params=pltpu.CompilerParams(
            dimension_semantics=("arbitrary", "arbitrary", "arbitrary"), vmem_limit_bytes=VMEM_BIG),
        name=f"attn_lat_a_l{l}",
    )(qkvu, qkvu, qkvu, cache_k4, cache_v4)


def _attn_lat_c_kernel(sink_ref, q_ref, k_ref, v_ref, kc_ref, vc_ref, o_ref):
    kv = pl.program_id(1)
    span = BQ_C + 2 * WINDOW
    kc = kc_ref[...].astype(BF16)
    vc = vc_ref[...].astype(BF16)
    g = C_HEADS // C_KV
    for sub in range(NQ_C):
        n = pl.program_id(2) * NQ_C + sub
        rows = slice(sub * BQ_C, (sub + 1) * BQ_C)
        start = pl.multiple_of(jnp.clip(n * BQ_C - WINDOW, 0, LAT_L - span), WINDOW)
        kw = k_ref[pl.ds(start, span), :]
        vw = v_ref[pl.ds(start, span), :]
        qpos = n * BQ_C + lax.broadcasted_iota(jnp.int32, (BQ_C, span), 0)
        kpos = start + lax.broadcasted_iota(jnp.int32, (BQ_C, span), 1)
        valid = jnp.abs(qpos - kpos) <= WINDOW
        for h in range(g):
            cols = slice(h * HD, (h + 1) * HD)
            q = q_ref[rows, cols]
            s1 = jnp.where(valid, _dot_nt(q, kw), NEG_INF)
            s2 = _dot_nt(q, kc)
            o_ref[rows, cols] = _softmax_pv([s1, s2], [vw, vc], sink_ref[kv * g + h]).astype(BF16)


def _attn_lat_c(qkvu, sink_l, cache_k4, cache_v4, l):
    gc = C_HEADS // C_KV * HD
    bq = BQ_C * NQ_C
    nq = LAT_L // bq
    row0 = T_CTX // bq
    lat_blk = T_CTX // LAT_L
    return pl.pallas_call(
        _attn_lat_c_kernel,
        grid=(LAT_B, C_KV, nq),
        in_specs=[
            pl.BlockSpec(memory_space=pltpu.SMEM),
            pl.BlockSpec((bq, gc), lambda b, k, q: (row0 + b * nq + q, COL_CQ // gc + k)),
            pl.BlockSpec((LAT_L, HD), lambda b, k, q: (lat_blk + b, COL_CK // HD + k)),
            pl.BlockSpec((LAT_L, HD), lambda b, k, q: (lat_blk + b, COL_CV // HD + k)),
            pl.BlockSpec((None, None, PAST, HD), lambda b, k, q: (b, l, 0, k)),
            pl.BlockSpec((None, None, PAST, HD), lambda b, k, q: (b, l, 0, k)),
        ],
        out_specs=pl.BlockSpec((bq, gc), lambda b, k, q: (b * nq + q, k)),
        out_shape=jax.ShapeDtypeStruct((T_LAT, C_HEADS * HD), BF16),
        compiler_params=pltpu.CompilerParams(
            dimension_semantics=("arbitrary", "arbitrary", "arbitrary")),
        name=f"attn_lat_c_l{l}",
    )(sink_l, qkvu, qkvu, qkvu, cache_k4, cache_v4)


def _s5_param_kernel(rows_ref, bc_ref, w1_ref, w2_ref, a8_ref):
    a8_ref[...] = jnp.zeros_like(a8_ref)
    row_grp = lax.broadcasted_iota(jnp.int32, (128, S5_ST), 0) // S5_GC
    lane_grp = lax.broadcasted_iota(jnp.int32, (128, S5_ST), 1) // S5_P
    on_diag = row_grp == lane_grp

    def expand(a):
        return jnp.where(on_diag, jnp.concatenate([a] * S5_OG, axis=1), 0.0)

    taps = []
    for d in range(2):
        lr = rows_ref[0, d]
        li = rows_ref[1, d]
        dt = jnp.exp(rows_ref[2, d])
        mag = jnp.exp(lr * dt)
        ar = mag * jnp.cos(li * dt)
        ai = mag * jnp.sin(li * dt)
        den = lr * lr + li * li
        n_re = ar - 1.0
        f_re = (n_re * lr + ai * li) / den
        f_im = (ai * lr - n_re * li) / den
        pw = [(jnp.ones_like(ar), jnp.zeros_like(ar))]
        for _ in range(S5_T):
            pr, pi = pw[-1]
            pw.append((pr * ar - pi * ai, pr * ai + pi * ar))
        br, bi, cr, ci = (expand(bc_ref[k, d]) for k in range(4))
        bbr = f_re * br - f_im * bi
        bbi = f_re * bi + f_im * br
        xr_blocks, xi_blocks = [], []
        for t in range(S5_T):
            rows = slice(t * 128, (t + 1) * 128)
            pr, pi = pw[S5_T - 1 - t if d == 0 else t]
            xr = pr * bbr - pi * bbi
            xi = pr * bbi + pi * bbr
            c0 = S5_ROW + 2 * S5_ST * d
            w1_ref[rows, c0:c0 + S5_ST] = xr.astype(BF16)
            w1_ref[rows, c0 + S5_ST:c0 + 2 * S5_ST] = xi.astype(BF16)
            xr_blocks.append(xr)
            xi_blocks.append(xi)
            pr, pi = pw[t + 1 if d == 0 else S5_T - t]
            c0 = 2 * S5_ST * d
            w2_ref[rows, c0:c0 + S5_ST] = (cr * pr - ci * pi).astype(BF16)
            w2_ref[rows, c0 + S5_ST:c0 + 2 * S5_ST] = (-(cr * pi + ci * pr)).astype(BF16)
        a8_ref[2 * d:2 * d + 1, :] = pw[S5_T][0]
        a8_ref[2 * d + 1:2 * d + 2, :] = pw[S5_T][1]
        xr_all = jnp.concatenate(xr_blocks, axis=0)
        xi_all = jnp.concatenate(xi_blocks, axis=0)
        taps.append(_dot_nt_split(xr_all, cr) - _dot_nt_split(xi_all, ci))
    tf, tb = taps
    for t in range(S5_T):
        for t2 in range(S5_T):
            blk = None
            if t2 >= t:
                r0 = (S5_T - 1 - (t2 - t)) * 128
                blk = tf[r0:r0 + 128, :]
            if t2 <= t:
                r0 = (t - t2) * 128
                b2 = tb[r0:r0 + 128, :]
                blk = b2 if blk is None else blk + b2
            w1_ref[t * 128:(t + 1) * 128, t2 * 128:(t2 + 1) * 128] = blk.astype(BF16)


def _s5_params(s5_lam_re, s5_lam_im, s5_log_step, s5_b_re, s5_b_im, s5_c_re, s5_c_im):
    rows = jnp.stack([s5_lam_re, s5_lam_im, jnp.repeat(s5_log_step[..., None], S5_P, axis=-1)], axis=0)
    rows = rows.reshape(3, DEPTH, 2, S5_OCT, 1, S5_ST)
    bc = jnp.stack([jnp.swapaxes(s5_b_re, -1, -2), jnp.swapaxes(s5_b_im, -1, -2), s5_c_re, s5_c_im], axis=0)
    bc = bc.reshape(4, DEPTH, 2, S5_OCT, 128, S5_P)
    spec = lambda n, r, c: pl.BlockSpec((n, None, 2, None, r, c), lambda l, s: (0, l, 0, s, 0, 0))
    return pl.pallas_call(
        _s5_param_kernel,
        grid=(DEPTH, S5_OCT),
        in_specs=[spec(3, 1, S5_ST), spec(4, 128, S5_P)],
        out_specs=[
            pl.BlockSpec((None, None, S5_ROW, S5_W1), lambda l, s: (l, s, 0, 0)),
            pl.BlockSpec((None, None, S5_ROW, 4 * S5_ST), lambda l, s: (l, s, 0, 0)),
            pl.BlockSpec((None, None, 8, S5_ST), lambda l, s: (l, s, 0, 0)),
        ],
        out_shape=[
            jax.ShapeDtypeStruct((DEPTH, S5_OCT, S5_ROW, S5_W1), BF16),
            jax.ShapeDtypeStruct((DEPTH, S5_OCT, S5_ROW, 4 * S5_ST), BF16),
            jax.ShapeDtypeStruct((DEPTH, S5_OCT, 8, S5_ST), F32),
        ],
        compiler_params=pltpu.CompilerParams(
            dimension_semantics=("arbitrary", "arbitrary"), vmem_limit_bytes=VMEM_MID),
        name="s5_params",
    )(rows, bc)


def _s5_mix_kernel(u_ref, w1_ref, w2_ref, a8_ref, h0_ref, y_ref, hfin_ref, lhs_ref, a_ref, hp_ref, y8_ref):
    def stream(nb, nc, is_ctx):
        for b in range(nb):
            for t in range(S5_T):
                lhs_ref[t, pl.ds(b, nc, stride=nb), :] = u_ref[pl.ds(b * nc * S5_T + t, nc, stride=S5_T), :]
        lhs = jnp.concatenate([lhs_ref[t] for t in range(S5_T)], axis=1).astype(BF16)
        a_ref[...] = _dot(lhs, w1_ref[...])

        coef = [a8_ref[k:k + 1, :] for k in range(4)]
        if is_ctx:
            init = tuple(jnp.zeros((nb, S5_ST), F32) for _ in range(4))
        else:
            init = tuple(h0_ref[k] for k in range(4))

        def step(c, st):
            new = []
            for d in range(2):
                cc = c if d == 0 else nc - 1 - c
                rows = pl.ds(cc * nb, nb) if isinstance(c, int) else pl.ds(pl.multiple_of(cc * nb, nb), nb)
                hr, hi = st[2 * d], st[2 * d + 1]
                hp_ref[rows, 2 * S5_ST * d:2 * S5_ST * d + S5_ST] = hr
                hp_ref[rows, 2 * S5_ST * d + S5_ST:2 * S5_ST * (d + 1)] = hi
                c0 = S5_ROW + 2 * S5_ST * d
                gr = a_ref[rows, c0:c0 + S5_ST]
                gi = a_ref[rows, c0 + S5_ST:c0 + 2 * S5_ST]
                ar, ai = coef[2 * d], coef[2 * d + 1]
                new += [ar * hr - ai * hi + gr, ar * hi + ai * hr + gi]
            return tuple(new)

        if nb % 8 == 0:
            fin = lax.fori_loop(0, nc, step, init, unroll=S5_UNROLL)
        else:
            fin = init
            for c in range(nc):
                fin = step(c, fin)
        if is_ctx:
            for k in range(4):
                hfin_ref[k] = fin[k]

        y8 = a_ref[:, 0:S5_ROW] + _dot_nt(hp_ref[...].astype(BF16), w2_ref[...])
        for t in range(S5_T):
            y8_ref[t] = y8[:, t * 128:(t + 1) * 128]
        for b in range(nb):
            for t in range(S5_T):
                y_ref[pl.ds(b * nc * S5_T + t, nc, stride=S5_T), :] = y8_ref[t, pl.ds(b, nc, stride=nb), :]

    pl.when(pl.program_id(1) == 0)(functools.partial(stream, CTX_B, CTX_NC, True))
    pl.when(pl.program_id(1) == 1)(functools.partial(stream, LAT_B, LAT_NC, False))


def _s5_mix(uf, w1, w2, a8, h0, l):
    return pl.pallas_call(
        _s5_mix_kernel,
        grid=(S5_OCT, 2),
        in_specs=[
            pl.BlockSpec((T_CTX, 128), lambda s, k: (k, s)),
            pl.BlockSpec((None, None, S5_ROW, S5_W1), lambda s, k: (l, s, 0, 0)),
            pl.BlockSpec((None, None, S5_ROW, 4 * S5_ST), lambda s, k: (l, s, 0, 0)),
            pl.BlockSpec((None, None, 8, S5_ST), lambda s, k: (l, s, 0, 0)),
            pl.BlockSpec((None, 4, LAT_B, S5_ST), lambda s, k: (s, 0, 0, 0)),
        ],
        out_specs=[
            pl.BlockSpec((T_CTX, 128), lambda s, k: (k, s)),
            pl.BlockSpec((None, 4, CTX_B, S5_ST), lambda s, k: (s, 0, 0, 0)),
        ],
        out_shape=[
            jax.ShapeDtypeStruct((T_ALL, S5_CH), F32),
            jax.ShapeDtypeStruct((S5_OCT, 4, CTX_B, S5_ST), F32),
        ],
        scratch_shapes=[
            pltpu.VMEM((S5_T, S5_ROWS, 128), F32),
            pltpu.VMEM((S5_ROWS, S5_W1), F32),
            pltpu.VMEM((S5_ROWS, 4 * S5_ST), F32),
            pltpu.VMEM((S5_T, S5_ROWS, 128), F32),
        ],
        compiler_params=pltpu.CompilerParams(
            dimension_semantics=("arbitrary", "arbitrary"), vmem_limit_bytes=VMEM_BIG),
        name=f"s5_mix_l{l}",
    )(uf, w1, w2, a8, h0)


def _s5_gate(y, u, d, w, b):
    y = y + d * u
    z = y * (0.5 * (1.0 + jnp.tanh(math.sqrt(2.0 / math.pi) * (y + 0.044715 * (y * y * y)))))
    t = _dot(z.astype(BF16), w) + b
    return (z * jax.nn.sigmoid(t)).astype(BF16)


def _outproj_kernel(x_ref, a_ctx_ref, a_lat_ref, c_ctx_ref, c_lat_ref, y_ref, u_ref, d_ref, wglu_ref, bglu_ref,
                    mod_ref, npost_ref, w_ref, o_ref, wb_ref):
    @pl.when(pl.program_id(0) == 0)
    def _():
        nk = 4
        rk = D_MODEL // nk
        for r in range(nk):
            wb_ref[r * rk:(r + 1) * rk, :] = w_ref[r * rk:(r + 1) * rk, :].astype(BF16)

    na = A_HEADS * HD
    nc = na + C_HEADS * HD

    def body(a_ref, c_ref):
        gate = mod_ref[:, 5 * D_MODEL:6 * D_MODEL]
        wglu = wglu_ref[...].astype(BF16)
        for r in range(BM_OUT // RC_OUT):
            rows = slice(r * RC_OUT, (r + 1) * RC_OUT)
            s = _s5_gate(y_ref[rows, :], u_ref[rows, :], d_ref[...], wglu, bglu_ref[...])
            y = (_dot(a_ref[rows, :], wb_ref[0:na, :]) + _dot(c_ref[rows, :], wb_ref[na:nc, :])
                 + _dot(s, wb_ref[nc:, :]))
            o_ref[rows, :] = x_ref[rows, :] + gate * _rms(y, npost_ref[...])

    is_lat = pl.program_id(0) >= T_CTX // BM_OUT
    pl.when(jnp.logical_not(is_lat))(functools.partial(body, a_ctx_ref, c_ctx_ref))
    pl.when(is_lat)(functools.partial(body, a_lat_ref, c_lat_ref))


def _outproj(x, a_ctx, a_lat, c_ctx, c_lat, y_s5, uf, s5_d3, w_glu, b_glu3, mods4, norm_post4, w_out, l):
    bm = BM_OUT
    nct = T_CTX // bm
    ctx_idx = lambda i: (jnp.minimum(i, nct - 1), 0)
    lat_idx = lambda i: (jnp.maximum(i - nct, 0), 0)
    return pl.pallas_call(
        _outproj_kernel,
        grid=(T_ALL // bm,),
        in_specs=[
            pl.BlockSpec((bm, D_MODEL), lambda i: (i, 0)),
            pl.BlockSpec((bm, A_HEADS * HD), ctx_idx),
            pl.BlockSpec((bm, A_HEADS * HD), lat_idx),
            pl.BlockSpec((bm, C_HEADS * HD), ctx_idx),
            pl.BlockSpec((bm, C_HEADS * HD), lat_idx),
            pl.BlockSpec((bm, S5_CH), lambda i: (i, 0)),
            pl.BlockSpec((bm, S5_CH), lambda i: (i, 0)),
            pl.BlockSpec((None, 1, S5_CH), lambda i: (l, 0, 0)),
            pl.BlockSpec((None, S5_CH, S5_CH), lambda i: (l, 0, 0)),
            pl.BlockSpec((None, 1, S5_CH), lambda i: (l, 0, 0)),
            pl.BlockSpec((None, None, 1, N_MOD * D_MODEL), lambda i: (l, _mod_index(i, bm), 0, 0)),
            pl.BlockSpec((None, None, 1, D_MODEL), lambda i: (l, 1, 0, 0)),
            pl.BlockSpec((None, D_MODEL, D_MODEL), lambda i: (l, 0, 0), pipeline_mode=pl.Buffered(1)),
        ],
        out_specs=pl.BlockSpec((bm, D_MODEL), lambda i: (i, 0)),
        out_shape=jax.ShapeDtypeStruct((T_ALL, D_MODEL), F32),
        scratch_shapes=[pltpu.VMEM((D_MODEL, D_MODEL), BF16)],
        compiler_params=pltpu.CompilerParams(
            dimension_semantics=("arbitrary",), vmem_limit_bytes=VMEM_BIG),
        name=f"outproj_l{l}",
    )(x, a_ctx, a_lat, c_ctx, c_lat, y_s5, uf, s5_d3, w_glu, b_glu3, mods4, norm_post4, w_out)


def _rope_tables():
    rows = LAT_L // GRID_W
    row = jnp.repeat(jnp.arange(rows, dtype=F32), GRID_W)
    col = jnp.tile(jnp.arange(GRID_W, dtype=F32), rows)
    axis_dim = HD // 2
    inv_freq = ROPE_BASE ** (-jnp.arange(0, axis_dim, 2, dtype=F32) / axis_dim)
    ang_row = row[:, None] * inv_freq
    ang_col = col[:, None] * inv_freq
    cr, sr = jnp.cos(ang_row), jnp.sin(ang_row)
    cc, sc = jnp.cos(ang_col), jnp.sin(ang_col)
    cos_t = jnp.concatenate([cr, cr, cc, cc], axis=-1)
    sin_t = jnp.concatenate([-sr, sr, -sc, sc], axis=-1)
    return cos_t, sin_t


def kernel(x_prompt, x_sample, cache_a_k, cache_a_v, cache_c_k, cache_c_v, state_ssm_re, state_ssm_im,
           c, c_ctx, w_mod, b_mod, norm_pre, norm_post, ffn_gate, ffn_up, ffn_down, w_in, w_out,
           q_norm, k_norm, sink, s5_lam_re, s5_lam_im, s5_log_step, s5_b_re, s5_b_im, s5_c_re, s5_c_im,
           s5_d, w_glu, b_glu):
    cvec8 = jnp.concatenate([c_ctx[None, :], c, jnp.zeros((8 - 1 - LAT_B, D_MODEL), F32)], axis=0)
    mods4 = _modulation(cvec8, w_mod, b_mod).reshape(DEPTH, 8, 1, N_MOD * D_MODEL)
    norm_pre4 = norm_pre.reshape(DEPTH, 3, 1, D_MODEL)
    norm_post4 = norm_post.reshape(DEPTH, 3, 1, D_MODEL)
    q_norm3 = q_norm.reshape(DEPTH, 1, HD)
    k_norm3 = k_norm.reshape(DEPTH, 1, HD)
    s5_d3 = s5_d.reshape(DEPTH, 1, S5_CH)
    b_glu3 = b_glu.reshape(DEPTH, 1, S5_CH)
    cos_t, sin_t = _rope_tables()
    w1, w2, a8 = _s5_params(s5_lam_re, s5_lam_im, s5_log_step, s5_b_re, s5_b_im, s5_c_re, s5_c_im)
    kv4 = lambda a: a.reshape(LAT_B, DEPTH, PAST, A_KV * HD)
    cak, cav, cck, ccv = kv4(cache_a_k), kv4(cache_a_v), kv4(cache_c_k), kv4(cache_c_v)
    h0_all = jnp.stack([state_ssm_re[:, :, 0], state_ssm_im[:, :, 0],
                        state_ssm_re[:, :, 1], state_ssm_im[:, :, 1]], axis=0)
    h0_all = h0_all.reshape(4, LAT_B, DEPTH, S5_OCT, S5_ST).transpose(2, 3, 0, 1, 4)

    w_in_bf = w_in.astype(BF16)
    ffn_w = (mods4, norm_pre4, norm_post4, ffn_gate, ffn_up, ffn_down)
    new_caches = ()
    new_state = []
    xs = [x_prompt.reshape(T_CTX, D_MODEL), x_sample.reshape(T_LAT, D_MODEL)]
    for l in range(DEPTH):
        x = _ffn(xs, *ffn_w, l, 0)

        qkvu, uf, *new_caches = _inproj(x, mods4, norm_pre4, q_norm3, k_norm3, cos_t, sin_t, w_in_bf, l,
                                        new_caches)
        a_ctx, c_ctx = _attn_ctx(qkvu, sink[l], l)
        a_lat = _attn_lat_a(qkvu, cak, cav, l)
        c_lat = _attn_lat_c(qkvu, sink[l], cck, ccv, l)
        y, hfin = _s5_mix(uf, w1, w2, a8, h0_all[l], l)
        x = _outproj(x, a_ctx, a_lat, c_ctx, c_lat, y, uf, s5_d3, w_glu, b_glu3, mods4, norm_post4, w_out, l)
        if l < DEPTH - 1:
            xs = [_ffn([x], *ffn_w, l, 1)]
        else:
            y_prompt, y_sample = _ffn([x], *ffn_w, l, 1, split_out=True)

        hf = hfin.reshape(S5_OCT, 4, CTX_B, S5_OG, S5_P).transpose(1, 2, 0, 3, 4).reshape(4, CTX_B, S5_G, S5_P)
        new_state.append((jnp.stack([hf[0], hf[2]], axis=1), jnp.stack([hf[1], hf[3]], axis=1)))

    y_prompt = y_prompt.reshape(CTX_B, CTX_L, D_MODEL)
    y_sample = y_sample.reshape(LAT_B, LAT_L, D_MODEL)
    caches = list(new_caches)
    st_re = jnp.stack([new_state[l][0] for l in range(DEPTH)], axis=1)
    st_im = jnp.stack([new_state[l][1] for l in range(DEPTH)], axis=1)
    return (y_prompt, y_sample, caches[0], caches[1], caches[2], caches[3], st_re, st_im)
```

```python
import functools
import math

import jax
import jax.numpy as jnp
from jax import lax
from jax.experimental import pallas as pl
from jax.experimental.pallas import tpu as pltpu

F32 = jnp.float32
BF16 = jnp.bfloat16

D_MODEL = 2048
CTX_B, CTX_L = 16, 256
LAT_B, LAT_L = 2, 2048
DEPTH = 2
PAST = 512
GRID_W = 64
HD = 128
A_HEADS, A_KV = 8, 2
C_HEADS, C_KV = 4, 2
WINDOW = 128
S5_GC = 16
S5_CH = 512
S5_G = 32
S5_P = 64
D_FF = 5632
N_MOD = 9
IN_WIDTH = 3072
ROPE_BASE = 10000.0
EPS = 1e-6
HALF_STEP = 0.5
NEG_INF = -1e30
SCALE = HD ** -0.5
LOG2E = math.log2(math.e)

T_CTX = CTX_B * CTX_L
T_LAT = LAT_B * LAT_L
T_ALL = T_CTX + T_LAT

COL_AQ, COL_AK, COL_AV = 0, 1024, 1280
COL_CQ, COL_CK, COL_CV = 1536, 2048, 2304
COL_U = 2560

S5_T = 8
S5_OCT = S5_CH // 128
S5_OG = S5_G // S5_OCT
S5_ROW = S5_T * 128
S5_ST = S5_OG * S5_P
S5_W1 = S5_ROW + 4 * S5_ST
CTX_NC = CTX_L // S5_T
LAT_NC = LAT_L // S5_T
S5_ROWS = CTX_NC * CTX_B
assert S5_ROWS == LAT_NC * LAT_B and T_CTX == T_LAT

V7X_VMEM_BYTES = 64 * 1024 * 1024
VMEM_FFN = 60 * 1024 * 1024
VMEM_BIG = 56 * 1024 * 1024
VMEM_MID = 40 * 1024 * 1024

BM = 1024
BM_OUT = 512
RC_OUT = 256
BF = 512
RC = 512
RCX = 256
FFN_PREFETCH_STEP = 2
BN_MOD = 1024
BM_IN = 512
BQ_A = 512
BQ_C = 256
NQ_C = 4
NB_CTX = 4
S5_UNROLL = 4


def _dot(a, b):
    return jnp.dot(a, b, preferred_element_type=F32)


def _dot_nt(a, b, precision=None):
    return lax.dot_general(a, b, (((1,), (1,)), ((), ())), preferred_element_type=F32, precision=precision)


def _dot_nt_split(a, b):
    ah = a.astype(BF16)
    bh = b.astype(BF16)
    al = (a - ah.astype(F32)).astype(BF16)
    bl = (b - bh.astype(F32)).astype(BF16)
    return _dot_nt(ah, bh) + (_dot_nt(ah, bl) + _dot_nt(al, bh))


def _rms(x, g):
    return x * lax.rsqrt(jnp.mean(x * x, axis=-1, keepdims=True) + EPS) * g


def _mod_index(i, bm):
    nct = T_CTX // bm
    return jnp.where(i < nct, 0, 1 + (i - nct) // (LAT_L // bm))


def _mod_columns(c_ref, w_ref, b_ref, o_ref):
    c = c_ref[...]
    s = (c * jax.nn.sigmoid(c)).astype(BF16)
    o_ref[...] = _dot(s, w_ref[...].astype(BF16)) + b_ref[...]


def _modulation(cvec8, w_mod, b_mod3, l):
    n = N_MOD * D_MODEL
    return pl.pallas_call(
        _mod_columns,
        grid=(n // BN_MOD,),
        in_specs=[
            pl.BlockSpec((8, D_MODEL), lambda j: (0, 0)),
            pl.BlockSpec((None, D_MODEL, BN_MOD), lambda j: (l, 0, j)),
            pl.BlockSpec((None, 1, BN_MOD), lambda j: (l, 0, j)),
        ],
        out_specs=pl.BlockSpec((8, BN_MOD), lambda j: (0, j)),
        out_shape=jax.ShapeDtypeStruct((8, n), F32),
        compiler_params=pltpu.CompilerParams(
            dimension_semantics=("arbitrary",), vmem_limit_bytes=VMEM_MID),
        name=f"modulation_l{l}",
    )(cvec8, w_mod, b_mod3)


def _ffn_kernel(*refs, mo, n_x, n_out):
    x_hbms = refs[:n_x]
    mod_ref, npre_ref, npost_ref, wg_ref, wu_ref, wd_ref = refs[n_x:n_x + 6]
    out_hbms = refs[n_x + 6:n_x + 6 + n_out]
    acc_ref, xc_ref, h_ref, sem_x, sem_c, sem_o = refs[n_x + 6 + n_out:]
    i = pl.program_id(0)
    j = pl.program_id(1)
    n_tiles = pl.num_programs(0)
    last = pl.num_programs(1) - 1
    slot = i % 2
    nq = RC // RCX
    nct = T_CTX // BM

    def per_stream(arrays, tile, fn):
        if len(arrays) == 1:
            fn(arrays[0], pl.multiple_of(tile * BM, BM))
        else:
            pl.when(tile < nct)(lambda: fn(arrays[0], pl.multiple_of(tile * BM, BM)))
            pl.when(tile >= nct)(lambda: fn(arrays[1], pl.multiple_of((tile - nct) * BM, BM)))

    def x_tile_copy(arr, row0, sl):
        return pltpu.make_async_copy(arr.at[pl.ds(row0, BM), :], acc_ref.at[sl], sem_x)

    def x_chunk_copy(arr, row0, cs):
        return pltpu.make_async_copy(arr.at[pl.ds(row0, RCX), :], xc_ref.at[cs], sem_c.at[cs])

    def out_copy(arr, row0, sl):
        return pltpu.make_async_copy(acc_ref.at[sl], arr.at[pl.ds(row0, BM), :], sem_o.at[sl])

    wait_x_tile = lambda sl: x_tile_copy(x_hbms[0], 0, sl).wait()
    wait_x_chunk = lambda cs: x_chunk_copy(x_hbms[0], 0, cs).wait()
    wait_out = lambda sl: out_copy(out_hbms[0], 0, sl).wait()

    def start_x_chunk(q):
        per_stream(x_hbms, i, lambda arr, row0: x_chunk_copy(
            arr, pl.multiple_of(row0 + q * RCX, RCX), q % 2).start())

    @pl.when(jnp.logical_and(i == 0, j == 0))
    def _():
        x_tile_copy(x_hbms[0], 0, 0).start()

    @pl.when(j == 0)
    def _():
        wait_x_tile(slot)

    @pl.when(j == FFN_PREFETCH_STEP)
    def _():
        pl.when(i >= 1)(lambda: wait_out(1 - slot))

        @pl.when(i + 1 < n_tiles)
        def _():
            per_stream(x_hbms, i + 1, lambda arr, row0: x_tile_copy(arr, row0, 1 - slot).start())

    @pl.when(j == last - 1)
    def _():
        start_x_chunk(0)

    def step(first, final):
        acc_slot = acc_ref.at[slot]
        wg = wg_ref[...].astype(BF16)
        wu = wu_ref[...].astype(BF16)
        wd = wd_ref[...].astype(BF16)
        for r in range(BM // RC):
            rows = slice(r * RC, (r + 1) * RC)
            if first:
                shift = mod_ref[:, mo * D_MODEL:(mo + 1) * D_MODEL]
                scale = mod_ref[:, (mo + 1) * D_MODEL:(mo + 2) * D_MODEL]
                hn = _rms(acc_slot[rows, :], npre_ref[...])
                h = (hn * (1.0 + scale) + shift).astype(BF16)
                h_ref[rows, :] = h
            else:
                h = h_ref[rows, :]
            g = _dot(h, wg)
            u = _dot(h, wu)
            a = (g * jax.nn.sigmoid(g) * u).astype(BF16)
            acc = _dot(a, wd)
            if not first:
                acc = acc_slot[rows, :] + acc
            if not final:
                acc_slot[rows, :] = acc
                continue
            gate = mod_ref[:, (mo + 2) * D_MODEL:(mo + 3) * D_MODEL]
            for qq in range(nq):
                q = r * nq + qq
                wait_x_chunk(q % 2)
                if q + 1 < BM // RCX:
                    start_x_chunk(q + 1)
                sub = slice(qq * RCX, (qq + 1) * RCX)
                y = xc_ref[q % 2] + (HALF_STEP * gate) * _rms(acc[sub, :], npost_ref[...])
                acc_slot[r * RC + qq * RCX:r * RC + (qq + 1) * RCX, :] = y
        if final:
            per_stream(out_hbms, i, lambda arr, row0: out_copy(arr, row0, slot).start())

    pl.when(j == 0)(functools.partial(step, True, False))
    pl.when(jnp.logical_and(j > 0, j < last))(functools.partial(step, False, False))
    pl.when(j == last)(functools.partial(step, False, True))

    @pl.when(jnp.logical_and(i == n_tiles - 1, j == last))
    def _():
        wait_out(slot)


def _ffn(xs, mods3, norm_pre4, norm_post4, ffn_gate, ffn_up, ffn_down, l, s, *, split_out=False):
    mo = 6 * s
    ni = 2 * s
    in_specs = [pl.BlockSpec(memory_space=pl.ANY)] * len(xs) + [
        pl.BlockSpec((None, 1, N_MOD * D_MODEL), lambda i, j: (_mod_index(i, BM), 0, 0)),
        pl.BlockSpec((None, None, 1, D_MODEL), lambda i, j: (l, ni, 0, 0)),
        pl.BlockSpec((None, None, 1, D_MODEL), lambda i, j: (l, ni, 0, 0)),
        pl.BlockSpec((None, None, D_MODEL, BF), lambda i, j: (l, s, 0, j)),
        pl.BlockSpec((None, None, D_MODEL, BF), lambda i, j: (l, s, 0, j)),
        pl.BlockSpec((None, None, BF, D_MODEL), lambda i, j: (l, s, j, 0)),
    ]
    out_rows = (T_CTX, T_LAT) if split_out else (T_ALL,)
    outs = pl.pallas_call(
        functools.partial(_ffn_kernel, mo=mo, n_x=len(xs), n_out=len(out_rows)),
        grid=(T_ALL // BM, D_FF // BF),
        in_specs=in_specs,
        out_specs=[pl.BlockSpec(memory_space=pl.ANY)] * len(out_rows),
        out_shape=[jax.ShapeDtypeStruct((r, D_MODEL), F32) for r in out_rows],
        scratch_shapes=[
            pltpu.VMEM((2, BM, D_MODEL), F32),
            pltpu.VMEM((2, RCX, D_MODEL), F32),
            pltpu.VMEM((BM, D_MODEL), BF16),
            pltpu.SemaphoreType.DMA(()),
            pltpu.SemaphoreType.DMA((2,)),
            pltpu.SemaphoreType.DMA((2,)),
        ],
        compiler_params=pltpu.CompilerParams(
            dimension_semantics=("arbitrary", "arbitrary"), vmem_limit_bytes=VMEM_FFN),
        name=f"ffn_l{l}_s{s}",
    )(*xs, mods3, norm_pre4, norm_post4, ffn_gate, ffn_up, ffn_down)
    return outs if split_out else outs[0]


def _rope(y, cos, sins):
    lane = lax.broadcasted_iota(jnp.int32, y.shape, 1)
    first = (lane & 63) < 32
    partner = jnp.where(first, pltpu.roll(y, 96, 1), pltpu.roll(y, 32, 1))
    return y * cos + partner * sins


_IN_SEGMENTS = (
    (COL_AQ, A_HEADS, "q", True, None),
    (COL_AK, A_KV, "k", True, 0),
    (COL_AV, A_KV, None, False, 1),
    (COL_CQ, C_HEADS, None, True, None),
    (COL_CK, C_KV, None, True, 2),
    (COL_CV, C_KV, None, False, 3),
)


def _inproj_kernel(x_ref, mod_ref, npre_ref, qn_ref, kn_ref, cos_ref, sin_ref, w_ref, *rest, first_layer):
    qkvu_ref, uf_ref = rest[-6], rest[-5]
    cache_refs = rest[-4:]
    nb = BM_IN // CTX_L
    if first_layer:
        cache_refs = [c.at[:, 0] for c in rest[-4:]]

    def body(lat):
        shift = mod_ref[:, 3 * D_MODEL:4 * D_MODEL]
        scale = mod_ref[:, 4 * D_MODEL:5 * D_MODEL]
        h = (_rms(x_ref[...], npre_ref[...]) * (1.0 + scale) + shift).astype(BF16)
        for col0, heads, norm, rot, cache in _IN_SEGMENTS:
            p = _dot(h, w_ref[:, col0:col0 + heads * HD])
            for k in range(heads):
                y = p[:, k * HD:(k + 1) * HD]
                if norm == "q":
                    y = _rms(y, qn_ref[...])
                elif norm == "k":
                    y = _rms(y, kn_ref[...])
                if rot and lat:
                    y = _rope(y, cos_ref[...], sin_ref[...])
                qkvu_ref[:, col0 + k * HD:col0 + (k + 1) * HD] = y.astype(BF16)
                if cache is not None and not lat:
                    cache_refs[cache][:, :, k, :] = y.reshape(nb, CTX_L, HD)
        u = _dot(h, w_ref[:, COL_U:])
        uf_ref[...] = u
        qkvu_ref[:, COL_U:] = u.astype(BF16)
        if first_layer and not lat:
            for c in rest[-4:]:
                c[:, 1:] = jnp.zeros((nb, DEPTH - 1, CTX_L, A_KV, HD), F32)

    is_lat = pl.program_id(0) >= T_CTX // BM_IN
    pl.when(is_lat)(functools.partial(body, True))
    pl.when(jnp.logical_not(is_lat))(functools.partial(body, False))


def _inproj(x, mods3, norm_pre4, q_norm3, k_norm3, cos_t, sin_t, w_in_bf, l, prev_caches):
    bm = BM_IN
    nct = T_CTX // bm
    nb = bm // CTX_L
    tab_idx = lambda i: (jnp.maximum(i - nct, 0) % (LAT_L // bm), 0)
    cache_shape = jax.ShapeDtypeStruct((CTX_B, DEPTH, CTX_L, A_KV, HD), F32)
    first_layer = not prev_caches
    if first_layer:
        cache_spec = pl.BlockSpec((nb, DEPTH, CTX_L, A_KV, HD), lambda i: (jnp.minimum(i, nct - 1), 0, 0, 0, 0))
    else:
        cache_spec = pl.BlockSpec((nb, None, CTX_L, A_KV, HD), lambda i: (jnp.minimum(i, nct - 1), l, 0, 0, 0))
    n_in = 8
    return pl.pallas_call(
        functools.partial(_inproj_kernel, first_layer=first_layer),
        grid=(T_ALL // bm,),
        in_specs=[
            pl.BlockSpec((bm, D_MODEL), lambda i: (i, 0)),
            pl.BlockSpec((None, 1, N_MOD * D_MODEL), lambda i: (_mod_index(i, bm), 0, 0)),
            pl.BlockSpec((None, None, 1, D_MODEL), lambda i: (l, 1, 0, 0)),
            pl.BlockSpec((None, 1, HD), lambda i: (l, 0, 0)),
            pl.BlockSpec((None, 1, HD), lambda i: (l, 0, 0)),
            pl.BlockSpec((bm, HD), tab_idx),
            pl.BlockSpec((bm, HD), tab_idx),
            pl.BlockSpec((None, D_MODEL, IN_WIDTH), lambda i: (l, 0, 0), pipeline_mode=pl.Buffered(1)),
        ] + [pl.BlockSpec(memory_space=pl.ANY)] * len(prev_caches),
        out_specs=[
            pl.BlockSpec((bm, IN_WIDTH), lambda i: (i, 0)),
            pl.BlockSpec((bm, S5_CH), lambda i: (i, 0)),
        ] + [cache_spec] * 4,
        out_shape=[
            jax.ShapeDtypeStruct((T_ALL, IN_WIDTH), BF16),
            jax.ShapeDtypeStruct((T_ALL, S5_CH), F32),
        ] + [cache_shape] * 4,
        input_output_aliases={n_in + k: 2 + k for k in range(len(prev_caches))},
        compiler_params=pltpu.CompilerParams(
            dimension_semantics=("arbitrary",), vmem_limit_bytes=VMEM_BIG),
        name=f"inproj_l{l}",
    )(x, mods3, norm_pre4, q_norm3, k_norm3, cos_t, sin_t, w_in_bf, *prev_caches)


def _softmax_pv(dots, values, sink=None):
    m = functools.reduce(jnp.maximum, [jnp.max(d, axis=-1, keepdims=True) for d in dots]) * SCALE
    if sink is not None:
        m = jnp.maximum(m, sink)
    m2 = m * LOG2E
    ps = [jnp.exp2(d * (SCALE * LOG2E) - m2) for d in dots]
    den = functools.reduce(jnp.add, [jnp.sum(p, axis=-1, keepdims=True) for p in ps])
    if sink is not None:
        den = den + jnp.exp2(sink * LOG2E - m2)
    o = functools.reduce(jnp.add, [_dot(p.astype(BF16), v) for p, v in zip(ps, values)])
    return o / den


def _attn_ctx_kernel(sink_ref, aq_ref, ak_ref, av_ref, cq_ref, ck_ref, cv_ref, ao_ref, co_ref):
    kv = pl.program_id(1)
    g = C_HEADS // C_KV
    for b in range(NB_CTX):
        rows = slice(b * CTX_L, (b + 1) * CTX_L)
        k = ak_ref[rows, :]
        v = av_ref[rows, :]
        for h in range(A_HEADS // A_KV):
            cols = slice(h * HD, (h + 1) * HD)
            s = _dot_nt(aq_ref[rows, cols], k)
            ao_ref[rows, cols] = _softmax_pv([s], [v]).astype(BF16)
        k = ck_ref[rows, :]
        v = cv_ref[rows, :]
        for h in range(g):
            cols = slice(h * HD, (h + 1) * HD)
            s = _dot_nt(cq_ref[rows, cols], k)
            co_ref[rows, cols] = _softmax_pv([s], [v], sink_ref[kv * g + h]).astype(BF16)


def _attn_ctx(qkvu, sink_l, l):
    ga = A_HEADS // A_KV * HD
    gc = C_HEADS // C_KV * HD
    rows = NB_CTX * CTX_L
    blk = lambda width, col0: pl.BlockSpec((rows, width), lambda b, k: (b, col0 // width + k))
    return pl.pallas_call(
        _attn_ctx_kernel,
        grid=(CTX_B // NB_CTX, A_KV),
        in_specs=[
            pl.BlockSpec(memory_space=pltpu.SMEM),
            blk(ga, COL_AQ), blk(HD, COL_AK), blk(HD, COL_AV),
            blk(gc, COL_CQ), blk(HD, COL_CK), blk(HD, COL_CV),
        ],
        out_specs=[
            pl.BlockSpec((rows, ga), lambda b, k: (b, k)),
            pl.BlockSpec((rows, gc), lambda b, k: (b, k)),
        ],
        out_shape=[
            jax.ShapeDtypeStruct((T_CTX, A_HEADS * HD), BF16),
            jax.ShapeDtypeStruct((T_CTX, C_HEADS * HD), BF16),
        ],
        compiler_params=pltpu.CompilerParams(dimension_semantics=("arbitrary", "arbitrary")),
        name=f"attn_ctx_l{l}",
    )(sink_l, qkvu, qkvu, qkvu, qkvu, qkvu, qkvu)


def _attn_lat_a_kernel(q_ref, k_ref, v_ref, kc_ref, vc_ref, *rest):
    o_ref = rest[-1] if len(rest) == 1 else rest[-2]
    k = k_ref[...]
    v = v_ref[...]
    kc = kc_ref[...].astype(BF16)
    vc = vc_ref[...].astype(BF16)
    for h in range(A_HEADS // A_KV):
        cols = slice(h * HD, (h + 1) * HD)
        q = q_ref[:, cols]
        s1 = _dot_nt(q, k)
        s2 = _dot_nt(q, kc)
        o_ref[:, cols] = _softmax_pv([s1, s2], [v, vc]).astype(BF16)
    if len(rest) > 1:
        c_ref, w_ref, b_ref, _, mod_ref = rest
        _mod_columns(c_ref, w_ref, b_ref, mod_ref)


def _attn_lat_a(qkvu, cache_k4, cache_v4, l, next_mod=None):
    ga = A_HEADS // A_KV * HD
    nq = LAT_L // BQ_A
    row0 = T_CTX // BQ_A
    lat_blk = T_CTX // LAT_L
    in_specs = [
        pl.BlockSpec((BQ_A, ga), lambda b, k, q: (row0 + b * nq + q, COL_AQ // ga + k)),
        pl.BlockSpec((LAT_L, HD), lambda b, k, q: (lat_blk + b, COL_AK // HD + k)),
        pl.BlockSpec((LAT_L, HD), lambda b, k, q: (lat_blk + b, COL_AV // HD + k)),
        pl.BlockSpec((None, None, PAST, HD), lambda b, k, q: (b, l, 0, k)),
        pl.BlockSpec((None, None, PAST, HD), lambda b, k, q: (b, l, 0, k)),
    ]
    out_specs = [pl.BlockSpec((BQ_A, ga), lambda b, k, q: (b * nq + q, k))]
    out_shape = [jax.ShapeDtypeStruct((T_LAT, A_HEADS * HD), BF16)]
    args = [qkvu, qkvu, qkvu, cache_k4, cache_v4]
    if next_mod is not None:
        n = N_MOD * D_MODEL
        steps = LAT_B * A_KV * nq
        bn = n // steps
        step = lambda b, k, q: (b * A_KV + k) * nq + q
        in_specs += [
            pl.BlockSpec((8, D_MODEL), lambda b, k, q: (0, 0)),
            pl.BlockSpec((None, D_MODEL, bn), lambda b, k, q: (l + 1, 0, step(b, k, q))),
            pl.BlockSpec((None, 1, bn), lambda b, k, q: (l + 1, 0, step(b, k, q))),
        ]
        out_specs.append(pl.BlockSpec((8, bn), lambda b, k, q: (0, step(b, k, q))))
        out_shape.append(jax.ShapeDtypeStruct((8, n), F32))
        args += list(next_mod)
    outs = pl.pallas_call(
        _attn_lat_a_kernel,
        grid=(LAT_B, A_KV, nq),
        in_specs=in_specs,
        out_specs=out_specs,
        out_shape=out_shape,
        compiler_params=pltpu.CompilerParams(
            dimension_semantics=("arbitrary", "arbitrary", "arbitrary"), vmem_limit_bytes=VMEM_BIG),
        name=f"attn_lat_a_l{l}",
    )(*args)
    return outs if next_mod is not None else outs[0]


def _attn_lat_c_kernel(sink_ref, q_ref, k_ref, v_ref, kc_ref, vc_ref, o_ref):
    kv = pl.program_id(1)
    span = BQ_C + 2 * WINDOW
    kc = kc_ref[...].astype(BF16)
    vc = vc_ref[...].astype(BF16)
    g = C_HEADS // C_KV
    for sub in range(NQ_C):
        n = pl.program_id(2) * NQ_C + sub
        rows = slice(sub * BQ_C, (sub + 1) * BQ_C)
        start = pl.multiple_of(jnp.clip(n * BQ_C - WINDOW, 0, LAT_L - span), WINDOW)
        kw = k_ref[pl.ds(start, span), :]
        vw = v_ref[pl.ds(start, span), :]
        qpos = n * BQ_C + lax.broadcasted_iota(jnp.int32, (BQ_C, span), 0)
        kpos = start + lax.broadcasted_iota(jnp.int32, (BQ_C, span), 1)
        valid = jnp.abs(qpos - kpos) <= WINDOW
        for h in range(g):
            cols = slice(h * HD, (h + 1) * HD)
            q = q_ref[rows, cols]
            s1 = jnp.where(valid, _dot_nt(q, kw), NEG_INF)
            s2 = _dot_nt(q, kc)
            o_ref[rows, cols] = _softmax_pv([s1, s2], [vw, vc], sink_ref[kv * g + h]).astype(BF16)


def _attn_lat_c(qkvu, sink_l, cache_k4, cache_v4, l):
    gc = C_HEADS // C_KV * HD
    bq = BQ_C * NQ_C
    nq = LAT_L // bq
    row0 = T_CTX // bq
    lat_blk = T_CTX // LAT_L
    return pl.pallas_call(
        _attn_lat_c_kernel,
        grid=(LAT_B, C_KV, nq),
        in_specs=[
            pl.BlockSpec(memory_space=pltpu.SMEM),
            pl.BlockSpec((bq, gc), lambda b, k, q: (row0 + b * nq + q, COL_CQ // gc + k)),
            pl.BlockSpec((LAT_L, HD), lambda b, k, q: (lat_blk + b, COL_CK // HD + k)),
            pl.BlockSpec((LAT_L, HD), lambda b, k, q: (lat_blk + b, COL_CV // HD + k)),
            pl.BlockSpec((None, None, PAST, HD), lambda b, k, q: (b, l, 0, k)),
            pl.BlockSpec((None, None, PAST, HD), lambda b, k, q: (b, l, 0, k)),
        ],
        out_specs=pl.BlockSpec((bq, gc), lambda b, k, q: (b * nq + q, k)),
        out_shape=jax.ShapeDtypeStruct((T_LAT, C_HEADS * HD), BF16),
        compiler_params=pltpu.CompilerParams(
            dimension_semantics=("arbitrary", "arbitrary", "arbitrary")),
        name=f"attn_lat_c_l{l}",
    )(sink_l, qkvu, qkvu, qkvu, cache_k4, cache_v4)


def _s5_param_kernel(rows_ref, bc_ref, w1_ref, w2_ref, a8_ref):
    a8_ref[...] = jnp.zeros_like(a8_ref)
    row_grp = lax.broadcasted_iota(jnp.int32, (128, S5_ST), 0) // S5_GC
    lane_grp = lax.broadcasted_iota(jnp.int32, (128, S5_ST), 1) // S5_P
    on_diag = row_grp == lane_grp

    def expand(a):
        return jnp.where(on_diag, jnp.concatenate([a] * S5_OG, axis=1), 0.0)

    taps = []
    for d in range(2):
        lr = rows_ref[0, d]
        li = rows_ref[1, d]
        dt = jnp.exp(rows_ref[2, d])
        mag = jnp.exp(lr * dt)
        ar = mag * jnp.cos(li * dt)
        ai = mag * jnp.sin(li * dt)
        den = lr * lr + li * li
        n_re = ar - 1.0
        f_re = (n_re * lr + ai * li) / den
        f_im = (ai * lr - n_re * li) / den
        pw = [(jnp.ones_like(ar), jnp.zeros_like(ar))]
        for _ in range(S5_T):
            pr, pi = pw[-1]
            pw.append((pr * ar - pi * ai, pr * ai + pi * ar))
        br, bi, cr, ci = (expand(bc_ref[k, d]) for k in range(4))
        bbr = f_re * br - f_im * bi
        bbi = f_re * bi + f_im * br
        xr_blocks, xi_blocks = [], []
        for t in range(S5_T):
            rows = slice(t * 128, (t + 1) * 128)
            pr, pi = pw[S5_T - 1 - t if d == 0 else t]
            xr = pr * bbr - pi * bbi
            xi = pr * bbi + pi * bbr
            c0 = S5_ROW + 2 * S5_ST * d
            w1_ref[rows, c0:c0 + S5_ST] = xr.astype(BF16)
            w1_ref[rows, c0 + S5_ST:c0 + 2 * S5_ST] = xi.astype(BF16)
            xr_blocks.append(xr)
            xi_blocks.append(xi)
            pr, pi = pw[t + 1 if d == 0 else S5_T - t]
            c0 = 2 * S5_ST * d
            w2_ref[rows, c0:c0 + S5_ST] = (cr * pr - ci * pi).astype(BF16)
            w2_ref[rows, c0 + S5_ST:c0 + 2 * S5_ST] = (-(cr * pi + ci * pr)).astype(BF16)
        a8_ref[2 * d:2 * d + 1, :] = pw[S5_T][0]
        a8_ref[2 * d + 1:2 * d + 2, :] = pw[S5_T][1]
        xr_all = jnp.concatenate(xr_blocks, axis=0)
        xi_all = jnp.concatenate(xi_blocks, axis=0)
        taps.append(_dot_nt_split(xr_all, cr) - _dot_nt_split(xi_all, ci))
    tf, tb = taps
    for t in range(S5_T):
        for t2 in range(S5_T):
            blk = None
            if t2 >= t:
                r0 = (S5_T - 1 - (t2 - t)) * 128
                blk = tf[r0:r0 + 128, :]
            if t2 <= t:
                r0 = (t - t2) * 128
                b2 = tb[r0:r0 + 128, :]
                blk = b2 if blk is None else blk + b2
            w1_ref[t * 128:(t + 1) * 128, t2 * 128:(t2 + 1) * 128] = blk.astype(BF16)


def _s5_params(s5_lam_re, s5_lam_im, s5_log_step, s5_b_re, s5_b_im, s5_c_re, s5_c_im):
    rows = jnp.stack([s5_lam_re, s5_lam_im, jnp.repeat(s5_log_step[..., None], S5_P, axis=-1)], axis=0)
    rows = rows.reshape(3, DEPTH, 2, S5_OCT, 1, S5_ST)
    bc = jnp.stack([jnp.swapaxes(s5_b_re, -1, -2), jnp.swapaxes(s5_b_im, -1, -2), s5_c_re, s5_c_im], axis=0)
    bc = bc.reshape(4, DEPTH, 2, S5_OCT, 128, S5_P)
    spec = lambda n, r, c: pl.BlockSpec((n, None, 2, None, r, c), lambda l, s: (0, l, 0, s, 0, 0))
    return pl.pallas_call(
        _s5_param_kernel,
        grid=(DEPTH, S5_OCT),
        in_specs=[spec(3, 1, S5_ST), spec(4, 128, S5_P)],
        out_specs=[
            pl.BlockSpec((None, None, S5_ROW, S5_W1), lambda l, s: (l, s, 0, 0)),
            pl.BlockSpec((None, None, S5_ROW, 4 * S5_ST), lambda l, s: (l, s, 0, 0)),
            pl.BlockSpec((None, None, 8, S5_ST), lambda l, s: (l, s, 0, 0)),
        ],
        out_shape=[
            jax.ShapeDtypeStruct((DEPTH, S5_OCT, S5_ROW, S5_W1), BF16),
            jax.ShapeDtypeStruct((DEPTH, S5_OCT, S5_ROW, 4 * S5_ST), BF16),
            jax.ShapeDtypeStruct((DEPTH, S5_OCT, 8, S5_ST), F32),
        ],
        compiler_params=pltpu.CompilerParams(
            dimension_semantics=("arbitrary", "arbitrary"), vmem_limit_bytes=VMEM_MID),
        name="s5_params",
    )(rows, bc)


def _s5_mix_kernel(u_ref, w1_ref, w2_ref, a8_ref, h0_ref, y_ref, hfin_ref, lhs_ref, a_ref, hp_ref, y8_ref):
    def stream(nb, nc, is_ctx):
        for b in range(nb):
            for t in range(S5_T):
                lhs_ref[t, pl.ds(b, nc, stride=nb), :] = u_ref[pl.ds(b * nc * S5_T + t, nc, stride=S5_T), :]
        lhs = jnp.concatenate([lhs_ref[t] for t in range(S5_T)], axis=1).astype(BF16)
        a_ref[...] = _dot(lhs, w1_ref[...])

        coef = [a8_ref[k:k + 1, :] for k in range(4)]
        if is_ctx:
            init = tuple(jnp.zeros((nb, S5_ST), F32) for _ in range(4))
        else:
            init = tuple(h0_ref[k] for k in range(4))

        def step(c, st):
            new = []
            for d in range(2):
                cc = c if d == 0 else nc - 1 - c
                rows = pl.ds(cc * nb, nb) if isinstance(c, int) else pl.ds(pl.multiple_of(cc * nb, nb), nb)
                hr, hi = st[2 * d], st[2 * d + 1]
                hp_ref[rows, 2 * S5_ST * d:2 * S5_ST * d + S5_ST] = hr
                hp_ref[rows, 2 * S5_ST * d + S5_ST:2 * S5_ST * (d + 1)] = hi
                c0 = S5_ROW + 2 * S5_ST * d
                gr = a_ref[rows, c0:c0 + S5_ST]
                gi = a_ref[rows, c0 + S5_ST:c0 + 2 * S5_ST]
                ar, ai = coef[2 * d], coef[2 * d + 1]
                new += [ar * hr - ai * hi + gr, ar * hi + ai * hr + gi]
            return tuple(new)

        if nb % 8 == 0:
            fin = lax.fori_loop(0, nc, step, init, unroll=S5_UNROLL)
        else:
            fin = init
            for c in range(nc):
                fin = step(c, fin)
        if is_ctx:
            for k in range(4):
                hfin_ref[k] = fin[k]

        y8 = a_ref[:, 0:S5_ROW] + _dot_nt(hp_ref[...].astype(BF16), w2_ref[...])
        for t in range(S5_T):
            y8_ref[t] = y8[:, t * 128:(t + 1) * 128]
        for b in range(nb):
            for t in range(S5_T):
                y_ref[pl.ds(b * nc * S5_T + t, nc, stride=S5_T), :] = y8_ref[t, pl.ds(b, nc, stride=nb), :]

    pl.when(pl.program_id(1) == 0)(functools.partial(stream, CTX_B, CTX_NC, True))
    pl.when(pl.program_id(1) == 1)(functools.partial(stream, LAT_B, LAT_NC, False))


def _s5_mix(uf, w1, w2, a8, h0, l):
    return pl.pallas_call(
        _s5_mix_kernel,
        grid=(S5_OCT, 2),
        in_specs=[
            pl.BlockSpec((T_CTX, 128), lambda s, k: (k, s)),
            pl.BlockSpec((None, None, S5_ROW, S5_W1), lambda s, k: (l, s, 0, 0)),
            pl.BlockSpec((None, None, S5_ROW, 4 * S5_ST), lambda s, k: (l, s, 0, 0)),
            pl.BlockSpec((None, None, 8, S5_ST), lambda s, k: (l, s, 0, 0)),
            pl.BlockSpec((None, 4, LAT_B, S5_ST), lambda s, k: (s, 0, 0, 0)),
        ],
        out_specs=[
            pl.BlockSpec((T_CTX, 128), lambda s, k: (k, s)),
            pl.BlockSpec((None, 4, CTX_B, S5_ST), lambda s, k: (s, 0, 0, 0)),
        ],
        out_shape=[
            jax.ShapeDtypeStruct((T_ALL, S5_CH), F32),
            jax.ShapeDtypeStruct((S5_OCT, 4, CTX_B, S5_ST), F32),
        ],
        scratch_shapes=[
            pltpu.VMEM((S5_T, S5_ROWS, 128), F32),
            pltpu.VMEM((S5_ROWS, S5_W1), F32),
            pltpu.VMEM((S5_ROWS, 4 * S5_ST), F32),
            pltpu.VMEM((S5_T, S5_ROWS, 128), F32),
        ],
        compiler_params=pltpu.CompilerParams(
            dimension_semantics=("arbitrary", "arbitrary"), vmem_limit_bytes=VMEM_BIG),
        name=f"s5_mix_l{l}",
    )(uf, w1, w2, a8, h0)


def _s5_gate(y, u, d, w, b):
    y = y + d * u
    z = y * (0.5 * (1.0 + jnp.tanh(math.sqrt(2.0 / math.pi) * (y + 0.044715 * (y * y * y)))))
    t = _dot(z.astype(BF16), w) + b
    return (z * jax.nn.sigmoid(t)).astype(BF16)


def _outproj_kernel(x_ref, a_ctx_ref, a_lat_ref, c_ctx_ref, c_lat_ref, y_ref, u_ref, d_ref, wglu_ref, bglu_ref,
                    mod_ref, npost_ref, w_ref, o_ref, wb_ref):
    @pl.when(pl.program_id(0) == 0)
    def _():
        nk = 4
        rk = D_MODEL // nk
        for r in range(nk):
            wb_ref[r * rk:(r + 1) * rk, :] = w_ref[r * rk:(r + 1) * rk, :].astype(BF16)

    na = A_HEADS * HD
    nc = na + C_HEADS * HD

    def body(a_ref, c_ref):
        gate = mod_ref[:, 5 * D_MODEL:6 * D_MODEL]
        wglu = wglu_ref[...].astype(BF16)
        for r in range(BM_OUT // RC_OUT):
            rows = slice(r * RC_OUT, (r + 1) * RC_OUT)
            s = _s5_gate(y_ref[rows, :], u_ref[rows, :], d_ref[...], wglu, bglu_ref[...])
            y = (_dot(a_ref[rows, :], wb_ref[0:na, :]) + _dot(c_ref[rows, :], wb_ref[na:nc, :])
                 + _dot(s, wb_ref[nc:, :]))
            o_ref[rows, :] = x_ref[rows, :] + gate * _rms(y, npost_ref[...])

    is_lat = pl.program_id(0) >= T_CTX // BM_OUT
    pl.when(jnp.logical_not(is_lat))(functools.partial(body, a_ctx_ref, c_ctx_ref))
    pl.when(is_lat)(functools.partial(body, a_lat_ref, c_lat_ref))


def _outproj(x, a_ctx, a_lat, c_ctx, c_lat, y_s5, uf, s5_d3, w_glu, b_glu3, mods3, norm_post4, w_out, l):
    bm = BM_OUT
    nct = T_CTX // bm
    ctx_idx = lambda i: (jnp.minimum(i, nct - 1), 0)
    lat_idx = lambda i: (jnp.maximum(i - nct, 0), 0)
    return pl.pallas_call(
        _outproj_kernel,
        grid=(T_ALL // bm,),
        in_specs=[
            pl.BlockSpec((bm, D_MODEL), lambda i: (i, 0)),
            pl.BlockSpec((bm, A_HEADS * HD), ctx_idx),
            pl.BlockSpec((bm, A_HEADS * HD), lat_idx),
            pl.BlockSpec((bm, C_HEADS * HD), ctx_idx),
            pl.BlockSpec((bm, C_HEADS * HD), lat_idx),
            pl.BlockSpec((bm, S5_CH), lambda i: (i, 0)),
            pl.BlockSpec((bm, S5_CH), lambda i: (i, 0)),
            pl.BlockSpec((None, 1, S5_CH), lambda i: (l, 0, 0)),
            pl.BlockSpec((None, S5_CH, S5_CH), lambda i: (l, 0, 0)),
            pl.BlockSpec((None, 1, S5_CH), lambda i: (l, 0, 0)),
            pl.BlockSpec((None, 1, N_MOD * D_MODEL), lambda i: (_mod_index(i, bm), 0, 0)),
            pl.BlockSpec((None, None, 1, D_MODEL), lambda i: (l, 1, 0, 0)),
            pl.BlockSpec((None, D_MODEL, D_MODEL), lambda i: (l, 0, 0), pipeline_mode=pl.Buffered(1)),
        ],
        out_specs=pl.BlockSpec((bm, D_MODEL), lambda i: (i, 0)),
        out_shape=jax.ShapeDtypeStruct((T_ALL, D_MODEL), F32),
        scratch_shapes=[pltpu.VMEM((D_MODEL, D_MODEL), BF16)],
        compiler_params=pltpu.CompilerParams(
            dimension_semantics=("arbitrary",), vmem_limit_bytes=VMEM_BIG),
        name=f"outproj_l{l}",
    )(x, a_ctx, a_lat, c_ctx, c_lat, y_s5, uf, s5_d3, w_glu, b_glu3, mods3, norm_post4, w_out)


def _rope_tables():
    rows = LAT_L // GRID_W
    row = jnp.repeat(jnp.arange(rows, dtype=F32), GRID_W)
    col = jnp.tile(jnp.arange(GRID_W, dtype=F32), rows)
    axis_dim = HD // 2
    inv_freq = ROPE_BASE ** (-jnp.arange(0, axis_dim, 2, dtype=F32) / axis_dim)
    ang_row = row[:, None] * inv_freq
    ang_col = col[:, None] * inv_freq
    cr, sr = jnp.cos(ang_row), jnp.sin(ang_row)
    cc, sc = jnp.cos(ang_col), jnp.sin(ang_col)
    cos_t = jnp.concatenate([cr, cr, cc, cc], axis=-1)
    sin_t = jnp.concatenate([-sr, sr, -sc, sc], axis=-1)
    return cos_t, sin_t


def kernel(x_prompt, x_sample, cache_a_k, cache_a_v, cache_c_k, cache_c_v, state_ssm_re, state_ssm_im,
           c, c_ctx, w_mod, b_mod, norm_pre, norm_post, ffn_gate, ffn_up, ffn_down, w_in, w_out,
           q_norm, k_norm, sink, s5_lam_re, s5_lam_im, s5_log_step, s5_b_re, s5_b_im, s5_c_re, s5_c_im,
           s5_d, w_glu, b_glu):
    cvec8 = jnp.concatenate([c_ctx[None, :], c, jnp.zeros((8 - 1 - LAT_B, D_MODEL), F32)], axis=0)
    b_mod3 = b_mod.reshape(DEPTH, 1, N_MOD * D_MODEL)
    mods3 = _modulation(cvec8, w_mod, b_mod3, 0).reshape(8, 1, N_MOD * D_MODEL)
    norm_pre4 = norm_pre.reshape(DEPTH, 3, 1, D_MODEL)
    norm_post4 = norm_post.reshape(DEPTH, 3, 1, D_MODEL)
    q_norm3 = q_norm.reshape(DEPTH, 1, HD)
    k_norm3 = k_norm.reshape(DEPTH, 1, HD)
    s5_d3 = s5_d.reshape(DEPTH, 1, S5_CH)
    b_glu3 = b_glu.reshape(DEPTH, 1, S5_CH)
    cos_t, sin_t = _rope_tables()
    w1, w2, a8 = _s5_params(s5_lam_re, s5_lam_im, s5_log_step, s5_b_re, s5_b_im, s5_c_re, s5_c_im)
    kv4 = lambda a: a.reshape(LAT_B, DEPTH, PAST, A_KV * HD)
    cak, cav, cck, ccv = kv4(cache_a_k), kv4(cache_a_v), kv4(cache_c_k), kv4(cache_c_v)
    h0_all = jnp.stack([state_ssm_re[:, :, 0], state_ssm_im[:, :, 0],
                        state_ssm_re[:, :, 1], state_ssm_im[:, :, 1]], axis=0)
    h0_all = h0_all.reshape(4, LAT_B, DEPTH, S5_OCT, S5_ST).transpose(2, 3, 0, 1, 4)

    w_in_bf = w_in.astype(BF16)
    ffn_w = (norm_pre4, norm_post4, ffn_gate, ffn_up, ffn_down)
    new_caches = ()
    new_state = []
    xs = [x_prompt.reshape(T_CTX, D_MODEL), x_sample.reshape(T_LAT, D_MODEL)]
    for l in range(DEPTH):
        x = _ffn(xs, mods3, *ffn_w, l, 0)

        qkvu, uf, *new_caches = _inproj(x, mods3, norm_pre4, q_norm3, k_norm3, cos_t, sin_t, w_in_bf, l,
                                        new_caches)
        a_ctx, c_ctx = _attn_ctx(qkvu, sink[l], l)
        if l < DEPTH - 1:
            a_lat, mods_next = _attn_lat_a(qkvu, cak, cav, l, next_mod=(cvec8, w_mod, b_mod3))
        else:
            a_lat = _attn_lat_a(qkvu, cak, cav, l)
        c_lat = _attn_lat_c(qkvu, sink[l], cck, ccv, l)
        y, hfin = _s5_mix(uf, w1, w2, a8, h0_all[l], l)
        x = _outproj(x, a_ctx, a_lat, c_ctx, c_lat, y, uf, s5_d3, w_glu, b_glu3, mods3, norm_post4, w_out, l)
        if l < DEPTH - 1:
            xs = [_ffn([x], mods3, *ffn_w, l, 1)]
            mods3 = mods_next.reshape(8, 1, N_MOD * D_MODEL)
        else:
            y_prompt, y_sample = _ffn([x], mods3, *ffn_w, l, 1, split_out=True)

        hf = hfin.reshape(S5_OCT, 4, CTX_B, S5_OG, S5_P).transpose(1, 2, 0, 3, 4).reshape(4, CTX_B, S5_G, S5_P)
        new_state.append((jnp.stack([hf[0], hf[2]], axis=1), jnp.stack([hf[1], hf[3]], axis=1)))

    y_prompt = y_prompt.reshape(CTX_B, CTX_L, D_MODEL)
    y_sample = y_sample.reshape(LAT_B, LAT_L, D_MODEL)
    caches = list(new_caches)
    st_re = jnp.stack([new_state[l][0] for l in range(DEPTH)], axis=1)
    st_im = jnp.stack([new_state[l][1] for l in range(DEPTH)], axis=1)
    return (y_prompt, y_sample, caches[0], caches[1], caches[2], caches[3], st_re, st_im)
```

```python
import functools
import math

import jax
import jax.numpy as jnp
from jax import lax
from jax.experimental import pallas as pl
from jax.experimental.pallas import tpu as pltpu

F32 = jnp.float32
BF16 = jnp.bfloat16

D_MODEL = 2048
CTX_B, CTX_L = 16, 256
LAT_B, LAT_L = 2, 2048
DEPTH = 2
PAST = 512
GRID_W = 64
HD = 128
A_HEADS, A_KV = 8, 2
C_HEADS, C_KV = 4, 2
WINDOW = 128
S5_GC = 16
S5_CH = 512
S5_G = 32
S5_P = 64
D_FF = 5632
N_MOD = 9
IN_WIDTH = 3072
ROPE_BASE = 10000.0
EPS = 1e-6
HALF_STEP = 0.5
NEG_INF = -1e30
SCALE = HD ** -0.5
LOG2E = math.log2(math.e)

T_CTX = CTX_B * CTX_L
T_LAT = LAT_B * LAT_L
T_ALL = T_CTX + T_LAT

COL_AQ, COL_AK, COL_AV = 0, 1024, 1280
COL_CQ, COL_CK, COL_CV = 1536, 2048, 2304
COL_U = 2560

S5_T = 8
S5_OCT = S5_CH // 128
S5_OG = S5_G // S5_OCT
S5_ROW = S5_T * 128
S5_ST = S5_OG * S5_P
S5_W1 = S5_ROW + 4 * S5_ST
CTX_NC = CTX_L // S5_T
LAT_NC = LAT_L // S5_T
S5_ROWS = CTX_NC * CTX_B
assert S5_ROWS == LAT_NC * LAT_B and T_CTX == T_LAT

V7X_VMEM_BYTES = 64 * 1024 * 1024
VMEM_FFN = 60 * 1024 * 1024
VMEM_BIG = 56 * 1024 * 1024
VMEM_MID = 40 * 1024 * 1024

BM = 1024
BM_OUT = 512
RC_OUT = 256
BF = 512
RC = 512
RCX = 256
FFN_PREFETCH_STEP = 2
BN_MOD = 1024
MOD_HEAD = 3 * D_MODEL
MOD_MID = 6 * D_MODEL
BM_IN = 512
BQ_A = 512
BQ_C = 256
NQ_C = 4
NB_CTX = 4
S5_UNROLL = 4


def _dot(a, b):
    return jnp.dot(a, b, preferred_element_type=F32)


def _dot_nt(a, b, precision=None):
    return lax.dot_general(a, b, (((1,), (1,)), ((), ())), preferred_element_type=F32, precision=precision)


def _dot_nt_split(a, b):
    ah = a.astype(BF16)
    bh = b.astype(BF16)
    al = (a - ah.astype(F32)).astype(BF16)
    bl = (b - bh.astype(F32)).astype(BF16)
    return _dot_nt(ah, bh) + (_dot_nt(ah, bl) + _dot_nt(al, bh))


def _rms(x, g):
    return x * lax.rsqrt(jnp.mean(x * x, axis=-1, keepdims=True) + EPS) * g


def _mod_index(i, bm):
    nct = T_CTX // bm
    return jnp.where(i < nct, 0, 1 + (i - nct) // (LAT_L // bm))


def _mod_columns(c_ref, w_ref, b_ref, o_ref):
    c = c_ref[...]
    s = (c * jax.nn.sigmoid(c)).astype(BF16)
    o_ref[...] = _dot(s, w_ref[...].astype(BF16)) + b_ref[...]


def _modulation(cvec8, w_mod, b_mod3, l, n):
    return pl.pallas_call(
        _mod_columns,
        grid=(n // BN_MOD,),
        in_specs=[
            pl.BlockSpec((8, D_MODEL), lambda j: (0, 0)),
            pl.BlockSpec((None, D_MODEL, BN_MOD), lambda j: (l, 0, j)),
            pl.BlockSpec((None, 1, BN_MOD), lambda j: (l, 0, j)),
        ],
        out_specs=pl.BlockSpec((8, BN_MOD), lambda j: (0, j)),
        out_shape=jax.ShapeDtypeStruct((8, n), F32),
        compiler_params=pltpu.CompilerParams(
            dimension_semantics=("arbitrary",), vmem_limit_bytes=VMEM_MID),
        name=f"modulation_l{l}",
    )(cvec8, w_mod, b_mod3)


def _ffn_kernel(*refs, mo, n_x, n_out):
    x_hbms = refs[:n_x]
    mod_ref, npre_ref, npost_ref, wg_ref, wu_ref, wd_ref = refs[n_x:n_x + 6]
    out_hbms = refs[n_x + 6:n_x + 6 + n_out]
    acc_ref, xc_ref, h_ref, sem_x, sem_c, sem_o = refs[n_x + 6 + n_out:]
    i = pl.program_id(0)
    j = pl.program_id(1)
    n_tiles = pl.num_programs(0)
    last = pl.num_programs(1) - 1
    slot = i % 2
    nq = RC // RCX
    nct = T_CTX // BM

    def per_stream(arrays, tile, fn):
        if len(arrays) == 1:
            fn(arrays[0], pl.multiple_of(tile * BM, BM))
        else:
            pl.when(tile < nct)(lambda: fn(arrays[0], pl.multiple_of(tile * BM, BM)))
            pl.when(tile >= nct)(lambda: fn(arrays[1], pl.multiple_of((tile - nct) * BM, BM)))

    def x_tile_copy(arr, row0, sl):
        return pltpu.make_async_copy(arr.at[pl.ds(row0, BM), :], acc_ref.at[sl], sem_x)

    def x_chunk_copy(arr, row0, cs):
        return pltpu.make_async_copy(arr.at[pl.ds(row0, RCX), :], xc_ref.at[cs], sem_c.at[cs])

    def out_copy(arr, row0, sl):
        return pltpu.make_async_copy(acc_ref.at[sl], arr.at[pl.ds(row0, BM), :], sem_o.at[sl])

    wait_x_tile = lambda sl: x_tile_copy(x_hbms[0], 0, sl).wait()
    wait_x_chunk = lambda cs: x_chunk_copy(x_hbms[0], 0, cs).wait()
    wait_out = lambda sl: out_copy(out_hbms[0], 0, sl).wait()

    def start_x_chunk(q):
        per_stream(x_hbms, i, lambda arr, row0: x_chunk_copy(
            arr, pl.multiple_of(row0 + q * RCX, RCX), q % 2).start())

    @pl.when(jnp.logical_and(i == 0, j == 0))
    def _():
        x_tile_copy(x_hbms[0], 0, 0).start()

    @pl.when(j == 0)
    def _():
        wait_x_tile(slot)

    @pl.when(j == FFN_PREFETCH_STEP)
    def _():
        pl.when(i >= 1)(lambda: wait_out(1 - slot))

        @pl.when(i + 1 < n_tiles)
        def _():
            per_stream(x_hbms, i + 1, lambda arr, row0: x_tile_copy(arr, row0, 1 - slot).start())

    @pl.when(j == last - 1)
    def _():
        start_x_chunk(0)

    def step(first, final):
        acc_slot = acc_ref.at[slot]
        wg = wg_ref[...].astype(BF16)
        wu = wu_ref[...].astype(BF16)
        wd = wd_ref[...].astype(BF16)
        for r in range(BM // RC):
            rows = slice(r * RC, (r + 1) * RC)
            if first:
                shift = mod_ref[:, mo * D_MODEL:(mo + 1) * D_MODEL]
                scale = mod_ref[:, (mo + 1) * D_MODEL:(mo + 2) * D_MODEL]
                hn = _rms(acc_slot[rows, :], npre_ref[...])
                h = (hn * (1.0 + scale) + shift).astype(BF16)
                h_ref[rows, :] = h
            else:
                h = h_ref[rows, :]
            g = _dot(h, wg)
            u = _dot(h, wu)
            a = (g * jax.nn.sigmoid(g) * u).astype(BF16)
            acc = _dot(a, wd)
            if not first:
                acc = acc_slot[rows, :] + acc
            if not final:
                acc_slot[rows, :] = acc
                continue
            gate = mod_ref[:, (mo + 2) * D_MODEL:(mo + 3) * D_MODEL]
            for qq in range(nq):
                q = r * nq + qq
                wait_x_chunk(q % 2)
                if q + 1 < BM // RCX:
                    start_x_chunk(q + 1)
                sub = slice(qq * RCX, (qq + 1) * RCX)
                y = xc_ref[q % 2] + (HALF_STEP * gate) * _rms(acc[sub, :], npost_ref[...])
                acc_slot[r * RC + qq * RCX:r * RC + (qq + 1) * RCX, :] = y
        if final:
            per_stream(out_hbms, i, lambda arr, row0: out_copy(arr, row0, slot).start())

    pl.when(j == 0)(functools.partial(step, True, False))
    pl.when(jnp.logical_and(j > 0, j < last))(functools.partial(step, False, False))
    pl.when(j == last)(functools.partial(step, False, True))

    @pl.when(jnp.logical_and(i == n_tiles - 1, j == last))
    def _():
        wait_out(slot)


def _ffn(xs, mods3, norm_pre4, norm_post4, ffn_gate, ffn_up, ffn_down, l, s, *, split_out=False):
    mo = 6 * s
    ni = 2 * s
    in_specs = [pl.BlockSpec(memory_space=pl.ANY)] * len(xs) + [
        pl.BlockSpec((None, 1, N_MOD * D_MODEL), lambda i, j: (_mod_index(i, BM), 0, 0)),
        pl.BlockSpec((None, None, 1, D_MODEL), lambda i, j: (l, ni, 0, 0)),
        pl.BlockSpec((None, None, 1, D_MODEL), lambda i, j: (l, ni, 0, 0)),
        pl.BlockSpec((None, None, D_MODEL, BF), lambda i, j: (l, s, 0, j)),
        pl.BlockSpec((None, None, D_MODEL, BF), lambda i, j: (l, s, 0, j)),
        pl.BlockSpec((None, None, BF, D_MODEL), lambda i, j: (l, s, j, 0)),
    ]
    out_rows = (T_CTX, T_LAT) if split_out else (T_ALL,)
    outs = pl.pallas_call(
        functools.partial(_ffn_kernel, mo=mo, n_x=len(xs), n_out=len(out_rows)),
        grid=(T_ALL // BM, D_FF // BF),
        in_specs=in_specs,
        out_specs=[pl.BlockSpec(memory_space=pl.ANY)] * len(out_rows),
        out_shape=[jax.ShapeDtypeStruct((r, D_MODEL), F32) for r in out_rows],
        scratch_shapes=[
            pltpu.VMEM((2, BM, D_MODEL), F32),
            pltpu.VMEM((2, RCX, D_MODEL), F32),
            pltpu.VMEM((BM, D_MODEL), BF16),
            pltpu.SemaphoreType.DMA(()),
            pltpu.SemaphoreType.DMA((2,)),
            pltpu.SemaphoreType.DMA((2,)),
        ],
        compiler_params=pltpu.CompilerParams(
            dimension_semantics=("arbitrary", "arbitrary"), vmem_limit_bytes=VMEM_FFN),
        name=f"ffn_l{l}_s{s}",
    )(*xs, mods3, norm_pre4, norm_post4, ffn_gate, ffn_up, ffn_down)
    return outs if split_out else outs[0]


def _rope(y, cos, sins):
    lane = lax.broadcasted_iota(jnp.int32, y.shape, 1)
    first = (lane & 63) < 32
    partner = jnp.where(first, pltpu.roll(y, 96, 1), pltpu.roll(y, 32, 1))
    return y * cos + partner * sins


_IN_SEGMENTS = (
    (COL_AQ, A_HEADS, "q", True, None),
    (COL_AK, A_KV, "k", True, 0),
    (COL_AV, A_KV, None, False, 1),
    (COL_CQ, C_HEADS, None, True, None),
    (COL_CK, C_KV, None, True, 2),
    (COL_CV, C_KV, None, False, 3),
)


def _inproj_kernel(x_ref, mod_ref, npre_ref, qn_ref, kn_ref, cos_ref, sin_ref, w_ref, *rest, first_layer):
    nb = BM_IN // CTX_L
    if first_layer:
        c_ref, wmod_ref, bmod_ref, qkvu_ref, uf_ref, *cache_full, modo_ref = rest
        cache_refs = [c.at[:, 0] for c in cache_full]
        _mod_columns(c_ref, wmod_ref, bmod_ref, modo_ref)
    else:
        qkvu_ref, uf_ref, *cache_refs = rest[4:]
        cache_full = ()

    def body(lat):
        shift = mod_ref[:, 3 * D_MODEL:4 * D_MODEL]
        scale = mod_ref[:, 4 * D_MODEL:5 * D_MODEL]
        h = (_rms(x_ref[...], npre_ref[...]) * (1.0 + scale) + shift).astype(BF16)
        for col0, heads, norm, rot, cache in _IN_SEGMENTS:
            p = _dot(h, w_ref[:, col0:col0 + heads * HD])
            for k in range(heads):
                y = p[:, k * HD:(k + 1) * HD]
                if norm == "q":
                    y = _rms(y, qn_ref[...])
                elif norm == "k":
                    y = _rms(y, kn_ref[...])
                if rot and lat:
                    y = _rope(y, cos_ref[...], sin_ref[...])
                qkvu_ref[:, col0 + k * HD:col0 + (k + 1) * HD] = y.astype(BF16)
                if cache is not None and not lat:
                    cache_refs[cache][:, :, k, :] = y.reshape(nb, CTX_L, HD)
        u = _dot(h, w_ref[:, COL_U:])
        uf_ref[...] = u
        qkvu_ref[:, COL_U:] = u.astype(BF16)
        if first_layer and not lat:
            for c in cache_full:
                c[:, 1:] = jnp.zeros((nb, DEPTH - 1, CTX_L, A_KV, HD), F32)

    is_lat = pl.program_id(0) >= T_CTX // BM_IN
    pl.when(is_lat)(functools.partial(body, True))
    pl.when(jnp.logical_not(is_lat))(functools.partial(body, False))


def _inproj(x, mods3, norm_pre4, q_norm3, k_norm3, cos_t, sin_t, w_in_bf, l, prev_caches, mod_args=None,
            mod_cols=None):
    bm = BM_IN
    nct = T_CTX // bm
    nb = bm // CTX_L
    tab_idx = lambda i: (jnp.maximum(i - nct, 0) % (LAT_L // bm), 0)
    cache_shape = jax.ShapeDtypeStruct((CTX_B, DEPTH, CTX_L, A_KV, HD), F32)
    first_layer = not prev_caches
    if first_layer:
        cache_spec = pl.BlockSpec((nb, DEPTH, CTX_L, A_KV, HD), lambda i: (jnp.minimum(i, nct - 1), 0, 0, 0, 0))
    else:
        cache_spec = pl.BlockSpec((nb, None, CTX_L, A_KV, HD), lambda i: (jnp.minimum(i, nct - 1), l, 0, 0, 0))
    n_in = 8
    extra_in, extra_out, extra_shape, extra_args = [], [], [], []
    if first_layer:
        lo, hi = mod_cols
        steps = T_ALL // bm
        bn = (hi - lo) // steps
        extra_in = [
            pl.BlockSpec((8, D_MODEL), lambda i: (0, 0)),
            pl.BlockSpec((None, D_MODEL, bn), lambda i: (l, 0, lo // bn + i)),
            pl.BlockSpec((None, 1, bn), lambda i: (l, 0, lo // bn + i)),
        ]
        extra_out = [pl.BlockSpec((8, bn), lambda i: (0, i))]
        extra_shape = [jax.ShapeDtypeStruct((8, hi - lo), F32)]
        extra_args = list(mod_args)
    return pl.pallas_call(
        functools.partial(_inproj_kernel, first_layer=first_layer),
        grid=(T_ALL // bm,),
        in_specs=[
            pl.BlockSpec((bm, D_MODEL), lambda i: (i, 0)),
            pl.BlockSpec((None, 1, N_MOD * D_MODEL), lambda i: (_mod_index(i, bm), 0, 0)),
            pl.BlockSpec((None, None, 1, D_MODEL), lambda i: (l, 1, 0, 0)),
            pl.BlockSpec((None, 1, HD), lambda i: (l, 0, 0)),
            pl.BlockSpec((None, 1, HD), lambda i: (l, 0, 0)),
            pl.BlockSpec((bm, HD), tab_idx),
            pl.BlockSpec((bm, HD), tab_idx),
            pl.BlockSpec((D_MODEL, IN_WIDTH), lambda i: (0, 0), pipeline_mode=pl.Buffered(1)),
        ] + extra_in + [pl.BlockSpec(memory_space=pl.ANY)] * len(prev_caches),
        out_specs=[
            pl.BlockSpec((bm, IN_WIDTH), lambda i: (i, 0)),
            pl.BlockSpec((bm, S5_CH), lambda i: (i, 0)),
        ] + [cache_spec] * 4 + extra_out,
        out_shape=[
            jax.ShapeDtypeStruct((T_ALL, IN_WIDTH), BF16),
            jax.ShapeDtypeStruct((T_ALL, S5_CH), F32),
        ] + [cache_shape] * 4 + extra_shape,
        input_output_aliases={n_in + k: 2 + k for k in range(len(prev_caches))},
        compiler_params=pltpu.CompilerParams(
            dimension_semantics=("arbitrary",), vmem_limit_bytes=VMEM_BIG),
        name=f"inproj_l{l}",
    )(x, mods3, norm_pre4, q_norm3, k_norm3, cos_t, sin_t, w_in_bf, *extra_args, *prev_caches)


def _softmax_pv(dots, values, sink=None):
    m = functools.reduce(jnp.maximum, [jnp.max(d, axis=-1, keepdims=True) for d in dots]) * SCALE
    if sink is not None:
        m = jnp.maximum(m, sink)
    m2 = m * LOG2E
    ps = [jnp.exp2(d * (SCALE * LOG2E) - m2) for d in dots]
    den = functools.reduce(jnp.add, [jnp.sum(p, axis=-1, keepdims=True) for p in ps])
    if sink is not None:
        den = den + jnp.exp2(sink * LOG2E - m2)
    o = functools.reduce(jnp.add, [_dot(p.astype(BF16), v) for p, v in zip(ps, values)])
    return o / den


def _attn_ctx_kernel(sink_ref, aq_ref, ak_ref, av_ref, cq_ref, ck_ref, cv_ref, ao_ref, co_ref):
    kv = pl.program_id(1)
    g = C_HEADS // C_KV
    for b in range(NB_CTX):
        rows = slice(b * CTX_L, (b + 1) * CTX_L)
        k = ak_ref[rows, :]
        v = av_ref[rows, :]
        for h in range(A_HEADS // A_KV):
            cols = slice(h * HD, (h + 1) * HD)
            s = _dot_nt(aq_ref[rows, cols], k)
            ao_ref[rows, cols] = _softmax_pv([s], [v]).astype(BF16)
        k = ck_ref[rows, :]
        v = cv_ref[rows, :]
        for h in range(g):
            cols = slice(h * HD, (h + 1) * HD)
            s = _dot_nt(cq_ref[rows, cols], k)
            co_ref[rows, cols] = _softmax_pv([s], [v], sink_ref[kv * g + h]).astype(BF16)


def _attn_ctx(qkvu, sink_l, l):
    ga = A_HEADS // A_KV * HD
    gc = C_HEADS // C_KV * HD
    rows = NB_CTX * CTX_L
    blk = lambda width, col0: pl.BlockSpec((rows, width), lambda b, k: (b, col0 // width + k))
    return pl.pallas_call(
        _attn_ctx_kernel,
        grid=(CTX_B // NB_CTX, A_KV),
        in_specs=[
            pl.BlockSpec(memory_space=pltpu.SMEM),
            blk(ga, COL_AQ), blk(HD, COL_AK), blk(HD, COL_AV),
            blk(gc, COL_CQ), blk(HD, COL_CK), blk(HD, COL_CV),
        ],
        out_specs=[
            pl.BlockSpec((rows, ga), lambda b, k: (b, k)),
            pl.BlockSpec((rows, gc), lambda b, k: (b, k)),
        ],
        out_shape=[
            jax.ShapeDtypeStruct((T_CTX, A_HEADS * HD), BF16),
            jax.ShapeDtypeStruct((T_CTX, C_HEADS * HD), BF16),
        ],
        compiler_params=pltpu.CompilerParams(dimension_semantics=("arbitrary", "arbitrary")),
        name=f"attn_ctx_l{l}",
    )(sink_l, qkvu, qkvu, qkvu, qkvu, qkvu, qkvu)


def _attn_lat_a_kernel(q_ref, k_ref, v_ref, kc_ref, vc_ref, *rest):
    o_ref = rest[0] if len(rest) == 1 else rest[4]
    k = k_ref[...]
    v = v_ref[...]
    kc = kc_ref[...].astype(BF16)
    vc = vc_ref[...].astype(BF16)
    for h in range(A_HEADS // A_KV):
        cols = slice(h * HD, (h + 1) * HD)
        q = q_ref[:, cols]
        s1 = _dot_nt(q, k)
        s2 = _dot_nt(q, kc)
        o_ref[:, cols] = _softmax_pv([s1, s2], [v, vc]).astype(BF16)
    if len(rest) > 1:
        c_ref, w_ref, b_ref, win_ref, _, mod_ref, winb_ref = rest
        _mod_columns(c_ref, w_ref, b_ref, mod_ref)
        winb_ref[...] = win_ref[...].astype(BF16)


def _attn_lat_a(qkvu, cache_k4, cache_v4, l, next_mod=None):
    ga = A_HEADS // A_KV * HD
    nq = LAT_L // BQ_A
    row0 = T_CTX // BQ_A
    lat_blk = T_CTX // LAT_L
    in_specs = [
        pl.BlockSpec((BQ_A, ga), lambda b, k, q: (row0 + b * nq + q, COL_AQ // ga + k)),
        pl.BlockSpec((LAT_L, HD), lambda b, k, q: (lat_blk + b, COL_AK // HD + k)),
        pl.BlockSpec((LAT_L, HD), lambda b, k, q: (lat_blk + b, COL_AV // HD + k)),
        pl.BlockSpec((None, None, PAST, HD), lambda b, k, q: (b, l, 0, k)),
        pl.BlockSpec((None, None, PAST, HD), lambda b, k, q: (b, l, 0, k)),
    ]
    out_specs = [pl.BlockSpec((BQ_A, ga), lambda b, k, q: (b * nq + q, k))]
    out_shape = [jax.ShapeDtypeStruct((T_LAT, A_HEADS * HD), BF16)]
    args = [qkvu, qkvu, qkvu, cache_k4, cache_v4]
    if next_mod is not None:
        n = N_MOD * D_MODEL
        steps = LAT_B * A_KV * nq
        bn = n // steps
        step = lambda b, k, q: (b * A_KV + k) * nq + q
        in_specs += [
            pl.BlockSpec((8, D_MODEL), lambda b, k, q: (0, 0)),
            pl.BlockSpec((None, D_MODEL, bn), lambda b, k, q: (l + 1, 0, step(b, k, q))),
            pl.BlockSpec((None, 1, bn), lambda b, k, q: (l + 1, 0, step(b, k, q))),
            pl.BlockSpec((None, D_MODEL // steps, IN_WIDTH), lambda b, k, q: (l + 1, step(b, k, q), 0)),
        ]
        out_specs += [
            pl.BlockSpec((8, bn), lambda b, k, q: (0, step(b, k, q))),
            pl.BlockSpec((D_MODEL // steps, IN_WIDTH), lambda b, k, q: (step(b, k, q), 0)),
        ]
        out_shape += [
            jax.ShapeDtypeStruct((8, n), F32),
            jax.ShapeDtypeStruct((D_MODEL, IN_WIDTH), BF16),
        ]
        args += list(next_mod)
    outs = pl.pallas_call(
        _attn_lat_a_kernel,
        grid=(LAT_B, A_KV, nq),
        in_specs=in_specs,
        out_specs=out_specs,
        out_shape=out_shape,
        compiler_params=pltpu.CompilerParams(
            dimension_semantics=("arbitrary", "arbitrary", "arbitrary"), vmem_limit_bytes=VMEM_BIG),
        name=f"attn_lat_a_l{l}",
    )(*args)
    return outs if next_mod is not None else outs[0]


def _attn_lat_c_kernel(sink_ref, q_ref, k_ref, v_ref, kc_ref, vc_ref, o_ref):
    kv = pl.program_id(1)
    span = BQ_C + 2 * WINDOW
    kc = kc_ref[...].astype(BF16)
    vc = vc_ref[...].astype(BF16)
    g = C_HEADS // C_KV
    for sub in range(NQ_C):
        n = pl.program_id(2) * NQ_C + sub
        rows = slice(sub * BQ_C, (sub + 1) * BQ_C)
        start = pl.multiple_of(jnp.clip(n * BQ_C - WINDOW, 0, LAT_L - span), WINDOW)
        kw = k_ref[pl.ds(start, span), :]
        vw = v_ref[pl.ds(start, span), :]
        qpos = n * BQ_C + lax.broadcasted_iota(jnp.int32, (BQ_C, span), 0)
        kpos = start + lax.broadcasted_iota(jnp.int32, (BQ_C, span), 1)
        valid = jnp.abs(qpos - kpos) <= WINDOW
        for h in range(g):
            cols = slice(h * HD, (h + 1) * HD)
            q = q_ref[rows, cols]
            s1 = jnp.where(valid, _dot_nt(q, kw), NEG_INF)
            s2 = _dot_nt(q, kc)
            o_ref[rows, cols] = _softmax_pv([s1, s2], [vw, vc], sink_ref[kv * g + h]).astype(BF16)


def _attn_lat_c(qkvu, sink_l, cache_k4, cache_v4, l):
    gc = C_HEADS // C_KV * HD
    bq = BQ_C * NQ_C
    nq = LAT_L // bq
    row0 = T_CTX // bq
    lat_blk = T_CTX // LAT_L
    return pl.pallas_call(
        _attn_lat_c_kernel,
        grid=(LAT_B, C_KV, nq),
        in_specs=[
            pl.BlockSpec(memory_space=pltpu.SMEM),
            pl.BlockSpec((bq, gc), lambda b, k, q: (row0 + b * nq + q, COL_CQ // gc + k)),
            pl.BlockSpec((LAT_L, HD), lambda b, k, q: (lat_blk + b, COL_CK // HD + k)),
            pl.BlockSpec((LAT_L, HD), lambda b, k, q: (lat_blk + b, COL_CV // HD + k)),
            pl.BlockSpec((None, None, PAST, HD), lambda b, k, q: (b, l, 0, k)),
            pl.BlockSpec((None, None, PAST, HD), lambda b, k, q: (b, l, 0, k)),
        ],
        out_specs=pl.BlockSpec((bq, gc), lambda b, k, q: (b * nq + q, k)),
        out_shape=jax.ShapeDtypeStruct((T_LAT, C_HEADS * HD), BF16),
        compiler_params=pltpu.CompilerParams(
            dimension_semantics=("arbitrary", "arbitrary", "arbitrary")),
        name=f"attn_lat_c_l{l}",
    )(sink_l, qkvu, qkvu, qkvu, cache_k4, cache_v4)


def _s5_param_kernel(rows_ref, bc_ref, c_ref, wmod_ref, bmod_ref, win_ref, w1_ref, w2_ref, a8_ref, mod_ref,
                     winb_ref):
    _mod_columns(c_ref, wmod_ref, bmod_ref, mod_ref)
    winb_ref[...] = win_ref[...].astype(BF16)
    a8_ref[...] = jnp.zeros_like(a8_ref)
    row_grp = lax.broadcasted_iota(jnp.int32, (128, S5_ST), 0) // S5_GC
    lane_grp = lax.broadcasted_iota(jnp.int32, (128, S5_ST), 1) // S5_P
    on_diag = row_grp == lane_grp

    def expand(a):
        return jnp.where(on_diag, jnp.concatenate([a] * S5_OG, axis=1), 0.0)

    taps = []
    for d in range(2):
        lr = rows_ref[0, d]
        li = rows_ref[1, d]
        dt = jnp.exp(rows_ref[2, d])
        mag = jnp.exp(lr * dt)
        ar = mag * jnp.cos(li * dt)
        ai = mag * jnp.sin(li * dt)
        den = lr * lr + li * li
        n_re = ar - 1.0
        f_re = (n_re * lr + ai * li) / den
        f_im = (ai * lr - n_re * li) / den
        pw = [(jnp.ones_like(ar), jnp.zeros_like(ar))]
        for _ in range(S5_T):
            pr, pi = pw[-1]
            pw.append((pr * ar - pi * ai, pr * ai + pi * ar))
        br, bi, cr, ci = (expand(bc_ref[k, d]) for k in range(4))
        bbr = f_re * br - f_im * bi
        bbi = f_re * bi + f_im * br
        xr_blocks, xi_blocks = [], []
        for t in range(S5_T):
            rows = slice(t * 128, (t + 1) * 128)
            pr, pi = pw[S5_T - 1 - t if d == 0 else t]
            xr = pr * bbr - pi * bbi
            xi = pr * bbi + pi * bbr
            c0 = S5_ROW + 2 * S5_ST * d
            w1_ref[rows, c0:c0 + S5_ST] = xr.astype(BF16)
            w1_ref[rows, c0 + S5_ST:c0 + 2 * S5_ST] = xi.astype(BF16)
            xr_blocks.append(xr)
            xi_blocks.append(xi)
            pr, pi = pw[t + 1 if d == 0 else S5_T - t]
            c0 = 2 * S5_ST * d
            w2_ref[rows, c0:c0 + S5_ST] = (cr * pr - ci * pi).astype(BF16)
            w2_ref[rows, c0 + S5_ST:c0 + 2 * S5_ST] = (-(cr * pi + ci * pr)).astype(BF16)
        a8_ref[2 * d:2 * d + 1, :] = pw[S5_T][0]
        a8_ref[2 * d + 1:2 * d + 2, :] = pw[S5_T][1]
        xr_all = jnp.concatenate(xr_blocks, axis=0)
        xi_all = jnp.concatenate(xi_blocks, axis=0)
        taps.append(_dot_nt_split(xr_all, cr) - _dot_nt_split(xi_all, ci))
    tf, tb = taps
    for t in range(S5_T):
        for t2 in range(S5_T):
            blk = None
            if t2 >= t:
                r0 = (S5_T - 1 - (t2 - t)) * 128
                blk = tf[r0:r0 + 128, :]
            if t2 <= t:
                r0 = (t - t2) * 128
                b2 = tb[r0:r0 + 128, :]
                blk = b2 if blk is None else blk + b2
            w1_ref[t * 128:(t + 1) * 128, t2 * 128:(t2 + 1) * 128] = blk.astype(BF16)


def _s5_params(s5_lam_re, s5_lam_im, s5_log_step, s5_b_re, s5_b_im, s5_c_re, s5_c_im, cvec8, w_mod, b_mod3, w_in):
    rows = jnp.stack([s5_lam_re, s5_lam_im, jnp.repeat(s5_log_step[..., None], S5_P, axis=-1)], axis=0)
    rows = rows.reshape(3, DEPTH, 2, S5_OCT, 1, S5_ST)
    bc = jnp.stack([jnp.swapaxes(s5_b_re, -1, -2), jnp.swapaxes(s5_b_im, -1, -2), s5_c_re, s5_c_im], axis=0)
    bc = bc.reshape(4, DEPTH, 2, S5_OCT, 128, S5_P)
    spec = lambda n, r, c: pl.BlockSpec((n, None, 2, None, r, c), lambda l, s: (0, l, 0, s, 0, 0))
    steps = DEPTH * S5_OCT
    step = lambda l, s: l * S5_OCT + s
    n_tail = MOD_MID - MOD_HEAD
    bn = n_tail // steps
    br = D_MODEL // steps
    return pl.pallas_call(
        _s5_param_kernel,
        grid=(DEPTH, S5_OCT),
        in_specs=[
            spec(3, 1, S5_ST), spec(4, 128, S5_P),
            pl.BlockSpec((8, D_MODEL), lambda l, s: (0, 0)),
            pl.BlockSpec((None, D_MODEL, bn), lambda l, s: (0, 0, MOD_HEAD // bn + step(l, s))),
            pl.BlockSpec((None, 1, bn), lambda l, s: (0, 0, MOD_HEAD // bn + step(l, s))),
            pl.BlockSpec((None, br, IN_WIDTH), lambda l, s: (0, step(l, s), 0)),
        ],
        out_specs=[
            pl.BlockSpec((None, None, S5_ROW, S5_W1), lambda l, s: (l, s, 0, 0)),
            pl.BlockSpec((None, None, S5_ROW, 4 * S5_ST), lambda l, s: (l, s, 0, 0)),
            pl.BlockSpec((None, None, 8, S5_ST), lambda l, s: (l, s, 0, 0)),
            pl.BlockSpec((8, bn), lambda l, s: (0, step(l, s))),
            pl.BlockSpec((br, IN_WIDTH), lambda l, s: (step(l, s), 0)),
        ],
        out_shape=[
            jax.ShapeDtypeStruct((DEPTH, S5_OCT, S5_ROW, S5_W1), BF16),
            jax.ShapeDtypeStruct((DEPTH, S5_OCT, S5_ROW, 4 * S5_ST), BF16),
            jax.ShapeDtypeStruct((DEPTH, S5_OCT, 8, S5_ST), F32),
            jax.ShapeDtypeStruct((8, n_tail), F32),
            jax.ShapeDtypeStruct((D_MODEL, IN_WIDTH), BF16),
        ],
        compiler_params=pltpu.CompilerParams(
            dimension_semantics=("arbitrary", "arbitrary"), vmem_limit_bytes=VMEM_BIG),
        name="s5_params",
    )(rows, bc, cvec8, w_mod, b_mod3, w_in)


def _s5_mix_kernel(u_ref, w1_ref, w2_ref, a8_ref, h0_ref, y_ref, hfin_ref, lhs_ref, a_ref, hp_ref, y8_ref):
    def stream(nb, nc, is_ctx):
        for b in range(nb):
            for t in range(S5_T):
                lhs_ref[t, pl.ds(b, nc, stride=nb), :] = u_ref[pl.ds(b * nc * S5_T + t, nc, stride=S5_T), :]
        lhs = jnp.concatenate([lhs_ref[t] for t in range(S5_T)], axis=1).astype(BF16)
        a_ref[...] = _dot(lhs, w1_ref[...])

        coef = [a8_ref[k:k + 1, :] for k in range(4)]
        if is_ctx:
            init = tuple(jnp.zeros((nb, S5_ST), F32) for _ in range(4))
        else:
            init = tuple(h0_ref[k] for k in range(4))

        def step(c, st):
            new = []
            for d in range(2):
                cc = c if d == 0 else nc - 1 - c
                rows = pl.ds(cc * nb, nb) if isinstance(c, int) else pl.ds(pl.multiple_of(cc * nb, nb), nb)
                hr, hi = st[2 * d], st[2 * d + 1]
                hp_ref[rows, 2 * S5_ST * d:2 * S5_ST * d + S5_ST] = hr
                hp_ref[rows, 2 * S5_ST * d + S5_ST:2 * S5_ST * (d + 1)] = hi
                c0 = S5_ROW + 2 * S5_ST * d
                gr = a_ref[rows, c0:c0 + S5_ST]
                gi = a_ref[rows, c0 + S5_ST:c0 + 2 * S5_ST]
                ar, ai = coef[2 * d], coef[2 * d + 1]
                new += [ar * hr - ai * hi + gr, ar * hi + ai * hr + gi]
            return tuple(new)

        if nb % 8 == 0:
            fin = lax.fori_loop(0, nc, step, init, unroll=S5_UNROLL)
        else:
            fin = init
            for c in range(nc):
                fin = step(c, fin)
        if is_ctx:
            for k in range(4):
                hfin_ref[k] = fin[k]

        y8 = a_ref[:, 0:S5_ROW] + _dot_nt(hp_ref[...].astype(BF16), w2_ref[...])
        for t in range(S5_T):
            y8_ref[t] = y8[:, t * 128:(t + 1) * 128]
        for b in range(nb):
            for t in range(S5_T):
                y_ref[pl.ds(b * nc * S5_T + t, nc, stride=S5_T), :] = y8_ref[t, pl.ds(b, nc, stride=nb), :]

    pl.when(pl.program_id(1) == 0)(functools.partial(stream, CTX_B, CTX_NC, True))
    pl.when(pl.program_id(1) == 1)(functools.partial(stream, LAT_B, LAT_NC, False))


def _s5_mix(uf, w1, w2, a8, h0, l):
    return pl.pallas_call(
        _s5_mix_kernel,
        grid=(S5_OCT, 2),
        in_specs=[
            pl.BlockSpec((T_CTX, 128), lambda s, k: (k, s)),
            pl.BlockSpec((None, None, S5_ROW, S5_W1), lambda s, k: (l, s, 0, 0)),
            pl.BlockSpec((None, None, S5_ROW, 4 * S5_ST), lambda s, k: (l, s, 0, 0)),
            pl.BlockSpec((None, None, 8, S5_ST), lambda s, k: (l, s, 0, 0)),
            pl.BlockSpec((None, 4, LAT_B, S5_ST), lambda s, k: (s, 0, 0, 0)),
        ],
        out_specs=[
            pl.BlockSpec((T_CTX, 128), lambda s, k: (k, s)),
            pl.BlockSpec((None, 4, CTX_B, S5_ST), lambda s, k: (s, 0, 0, 0)),
        ],
        out_shape=[
            jax.ShapeDtypeStruct((T_ALL, S5_CH), F32),
            jax.ShapeDtypeStruct((S5_OCT, 4, CTX_B, S5_ST), F32),
        ],
        scratch_shapes=[
            pltpu.VMEM((S5_T, S5_ROWS, 128), F32),
            pltpu.VMEM((S5_ROWS, S5_W1), F32),
            pltpu.VMEM((S5_ROWS, 4 * S5_ST), F32),
            pltpu.VMEM((S5_T, S5_ROWS, 128), F32),
        ],
        compiler_params=pltpu.CompilerParams(
            dimension_semantics=("arbitrary", "arbitrary"), vmem_limit_bytes=VMEM_BIG),
        name=f"s5_mix_l{l}",
    )(uf, w1, w2, a8, h0)


def _s5_gate(y, u, d, w, b):
    y = y + d * u
    z = y * (0.5 * (1.0 + jnp.tanh(math.sqrt(2.0 / math.pi) * (y + 0.044715 * (y * y * y)))))
    t = _dot(z.astype(BF16), w) + b
    return (z * jax.nn.sigmoid(t)).astype(BF16)


def _outproj_kernel(x_ref, a_ctx_ref, a_lat_ref, c_ctx_ref, c_lat_ref, y_ref, u_ref, d_ref, wglu_ref, bglu_ref,
                    mod_ref, npost_ref, w_ref, o_ref, wb_ref):
    @pl.when(pl.program_id(0) == 0)
    def _():
        nk = 4
        rk = D_MODEL // nk
        for r in range(nk):
            wb_ref[r * rk:(r + 1) * rk, :] = w_ref[r * rk:(r + 1) * rk, :].astype(BF16)

    na = A_HEADS * HD
    nc = na + C_HEADS * HD

    def body(a_ref, c_ref):
        gate = mod_ref[:, 5 * D_MODEL:6 * D_MODEL]
        wglu = wglu_ref[...].astype(BF16)
        for r in range(BM_OUT // RC_OUT):
            rows = slice(r * RC_OUT, (r + 1) * RC_OUT)
            s = _s5_gate(y_ref[rows, :], u_ref[rows, :], d_ref[...], wglu, bglu_ref[...])
            y = (_dot(a_ref[rows, :], wb_ref[0:na, :]) + _dot(c_ref[rows, :], wb_ref[na:nc, :])
                 + _dot(s, wb_ref[nc:, :]))
            o_ref[rows, :] = x_ref[rows, :] + gate * _rms(y, npost_ref[...])

    is_lat = pl.program_id(0) >= T_CTX // BM_OUT
    pl.when(jnp.logical_not(is_lat))(functools.partial(body, a_ctx_ref, c_ctx_ref))
    pl.when(is_lat)(functools.partial(body, a_lat_ref, c_lat_ref))


def _outproj(x, a_ctx, a_lat, c_ctx, c_lat, y_s5, uf, s5_d3, w_glu, b_glu3, mods3, norm_post4, w_out, l):
    bm = BM_OUT
    nct = T_CTX // bm
    ctx_idx = lambda i: (jnp.minimum(i, nct - 1), 0)
    lat_idx = lambda i: (jnp.maximum(i - nct, 0), 0)
    return pl.pallas_call(
        _outproj_kernel,
        grid=(T_ALL // bm,),
        in_specs=[
            pl.BlockSpec((bm, D_MODEL), lambda i: (i, 0)),
            pl.BlockSpec((bm, A_HEADS * HD), ctx_idx),
            pl.BlockSpec((bm, A_HEADS * HD), lat_idx),
            pl.BlockSpec((bm, C_HEADS * HD), ctx_idx),
            pl.BlockSpec((bm, C_HEADS * HD), lat_idx),
            pl.BlockSpec((bm, S5_CH), lambda i: (i, 0)),
            pl.BlockSpec((bm, S5_CH), lambda i: (i, 0)),
            pl.BlockSpec((None, 1, S5_CH), lambda i: (l, 0, 0)),
            pl.BlockSpec((None, S5_CH, S5_CH), lambda i: (l, 0, 0)),
            pl.BlockSpec((None, 1, S5_CH), lambda i: (l, 0, 0)),
            pl.BlockSpec((None, 1, N_MOD * D_MODEL), lambda i: (_mod_index(i, bm), 0, 0)),
            pl.BlockSpec((None, None, 1, D_MODEL), lambda i: (l, 1, 0, 0)),
            pl.BlockSpec((None, D_MODEL, D_MODEL), lambda i: (l, 0, 0), pipeline_mode=pl.Buffered(1)),
        ],
        out_specs=pl.BlockSpec((bm, D_MODEL), lambda i: (i, 0)),
        out_shape=jax.ShapeDtypeStruct((T_ALL, D_MODEL), F32),
        scratch_shapes=[pltpu.VMEM((D_MODEL, D_MODEL), BF16)],
        compiler_params=pltpu.CompilerParams(
            dimension_semantics=("arbitrary",), vmem_limit_bytes=VMEM_BIG),
        name=f"outproj_l{l}",
    )(x, a_ctx, a_lat, c_ctx, c_lat, y_s5, uf, s5_d3, w_glu, b_glu3, mods3, norm_post4, w_out)


def _rope_tables():
    rows = LAT_L // GRID_W
    row = jnp.repeat(jnp.arange(rows, dtype=F32), GRID_W)
    col = jnp.tile(jnp.arange(GRID_W, dtype=F32), rows)
    axis_dim = HD // 2
    inv_freq = ROPE_BASE ** (-jnp.arange(0, axis_dim, 2, dtype=F32) / axis_dim)
    ang_row = row[:, None] * inv_freq
    ang_col = col[:, None] * inv_freq
    cr, sr = jnp.cos(ang_row), jnp.sin(ang_row)
    cc, sc = jnp.cos(ang_col), jnp.sin(ang_col)
    cos_t = jnp.concatenate([cr, cr, cc, cc], axis=-1)
    sin_t = jnp.concatenate([-sr, sr, -sc, sc], axis=-1)
    return cos_t, sin_t


def kernel(x_prompt, x_sample, cache_a_k, cache_a_v, cache_c_k, cache_c_v, state_ssm_re, state_ssm_im,
           c, c_ctx, w_mod, b_mod, norm_pre, norm_post, ffn_gate, ffn_up, ffn_down, w_in, w_out,
           q_norm, k_norm, sink, s5_lam_re, s5_lam_im, s5_log_step, s5_b_re, s5_b_im, s5_c_re, s5_c_im,
           s5_d, w_glu, b_glu):
    cvec8 = jnp.concatenate([c_ctx[None, :], c, jnp.zeros((8 - 1 - LAT_B, D_MODEL), F32)], axis=0)
    b_mod3 = b_mod.reshape(DEPTH, 1, N_MOD * D_MODEL)
    norm_pre4 = norm_pre.reshape(DEPTH, 3, 1, D_MODEL)
    norm_post4 = norm_post.reshape(DEPTH, 3, 1, D_MODEL)
    q_norm3 = q_norm.reshape(DEPTH, 1, HD)
    k_norm3 = k_norm.reshape(DEPTH, 1, HD)
    s5_d3 = s5_d.reshape(DEPTH, 1, S5_CH)
    b_glu3 = b_glu.reshape(DEPTH, 1, S5_CH)
    cos_t, sin_t = _rope_tables()
    mods_head = _modulation(cvec8, w_mod, b_mod3, 0, MOD_HEAD)
    w1, w2, a8, mods_mid, w_in_bf = _s5_params(s5_lam_re, s5_lam_im, s5_log_step, s5_b_re, s5_b_im,
                                               s5_c_re, s5_c_im, cvec8, w_mod, b_mod3, w_in)
    mods_tail = jnp.zeros((8, N_MOD * D_MODEL - MOD_MID), F32)
    table = lambda tail: jnp.concatenate([mods_head, mods_mid, tail], axis=1).reshape(8, 1, N_MOD * D_MODEL)
    mods3 = table(mods_tail)
    kv4 = lambda a: a.reshape(LAT_B, DEPTH, PAST, A_KV * HD)
    cak, cav, cck, ccv = kv4(cache_a_k), kv4(cache_a_v), kv4(cache_c_k), kv4(cache_c_v)
    h0_all = jnp.stack([state_ssm_re[:, :, 0], state_ssm_im[:, :, 0],
                        state_ssm_re[:, :, 1], state_ssm_im[:, :, 1]], axis=0)
    h0_all = h0_all.reshape(4, LAT_B, DEPTH, S5_OCT, S5_ST).transpose(2, 3, 0, 1, 4)

    ffn_w = (norm_pre4, norm_post4, ffn_gate, ffn_up, ffn_down)
    new_caches = ()
    new_state = []
    xs = [x_prompt.reshape(T_CTX, D_MODEL), x_sample.reshape(T_LAT, D_MODEL)]
    for l in range(DEPTH):
        x = _ffn(xs, mods3, *ffn_w, l, 0)

        if l == 0:
            qkvu, uf, *new_caches, mods_tail = _inproj(
                x, mods3, norm_pre4, q_norm3, k_norm3, cos_t, sin_t, w_in_bf, l, (),
                mod_args=(cvec8, w_mod, b_mod3), mod_cols=(MOD_MID, N_MOD * D_MODEL))
            mods3 = table(mods_tail)
        else:
            qkvu, uf, *new_caches = _inproj(x, mods3, norm_pre4, q_norm3, k_norm3, cos_t, sin_t, w_in_bf, l,
                                            new_caches)
        a_ctx, c_ctx = _attn_ctx(qkvu, sink[l], l)
        if l < DEPTH - 1:
            a_lat, mods_next, w_in_next = _attn_lat_a(qkvu, cak, cav, l, next_mod=(cvec8, w_mod, b_mod3, w_in))
        else:
            a_lat = _attn_lat_a(qkvu, cak, cav, l)
        c_lat = _attn_lat_c(qkvu, sink[l], cck, ccv, l)
        y, hfin = _s5_mix(uf, w1, w2, a8, h0_all[l], l)
        x = _outproj(x, a_ctx, a_lat, c_ctx, c_lat, y, uf, s5_d3, w_glu, b_glu3, mods3, norm_post4, w_out, l)
        if l < DEPTH - 1:
            xs = [_ffn([x], mods3, *ffn_w, l, 1)]
            mods3 = mods_next.reshape(8, 1, N_MOD * D_MODEL)
            w_in_bf = w_in_next
        else:
            y_prompt, y_sample = _ffn([x], mods3, *ffn_w, l, 1, split_out=True)

        hf = hfin.reshape(S5_OCT, 4, CTX_B, S5_OG, S5_P).transpose(1, 2, 0, 3, 4).reshape(4, CTX_B, S5_G, S5_P)
        new_state.append((jnp.stack([hf[0], hf[2]], axis=1), jnp.stack([hf[1], hf[3]], axis=1)))

    y_prompt = y_prompt.reshape(CTX_B, CTX_L, D_MODEL)
    y_sample = y_sample.reshape(LAT_B, LAT_L, D_MODEL)
    caches = list(new_caches)
    st_re = jnp.stack([new_state[l][0] for l in range(DEPTH)], axis=1)
    st_im = jnp.stack([new_state[l][1] for l in range(DEPTH)], axis=1)
    return (y_prompt, y_sample, caches[0], caches[1], caches[2], caches[3], st_re, st_im)
```

```python
import functools
import math

import jax
import jax.numpy as jnp
from jax import lax
from jax.experimental import pallas as pl
from jax.experimental.pallas import tpu as pltpu

F32 = jnp.float32
BF16 = jnp.bfloat16

D_MODEL = 2048
CTX_B, CTX_L = 16, 256
LAT_B, LAT_L = 2, 2048
DEPTH = 2
PAST = 512
GRID_W = 64
HD = 128
A_HEADS, A_KV = 8, 2
C_HEADS, C_KV = 4, 2
WINDOW = 128
S5_GC = 16
S5_CH = 512
S5_G = 32
S5_P = 64
D_FF = 5632
N_MOD = 9
IN_WIDTH = 3072
ROPE_BASE = 10000.0
EPS = 1e-6
HALF_STEP = 0.5
NEG_INF = -1e30
SCALE = HD ** -0.5
LOG2E = math.log2(math.e)

T_CTX = CTX_B * CTX_L
T_LAT = LAT_B * LAT_L
T_ALL = T_CTX + T_LAT

COL_AQ, COL_AK, COL_AV = 0, 1024, 1280
COL_CQ, COL_CK, COL_CV = 1536, 2048, 2304
COL_U = 2560

S5_T = 8
S5_OCT = S5_CH // 128
S5_OG = S5_G // S5_OCT
S5_ROW = S5_T * 128
S5_ST = S5_OG * S5_P
S5_W1 = S5_ROW + 4 * S5_ST
CTX_NC = CTX_L // S5_T
LAT_NC = LAT_L // S5_T
S5_ROWS = CTX_NC * CTX_B
assert S5_ROWS == LAT_NC * LAT_B and T_CTX == T_LAT

V7X_VMEM_BYTES = 64 * 1024 * 1024
VMEM_FFN = 60 * 1024 * 1024
VMEM_BIG = 56 * 1024 * 1024
VMEM_MID = 40 * 1024 * 1024

BM = 1024
BM_OUT = 512
RC_OUT = 256
BF = 512
RC = 512
RCX = 256
FFN_PREFETCH_STEP = 2
BN_MOD = 1024
MOD_HEAD = 3 * D_MODEL
MOD_MID = 6 * D_MODEL
BM_IN = 512
BQ_A = 512
BQ_C = 256
NQ_C = 4
NB_CTX = 4


def _dot(a, b):
    return jnp.dot(a, b, preferred_element_type=F32)


def _dot_nt(a, b, precision=None):
    return lax.dot_general(a, b, (((1,), (1,)), ((), ())), preferred_element_type=F32, precision=precision)


def _dot_nt_split(a, b):
    ah = a.astype(BF16)
    bh = b.astype(BF16)
    al = (a - ah.astype(F32)).astype(BF16)
    bl = (b - bh.astype(F32)).astype(BF16)
    return _dot_nt(ah, bh) + (_dot_nt(ah, bl) + _dot_nt(al, bh))


def _rms(x, g):
    return x * lax.rsqrt(jnp.mean(x * x, axis=-1, keepdims=True) + EPS) * g


def _mod_index(i, bm):
    nct = T_CTX // bm
    return jnp.where(i < nct, 0, 1 + (i - nct) // (LAT_L // bm))


def _mod_columns(c_ref, w_ref, b_ref, o_ref):
    c = c_ref[...]
    s = (c * jax.nn.sigmoid(c)).astype(BF16)
    o_ref[...] = _dot(s, w_ref[...].astype(BF16)) + b_ref[...]


def _modulation(cvec8, w_mod, b_mod3, l, n):
    return pl.pallas_call(
        _mod_columns,
        grid=(n // BN_MOD,),
        in_specs=[
            pl.BlockSpec((8, D_MODEL), lambda j: (0, 0)),
            pl.BlockSpec((None, D_MODEL, BN_MOD), lambda j: (l, 0, j)),
            pl.BlockSpec((None, 1, BN_MOD), lambda j: (l, 0, j)),
        ],
        out_specs=pl.BlockSpec((8, BN_MOD), lambda j: (0, j)),
        out_shape=jax.ShapeDtypeStruct((8, n), F32),
        compiler_params=pltpu.CompilerParams(
            dimension_semantics=("arbitrary",), vmem_limit_bytes=VMEM_MID),
        name=f"modulation_l{l}",
    )(cvec8, w_mod, b_mod3)


def _ffn_kernel(*refs, mo, n_x, n_out):
    x_hbms = refs[:n_x]
    mod_ref, npre_ref, npost_ref, wg_ref, wu_ref, wd_ref = refs[n_x:n_x + 6]
    out_hbms = refs[n_x + 6:n_x + 6 + n_out]
    acc_ref, xc_ref, h_ref, sem_x, sem_c, sem_o = refs[n_x + 6 + n_out:]
    i = pl.program_id(0)
    j = pl.program_id(1)
    n_tiles = pl.num_programs(0)
    last = pl.num_programs(1) - 1
    slot = i % 2
    nq = RC // RCX
    nct = T_CTX // BM

    def per_stream(arrays, tile, fn):
        if len(arrays) == 1:
            fn(arrays[0], pl.multiple_of(tile * BM, BM))
        else:
            pl.when(tile < nct)(lambda: fn(arrays[0], pl.multiple_of(tile * BM, BM)))
            pl.when(tile >= nct)(lambda: fn(arrays[1], pl.multiple_of((tile - nct) * BM, BM)))

    def x_tile_copy(arr, row0, sl):
        return pltpu.make_async_copy(arr.at[pl.ds(row0, BM), :], acc_ref.at[sl], sem_x)

    def x_chunk_copy(arr, row0, cs):
        return pltpu.make_async_copy(arr.at[pl.ds(row0, RCX), :], xc_ref.at[cs], sem_c.at[cs])

    def out_copy(arr, row0, sl):
        return pltpu.make_async_copy(acc_ref.at[sl], arr.at[pl.ds(row0, BM), :], sem_o.at[sl])

    wait_x_tile = lambda sl: x_tile_copy(x_hbms[0], 0, sl).wait()
    wait_x_chunk = lambda cs: x_chunk_copy(x_hbms[0], 0, cs).wait()
    wait_out = lambda sl: out_copy(out_hbms[0], 0, sl).wait()

    def start_x_chunk(q):
        per_stream(x_hbms, i, lambda arr, row0: x_chunk_copy(
            arr, pl.multiple_of(row0 + q * RCX, RCX), q % 2).start())

    @pl.when(jnp.logical_and(i == 0, j == 0))
    def _():
        x_tile_copy(x_hbms[0], 0, 0).start()

    @pl.when(j == 0)
    def _():
        wait_x_tile(slot)

    @pl.when(j == FFN_PREFETCH_STEP)
    def _():
        pl.when(i >= 1)(lambda: wait_out(1 - slot))

        @pl.when(i + 1 < n_tiles)
        def _():
            per_stream(x_hbms, i + 1, lambda arr, row0: x_tile_copy(arr, row0, 1 - slot).start())

    @pl.when(j == last - 1)
    def _():
        start_x_chunk(0)

    def step(first, final):
        acc_slot = acc_ref.at[slot]
        wg = wg_ref[...].astype(BF16)
        wu = wu_ref[...].astype(BF16)
        wd = wd_ref[...].astype(BF16)
        for r in range(BM // RC):
            rows = slice(r * RC, (r + 1) * RC)
            if first:
                shift = mod_ref[:, mo * D_MODEL:(mo + 1) * D_MODEL]
                scale = mod_ref[:, (mo + 1) * D_MODEL:(mo + 2) * D_MODEL]
                hn = _rms(acc_slot[rows, :], npre_ref[...])
                h = (hn * (1.0 + scale) + shift).astype(BF16)
                h_ref[rows, :] = h
            else:
                h = h_ref[rows, :]
            g = _dot(h, wg)
            u = _dot(h, wu)
            a = (g * jax.nn.sigmoid(g) * u).astype(BF16)
            acc = _dot(a, wd)
            if not first:
                acc = acc_slot[rows, :] + acc
            if not final:
                acc_slot[rows, :] = acc
                continue
            gate = mod_ref[:, (mo + 2) * D_MODEL:(mo + 3) * D_MODEL]
            for qq in range(nq):
                q = r * nq + qq
                wait_x_chunk(q % 2)
                if q + 1 < BM // RCX:
                    start_x_chunk(q + 1)
                sub = slice(qq * RCX, (qq + 1) * RCX)
                y = xc_ref[q % 2] + (HALF_STEP * gate) * _rms(acc[sub, :], npost_ref[...])
                acc_slot[r * RC + qq * RCX:r * RC + (qq + 1) * RCX, :] = y
        if final:
            per_stream(out_hbms, i, lambda arr, row0: out_copy(arr, row0, slot).start())

    pl.when(j == 0)(functools.partial(step, True, False))
    pl.when(jnp.logical_and(j > 0, j < last))(functools.partial(step, False, False))
    pl.when(j == last)(functools.partial(step, False, True))

    @pl.when(jnp.logical_and(i == n_tiles - 1, j == last))
    def _():
        wait_out(slot)


def _ffn(xs, mods3, norm_pre4, norm_post4, ffn_gate, ffn_up, ffn_down, l, s, *, split_out=False):
    mo = 6 * s
    ni = 2 * s
    in_specs = [pl.BlockSpec(memory_space=pl.ANY)] * len(xs) + [
        pl.BlockSpec((None, 1, N_MOD * D_MODEL), lambda i, j: (_mod_index(i, BM), 0, 0)),
        pl.BlockSpec((None, None, 1, D_MODEL), lambda i, j: (l, ni, 0, 0)),
        pl.BlockSpec((None, None, 1, D_MODEL), lambda i, j: (l, ni, 0, 0)),
        pl.BlockSpec((None, None, D_MODEL, BF), lambda i, j: (l, s, 0, j)),
        pl.BlockSpec((None, None, D_MODEL, BF), lambda i, j: (l, s, 0, j)),
        pl.BlockSpec((None, None, BF, D_MODEL), lambda i, j: (l, s, j, 0)),
    ]
    out_rows = (T_CTX, T_LAT) if split_out else (T_ALL,)
    outs = pl.pallas_call(
        functools.partial(_ffn_kernel, mo=mo, n_x=len(xs), n_out=len(out_rows)),
        grid=(T_ALL // BM, D_FF // BF),
        in_specs=in_specs,
        out_specs=[pl.BlockSpec(memory_space=pl.ANY)] * len(out_rows),
        out_shape=[jax.ShapeDtypeStruct((r, D_MODEL), F32) for r in out_rows],
        scratch_shapes=[
            pltpu.VMEM((2, BM, D_MODEL), F32),
            pltpu.VMEM((2, RCX, D_MODEL), F32),
            pltpu.VMEM((BM, D_MODEL), BF16),
            pltpu.SemaphoreType.DMA(()),
            pltpu.SemaphoreType.DMA((2,)),
            pltpu.SemaphoreType.DMA((2,)),
        ],
        compiler_params=pltpu.CompilerParams(
            dimension_semantics=("arbitrary", "arbitrary"), vmem_limit_bytes=VMEM_FFN),
        name=f"ffn_l{l}_s{s}",
    )(*xs, mods3, norm_pre4, norm_post4, ffn_gate, ffn_up, ffn_down)
    return outs if split_out else outs[0]


def _rope(y, cos, sins):
    lane = lax.broadcasted_iota(jnp.int32, y.shape, 1)
    first = (lane & 63) < 32
    partner = jnp.where(first, pltpu.roll(y, 96, 1), pltpu.roll(y, 32, 1))
    return y * cos + partner * sins


_IN_SEGMENTS = (
    (COL_AQ, A_HEADS, "q", True, None),
    (COL_AK, A_KV, "k", True, 0),
    (COL_AV, A_KV, None, False, 1),
    (COL_CQ, C_HEADS, None, True, None),
    (COL_CK, C_KV, None, True, 2),
    (COL_CV, C_KV, None, False, 3),
)


def _inproj_kernel(x_ref, mod_ref, npre_ref, qn_ref, kn_ref, cos_ref, sin_ref, w_ref, *rest, first_layer):
    nb = BM_IN // CTX_L
    if first_layer:
        c_ref, wmod_ref, bmod_ref, qkvu_ref, uf_ref, *cache_full, modo_ref = rest
        cache_refs = [c.at[:, 0] for c in cache_full]
        _mod_columns(c_ref, wmod_ref, bmod_ref, modo_ref)
    else:
        qkvu_ref, uf_ref, *cache_refs = rest[4:]
        cache_full = ()

    def body(lat):
        shift = mod_ref[:, 3 * D_MODEL:4 * D_MODEL]
        scale = mod_ref[:, 4 * D_MODEL:5 * D_MODEL]
        h = (_rms(x_ref[...], npre_ref[...]) * (1.0 + scale) + shift).astype(BF16)
        for col0, heads, norm, rot, cache in _IN_SEGMENTS:
            p = _dot(h, w_ref[:, col0:col0 + heads * HD])
            for k in range(heads):
                y = p[:, k * HD:(k + 1) * HD]
                if norm == "q":
                    y = _rms(y, qn_ref[...])
                elif norm == "k":
                    y = _rms(y, kn_ref[...])
                if rot and lat:
                    y = _rope(y, cos_ref[...], sin_ref[...])
                qkvu_ref[:, col0 + k * HD:col0 + (k + 1) * HD] = y.astype(BF16)
                if cache is not None and not lat:
                    cache_refs[cache][:, :, k, :] = y.reshape(nb, CTX_L, HD)
        u = _dot(h, w_ref[:, COL_U:])
        uf_ref[...] = u
        qkvu_ref[:, COL_U:] = u.astype(BF16)
        if first_layer and not lat:
            for c in cache_full:
                c[:, 1:] = jnp.zeros((nb, DEPTH - 1, CTX_L, A_KV, HD), F32)

    is_lat = pl.program_id(0) >= T_CTX // BM_IN
    pl.when(is_lat)(functools.partial(body, True))
    pl.when(jnp.logical_not(is_lat))(functools.partial(body, False))


def _inproj(x, mods3, norm_pre4, q_norm3, k_norm3, cos_t, sin_t, w_in_bf, l, prev_caches, mod_args=None,
            mod_cols=None):
    bm = BM_IN
    nct = T_CTX // bm
    nb = bm // CTX_L
    tab_idx = lambda i: (jnp.maximum(i - nct, 0) % (LAT_L // bm), 0)
    cache_shape = jax.ShapeDtypeStruct((CTX_B, DEPTH, CTX_L, A_KV, HD), F32)
    first_layer = not prev_caches
    if first_layer:
        cache_spec = pl.BlockSpec((nb, DEPTH, CTX_L, A_KV, HD), lambda i: (jnp.minimum(i, nct - 1), 0, 0, 0, 0))
    else:
        cache_spec = pl.BlockSpec((nb, None, CTX_L, A_KV, HD), lambda i: (jnp.minimum(i, nct - 1), l, 0, 0, 0))
    n_in = 8
    extra_in, extra_out, extra_shape, extra_args = [], [], [], []
    if first_layer:
        lo, hi = mod_cols
        steps = T_ALL // bm
        bn = (hi - lo) // steps
        extra_in = [
            pl.BlockSpec((8, D_MODEL), lambda i: (0, 0)),
            pl.BlockSpec((None, D_MODEL, bn), lambda i: (l, 0, lo // bn + i)),
            pl.BlockSpec((None, 1, bn), lambda i: (l, 0, lo // bn + i)),
        ]
        extra_out = [pl.BlockSpec((8, bn), lambda i: (0, i))]
        extra_shape = [jax.ShapeDtypeStruct((8, hi - lo), F32)]
        extra_args = list(mod_args)
    return pl.pallas_call(
        functools.partial(_inproj_kernel, first_layer=first_layer),
        grid=(T_ALL // bm,),
        in_specs=[
            pl.BlockSpec((bm, D_MODEL), lambda i: (i, 0)),
            pl.BlockSpec((None, 1, N_MOD * D_MODEL), lambda i: (_mod_index(i, bm), 0, 0)),
            pl.BlockSpec((None, None, 1, D_MODEL), lambda i: (l, 1, 0, 0)),
            pl.BlockSpec((None, 1, HD), lambda i: (l, 0, 0)),
            pl.BlockSpec((None, 1, HD), lambda i: (l, 0, 0)),
            pl.BlockSpec((bm, HD), tab_idx),
            pl.BlockSpec((bm, HD), tab_idx),
            pl.BlockSpec((D_MODEL, IN_WIDTH), lambda i: (0, 0), pipeline_mode=pl.Buffered(1)),
        ] + extra_in + [pl.BlockSpec(memory_space=pl.ANY)] * len(prev_caches),
        out_specs=[
            pl.BlockSpec((bm, IN_WIDTH), lambda i: (i, 0)),
            pl.BlockSpec((bm, S5_CH), lambda i: (i, 0)),
        ] + [cache_spec] * 4 + extra_out,
        out_shape=[
            jax.ShapeDtypeStruct((T_ALL, IN_WIDTH), BF16),
            jax.ShapeDtypeStruct((T_ALL, S5_CH), F32),
        ] + [cache_shape] * 4 + extra_shape,
        input_output_aliases={n_in + k: 2 + k for k in range(len(prev_caches))},
        compiler_params=pltpu.CompilerParams(
            dimension_semantics=("arbitrary",), vmem_limit_bytes=VMEM_BIG),
        name=f"inproj_l{l}",
    )(x, mods3, norm_pre4, q_norm3, k_norm3, cos_t, sin_t, w_in_bf, *extra_args, *prev_caches)


def _softmax_pv(dots, values, sink=None):
    m = functools.reduce(jnp.maximum, [jnp.max(d, axis=-1, keepdims=True) for d in dots]) * SCALE
    if sink is not None:
        m = jnp.maximum(m, sink)
    m2 = m * LOG2E
    ps = [jnp.exp2(d * (SCALE * LOG2E) - m2) for d in dots]
    den = functools.reduce(jnp.add, [jnp.sum(p, axis=-1, keepdims=True) for p in ps])
    if sink is not None:
        den = den + jnp.exp2(sink * LOG2E - m2)
    o = functools.reduce(jnp.add, [_dot(p.astype(BF16), v) for p, v in zip(ps, values)])
    return o / den


def _attn_ctx_kernel(sink_ref, aq_ref, ak_ref, av_ref, cq_ref, ck_ref, cv_ref, ao_ref, co_ref):
    kv = pl.program_id(1)
    g = C_HEADS // C_KV
    for b in range(NB_CTX):
        rows = slice(b * CTX_L, (b + 1) * CTX_L)
        k = ak_ref[rows, :]
        v = av_ref[rows, :]
        for h in range(A_HEADS // A_KV):
            cols = slice(h * HD, (h + 1) * HD)
            s = _dot_nt(aq_ref[rows, cols], k)
            ao_ref[rows, cols] = _softmax_pv([s], [v]).astype(BF16)
        k = ck_ref[rows, :]
        v = cv_ref[rows, :]
        for h in range(g):
            cols = slice(h * HD, (h + 1) * HD)
            s = _dot_nt(cq_ref[rows, cols], k)
            co_ref[rows, cols] = _softmax_pv([s], [v], sink_ref[kv * g + h]).astype(BF16)


def _attn_ctx(qkvu, sink_l, l):
    ga = A_HEADS // A_KV * HD
    gc = C_HEADS // C_KV * HD
    rows = NB_CTX * CTX_L
    blk = lambda width, col0: pl.BlockSpec((rows, width), lambda b, k: (b, col0 // width + k))
    return pl.pallas_call(
        _attn_ctx_kernel,
        grid=(CTX_B // NB_CTX, A_KV),
        in_specs=[
            pl.BlockSpec(memory_space=pltpu.SMEM),
            blk(ga, COL_AQ), blk(HD, COL_AK), blk(HD, COL_AV),
            blk(gc, COL_CQ), blk(HD, COL_CK), blk(HD, COL_CV),
        ],
        out_specs=[
            pl.BlockSpec((rows, ga), lambda b, k: (b, k)),
            pl.BlockSpec((rows, gc), lambda b, k: (b, k)),
        ],
        out_shape=[
            jax.ShapeDtypeStruct((T_CTX, A_HEADS * HD), BF16),
            jax.ShapeDtypeStruct((T_CTX, C_HEADS * HD), BF16),
        ],
        compiler_params=pltpu.CompilerParams(dimension_semantics=("arbitrary", "arbitrary")),
        name=f"attn_ctx_l{l}",
    )(sink_l, qkvu, qkvu, qkvu, qkvu, qkvu, qkvu)


def _attn_lat_a_kernel(q_ref, k_ref, v_ref, kc_ref, vc_ref, *rest):
    o_ref = rest[0] if len(rest) == 1 else rest[4]
    k = k_ref[...]
    v = v_ref[...]
    kc = kc_ref[...].astype(BF16)
    vc = vc_ref[...].astype(BF16)
    for h in range(A_HEADS // A_KV):
        cols = slice(h * HD, (h + 1) * HD)
        q = q_ref[:, cols]
        s1 = _dot_nt(q, k)
        s2 = _dot_nt(q, kc)
        o_ref[:, cols] = _softmax_pv([s1, s2], [v, vc]).astype(BF16)
    if len(rest) > 1:
        c_ref, w_ref, b_ref, win_ref, _, mod_ref, winb_ref = rest
        _mod_columns(c_ref, w_ref, b_ref, mod_ref)
        winb_ref[...] = win_ref[...].astype(BF16)


def _attn_lat_a(qkvu, cache_k4, cache_v4, l, next_mod=None):
    ga = A_HEADS // A_KV * HD
    nq = LAT_L // BQ_A
    row0 = T_CTX // BQ_A
    lat_blk = T_CTX // LAT_L
    in_specs = [
        pl.BlockSpec((BQ_A, ga), lambda b, k, q: (row0 + b * nq + q, COL_AQ // ga + k)),
        pl.BlockSpec((LAT_L, HD), lambda b, k, q: (lat_blk + b, COL_AK // HD + k)),
        pl.BlockSpec((LAT_L, HD), lambda b, k, q: (lat_blk + b, COL_AV // HD + k)),
        pl.BlockSpec((None, None, PAST, HD), lambda b, k, q: (b, l, 0, k)),
        pl.BlockSpec((None, None, PAST, HD), lambda b, k, q: (b, l, 0, k)),
    ]
    out_specs = [pl.BlockSpec((BQ_A, ga), lambda b, k, q: (b * nq + q, k))]
    out_shape = [jax.ShapeDtypeStruct((T_LAT, A_HEADS * HD), BF16)]
    args = [qkvu, qkvu, qkvu, cache_k4, cache_v4]
    if next_mod is not None:
        n = N_MOD * D_MODEL
        steps = LAT_B * A_KV * nq
        bn = n // steps
        step = lambda b, k, q: (b * A_KV + k) * nq + q
        in_specs += [
            pl.BlockSpec((8, D_MODEL), lambda b, k, q: (0, 0)),
            pl.BlockSpec((None, D_MODEL, bn), lambda b, k, q: (l + 1, 0, step(b, k, q))),
            pl.BlockSpec((None, 1, bn), lambda b, k, q: (l + 1, 0, step(b, k, q))),
            pl.BlockSpec((None, D_MODEL // steps, IN_WIDTH), lambda b, k, q: (l + 1, step(b, k, q), 0)),
        ]
        out_specs += [
            pl.BlockSpec((8, bn), lambda b, k, q: (0, step(b, k, q))),
            pl.BlockSpec((D_MODEL // steps, IN_WIDTH), lambda b, k, q: (step(b, k, q), 0)),
        ]
        out_shape += [
            jax.ShapeDtypeStruct((8, n), F32),
            jax.ShapeDtypeStruct((D_MODEL, IN_WIDTH), BF16),
        ]
        args += list(next_mod)
    outs = pl.pallas_call(
        _attn_lat_a_kernel,
        grid=(LAT_B, A_KV, nq),
        in_specs=in_specs,
        out_specs=out_specs,
        out_shape=out_shape,
        compiler_params=pltpu.CompilerParams(
            dimension_semantics=("arbitrary", "arbitrary", "arbitrary"), vmem_limit_bytes=VMEM_BIG),
        name=f"attn_lat_a_l{l}",
    )(*args)
    return outs if next_mod is not None else outs[0]


def _attn_lat_c_kernel(sink_ref, q_ref, k_ref, v_ref, kc_ref, vc_ref, wout_ref, o_ref, woutb_ref):
    woutb_ref[...] = wout_ref[...].astype(BF16)
    kv = pl.program_id(1)
    span = BQ_C + 2 * WINDOW
    kc = kc_ref[...].astype(BF16)
    vc = vc_ref[...].astype(BF16)
    g = C_HEADS // C_KV
    for sub in range(NQ_C):
        n = pl.program_id(2) * NQ_C + sub
        rows = slice(sub * BQ_C, (sub + 1) * BQ_C)
        start = pl.multiple_of(jnp.clip(n * BQ_C - WINDOW, 0, LAT_L - span), WINDOW)
        kw = k_ref[pl.ds(start, span), :]
        vw = v_ref[pl.ds(start, span), :]
        qpos = n * BQ_C + lax.broadcasted_iota(jnp.int32, (BQ_C, span), 0)
        kpos = start + lax.broadcasted_iota(jnp.int32, (BQ_C, span), 1)
        valid = jnp.abs(qpos - kpos) <= WINDOW
        for h in range(g):
            cols = slice(h * HD, (h + 1) * HD)
            q = q_ref[rows, cols]
            s1 = jnp.where(valid, _dot_nt(q, kw), NEG_INF)
            s2 = _dot_nt(q, kc)
            o_ref[rows, cols] = _softmax_pv([s1, s2], [vw, vc], sink_ref[kv * g + h]).astype(BF16)


def _attn_lat_c(qkvu, sink_l, cache_k4, cache_v4, w_out, l):
    gc = C_HEADS // C_KV * HD
    bq = BQ_C * NQ_C
    nq = LAT_L // bq
    row0 = T_CTX // bq
    lat_blk = T_CTX // LAT_L
    steps = LAT_B * C_KV * nq
    step = lambda b, k, q: (b * C_KV + k) * nq + q
    return pl.pallas_call(
        _attn_lat_c_kernel,
        grid=(LAT_B, C_KV, nq),
        in_specs=[
            pl.BlockSpec(memory_space=pltpu.SMEM),
            pl.BlockSpec((bq, gc), lambda b, k, q: (row0 + b * nq + q, COL_CQ // gc + k)),
            pl.BlockSpec((LAT_L, HD), lambda b, k, q: (lat_blk + b, COL_CK // HD + k)),
            pl.BlockSpec((LAT_L, HD), lambda b, k, q: (lat_blk + b, COL_CV // HD + k)),
            pl.BlockSpec((None, None, PAST, HD), lambda b, k, q: (b, l, 0, k)),
            pl.BlockSpec((None, None, PAST, HD), lambda b, k, q: (b, l, 0, k)),
            pl.BlockSpec((None, D_MODEL // steps, D_MODEL), lambda b, k, q: (l, step(b, k, q), 0)),
        ],
        out_specs=[
            pl.BlockSpec((bq, gc), lambda b, k, q: (b * nq + q, k)),
            pl.BlockSpec((D_MODEL // steps, D_MODEL), lambda b, k, q: (step(b, k, q), 0)),
        ],
        out_shape=[
            jax.ShapeDtypeStruct((T_LAT, C_HEADS * HD), BF16),
            jax.ShapeDtypeStruct((D_MODEL, D_MODEL), BF16),
        ],
        compiler_params=pltpu.CompilerParams(
            dimension_semantics=("arbitrary", "arbitrary", "arbitrary")),
        name=f"attn_lat_c_l{l}",
    )(sink_l, qkvu, qkvu, qkvu, cache_k4, cache_v4, w_out)


def _s5_param_kernel(rows_ref, bc_ref, c_ref, wmod_ref, bmod_ref, win_ref, w1_ref, w2_ref, a8_ref, mod_ref,
                     winb_ref):
    _mod_columns(c_ref, wmod_ref, bmod_ref, mod_ref)
    winb_ref[...] = win_ref[...].astype(BF16)
    a8_ref[...] = jnp.zeros_like(a8_ref)
    row_grp = lax.broadcasted_iota(jnp.int32, (128, S5_ST), 0) // S5_GC
    lane_grp = lax.broadcasted_iota(jnp.int32, (128, S5_ST), 1) // S5_P
    on_diag = row_grp == lane_grp

    def expand(a):
        return jnp.where(on_diag, jnp.concatenate([a] * S5_OG, axis=1), 0.0)

    taps = []
    for d in range(2):
        lr = rows_ref[0, d]
        li = rows_ref[1, d]
        dt = jnp.exp(rows_ref[2, d])
        mag = jnp.exp(lr * dt)
        ar = mag * jnp.cos(li * dt)
        ai = mag * jnp.sin(li * dt)
        den = lr * lr + li * li
        n_re = ar - 1.0
        f_re = (n_re * lr + ai * li) / den
        f_im = (ai * lr - n_re * li) / den
        pw = [(jnp.ones_like(ar), jnp.zeros_like(ar))]
        for _ in range(S5_T):
            pr, pi = pw[-1]
            pw.append((pr * ar - pi * ai, pr * ai + pi * ar))
        br, bi, cr, ci = (expand(bc_ref[k, d]) for k in range(4))
        bbr = f_re * br - f_im * bi
        bbi = f_re * bi + f_im * br
        xr_blocks, xi_blocks = [], []
        for t in range(S5_T):
            rows = slice(t * 128, (t + 1) * 128)
            pr, pi = pw[S5_T - 1 - t if d == 0 else t]
            xr = pr * bbr - pi * bbi
            xi = pr * bbi + pi * bbr
            c0 = S5_ROW + 2 * S5_ST * d
            w1_ref[rows, c0:c0 + S5_ST] = xr.astype(BF16)
            w1_ref[rows, c0 + S5_ST:c0 + 2 * S5_ST] = xi.astype(BF16)
            xr_blocks.append(xr)
            xi_blocks.append(xi)
            pr, pi = pw[t + 1 if d == 0 else S5_T - t]
            c0 = 2 * S5_ST * d
            w2_ref[rows, c0:c0 + S5_ST] = (cr * pr - ci * pi).astype(BF16)
            w2_ref[rows, c0 + S5_ST:c0 + 2 * S5_ST] = (-(cr * pi + ci * pr)).astype(BF16)
        a8_ref[2 * d:2 * d + 1, :] = pw[S5_T][0]
        a8_ref[2 * d + 1:2 * d + 2, :] = pw[S5_T][1]
        xr_all = jnp.concatenate(xr_blocks, axis=0)
        xi_all = jnp.concatenate(xi_blocks, axis=0)
        taps.append(_dot_nt_split(xr_all, cr) - _dot_nt_split(xi_all, ci))
    tf, tb = taps
    for t in range(S5_T):
        for t2 in range(S5_T):
            blk = None
            if t2 >= t:
                r0 = (S5_T - 1 - (t2 - t)) * 128
                blk = tf[r0:r0 + 128, :]
            if t2 <= t:
                r0 = (t - t2) * 128
                b2 = tb[r0:r0 + 128, :]
                blk = b2 if blk is None else blk + b2
            w1_ref[t * 128:(t + 1) * 128, t2 * 128:(t2 + 1) * 128] = blk.astype(BF16)


def _s5_params(s5_lam_re, s5_lam_im, s5_log_step, s5_b_re, s5_b_im, s5_c_re, s5_c_im, cvec8, w_mod, b_mod3, w_in):
    rows = jnp.stack([s5_lam_re, s5_lam_im, jnp.repeat(s5_log_step[..., None], S5_P, axis=-1)], axis=0)
    rows = rows.reshape(3, DEPTH, 2, S5_OCT, 1, S5_ST)
    bc = jnp.stack([jnp.swapaxes(s5_b_re, -1, -2), jnp.swapaxes(s5_b_im, -1, -2), s5_c_re, s5_c_im], axis=0)
    bc = bc.reshape(4, DEPTH, 2, S5_OCT, 128, S5_P)
    spec = lambda n, r, c: pl.BlockSpec((n, None, 2, None, r, c), lambda l, s: (0, l, 0, s, 0, 0))
    steps = DEPTH * S5_OCT
    step = lambda l, s: l * S5_OCT + s
    n_tail = MOD_MID - MOD_HEAD
    bn = n_tail // steps
    br = D_MODEL // steps
    return pl.pallas_call(
        _s5_param_kernel,
        grid=(DEPTH, S5_OCT),
        in_specs=[
            spec(3, 1, S5_ST), spec(4, 128, S5_P),
            pl.BlockSpec((8, D_MODEL), lambda l, s: (0, 0)),
            pl.BlockSpec((None, D_MODEL, bn), lambda l, s: (0, 0, MOD_HEAD // bn + step(l, s))),
            pl.BlockSpec((None, 1, bn), lambda l, s: (0, 0, MOD_HEAD // bn + step(l, s))),
            pl.BlockSpec((None, br, IN_WIDTH), lambda l, s: (0, step(l, s), 0)),
        ],
        out_specs=[
            pl.BlockSpec((None, None, S5_ROW, S5_W1), lambda l, s: (l, s, 0, 0)),
            pl.BlockSpec((None, None, S5_ROW, 4 * S5_ST), lambda l, s: (l, s, 0, 0)),
            pl.BlockSpec((None, None, 8, S5_ST), lambda l, s: (l, s, 0, 0)),
            pl.BlockSpec((8, bn), lambda l, s: (0, step(l, s))),
            pl.BlockSpec((br, IN_WIDTH), lambda l, s: (step(l, s), 0)),
        ],
        out_shape=[
            jax.ShapeDtypeStruct((DEPTH, S5_OCT, S5_ROW, S5_W1), BF16),
            jax.ShapeDtypeStruct((DEPTH, S5_OCT, S5_ROW, 4 * S5_ST), BF16),
            jax.ShapeDtypeStruct((DEPTH, S5_OCT, 8, S5_ST), F32),
            jax.ShapeDtypeStruct((8, n_tail), F32),
            jax.ShapeDtypeStruct((D_MODEL, IN_WIDTH), BF16),
        ],
        compiler_params=pltpu.CompilerParams(
            dimension_semantics=("arbitrary", "arbitrary"), vmem_limit_bytes=VMEM_BIG),
        name="s5_params",
    )(rows, bc, cvec8, w_mod, b_mod3, w_in)


def _s5_mix_kernel(u_ref, w1_ref, w2_ref, a8_ref, h0_ref, y_ref, hfin_ref, lhs_ref, a_ref, hp_ref, y8_ref):
    def stream(nb, nc, is_ctx):
        for b in range(nb):
            for t in range(S5_T):
                lhs_ref[t, pl.ds(b, nc, stride=nb), :] = u_ref[pl.ds(b * nc * S5_T + t, nc, stride=S5_T), :]
        lhs = jnp.concatenate([lhs_ref[t] for t in range(S5_T)], axis=1).astype(BF16)
        a_ref[...] = _dot(lhs, w1_ref[:, S5_ROW:])
        y_intra = _dot(lhs, w1_ref[:, 0:S5_ROW])

        coef = [a8_ref[k:k + 1, :] for k in range(4)]
        if is_ctx:
            init = tuple(jnp.zeros((nb, S5_ST), F32) for _ in range(4))
        else:
            init = tuple(h0_ref[k] for k in range(4))

        def step(c, st):
            new = []
            for d in range(2):
                cc = c if d == 0 else nc - 1 - c
                rows = slice(cc * nb, (cc + 1) * nb)
                hr, hi = st[2 * d], st[2 * d + 1]
                hp_ref[rows, 2 * S5_ST * d:2 * S5_ST * d + S5_ST] = hr
                hp_ref[rows, 2 * S5_ST * d + S5_ST:2 * S5_ST * (d + 1)] = hi
                c0 = 2 * S5_ST * d
                gr = a_ref[rows, c0:c0 + S5_ST]
                gi = a_ref[rows, c0 + S5_ST:c0 + 2 * S5_ST]
                ar, ai = coef[2 * d], coef[2 * d + 1]
                new += [ar * hr - ai * hi + gr, ar * hi + ai * hr + gi]
            return tuple(new)

        fin = init
        for c in range(nc):
            fin = step(c, fin)
        if is_ctx:
            for k in range(4):
                hfin_ref[k] = fin[k]

        y8 = y_intra + _dot_nt(hp_ref[...].astype(BF16), w2_ref[...])
        for t in range(S5_T):
            y8_ref[t] = y8[:, t * 128:(t + 1) * 128]
        for b in range(nb):
            for t in range(S5_T):
                y_ref[pl.ds(b * nc * S5_T + t, nc, stride=S5_T), :] = y8_ref[t, pl.ds(b, nc, stride=nb), :]

    pl.when(pl.program_id(1) == 0)(functools.partial(stream, CTX_B, CTX_NC, True))
    pl.when(pl.program_id(1) == 1)(functools.partial(stream, LAT_B, LAT_NC, False))


def _s5_mix(uf, w1, w2, a8, h0, l):
    return pl.pallas_call(
        _s5_mix_kernel,
        grid=(S5_OCT, 2),
        in_specs=[
            pl.BlockSpec((T_CTX, 128), lambda s, k: (k, s)),
            pl.BlockSpec((None, None, S5_ROW, S5_W1), lambda s, k: (l, s, 0, 0)),
            pl.BlockSpec((None, None, S5_ROW, 4 * S5_ST), lambda s, k: (l, s, 0, 0)),
            pl.BlockSpec((None, None, 8, S5_ST), lambda s, k: (l, s, 0, 0)),
            pl.BlockSpec((None, 4, LAT_B, S5_ST), lambda s, k: (s, 0, 0, 0)),
        ],
        out_specs=[
            pl.BlockSpec((T_CTX, 128), lambda s, k: (k, s)),
            pl.BlockSpec((None, 4, CTX_B, S5_ST), lambda s, k: (s, 0, 0, 0)),
        ],
        out_shape=[
            jax.ShapeDtypeStruct((T_ALL, S5_CH), F32),
            jax.ShapeDtypeStruct((S5_OCT, 4, CTX_B, S5_ST), F32),
        ],
        scratch_shapes=[
            pltpu.VMEM((S5_T, S5_ROWS, 128), F32),
            pltpu.VMEM((S5_ROWS, 4 * S5_ST), F32),
            pltpu.VMEM((S5_ROWS, 4 * S5_ST), F32),
            pltpu.VMEM((S5_T, S5_ROWS, 128), F32),
        ],
        compiler_params=pltpu.CompilerParams(
            dimension_semantics=("arbitrary", "arbitrary"), vmem_limit_bytes=VMEM_BIG),
        name=f"s5_mix_l{l}",
    )(uf, w1, w2, a8, h0)


def _s5_gate(y, u, d, w, b):
    y = y + d * u
    z = y * (0.5 * (1.0 + jnp.tanh(math.sqrt(2.0 / math.pi) * (y + 0.044715 * (y * y * y)))))
    t = _dot(z.astype(BF16), w) + b
    return (z * jax.nn.sigmoid(t)).astype(BF16)


def _outproj_kernel(x_ref, a_ctx_ref, a_lat_ref, c_ctx_ref, c_lat_ref, y_ref, u_ref, d_ref, wglu_ref, bglu_ref,
                    mod_ref, npost_ref, wb_ref, o_ref):
    na = A_HEADS * HD
    nc = na + C_HEADS * HD

    def body(a_ref, c_ref):
        gate = mod_ref[:, 5 * D_MODEL:6 * D_MODEL]
        wglu = wglu_ref[...].astype(BF16)
        for r in range(BM_OUT // RC_OUT):
            rows = slice(r * RC_OUT, (r + 1) * RC_OUT)
            s = _s5_gate(y_ref[rows, :], u_ref[rows, :], d_ref[...], wglu, bglu_ref[...])
            y = (_dot(a_ref[rows, :], wb_ref[0:na, :]) + _dot(c_ref[rows, :], wb_ref[na:nc, :])
                 + _dot(s, wb_ref[nc:, :]))
            o_ref[rows, :] = x_ref[rows, :] + gate * _rms(y, npost_ref[...])

    is_lat = pl.program_id(0) >= T_CTX // BM_OUT
    pl.when(jnp.logical_not(is_lat))(functools.partial(body, a_ctx_ref, c_ctx_ref))
    pl.when(is_lat)(functools.partial(body, a_lat_ref, c_lat_ref))


def _outproj(x, a_ctx, a_lat, c_ctx, c_lat, y_s5, uf, s5_d3, w_glu, b_glu3, mods3, norm_post4, w_out_bf, l):
    bm = BM_OUT
    nct = T_CTX // bm
    ctx_idx = lambda i: (jnp.minimum(i, nct - 1), 0)
    lat_idx = lambda i: (jnp.maximum(i - nct, 0), 0)
    return pl.pallas_call(
        _outproj_kernel,
        grid=(T_ALL // bm,),
        in_specs=[
            pl.BlockSpec((bm, D_MODEL), lambda i: (i, 0)),
            pl.BlockSpec((bm, A_HEADS * HD), ctx_idx),
            pl.BlockSpec((bm, A_HEADS * HD), lat_idx),
            pl.BlockSpec((bm, C_HEADS * HD), ctx_idx),
            pl.BlockSpec((bm, C_HEADS * HD), lat_idx),
            pl.BlockSpec((bm, S5_CH), lambda i: (i, 0)),
            pl.BlockSpec((bm, S5_CH), lambda i: (i, 0)),
            pl.BlockSpec((None, 1, S5_CH), lambda i: (l, 0, 0)),
            pl.BlockSpec((None, S5_CH, S5_CH), lambda i: (l, 0, 0)),
            pl.BlockSpec((None, 1, S5_CH), lambda i: (l, 0, 0)),
            pl.BlockSpec((None, 1, N_MOD * D_MODEL), lambda i: (_mod_index(i, bm), 0, 0)),
            pl.BlockSpec((None, None, 1, D_MODEL), lambda i: (l, 1, 0, 0)),
            pl.BlockSpec((D_MODEL, D_MODEL), lambda i: (0, 0), pipeline_mode=pl.Buffered(1)),
        ],
        out_specs=pl.BlockSpec((bm, D_MODEL), lambda i: (i, 0)),
        out_shape=jax.ShapeDtypeStruct((T_ALL, D_MODEL), F32),
        compiler_params=pltpu.CompilerParams(
            dimension_semantics=("arbitrary",), vmem_limit_bytes=VMEM_BIG),
        name=f"outproj_l{l}",
    )(x, a_ctx, a_lat, c_ctx, c_lat, y_s5, uf, s5_d3, w_glu, b_glu3, mods3, norm_post4, w_out_bf)


def _rope_tables():
    rows = LAT_L // GRID_W
    row = jnp.repeat(jnp.arange(rows, dtype=F32), GRID_W)
    col = jnp.tile(jnp.arange(GRID_W, dtype=F32), rows)
    axis_dim = HD // 2
    inv_freq = ROPE_BASE ** (-jnp.arange(0, axis_dim, 2, dtype=F32) / axis_dim)
    ang_row = row[:, None] * inv_freq
    ang_col = col[:, None] * inv_freq
    cr, sr = jnp.cos(ang_row), jnp.sin(ang_row)
    cc, sc = jnp.cos(ang_col), jnp.sin(ang_col)
    cos_t = jnp.concatenate([cr, cr, cc, cc], axis=-1)
    sin_t = jnp.concatenate([-sr, sr, -sc, sc], axis=-1)
    return cos_t, sin_t


def kernel(x_prompt, x_sample, cache_a_k, cache_a_v, cache_c_k, cache_c_v, state_ssm_re, state_ssm_im,
           c, c_ctx, w_mod, b_mod, norm_pre, norm_post, ffn_gate, ffn_up, ffn_down, w_in, w_out,
           q_norm, k_norm, sink, s5_lam_re, s5_lam_im, s5_log_step, s5_b_re, s5_b_im, s5_c_re, s5_c_im,
           s5_d, w_glu, b_glu):
    cvec8 = jnp.concatenate([c_ctx[None, :], c, jnp.zeros((8 - 1 - LAT_B, D_MODEL), F32)], axis=0)
    b_mod3 = b_mod.reshape(DEPTH, 1, N_MOD * D_MODEL)
    norm_pre4 = norm_pre.reshape(DEPTH, 3, 1, D_MODEL)
    norm_post4 = norm_post.reshape(DEPTH, 3, 1, D_MODEL)
    q_norm3 = q_norm.reshape(DEPTH, 1, HD)
    k_norm3 = k_norm.reshape(DEPTH, 1, HD)
    s5_d3 = s5_d.reshape(DEPTH, 1, S5_CH)
    b_glu3 = b_glu.reshape(DEPTH, 1, S5_CH)
    cos_t, sin_t = _rope_tables()
    mods_head = _modulation(cvec8, w_mod, b_mod3, 0, MOD_HEAD)
    w1, w2, a8, mods_mid, w_in_bf = _s5_params(s5_lam_re, s5_lam_im, s5_log_step, s5_b_re, s5_b_im,
                                               s5_c_re, s5_c_im, cvec8, w_mod, b_mod3, w_in)
    mods_tail = jnp.zeros((8, N_MOD * D_MODEL - MOD_MID), F32)
    table = lambda tail: jnp.concatenate([mods_head, mods_mid, tail], axis=1).reshape(8, 1, N_MOD * D_MODEL)
    mods3 = table(mods_tail)
    kv4 = lambda a: a.reshape(LAT_B, DEPTH, PAST, A_KV * HD)
    cak, cav, cck, ccv = kv4(cache_a_k), kv4(cache_a_v), kv4(cache_c_k), kv4(cache_c_v)
    h0_all = jnp.stack([state_ssm_re[:, :, 0], state_ssm_im[:, :, 0],
                        state_ssm_re[:, :, 1], state_ssm_im[:, :, 1]], axis=0)
    h0_all = h0_all.reshape(4, LAT_B, DEPTH, S5_OCT, S5_ST).transpose(2, 3, 0, 1, 4)

    ffn_w = (norm_pre4, norm_post4, ffn_gate, ffn_up, ffn_down)
    new_caches = ()
    new_state = []
    xs = [x_prompt.reshape(T_CTX, D_MODEL), x_sample.reshape(T_LAT, D_MODEL)]
    for l in range(DEPTH):
        x = _ffn(xs, mods3, *ffn_w, l, 0)

        if l == 0:
            qkvu, uf, *new_caches, mods_tail = _inproj(
                x, mods3, norm_pre4, q_norm3, k_norm3, cos_t, sin_t, w_in_bf, l, (),
                mod_args=(cvec8, w_mod, b_mod3), mod_cols=(MOD_MID, N_MOD * D_MODEL))
            mods3 = table(mods_tail)
        else:
            qkvu, uf, *new_caches = _inproj(x, mods3, norm_pre4, q_norm3, k_norm3, cos_t, sin_t, w_in_bf, l,
                                            new_caches)
        a_ctx, c_ctx = _attn_ctx(qkvu, sink[l], l)
        if l < DEPTH - 1:
            a_lat, mods_next, w_in_next = _attn_lat_a(qkvu, cak, cav, l, next_mod=(cvec8, w_mod, b_mod3, w_in))
        else:
            a_lat = _attn_lat_a(qkvu, cak, cav, l)
        c_lat, w_out_bf = _attn_lat_c(qkvu, sink[l], cck, ccv, w_out, l)
        y, hfin = _s5_mix(uf, w1, w2, a8, h0_all[l], l)
        x = _outproj(x, a_ctx, a_lat, c_ctx, c_lat, y, uf, s5_d3, w_glu, b_glu3, mods3, norm_post4, w_out_bf, l)
        if l < DEPTH - 1:
            xs = [_ffn([x], mods3, *ffn_w, l, 1)]
            mods3 = mods_next.reshape(8, 1, N_MOD * D_MODEL)
            w_in_bf = w_in_next
        else:
            y_prompt, y_sample = _ffn([x], mods3, *ffn_w, l, 1, split_out=True)

        hf = hfin.reshape(S5_OCT, 4, CTX_B, S5_OG, S5_P).transpose(1, 2, 0, 3, 4).reshape(4, CTX_B, S5_G, S5_P)
        new_state.append((jnp.stack([hf[0], hf[2]], axis=1), jnp.stack([hf[1], hf[3]], axis=1)))

    y_prompt = y_prompt.reshape(CTX_B, CTX_L, D_MODEL)
    y_sample = y_sample.reshape(LAT_B, LAT_L, D_MODEL)
    caches = list(new_caches)
    st_re = jnp.stack([new_state[l][0] for l in range(DEPTH)], axis=1)
    st_im = jnp.stack([new_state[l][1] for l in range(DEPTH)], axis=1)
    return (y_prompt, y_sample, caches[0], caches[1], caches[2], caches[3], st_re, st_im)
```

```python
import functools
import math

import jax
import jax.numpy as jnp
from jax import lax
from jax.experimental import pallas as pl
from jax.experimental.pallas import tpu as pltpu

F32 = jnp.float32
BF16 = jnp.bfloat16

D_MODEL = 2048
CTX_B, CTX_L = 16, 256
LAT_B, LAT_L = 2, 2048
DEPTH = 2
PAST = 512
GRID_W = 64
HD = 128
A_HEADS, A_KV = 8, 2
C_HEADS, C_KV = 4, 2
WINDOW = 128
S5_GC = 16
S5_CH = 512
S5_G = 32
S5_P = 64
D_FF = 5632
N_MOD = 9
IN_WIDTH = 3072
ROPE_BASE = 10000.0
EPS = 1e-6
HALF_STEP = 0.5
NEG_INF = -1e30
SCALE = HD ** -0.5
LOG2E = math.log2(math.e)

T_CTX = CTX_B * CTX_L
T_LAT = LAT_B * LAT_L
T_ALL = T_CTX + T_LAT

COL_AQ, COL_AK, COL_AV = 0, 1024, 1280
COL_CQ, COL_CK, COL_CV = 1536, 2048, 2304
COL_U = 2560

S5_T = 8
S5_OCT = S5_CH // 128
S5_OG = S5_G // S5_OCT
S5_ROW = S5_T * 128
S5_ST = S5_OG * S5_P
S5_W1 = S5_ROW + 4 * S5_ST
CTX_NC = CTX_L // S5_T
LAT_NC = LAT_L // S5_T
S5_ROWS = CTX_NC * CTX_B
assert S5_ROWS == LAT_NC * LAT_B and T_CTX == T_LAT

V7X_VMEM_BYTES = 64 * 1024 * 1024
VMEM_FFN = 60 * 1024 * 1024
VMEM_BIG = 56 * 1024 * 1024
VMEM_MID = 40 * 1024 * 1024

BM = 1024
BM_OUT = 512
RC_OUT = 256
BF = 512
RC = 512
RCX = 256
assert RC == 2 * RCX
FFN_PREFETCH_STEP = 2
BN_MOD = 1024
MOD_HEAD = 3 * D_MODEL
MOD_MID = 6 * D_MODEL
BM_IN = 512
BQ_A = 512
BQ_C = 256
NQ_C = 4
NB_CTX = 4


def _dot(a, b):
    return jnp.dot(a, b, preferred_element_type=F32)


def _dot_nt(a, b, precision=None):
    return lax.dot_general(a, b, (((1,), (1,)), ((), ())), preferred_element_type=F32, precision=precision)


def _dot_nt_split(a, b):
    ah = a.astype(BF16)
    bh = b.astype(BF16)
    al = (a - ah.astype(F32)).astype(BF16)
    bl = (b - bh.astype(F32)).astype(BF16)
    return _dot_nt(ah, bh) + (_dot_nt(ah, bl) + _dot_nt(al, bh))


def _rms(x, g):
    return x * lax.rsqrt(jnp.mean(x * x, axis=-1, keepdims=True) + EPS) * g


def _mod_index(i, bm):
    nct = T_CTX // bm
    return jnp.where(i < nct, 0, 1 + (i - nct) // (LAT_L // bm))


def _mod_columns(c_ref, w_ref, b_ref, o_ref):
    c = c_ref[...]
    s = (c * jax.nn.sigmoid(c)).astype(BF16)
    o_ref[...] = _dot(s, w_ref[...].astype(BF16)) + b_ref[...]


def _modulation(cvec8, w_mod, b_mod3, l, n):
    return pl.pallas_call(
        _mod_columns,
        grid=(n // BN_MOD,),
        in_specs=[
            pl.BlockSpec((8, D_MODEL), lambda j: (0, 0)),
            pl.BlockSpec((None, D_MODEL, BN_MOD), lambda j: (l, 0, j)),
            pl.BlockSpec((None, 1, BN_MOD), lambda j: (l, 0, j)),
        ],
        out_specs=pl.BlockSpec((8, BN_MOD), lambda j: (0, j)),
        out_shape=jax.ShapeDtypeStruct((8, n), F32),
        compiler_params=pltpu.CompilerParams(
            dimension_semantics=("arbitrary",), vmem_limit_bytes=VMEM_MID),
        name=f"modulation_l{l}",
    )(cvec8, w_mod, b_mod3)


def _ffn_kernel(*refs, mo, n_x, n_out):
    x_hbms = refs[:n_x]
    mod_ref, npre_ref, npost_ref, wg_ref, wu_ref, wd_ref = refs[n_x:n_x + 6]
    out_hbms = refs[n_x + 6:n_x + 6 + n_out]
    acc_ref, xc_ref, h_ref, sem_x, sem_c, sem_o = refs[n_x + 6 + n_out:]
    i = pl.program_id(0)
    j = pl.program_id(1)
    n_tiles = pl.num_programs(0)
    last = pl.num_programs(1) - 1
    slot = i % 2
    nq = RC // RCX
    nct = T_CTX // BM

    def per_stream(arrays, tile, fn):
        if len(arrays) == 1:
            fn(arrays[0], pl.multiple_of(tile * BM, BM))
        else:
            pl.when(tile < nct)(lambda: fn(arrays[0], pl.multiple_of(tile * BM, BM)))
            pl.when(tile >= nct)(lambda: fn(arrays[1], pl.multiple_of((tile - nct) * BM, BM)))

    def x_tile_copy(arr, row0, sl):
        return pltpu.make_async_copy(arr.at[pl.ds(row0, BM), :], acc_ref.at[sl], sem_x)

    def x_chunk_copy(arr, row0, cs):
        return pltpu.make_async_copy(arr.at[pl.ds(row0, RCX), :], xc_ref.at[cs], sem_c.at[cs])

    def out_copy(arr, row0, sl):
        return pltpu.make_async_copy(acc_ref.at[sl], arr.at[pl.ds(row0, BM), :], sem_o.at[sl])

    wait_x_tile = lambda sl: x_tile_copy(x_hbms[0], 0, sl).wait()
    wait_x_chunk = lambda cs: x_chunk_copy(x_hbms[0], 0, cs).wait()
    wait_out = lambda sl: out_copy(out_hbms[0], 0, sl).wait()

    def start_x_chunk(q):
        per_stream(x_hbms, i, lambda arr, row0: x_chunk_copy(
            arr, pl.multiple_of(row0 + q * RCX, RCX), q % 2).start())

    @pl.when(jnp.logical_and(i == 0, j == 0))
    def _():
        x_tile_copy(x_hbms[0], 0, 0).start()

    @pl.when(j == 0)
    def _():
        wait_x_tile(slot)

    @pl.when(j == FFN_PREFETCH_STEP)
    def _():
        pl.when(i >= 1)(lambda: wait_out(1 - slot))

        @pl.when(i + 1 < n_tiles)
        def _():
            per_stream(x_hbms, i + 1, lambda arr, row0: x_tile_copy(arr, row0, 1 - slot).start())

    @pl.when(j == last - 1)
    def _():
        for q in range(nq):
            start_x_chunk(q)

    def step(first, final):
        if final:
            for q in range(nq):
                wait_x_chunk(q)
        acc_slot = acc_ref.at[slot]
        wg = wg_ref[...].astype(BF16)
        wu = wu_ref[...].astype(BF16)
        wd = wd_ref[...].astype(BF16)
        for r in range(BM // RC):
            rows = slice(r * RC, (r + 1) * RC)
            if first:
                shift = mod_ref[:, mo * D_MODEL:(mo + 1) * D_MODEL]
                scale = mod_ref[:, (mo + 1) * D_MODEL:(mo + 2) * D_MODEL]
                hn = _rms(acc_slot[rows, :], npre_ref[...])
                h = (hn * (1.0 + scale) + shift).astype(BF16)
                h_ref[rows, :] = h
            else:
                h = h_ref[rows, :]
            g = _dot(h, wg)
            u = _dot(h, wu)
            a = (g * jax.nn.sigmoid(g) * u).astype(BF16)
            acc = _dot(a, wd)
            if not first:
                acc = acc_slot[rows, :] + acc
            if not final:
                acc_slot[rows, :] = acc
                continue
            gate = mod_ref[:, (mo + 2) * D_MODEL:(mo + 3) * D_MODEL]
            if r > 0:
                for qq in range(nq):
                    wait_x_chunk(qq)
            for qq in range(nq):
                sub = slice(qq * RCX, (qq + 1) * RCX)
                y = xc_ref[qq] + (HALF_STEP * gate) * _rms(acc[sub, :], npost_ref[...])
                acc_slot[r * RC + qq * RCX:r * RC + (qq + 1) * RCX, :] = y
            if r + 1 < BM // RC:
                for qq in range(nq):
                    start_x_chunk((r + 1) * nq + qq)
        if final:
            per_stream(out_hbms, i, lambda arr, row0: out_copy(arr, row0, slot).start())

    pl.when(j == 0)(functools.partial(step, True, False))
    pl.when(jnp.logical_and(j > 0, j < last))(functools.partial(step, False, False))
    pl.when(j == last)(functools.partial(step, False, True))

    @pl.when(jnp.logical_and(i == n_tiles - 1, j == last))
    def _():
        wait_out(slot)


def _ffn(xs, mods3, norm_pre4, norm_post4, ffn_gate, ffn_up, ffn_down, l, s, *, split_out=False):
    mo = 6 * s
    ni = 2 * s
    in_specs = [pl.BlockSpec(memory_space=pl.ANY)] * len(xs) + [
        pl.BlockSpec((None, 1, N_MOD * D_MODEL), lambda i, j: (_mod_index(i, BM), 0, 0)),
        pl.BlockSpec((None, None, 1, D_MODEL), lambda i, j: (l, ni, 0, 0)),
        pl.BlockSpec((None, None, 1, D_MODEL), lambda i, j: (l, ni, 0, 0)),
        pl.BlockSpec((None, None, D_MODEL, BF), lambda i, j: (l, s, 0, j)),
        pl.BlockSpec((None, None, D_MODEL, BF), lambda i, j: (l, s, 0, j)),
        pl.BlockSpec((None, None, BF, D_MODEL), lambda i, j: (l, s, j, 0)),
    ]
    out_rows = (T_CTX, T_LAT) if split_out else (T_ALL,)
    outs = pl.pallas_call(
        functools.partial(_ffn_kernel, mo=mo, n_x=len(xs), n_out=len(out_rows)),
        grid=(T_ALL // BM, D_FF // BF),
        in_specs=in_specs,
        out_specs=[pl.BlockSpec(memory_space=pl.ANY)] * len(out_rows),
        out_shape=[jax.ShapeDtypeStruct((r, D_MODEL), F32) for r in out_rows],
        scratch_shapes=[
            pltpu.VMEM((2, BM, D_MODEL), F32),
            pltpu.VMEM((2, RCX, D_MODEL), F32),
            pltpu.VMEM((BM, D_MODEL), BF16),
            pltpu.SemaphoreType.DMA(()),
            pltpu.SemaphoreType.DMA((2,)),
            pltpu.SemaphoreType.DMA((2,)),
        ],
        compiler_params=pltpu.CompilerParams(
            dimension_semantics=("arbitrary", "arbitrary"), vmem_limit_bytes=VMEM_FFN),
        name=f"ffn_l{l}_s{s}",
    )(*xs, mods3, norm_pre4, norm_post4, ffn_gate, ffn_up, ffn_down)
    return outs if split_out else outs[0]


def _rope(y, cos, sins):
    lane = lax.broadcasted_iota(jnp.int32, y.shape, 1)
    first = (lane & 63) < 32
    partner = jnp.where(first, pltpu.roll(y, 96, 1), pltpu.roll(y, 32, 1))
    return y * cos + partner * sins


_IN_SEGMENTS = (
    (COL_AQ, A_HEADS, "q", True, None),
    (COL_AK, A_KV, "k", True, 0),
    (COL_AV, A_KV, None, False, 1),
    (COL_CQ, C_HEADS, None, True, None),
    (COL_CK, C_KV, None, True, 2),
    (COL_CV, C_KV, None, False, 3),
)


def _inproj_kernel(x_ref, mod_ref, npre_ref, qn_ref, kn_ref, cos_ref, sin_ref, w_ref, *rest, first_layer):
    nb = BM_IN // CTX_L
    if first_layer:
        c_ref, wmod_ref, bmod_ref, qkvu_ref, uf_ref, *cache_full, modo_ref = rest
        cache_refs = [c.at[:, 0] for c in cache_full]
        _mod_columns(c_ref, wmod_ref, bmod_ref, modo_ref)
    else:
        qkvu_ref, uf_ref, *cache_refs = rest[4:]
        cache_full = ()

    def body(lat):
        shift = mod_ref[:, 3 * D_MODEL:4 * D_MODEL]
        scale = mod_ref[:, 4 * D_MODEL:5 * D_MODEL]
        h = (_rms(x_ref[...], npre_ref[...]) * (1.0 + scale) + shift).astype(BF16)
        for col0, heads, norm, rot, cache in _IN_SEGMENTS:
            p = _dot(h, w_ref[:, col0:col0 + heads * HD])
            for k in range(heads):
                y = p[:, k * HD:(k + 1) * HD]
                if norm == "q":
                    y = _rms(y, qn_ref[...])
                elif norm == "k":
                    y = _rms(y, kn_ref[...])
                if rot and lat:
                    y = _rope(y, cos_ref[...], sin_ref[...])
                qkvu_ref[:, col0 + k * HD:col0 + (k + 1) * HD] = y.astype(BF16)
                if cache is not None and not lat:
                    cache_refs[cache][:, :, k, :] = y.reshape(nb, CTX_L, HD)
        u = _dot(h, w_ref[:, COL_U:])
        uf_ref[...] = u
        qkvu_ref[:, COL_U:] = u.astype(BF16)
        if first_layer and not lat:
            for c in cache_full:
                c[:, 1:] = jnp.zeros((nb, DEPTH - 1, CTX_L, A_KV, HD), F32)

    is_lat = pl.program_id(0) >= T_CTX // BM_IN
    pl.when(is_lat)(functools.partial(body, True))
    pl.when(jnp.logical_not(is_lat))(functools.partial(body, False))


def _inproj(x, mods3, norm_pre4, q_norm3, k_norm3, cos_t, sin_t, w_in_bf, l, prev_caches, mod_args=None,
            mod_cols=None):
    bm = BM_IN
    nct = T_CTX // bm
    nb = bm // CTX_L
    tab_idx = lambda i: (jnp.maximum(i - nct, 0) % (LAT_L // bm), 0)
    cache_shape = jax.ShapeDtypeStruct((CTX_B, DEPTH, CTX_L, A_KV, HD), F32)
    first_layer = not prev_caches
    if first_layer:
        cache_spec = pl.BlockSpec((nb, DEPTH, CTX_L, A_KV, HD), lambda i: (jnp.minimum(i, nct - 1), 0, 0, 0, 0))
    else:
        cache_spec = pl.BlockSpec((nb, None, CTX_L, A_KV, HD), lambda i: (jnp.minimum(i, nct - 1), l, 0, 0, 0))
    n_in = 8
    extra_in, extra_out, extra_shape, extra_args = [], [], [], []
    if first_layer:
        lo, hi = mod_cols
        steps = T_ALL // bm
        bn = (hi - lo) // steps
        extra_in = [
            pl.BlockSpec((8, D_MODEL), lambda i: (0, 0)),
            pl.BlockSpec((None, D_MODEL, bn), lambda i: (l, 0, lo // bn + i)),
            pl.BlockSpec((None, 1, bn), lambda i: (l, 0, lo // bn + i)),
        ]
        extra_out = [pl.BlockSpec((8, bn), lambda i: (0, i))]
        extra_shape = [jax.ShapeDtypeStruct((8, hi - lo), F32)]
        extra_args = list(mod_args)
    return pl.pallas_call(
        functools.partial(_inproj_kernel, first_layer=first_layer),
        grid=(T_ALL // bm,),
        in_specs=[
            pl.BlockSpec((bm, D_MODEL), lambda i: (i, 0)),
            pl.BlockSpec((None, 1, N_MOD * D_MODEL), lambda i: (_mod_index(i, bm), 0, 0)),
            pl.BlockSpec((None, None, 1, D_MODEL), lambda i: (l, 1, 0, 0)),
            pl.BlockSpec((None, 1, HD), lambda i: (l, 0, 0)),
            pl.BlockSpec((None, 1, HD), lambda i: (l, 0, 0)),
            pl.BlockSpec((bm, HD), tab_idx),
            pl.BlockSpec((bm, HD), tab_idx),
            pl.BlockSpec((D_MODEL, IN_WIDTH), lambda i: (0, 0), pipeline_mode=pl.Buffered(1)),
        ] + extra_in + [pl.BlockSpec(memory_space=pl.ANY)] * len(prev_caches),
        out_specs=[
            pl.BlockSpec((bm, IN_WIDTH), lambda i: (i, 0)),
            pl.BlockSpec((bm, S5_CH), lambda i: (i, 0)),
        ] + [cache_spec] * 4 + extra_out,
        out_shape=[
            jax.ShapeDtypeStruct((T_ALL, IN_WIDTH), BF16),
            jax.ShapeDtypeStruct((T_ALL, S5_CH), F32),
        ] + [cache_shape] * 4 + extra_shape,
        input_output_aliases={n_in + k: 2 + k for k in range(len(prev_caches))},
        compiler_params=pltpu.CompilerParams(
            dimension_semantics=("arbitrary",), vmem_limit_bytes=VMEM_BIG),
        name=f"inproj_l{l}",
    )(x, mods3, norm_pre4, q_norm3, k_norm3, cos_t, sin_t, w_in_bf, *extra_args, *prev_caches)


def _softmax_pv(dots, values, sink=None):
    m = functools.reduce(jnp.maximum, [jnp.max(d, axis=-1, keepdims=True) for d in dots]) * SCALE
    if sink is not None:
        m = jnp.maximum(m, sink)
    m2 = m * LOG2E
    ps = [jnp.exp2(d * (SCALE * LOG2E) - m2) for d in dots]
    den = functools.reduce(jnp.add, [jnp.sum(p, axis=-1, keepdims=True) for p in ps])
    if sink is not None:
        den = den + jnp.exp2(sink * LOG2E - m2)
    o = functools.reduce(jnp.add, [_dot(p.astype(BF16), v) for p, v in zip(ps, values)])
    return o / den


def _attn_ctx_kernel(sink_ref, aq_ref, ak_ref, av_ref, cq_ref, ck_ref, cv_ref, ao_ref, co_ref):
    kv = pl.program_id(1)
    g = C_HEADS // C_KV
    for b in range(NB_CTX):
        rows = slice(b * CTX_L, (b + 1) * CTX_L)
        k = ak_ref[rows, :]
        v = av_ref[rows, :]
        for h in range(A_HEADS // A_KV):
            cols = slice(h * HD, (h + 1) * HD)
            s = _dot_nt(aq_ref[rows, cols], k)
            ao_ref[rows, cols] = _softmax_pv([s], [v]).astype(BF16)
        k = ck_ref[rows, :]
        v = cv_ref[rows, :]
        for h in range(g):
            cols = slice(h * HD, (h + 1) * HD)
            s = _dot_nt(cq_ref[rows, cols], k)
            co_ref[rows, cols] = _softmax_pv([s], [v], sink_ref[kv * g + h]).astype(BF16)


def _attn_ctx(qkvu, sink_l, l):
    ga = A_HEADS // A_KV * HD
    gc = C_HEADS // C_KV * HD
    rows = NB_CTX * CTX_L
    blk = lambda width, col0: pl.BlockSpec((rows, width), lambda b, k: (b, col0 // width + k))
    return pl.pallas_call(
        _attn_ctx_kernel,
        grid=(CTX_B // NB_CTX, A_KV),
        in_specs=[
            pl.BlockSpec(memory_space=pltpu.SMEM),
            blk(ga, COL_AQ), blk(HD, COL_AK), blk(HD, COL_AV),
            blk(gc, COL_CQ), blk(HD, COL_CK), blk(HD, COL_CV),
        ],
        out_specs=[
            pl.BlockSpec((rows, ga), lambda b, k: (b, k)),
            pl.BlockSpec((rows, gc), lambda b, k: (b, k)),
        ],
        out_shape=[
            jax.ShapeDtypeStruct((T_CTX, A_HEADS * HD), BF16),
            jax.ShapeDtypeStruct((T_CTX, C_HEADS * HD), BF16),
        ],
        compiler_params=pltpu.CompilerParams(dimension_semantics=("arbitrary", "arbitrary")),
        name=f"attn_ctx_l{l}",
    )(sink_l, qkvu, qkvu, qkvu, qkvu, qkvu, qkvu)


def _attn_lat_a_kernel(q_ref, k_ref, v_ref, kc_ref, vc_ref, *rest):
    o_ref = rest[0] if len(rest) == 1 else rest[4]
    k = k_ref[...]
    v = v_ref[...]
    kc = kc_ref[...].astype(BF16)
    vc = vc_ref[...].astype(BF16)
    for h in range(A_HEADS // A_KV):
        cols = slice(h * HD, (h + 1) * HD)
        q = q_ref[:, cols]
        s1 = _dot_nt(q, k)
        s2 = _dot_nt(q, kc)
        o_ref[:, cols] = _softmax_pv([s1, s2], [v, vc]).astype(BF16)
    if len(rest) > 1:
        c_ref, w_ref, b_ref, win_ref, _, mod_ref, winb_ref = rest
        _mod_columns(c_ref, w_ref, b_ref, mod_ref)
        winb_ref[...] = win_ref[...].astype(BF16)


def _attn_lat_a(qkvu, cache_k4, cache_v4, l, next_mod=None):
    ga = A_HEADS // A_KV * HD
    nq = LAT_L // BQ_A
    row0 = T_CTX // BQ_A
    lat_blk = T_CTX // LAT_L
    in_specs = [
        pl.BlockSpec((BQ_A, ga), lambda b, k, q: (row0 + b * nq + q, COL_AQ // ga + k)),
        pl.BlockSpec((LAT_L, HD), lambda b, k, q: (lat_blk + b, COL_AK // HD + k)),
        pl.BlockSpec((LAT_L, HD), lambda b, k, q: (lat_blk + b, COL_AV // HD + k)),
        pl.BlockSpec((None, None, PAST, HD), lambda b, k, q: (b, l, 0, k)),
        pl.BlockSpec((None, None, PAST, HD), lambda b, k, q: (b, l, 0, k)),
    ]
    out_specs = [pl.BlockSpec((BQ_A, ga), lambda b, k, q: (b * nq + q, k))]
    out_shape = [jax.ShapeDtypeStruct((T_LAT, A_HEADS * HD), BF16)]
    args = [qkvu, qkvu, qkvu, cache_k4, cache_v4]
    if next_mod is not None:
        n = N_MOD * D_MODEL
        steps = LAT_B * A_KV * nq
        bn = n // steps
        step = lambda b, k, q: (b * A_KV + k) * nq + q
        in_specs += [
            pl.BlockSpec((8, D_MODEL), lambda b, k, q: (0, 0)),
            pl.BlockSpec((None, D_MODEL, bn), lambda b, k, q: (l + 1, 0, step(b, k, q))),
            pl.BlockSpec((None, 1, bn), lambda b, k, q: (l + 1, 0, step(b, k, q))),
            pl.BlockSpec((None, D_MODEL // steps, IN_WIDTH), lambda b, k, q: (l + 1, step(b, k, q), 0)),
        ]
        out_specs += [
            pl.BlockSpec((8, bn), lambda b, k, q: (0, step(b, k, q))),
            pl.BlockSpec((D_MODEL // steps, IN_WIDTH), lambda b, k, q: (step(b, k, q), 0)),
        ]
        out_shape += [
            jax.ShapeDtypeStruct((8, n), F32),
            jax.ShapeDtypeStruct((D_MODEL, IN_WIDTH), BF16),
        ]
        args += list(next_mod)
    outs = pl.pallas_call(
        _attn_lat_a_kernel,
        grid=(LAT_B, A_KV, nq),
        in_specs=in_specs,
        out_specs=out_specs,
        out_shape=out_shape,
        compiler_params=pltpu.CompilerParams(
            dimension_semantics=("arbitrary", "arbitrary", "arbitrary"), vmem_limit_bytes=VMEM_BIG),
        name=f"attn_lat_a_l{l}",
    )(*args)
    return outs if next_mod is not None else outs[0]


def _attn_lat_c_kernel(sink_ref, q_ref, k_ref, v_ref, kc_ref, vc_ref, wout_ref, o_ref, woutb_ref):
    woutb_ref[...] = wout_ref[...].astype(BF16)
    kv = pl.program_id(1)
    span = BQ_C + 2 * WINDOW
    kc = kc_ref[...].astype(BF16)
    vc = vc_ref[...].astype(BF16)
    g = C_HEADS // C_KV
    for sub in range(NQ_C):
        n = pl.program_id(2) * NQ_C + sub
        rows = slice(sub * BQ_C, (sub + 1) * BQ_C)
        start = pl.multiple_of(jnp.clip(n * BQ_C - WINDOW, 0, LAT_L - span), WINDOW)
        kw = k_ref[pl.ds(start, span), :]
        vw = v_ref[pl.ds(start, span), :]
        qpos = n * BQ_C + lax.broadcasted_iota(jnp.int32, (BQ_C, span), 0)
        kpos = start + lax.broadcasted_iota(jnp.int32, (BQ_C, span), 1)
        valid = jnp.abs(qpos - kpos) <= WINDOW
        for h in range(g):
            cols = slice(h * HD, (h + 1) * HD)
            q = q_ref[rows, cols]
            s1 = jnp.where(valid, _dot_nt(q, kw), NEG_INF)
            s2 = _dot_nt(q, kc)
            o_ref[rows, cols] = _softmax_pv([s1, s2], [vw, vc], sink_ref[kv * g + h]).astype(BF16)


def _attn_lat_c(qkvu, sink_l, cache_k4, cache_v4, w_out, l):
    gc = C_HEADS // C_KV * HD
    bq = BQ_C * NQ_C
    nq = LAT_L // bq
    row0 = T_CTX // bq
    lat_blk = T_CTX // LAT_L
    steps = LAT_B * C_KV * nq
    step = lambda b, k, q: (b * C_KV + k) * nq + q
    return pl.pallas_call(
        _attn_lat_c_kernel,
        grid=(LAT_B, C_KV, nq),
        in_specs=[
            pl.BlockSpec(memory_space=pltpu.SMEM),
            pl.BlockSpec((bq, gc), lambda b, k, q: (row0 + b * nq + q, COL_CQ // gc + k)),
            pl.BlockSpec((LAT_L, HD), lambda b, k, q: (lat_blk + b, COL_CK // HD + k)),
            pl.BlockSpec((LAT_L, HD), lambda b, k, q: (lat_blk + b, COL_CV // HD + k)),
            pl.BlockSpec((None, None, PAST, HD), lambda b, k, q: (b, l, 0, k)),
            pl.BlockSpec((None, None, PAST, HD), lambda b, k, q: (b, l, 0, k)),
            pl.BlockSpec((None, D_MODEL // steps, D_MODEL), lambda b, k, q: (l, step(b, k, q), 0)),
        ],
        out_specs=[
            pl.BlockSpec((bq, gc), lambda b, k, q: (b * nq + q, k)),
            pl.BlockSpec((D_MODEL // steps, D_MODEL), lambda b, k, q: (step(b, k, q), 0)),
        ],
        out_shape=[
            jax.ShapeDtypeStruct((T_LAT, C_HEADS * HD), BF16),
            jax.ShapeDtypeStruct((D_MODEL, D_MODEL), BF16),
        ],
        compiler_params=pltpu.CompilerParams(
            dimension_semantics=("arbitrary", "arbitrary", "arbitrary")),
        name=f"attn_lat_c_l{l}",
    )(sink_l, qkvu, qkvu, qkvu, cache_k4, cache_v4, w_out)


def _s5_param_kernel(rows_ref, bc_ref, c_ref, wmod_ref, bmod_ref, win_ref, w1_ref, w2_ref, a8_ref, mod_ref,
                     winb_ref):
    _mod_columns(c_ref, wmod_ref, bmod_ref, mod_ref)
    winb_ref[...] = win_ref[...].astype(BF16)
    a8_ref[...] = jnp.zeros_like(a8_ref)
    row_grp = lax.broadcasted_iota(jnp.int32, (128, S5_ST), 0) // S5_GC
    lane_grp = lax.broadcasted_iota(jnp.int32, (128, S5_ST), 1) // S5_P
    on_diag = row_grp == lane_grp

    def expand(a):
        return jnp.where(on_diag, jnp.concatenate([a] * S5_OG, axis=1), 0.0)

    taps = []
    for d in range(2):
        lr = rows_ref[0, d]
        li = rows_ref[1, d]
        dt = jnp.exp(rows_ref[2, d])
        mag = jnp.exp(lr * dt)
        ar = mag * jnp.cos(li * dt)
        ai = mag * jnp.sin(li * dt)
        den = lr * lr + li * li
        n_re = ar - 1.0
        f_re = (n_re * lr + ai * li) / den
        f_im = (ai * lr - n_re * li) / den
        pw = [(jnp.ones_like(ar), jnp.zeros_like(ar))]
        for _ in range(S5_T):
            pr, pi = pw[-1]
            pw.append((pr * ar - pi * ai, pr * ai + pi * ar))
        br, bi, cr, ci = (expand(bc_ref[k, d]) for k in range(4))
        bbr = f_re * br - f_im * bi
        bbi = f_re * bi + f_im * br
        xr_blocks, xi_blocks = [], []
        for t in range(S5_T):
            rows = slice(t * 128, (t + 1) * 128)
            pr, pi = pw[S5_T - 1 - t if d == 0 else t]
            xr = pr * bbr - pi * bbi
            xi = pr * bbi + pi * bbr
            c0 = S5_ROW + 2 * S5_ST * d
            w1_ref[rows, c0:c0 + S5_ST] = xr.astype(BF16)
            w1_ref[rows, c0 + S5_ST:c0 + 2 * S5_ST] = xi.astype(BF16)
            xr_blocks.append(xr)
            xi_blocks.append(xi)
            pr, pi = pw[t + 1 if d == 0 else S5_T - t]
            c0 = 2 * S5_ST * d
            w2_ref[rows, c0:c0 + S5_ST] = (cr * pr - ci * pi).astype(BF16)
            w2_ref[rows, c0 + S5_ST:c0 + 2 * S5_ST] = (-(cr * pi + ci * pr)).astype(BF16)
        a8_ref[2 * d:2 * d + 1, :] = pw[S5_T][0]
        a8_ref[2 * d + 1:2 * d + 2, :] = pw[S5_T][1]
        xr_all = jnp.concatenate(xr_blocks, axis=0)
        xi_all = jnp.concatenate(xi_blocks, axis=0)
        taps.append(_dot_nt_split(xr_all, cr) - _dot_nt_split(xi_all, ci))
    tf, tb = taps
    for t in range(S5_T):
        for t2 in range(S5_T):
            blk = None
            if t2 >= t:
                r0 = (S5_T - 1 - (t2 - t)) * 128
                blk = tf[r0:r0 + 128, :]
            if t2 <= t:
                r0 = (t - t2) * 128
                b2 = tb[r0:r0 + 128, :]
                blk = b2 if blk is None else blk + b2
            w1_ref[t * 128:(t + 1) * 128, t2 * 128:(t2 + 1) * 128] = blk.astype(BF16)


def _s5_params(s5_lam_re, s5_lam_im, s5_log_step, s5_b_re, s5_b_im, s5_c_re, s5_c_im, cvec8, w_mod, b_mod3, w_in):
    rows = jnp.stack([s5_lam_re, s5_lam_im, jnp.repeat(s5_log_step[..., None], S5_P, axis=-1)], axis=0)
    rows = rows.reshape(3, DEPTH, 2, S5_OCT, 1, S5_ST)
    bc = jnp.stack([jnp.swapaxes(s5_b_re, -1, -2), jnp.swapaxes(s5_b_im, -1, -2), s5_c_re, s5_c_im], axis=0)
    bc = bc.reshape(4, DEPTH, 2, S5_OCT, 128, S5_P)
    spec = lambda n, r, c: pl.BlockSpec((n, None, 2, None, r, c), lambda l, s: (0, l, 0, s, 0, 0))
    steps = DEPTH * S5_OCT
    step = lambda l, s: l * S5_OCT + s
    n_tail = MOD_MID - MOD_HEAD
    bn = n_tail // steps
    br = D_MODEL // steps
    return pl.pallas_call(
        _s5_param_kernel,
        grid=(DEPTH, S5_OCT),
        in_specs=[
            spec(3, 1, S5_ST), spec(4, 128, S5_P),
            pl.BlockSpec((8, D_MODEL), lambda l, s: (0, 0)),
            pl.BlockSpec((None, D_MODEL, bn), lambda l, s: (0, 0, MOD_HEAD // bn + step(l, s))),
            pl.BlockSpec((None, 1, bn), lambda l, s: (0, 0, MOD_HEAD // bn + step(l, s))),
            pl.BlockSpec((None, br, IN_WIDTH), lambda l, s: (0, step(l, s), 0)),
        ],
        out_specs=[
            pl.BlockSpec((None, None, S5_ROW, S5_W1), lambda l, s: (l, s, 0, 0)),
            pl.BlockSpec((None, None, S5_ROW, 4 * S5_ST), lambda l, s: (l, s, 0, 0)),
            pl.BlockSpec((None, None, 8, S5_ST), lambda l, s: (l, s, 0, 0)),
            pl.BlockSpec((8, bn), lambda l, s: (0, step(l, s))),
            pl.BlockSpec((br, IN_WIDTH), lambda l, s: (step(l, s), 0)),
        ],
        out_shape=[
            jax.ShapeDtypeStruct((DEPTH, S5_OCT, S5_ROW, S5_W1), BF16),
            jax.ShapeDtypeStruct((DEPTH, S5_OCT, S5_ROW, 4 * S5_ST), BF16),
            jax.ShapeDtypeStruct((DEPTH, S5_OCT, 8, S5_ST), F32),
            jax.ShapeDtypeStruct((8, n_tail), F32),
            jax.ShapeDtypeStruct((D_MODEL, IN_WIDTH), BF16),
        ],
        compiler_params=pltpu.CompilerParams(
            dimension_semantics=("arbitrary", "arbitrary"), vmem_limit_bytes=VMEM_BIG),
        name="s5_params",
    )(rows, bc, cvec8, w_mod, b_mod3, w_in)


def _s5_mix_kernel(u_ref, w1_ref, w2_ref, a8_ref, h0_ref, y_ref, hfin_ref, lhs_ref, a_ref, hp_ref, y8_ref):
    def stream(nb, nc, is_ctx):
        for b in range(nb):
            for t in range(S5_T):
                lhs_ref[t, pl.ds(b, nc, stride=nb), :] = u_ref[pl.ds(b * nc * S5_T + t, nc, stride=S5_T), :]
        lhs = jnp.concatenate([lhs_ref[t] for t in range(S5_T)], axis=1).astype(BF16)
        a_ref[...] = _dot(lhs, w1_ref[:, S5_ROW:])
        y_intra = _dot(lhs, w1_ref[:, 0:S5_ROW])

        coef = [a8_ref[k:k + 1, :] for k in range(4)]
        if is_ctx:
            init = tuple(jnp.zeros((nb, S5_ST), F32) for _ in range(4))
        else:
            init = tuple(h0_ref[k] for k in range(4))

        def step(c, st):
            new = []
            for d in range(2):
                cc = c if d == 0 else nc - 1 - c
                rows = slice(cc * nb, (cc + 1) * nb)
                hr, hi = st[2 * d], st[2 * d + 1]
                hp_ref[rows, 2 * S5_ST * d:2 * S5_ST * d + S5_ST] = hr
                hp_ref[rows, 2 * S5_ST * d + S5_ST:2 * S5_ST * (d + 1)] = hi
                c0 = 2 * S5_ST * d
                gr = a_ref[rows, c0:c0 + S5_ST]
                gi = a_ref[rows, c0 + S5_ST:c0 + 2 * S5_ST]
                ar, ai = coef[2 * d], coef[2 * d + 1]
                new += [ar * hr - ai * hi + gr, ar * hi + ai * hr + gi]
            return tuple(new)

        fin = init
        for c in range(nc):
            fin = step(c, fin)
        if is_ctx:
            for k in range(4):
                hfin_ref[k] = fin[k]

        y8 = y_intra + _dot_nt(hp_ref[...].astype(BF16), w2_ref[...])
        for t in range(S5_T):
            y8_ref[t] = y8[:, t * 128:(t + 1) * 128]
        for b in range(nb):
            for t in range(S5_T):
                y_ref[pl.ds(b * nc * S5_T + t, nc, stride=S5_T), :] = y8_ref[t, pl.ds(b, nc, stride=nb), :]

    pl.when(pl.program_id(1) == 0)(functools.partial(stream, CTX_B, CTX_NC, True))
    pl.when(pl.program_id(1) == 1)(functools.partial(stream, LAT_B, LAT_NC, False))


def _s5_mix(uf, w1, w2, a8, h0, l):
    return pl.pallas_call(
        _s5_mix_kernel,
        grid=(S5_OCT, 2),
        in_specs=[
            pl.BlockSpec((T_CTX, 128), lambda s, k: (k, s)),
            pl.BlockSpec((None, None, S5_ROW, S5_W1), lambda s, k: (l, s, 0, 0)),
            pl.BlockSpec((None, None, S5_ROW, 4 * S5_ST), lambda s, k: (l, s, 0, 0)),
            pl.BlockSpec((None, None, 8, S5_ST), lambda s, k: (l, s, 0, 0)),
            pl.BlockSpec((None, 4, LAT_B, S5_ST), lambda s, k: (s, 0, 0, 0)),
        ],
        out_specs=[
            pl.BlockSpec((T_CTX, 128), lambda s, k: (k, s)),
            pl.BlockSpec((None, 4, CTX_B, S5_ST), lambda s, k: (s, 0, 0, 0)),
        ],
        out_shape=[
            jax.ShapeDtypeStruct((T_ALL, S5_CH), F32),
            jax.ShapeDtypeStruct((S5_OCT, 4, CTX_B, S5_ST), F32),
        ],
        scratch_shapes=[
            pltpu.VMEM((S5_T, S5_ROWS, 128), F32),
            pltpu.VMEM((S5_ROWS, 4 * S5_ST), F32),
            pltpu.VMEM((S5_ROWS, 4 * S5_ST), F32),
            pltpu.VMEM((S5_T, S5_ROWS, 128), F32),
        ],
        compiler_params=pltpu.CompilerParams(
            dimension_semantics=("arbitrary", "arbitrary"), vmem_limit_bytes=VMEM_BIG),
        name=f"s5_mix_l{l}",
    )(uf, w1, w2, a8, h0)


def _s5_gate(y, u, d, w, b):
    y = y + d * u
    z = y * (0.5 * (1.0 + jnp.tanh(math.sqrt(2.0 / math.pi) * (y + 0.044715 * (y * y * y)))))
    t = _dot(z.astype(BF16), w) + b
    return (z * jax.nn.sigmoid(t)).astype(BF16)


def _outproj_kernel(x_ref, a_ctx_ref, a_lat_ref, c_ctx_ref, c_lat_ref, y_ref, u_ref, d_ref, wglu_ref, bglu_ref,
                    mod_ref, npost_ref, wb_ref, o_ref):
    na = A_HEADS * HD
    nc = na + C_HEADS * HD

    def body(a_ref, c_ref):
        gate = mod_ref[:, 5 * D_MODEL:6 * D_MODEL]
        wglu = wglu_ref[...].astype(BF16)
        for r in range(BM_OUT // RC_OUT):
            rows = slice(r * RC_OUT, (r + 1) * RC_OUT)
            s = _s5_gate(y_ref[rows, :], u_ref[rows, :], d_ref[...], wglu, bglu_ref[...])
            y = (_dot(a_ref[rows, :], wb_ref[0:na, :]) + _dot(c_ref[rows, :], wb_ref[na:nc, :])
                 + _dot(s, wb_ref[nc:, :]))
            o_ref[rows, :] = x_ref[rows, :] + gate * _rms(y, npost_ref[...])

    is_lat = pl.program_id(0) >= T_CTX // BM_OUT
    pl.when(jnp.logical_not(is_lat))(functools.partial(body, a_ctx_ref, c_ctx_ref))
    pl.when(is_lat)(functools.partial(body, a_lat_ref, c_lat_ref))


def _outproj(x, a_ctx, a_lat, c_ctx, c_lat, y_s5, uf, s5_d3, w_glu, b_glu3, mods3, norm_post4, w_out_bf, l):
    bm = BM_OUT
    nct = T_CTX // bm
    ctx_idx = lambda i: (jnp.minimum(i, nct - 1), 0)
    lat_idx = lambda i: (jnp.maximum(i - nct, 0), 0)
    return pl.pallas_call(
        _outproj_kernel,
        grid=(T_ALL // bm,),
        in_specs=[
            pl.BlockSpec((bm, D_MODEL), lambda i: (i, 0)),
            pl.BlockSpec((bm, A_HEADS * HD), ctx_idx),
            pl.BlockSpec((bm, A_HEADS * HD), lat_idx),
            pl.BlockSpec((bm, C_HEADS * HD), ctx_idx),
            pl.BlockSpec((bm, C_HEADS * HD), lat_idx),
            pl.BlockSpec((bm, S5_CH), lambda i: (i, 0)),
            pl.BlockSpec((bm, S5_CH), lambda i: (i, 0)),
            pl.BlockSpec((None, 1, S5_CH), lambda i: (l, 0, 0)),
            pl.BlockSpec((None, S5_CH, S5_CH), lambda i: (l, 0, 0)),
            pl.BlockSpec((None, 1, S5_CH), lambda i: (l, 0, 0)),
            pl.BlockSpec((None, 1, N_MOD * D_MODEL), lambda i: (_mod_index(i, bm), 0, 0)),
            pl.BlockSpec((None, None, 1, D_MODEL), lambda i: (l, 1, 0, 0)),
            pl.BlockSpec((D_MODEL, D_MODEL), lambda i: (0, 0), pipeline_mode=pl.Buffered(1)),
        ],
        out_specs=pl.BlockSpec((bm, D_MODEL), lambda i: (i, 0)),
        out_shape=jax.ShapeDtypeStruct((T_ALL, D_MODEL), F32),
        compiler_params=pltpu.CompilerParams(
            dimension_semantics=("arbitrary",), vmem_limit_bytes=VMEM_BIG),
        name=f"outproj_l{l}",
    )(x, a_ctx, a_lat, c_ctx, c_lat, y_s5, uf, s5_d3, w_glu, b_glu3, mods3, norm_post4, w_out_bf)


def _rope_tables():
    rows = LAT_L // GRID_W
    row = jnp.repeat(jnp.arange(rows, dtype=F32), GRID_W)
    col = jnp.tile(jnp.arange(GRID_W, dtype=F32), rows)
    axis_dim = HD // 2
    inv_freq = ROPE_BASE ** (-jnp.arange(0, axis_dim, 2, dtype=F32) / axis_dim)
    ang_row = row[:, None] * inv_freq
    ang_col = col[:, None] * inv_freq
    cr, sr = jnp.cos(ang_row), jnp.sin(ang_row)
    cc, sc = jnp.cos(ang_col), jnp.sin(ang_col)
    cos_t = jnp.concatenate([cr, cr, cc, cc], axis=-1)
    sin_t = jnp.concatenate([-sr, sr, -sc, sc], axis=-1)
    return cos_t, sin_t


def kernel(x_prompt, x_sample, cache_a_k, cache_a_v, cache_c_k, cache_c_v, state_ssm_re, state_ssm_im,
           c, c_ctx, w_mod, b_mod, norm_pre, norm_post, ffn_gate, ffn_up, ffn_down, w_in, w_out,
           q_norm, k_norm, sink, s5_lam_re, s5_lam_im, s5_log_step, s5_b_re, s5_b_im, s5_c_re, s5_c_im,
           s5_d, w_glu, b_glu):
    cvec8 = jnp.concatenate([c_ctx[None, :], c, jnp.zeros((8 - 1 - LAT_B, D_MODEL), F32)], axis=0)
    b_mod3 = b_mod.reshape(DEPTH, 1, N_MOD * D_MODEL)
    norm_pre4 = norm_pre.reshape(DEPTH, 3, 1, D_MODEL)
    norm_post4 = norm_post.reshape(DEPTH, 3, 1, D_MODEL)
    q_norm3 = q_norm.reshape(DEPTH, 1, HD)
    k_norm3 = k_norm.reshape(DEPTH, 1, HD)
    s5_d3 = s5_d.reshape(DEPTH, 1, S5_CH)
    b_glu3 = b_glu.reshape(DEPTH, 1, S5_CH)
    cos_t, sin_t = _rope_tables()
    mods_head = _modulation(cvec8, w_mod, b_mod3, 0, MOD_HEAD)
    w1, w2, a8, mods_mid, w_in_bf = _s5_params(s5_lam_re, s5_lam_im, s5_log_step, s5_b_re, s5_b_im,
                                               s5_c_re, s5_c_im, cvec8, w_mod, b_mod3, w_in)
    mods_tail = jnp.zeros((8, N_MOD * D_MODEL - MOD_MID), F32)
    table = lambda tail: jnp.concatenate([mods_head, mods_mid, tail], axis=1).reshape(8, 1, N_MOD * D_MODEL)
    mods3 = table(mods_tail)
    kv4 = lambda a: a.reshape(LAT_B, DEPTH, PAST, A_KV * HD)
    cak, cav, cck, ccv = kv4(cache_a_k), kv4(cache_a_v), kv4(cache_c_k), kv4(cache_c_v)
    h0_all = jnp.stack([state_ssm_re[:, :, 0], state_ssm_im[:, :, 0],
                        state_ssm_re[:, :, 1], state_ssm_im[:, :, 1]], axis=0)
    h0_all = h0_all.reshape(4, LAT_B, DEPTH, S5_OCT, S5_ST).transpose(2, 3, 0, 1, 4)

    ffn_w = (norm_pre4, norm_post4, ffn_gate, ffn_up, ffn_down)
    new_caches = ()
    new_state = []
    xs = [x_prompt.reshape(T_CTX, D_MODEL), x_sample.reshape(T_LAT, D_MODEL)]
    for l in range(DEPTH):
        x = _ffn(xs, mods3, *ffn_w, l, 0)

        if l == 0:
            qkvu, uf, *new_caches, mods_tail = _inproj(
                x, mods3, norm_pre4, q_norm3, k_norm3, cos_t, sin_t, w_in_bf, l, (),
                mod_args=(cvec8, w_mod, b_mod3), mod_cols=(MOD_MID, N_MOD * D_MODEL))
            mods3 = table(mods_tail)
        else:
            qkvu, uf, *new_caches = _inproj(x, mods3, norm_pre4, q_norm3, k_norm3, cos_t, sin_t, w_in_bf, l,
                                            new_caches)
        a_ctx, c_ctx = _attn_ctx(qkvu, sink[l], l)
        if l < DEPTH - 1:
            a_lat, mods_next, w_in_next = _attn_lat_a(qkvu, cak, cav, l, next_mod=(cvec8, w_mod, b_mod3, w_in))
        else:
            a_lat = _attn_lat_a(qkvu, cak, cav, l)
        c_lat, w_out_bf = _attn_lat_c(qkvu, sink[l], cck, ccv, w_out, l)
        y, hfin = _s5_mix(uf, w1, w2, a8, h0_all[l], l)
        x = _outproj(x, a_ctx, a_lat, c_ctx, c_lat, y, uf, s5_d3, w_glu, b_glu3, mods3, norm_post4, w_out_bf, l)
        if l < DEPTH - 1:
            xs = [_ffn([x], mods3, *ffn_w, l, 1)]
            mods3 = mods_next.reshape(8, 1, N_MOD * D_MODEL)
            w_in_bf = w_in_next
        else:
            y_prompt, y_sample = _ffn([x], mods3, *ffn_w, l, 1, split_out=True)

        hf = hfin.reshape(S5_OCT, 4, CTX_B, S5_OG, S5_P).transpose(1, 2, 0, 3, 4).reshape(4, CTX_B, S5_G, S5_P)
        new_state.append((jnp.stack([hf[0], hf[2]], axis=1), jnp.stack([hf[1], hf[3]], axis=1)))

    y_prompt = y_prompt.reshape(CTX_B, CTX_L, D_MODEL)
    y_sample = y_sample.reshape(LAT_B, LAT_L, D_MODEL)
    caches = list(new_caches)
    st_re = jnp.stack([new_state[l][0] for l in range(DEPTH)], axis=1)
    st_im = jnp.stack([new_state[l][1] for l in range(DEPTH)], axis=1)
    return (y_prompt, y_sample, caches[0], caches[1], caches[2], caches[3], st_re, st_im)
```

```python
import functools
import math

import jax
import jax.numpy as jnp
from jax import lax
from jax.experimental import pallas as pl
from jax.experimental.pallas import tpu as pltpu

F32 = jnp.float32
BF16 = jnp.bfloat16

D_MODEL = 2048
CTX_B, CTX_L = 16, 256
LAT_B, LAT_L = 2, 2048
DEPTH = 2
PAST = 512
GRID_W = 64
HD = 128
A_HEADS, A_KV = 8, 2
C_HEADS, C_KV = 4, 2
WINDOW = 128
S5_GC = 16
S5_CH = 512
S5_G = 32
S5_P = 64
D_FF = 5632
N_MOD = 9
IN_WIDTH = 3072
ROPE_BASE = 10000.0
EPS = 1e-6
HALF_STEP = 0.5
NEG_INF = -1e30
SCALE = HD ** -0.5
LOG2E = math.log2(math.e)

T_CTX = CTX_B * CTX_L
T_LAT = LAT_B * LAT_L
T_ALL = T_CTX + T_LAT

COL_AQ, COL_AK, COL_AV = 0, 1024, 1280
COL_CQ, COL_CK, COL_CV = 1536, 2048, 2304
COL_U = 2560

S5_T = 8
S5_OCT = S5_CH // 128
S5_OG = S5_G // S5_OCT
S5_ROW = S5_T * 128
S5_ST = S5_OG * S5_P
S5_W1 = S5_ROW + 4 * S5_ST
CTX_NC = CTX_L // S5_T
LAT_NC = LAT_L // S5_T
S5_ROWS = CTX_NC * CTX_B
assert S5_ROWS == LAT_NC * LAT_B and T_CTX == T_LAT

V7X_VMEM_BYTES = 64 * 1024 * 1024
VMEM_FFN = 62 * 1024 * 1024
VMEM_BIG = 56 * 1024 * 1024
VMEM_MID = 40 * 1024 * 1024

BM = 1024
BM_OUT = 512
RC_OUT = 256
BF = 512
RC = 512
RCX = 256
N_XC = BM // RCX
FFN_PREFETCH_STEP = 2
BN_MOD = 1024
MOD_HEAD = 3 * D_MODEL
MOD_MID = 6 * D_MODEL
BM_IN = 512
BQ_A = 512
BQ_C = 256
NQ_C = 4
NB_CTX = 4


def _dot(a, b):
    return jnp.dot(a, b, preferred_element_type=F32)


def _dot_nt(a, b, precision=None):
    return lax.dot_general(a, b, (((1,), (1,)), ((), ())), preferred_element_type=F32, precision=precision)


def _dot_nt_split(a, b):
    ah = a.astype(BF16)
    bh = b.astype(BF16)
    al = (a - ah.astype(F32)).astype(BF16)
    bl = (b - bh.astype(F32)).astype(BF16)
    return _dot_nt(ah, bh) + (_dot_nt(ah, bl) + _dot_nt(al, bh))


def _rms(x, g):
    return x * lax.rsqrt(jnp.mean(x * x, axis=-1, keepdims=True) + EPS) * g


def _mod_index(i, bm):
    nct = T_CTX // bm
    return jnp.where(i < nct, 0, 1 + (i - nct) // (LAT_L // bm))


def _mod_columns(c_ref, w_ref, b_ref, o_ref):
    c = c_ref[...]
    s = (c * jax.nn.sigmoid(c)).astype(BF16)
    o_ref[...] = _dot(s, w_ref[...].astype(BF16)) + b_ref[...]


def _modulation(cvec8, w_mod, b_mod3, l, n):
    return pl.pallas_call(
        _mod_columns,
        grid=(n // BN_MOD,),
        in_specs=[
            pl.BlockSpec((8, D_MODEL), lambda j: (0, 0)),
            pl.BlockSpec((None, D_MODEL, BN_MOD), lambda j: (l, 0, j)),
            pl.BlockSpec((None, 1, BN_MOD), lambda j: (l, 0, j)),
        ],
        out_specs=pl.BlockSpec((8, BN_MOD), lambda j: (0, j)),
        out_shape=jax.ShapeDtypeStruct((8, n), F32),
        compiler_params=pltpu.CompilerParams(
            dimension_semantics=("arbitrary",), vmem_limit_bytes=VMEM_MID),
        name=f"modulation_l{l}",
    )(cvec8, w_mod, b_mod3)


def _ffn_kernel(*refs, mo, n_x, n_out):
    x_hbms = refs[:n_x]
    mod_ref, npre_ref, npost_ref, wg_ref, wu_ref, wd_ref = refs[n_x:n_x + 6]
    out_hbms = refs[n_x + 6:n_x + 6 + n_out]
    acc_ref, xc_ref, h_ref, sem_x, sem_c, sem_o = refs[n_x + 6 + n_out:]
    i = pl.program_id(0)
    j = pl.program_id(1)
    n_tiles = pl.num_programs(0)
    last = pl.num_programs(1) - 1
    slot = i % 2
    nq = RC // RCX
    nct = T_CTX // BM

    def per_stream(arrays, tile, fn):
        if len(arrays) == 1:
            fn(arrays[0], pl.multiple_of(tile * BM, BM))
        else:
            pl.when(tile < nct)(lambda: fn(arrays[0], pl.multiple_of(tile * BM, BM)))
            pl.when(tile >= nct)(lambda: fn(arrays[1], pl.multiple_of((tile - nct) * BM, BM)))

    def x_tile_copy(arr, row0, sl):
        return pltpu.make_async_copy(arr.at[pl.ds(row0, BM), :], acc_ref.at[sl], sem_x)

    def x_chunk_copy(arr, row0, cs):
        return pltpu.make_async_copy(arr.at[pl.ds(row0, RCX), :], xc_ref.at[cs], sem_c.at[cs])

    def out_copy(arr, row0, sl):
        return pltpu.make_async_copy(acc_ref.at[sl], arr.at[pl.ds(row0, BM), :], sem_o.at[sl])

    wait_x_tile = lambda sl: x_tile_copy(x_hbms[0], 0, sl).wait()
    wait_x_chunk = lambda cs: x_chunk_copy(x_hbms[0], 0, cs).wait()
    wait_out = lambda sl: out_copy(out_hbms[0], 0, sl).wait()

    def start_x_chunk(q):
        per_stream(x_hbms, i, lambda arr, row0: x_chunk_copy(
            arr, pl.multiple_of(row0 + q * RCX, RCX), q).start())

    @pl.when(jnp.logical_and(i == 0, j == 0))
    def _():
        x_tile_copy(x_hbms[0], 0, 0).start()

    @pl.when(j == 0)
    def _():
        wait_x_tile(slot)

    @pl.when(j == FFN_PREFETCH_STEP)
    def _():
        pl.when(i >= 1)(lambda: wait_out(1 - slot))

        @pl.when(i + 1 < n_tiles)
        def _():
            per_stream(x_hbms, i + 1, lambda arr, row0: x_tile_copy(arr, row0, 1 - slot).start())

    @pl.when(j == last - 1)
    def _():
        for q in range(N_XC):
            start_x_chunk(q)

    def step(first, final):
        if final:
            for q in range(N_XC):
                wait_x_chunk(q)
        acc_slot = acc_ref.at[slot]
        wg = wg_ref[...].astype(BF16)
        wu = wu_ref[...].astype(BF16)
        wd = wd_ref[...].astype(BF16)
        for r in range(BM // RC):
            rows = slice(r * RC, (r + 1) * RC)
            if first:
                shift = mod_ref[:, mo * D_MODEL:(mo + 1) * D_MODEL]
                scale = mod_ref[:, (mo + 1) * D_MODEL:(mo + 2) * D_MODEL]
                hn = _rms(acc_slot[rows, :], npre_ref[...])
                h = (hn * (1.0 + scale) + shift).astype(BF16)
                h_ref[rows, :] = h
            else:
                h = h_ref[rows, :]
            g = _dot(h, wg)
            u = _dot(h, wu)
            a = (g * jax.nn.sigmoid(g) * u).astype(BF16)
            acc = _dot(a, wd)
            if not first:
                acc = acc_slot[rows, :] + acc
            if not final:
                acc_slot[rows, :] = acc
                continue
            gate = mod_ref[:, (mo + 2) * D_MODEL:(mo + 3) * D_MODEL]
            for qq in range(nq):
                sub = slice(qq * RCX, (qq + 1) * RCX)
                y = xc_ref[r * nq + qq] + (HALF_STEP * gate) * _rms(acc[sub, :], npost_ref[...])
                acc_slot[r * RC + qq * RCX:r * RC + (qq + 1) * RCX, :] = y
        if final:
            per_stream(out_hbms, i, lambda arr, row0: out_copy(arr, row0, slot).start())

    pl.when(j == 0)(functools.partial(step, True, False))
    pl.when(jnp.logical_and(j > 0, j < last))(functools.partial(step, False, False))
    pl.when(j == last)(functools.partial(step, False, True))

    @pl.when(jnp.logical_and(i == n_tiles - 1, j == last))
    def _():
        wait_out(slot)


def _ffn(xs, mods3, norm_pre4, norm_post4, ffn_gate, ffn_up, ffn_down, l, s, *, split_out=False):
    mo = 6 * s
    ni = 2 * s
    in_specs = [pl.BlockSpec(memory_space=pl.ANY)] * len(xs) + [
        pl.BlockSpec((None, 1, N_MOD * D_MODEL), lambda i, j: (_mod_index(i, BM), 0, 0)),
        pl.BlockSpec((None, None, 1, D_MODEL), lambda i, j: (l, ni, 0, 0)),
        pl.BlockSpec((None, None, 1, D_MODEL), lambda i, j: (l, ni, 0, 0)),
        pl.BlockSpec((None, None, D_MODEL, BF), lambda i, j: (l, s, 0, j)),
        pl.BlockSpec((None, None, D_MODEL, BF), lambda i, j: (l, s, 0, j)),
        pl.BlockSpec((None, None, BF, D_MODEL), lambda i, j: (l, s, j, 0)),
    ]
    out_rows = (T_CTX, T_LAT) if split_out else (T_ALL,)
    outs = pl.pallas_call(
        functools.partial(_ffn_kernel, mo=mo, n_x=len(xs), n_out=len(out_rows)),
        grid=(T_ALL // BM, D_FF // BF),
        in_specs=in_specs,
        out_specs=[pl.BlockSpec(memory_space=pl.ANY)] * len(out_rows),
        out_shape=[jax.ShapeDtypeStruct((r, D_MODEL), F32) for r in out_rows],
        scratch_shapes=[
            pltpu.VMEM((2, BM, D_MODEL), F32),
            pltpu.VMEM((N_XC, RCX, D_MODEL), F32),
            pltpu.VMEM((BM, D_MODEL), BF16),
            pltpu.SemaphoreType.DMA(()),
            pltpu.SemaphoreType.DMA((N_XC,)),
            pltpu.SemaphoreType.DMA((2,)),
        ],
        compiler_params=pltpu.CompilerParams(
            dimension_semantics=("arbitrary", "arbitrary"), vmem_limit_bytes=VMEM_FFN),
        name=f"ffn_l{l}_s{s}",
    )(*xs, mods3, norm_pre4, norm_post4, ffn_gate, ffn_up, ffn_down)
    return outs if split_out else outs[0]


def _rope(y, cos, sins):
    lane = lax.broadcasted_iota(jnp.int32, y.shape, 1)
    first = (lane & 63) < 32
    partner = jnp.where(first, pltpu.roll(y, 96, 1), pltpu.roll(y, 32, 1))
    return y * cos + partner * sins


_IN_SEGMENTS = (
    (COL_AQ, A_HEADS, "q", True, None),
    (COL_AK, A_KV, "k", True, 0),
    (COL_AV, A_KV, None, False, 1),
    (COL_CQ, C_HEADS, None, True, None),
    (COL_CK, C_KV, None, True, 2),
    (COL_CV, C_KV, None, False, 3),
)


def _inproj_kernel(x_ref, mod_ref, npre_ref, qn_ref, kn_ref, cos_ref, sin_ref, w_ref, *rest, first_layer):
    nb = BM_IN // CTX_L
    if first_layer:
        c_ref, wmod_ref, bmod_ref, qkvu_ref, uf_ref, *cache_full, modo_ref = rest
        cache_refs = [c.at[:, 0] for c in cache_full]
        _mod_columns(c_ref, wmod_ref, bmod_ref, modo_ref)
    else:
        qkvu_ref, uf_ref, *cache_refs = rest[4:]
        cache_full = ()

    def body(lat):
        shift = mod_ref[:, 3 * D_MODEL:4 * D_MODEL]
        scale = mod_ref[:, 4 * D_MODEL:5 * D_MODEL]
        h = (_rms(x_ref[...], npre_ref[...]) * (1.0 + scale) + shift).astype(BF16)
        for col0, heads, norm, rot, cache in _IN_SEGMENTS:
            p = _dot(h, w_ref[:, col0:col0 + heads * HD])
            for k in range(heads):
                y = p[:, k * HD:(k + 1) * HD]
                if norm == "q":
                    y = _rms(y, qn_ref[...])
                elif norm == "k":
                    y = _rms(y, kn_ref[...])
                if rot and lat:
                    y = _rope(y, cos_ref[...], sin_ref[...])
                qkvu_ref[:, col0 + k * HD:col0 + (k + 1) * HD] = y.astype(BF16)
                if cache is not None and not lat:
                    cache_refs[cache][:, :, k, :] = y.reshape(nb, CTX_L, HD)
        u = _dot(h, w_ref[:, COL_U:])
        uf_ref[...] = u
        qkvu_ref[:, COL_U:] = u.astype(BF16)
        if first_layer and not lat:
            for c in cache_full:
                c[:, 1:] = jnp.zeros((nb, DEPTH - 1, CTX_L, A_KV, HD), F32)

    is_lat = pl.program_id(0) >= T_CTX // BM_IN
    pl.when(is_lat)(functools.partial(body, True))
    pl.when(jnp.logical_not(is_lat))(functools.partial(body, False))


def _inproj(x, mods3, norm_pre4, q_norm3, k_norm3, cos_t, sin_t, w_in_bf, l, prev_caches, mod_args=None,
            mod_cols=None):
    bm = BM_IN
    nct = T_CTX // bm
    nb = bm // CTX_L
    tab_idx = lambda i: (jnp.maximum(i - nct, 0) % (LAT_L // bm), 0)
    cache_shape = jax.ShapeDtypeStruct((CTX_B, DEPTH, CTX_L, A_KV, HD), F32)
    first_layer = not prev_caches
    if first_layer:
        cache_spec = pl.BlockSpec((nb, DEPTH, CTX_L, A_KV, HD), lambda i: (jnp.minimum(i, nct - 1), 0, 0, 0, 0))
    else:
        cache_spec = pl.BlockSpec((nb, None, CTX_L, A_KV, HD), lambda i: (jnp.minimum(i, nct - 1), l, 0, 0, 0))
    n_in = 8
    extra_in, extra_out, extra_shape, extra_args = [], [], [], []
    if first_layer:
        lo, hi = mod_cols
        steps = T_ALL // bm
        bn = (hi - lo) // steps
        extra_in = [
            pl.BlockSpec((8, D_MODEL), lambda i: (0, 0)),
            pl.BlockSpec((None, D_MODEL, bn), lambda i: (l, 0, lo // bn + i)),
            pl.BlockSpec((None, 1, bn), lambda i: (l, 0, lo // bn + i)),
        ]
        extra_out = [pl.BlockSpec((8, bn), lambda i: (0, i))]
        extra_shape = [jax.ShapeDtypeStruct((8, hi - lo), F32)]
        extra_args = list(mod_args)
    return pl.pallas_call(
        functools.partial(_inproj_kernel, first_layer=first_layer),
        grid=(T_ALL // bm,),
        in_specs=[
            pl.BlockSpec((bm, D_MODEL), lambda i: (i, 0)),
            pl.BlockSpec((None, 1, N_MOD * D_MODEL), lambda i: (_mod_index(i, bm), 0, 0)),
            pl.BlockSpec((None, None, 1, D_MODEL), lambda i: (l, 1, 0, 0)),
            pl.BlockSpec((None, 1, HD), lambda i: (l, 0, 0)),
            pl.BlockSpec((None, 1, HD), lambda i: (l, 0, 0)),
            pl.BlockSpec((bm, HD), tab_idx),
            pl.BlockSpec((bm, HD), tab_idx),
            pl.BlockSpec((D_MODEL, IN_WIDTH), lambda i: (0, 0), pipeline_mode=pl.Buffered(1)),
        ] + extra_in + [pl.BlockSpec(memory_space=pl.ANY)] * len(prev_caches),
        out_specs=[
            pl.BlockSpec((bm, IN_WIDTH), lambda i: (i, 0)),
            pl.BlockSpec((bm, S5_CH), lambda i: (i, 0)),
        ] + [cache_spec] * 4 + extra_out,
        out_shape=[
            jax.ShapeDtypeStruct((T_ALL, IN_WIDTH), BF16),
            jax.ShapeDtypeStruct((T_ALL, S5_CH), F32),
        ] + [cache_shape] * 4 + extra_shape,
        input_output_aliases={n_in + k: 2 + k for k in range(len(prev_caches))},
        compiler_params=pltpu.CompilerParams(
            dimension_semantics=("arbitrary",), vmem_limit_bytes=VMEM_BIG),
        name=f"inproj_l{l}",
    )(x, mods3, norm_pre4, q_norm3, k_norm3, cos_t, sin_t, w_in_bf, *extra_args, *prev_caches)


def _softmax_pv(dots, values, sink=None):
    m = functools.reduce(jnp.maximum, [jnp.max(d, axis=-1, keepdims=True) for d in dots]) * SCALE
    if sink is not None:
        m = jnp.maximum(m, sink)
    m2 = m * LOG2E
    ps = [jnp.exp2(d * (SCALE * LOG2E) - m2) for d in dots]
    den = functools.reduce(jnp.add, [jnp.sum(p, axis=-1, keepdims=True) for p in ps])
    if sink is not None:
        den = den + jnp.exp2(sink * LOG2E - m2)
    o = functools.reduce(jnp.add, [_dot(p.astype(BF16), v) for p, v in zip(ps, values)])
    return o / den


def _attn_ctx_kernel(sink_ref, aq_ref, ak_ref, av_ref, cq_ref, ck_ref, cv_ref, ao_ref, co_ref):
    kv = pl.program_id(1)
    g = C_HEADS // C_KV
    for b in range(NB_CTX):
        rows = slice(b * CTX_L, (b + 1) * CTX_L)
        k = ak_ref[rows, :]
        v = av_ref[rows, :]
        for h in range(A_HEADS // A_KV):
            cols = slice(h * HD, (h + 1) * HD)
            s = _dot_nt(aq_ref[rows, cols], k)
            ao_ref[rows, cols] = _softmax_pv([s], [v]).astype(BF16)
        k = ck_ref[rows, :]
        v = cv_ref[rows, :]
        for h in range(g):
            cols = slice(h * HD, (h + 1) * HD)
            s = _dot_nt(cq_ref[rows, cols], k)
            co_ref[rows, cols] = _softmax_pv([s], [v], sink_ref[kv * g + h]).astype(BF16)


def _attn_ctx(qkvu, sink_l, l):
    ga = A_HEADS // A_KV * HD
    gc = C_HEADS // C_KV * HD
    rows = NB_CTX * CTX_L
    blk = lambda width, col0: pl.BlockSpec((rows, width), lambda b, k: (b, col0 // width + k))
    return pl.pallas_call(
        _attn_ctx_kernel,
        grid=(CTX_B // NB_CTX, A_KV),
        in_specs=[
            pl.BlockSpec(memory_space=pltpu.SMEM),
            blk(ga, COL_AQ), blk(HD, COL_AK), blk(HD, COL_AV),
            blk(gc, COL_CQ), blk(HD, COL_CK), blk(HD, COL_CV),
        ],
        out_specs=[
            pl.BlockSpec((rows, ga), lambda b, k: (b, k)),
            pl.BlockSpec((rows, gc), lambda b, k: (b, k)),
        ],
        out_shape=[
            jax.ShapeDtypeStruct((T_CTX, A_HEADS * HD), BF16),
            jax.ShapeDtypeStruct((T_CTX, C_HEADS * HD), BF16),
        ],
        compiler_params=pltpu.CompilerParams(dimension_semantics=("arbitrary", "arbitrary")),
        name=f"attn_ctx_l{l}",
    )(sink_l, qkvu, qkvu, qkvu, qkvu, qkvu, qkvu)


def _attn_lat_a_kernel(q_ref, k_ref, v_ref, kc_ref, vc_ref, *rest):
    o_ref = rest[0] if len(rest) == 1 else rest[4]
    k = k_ref[...]
    v = v_ref[...]
    kc = kc_ref[...].astype(BF16)
    vc = vc_ref[...].astype(BF16)
    for h in range(A_HEADS // A_KV):
        cols = slice(h * HD, (h + 1) * HD)
        q = q_ref[:, cols]
        s1 = _dot_nt(q, k)
        s2 = _dot_nt(q, kc)
        o_ref[:, cols] = _softmax_pv([s1, s2], [v, vc]).astype(BF16)
    if len(rest) > 1:
        c_ref, w_ref, b_ref, win_ref, _, mod_ref, winb_ref = rest
        _mod_columns(c_ref, w_ref, b_ref, mod_ref)
        winb_ref[...] = win_ref[...].astype(BF16)


def _attn_lat_a(qkvu, cache_k4, cache_v4, l, next_mod=None):
    ga = A_HEADS // A_KV * HD
    nq = LAT_L // BQ_A
    row0 = T_CTX // BQ_A
    lat_blk = T_CTX // LAT_L
    in_specs = [
        pl.BlockSpec((BQ_A, ga), lambda b, k, q: (row0 + b * nq + q, COL_AQ // ga + k)),
        pl.BlockSpec((LAT_L, HD), lambda b, k, q: (lat_blk + b, COL_AK // HD + k)),
        pl.BlockSpec((LAT_L, HD), lambda b, k, q: (lat_blk + b, COL_AV // HD + k)),
        pl.BlockSpec((None, None, PAST, HD), lambda b, k, q: (b, l, 0, k)),
        pl.BlockSpec((None, None, PAST, HD), lambda b, k, q: (b, l, 0, k)),
    ]
    out_specs = [pl.BlockSpec((BQ_A, ga), lambda b, k, q: (b * nq + q, k))]
    out_shape = [jax.ShapeDtypeStruct((T_LAT, A_HEADS * HD), BF16)]
    args = [qkvu, qkvu, qkvu, cache_k4, cache_v4]
    if next_mod is not None:
        n = N_MOD * D_MODEL
        steps = LAT_B * A_KV * nq
        bn = n // steps
        step = lambda b, k, q: (b * A_KV + k) * nq + q
        in_specs += [
            pl.BlockSpec((8, D_MODEL), lambda b, k, q: (0, 0)),
            pl.BlockSpec((None, D_MODEL, bn), lambda b, k, q: (l + 1, 0, step(b, k, q))),
            pl.BlockSpec((None, 1, bn), lambda b, k, q: (l + 1, 0, step(b, k, q))),
            pl.BlockSpec((None, D_MODEL // steps, IN_WIDTH), lambda b, k, q: (l + 1, step(b, k, q), 0)),
        ]
        out_specs += [
            pl.BlockSpec((8, bn), lambda b, k, q: (0, step(b, k, q))),
            pl.BlockSpec((D_MODEL // steps, IN_WIDTH), lambda b, k, q: (step(b, k, q), 0)),
        ]
        out_shape += [
            jax.ShapeDtypeStruct((8, n), F32),
            jax.ShapeDtypeStruct((D_MODEL, IN_WIDTH), BF16),
        ]
        args += list(next_mod)
    outs = pl.pallas_call(
        _attn_lat_a_kernel,
        grid=(LAT_B, A_KV, nq),
        in_specs=in_specs,
        out_specs=out_specs,
        out_shape=out_shape,
        compiler_params=pltpu.CompilerParams(
            dimension_semantics=("arbitrary", "arbitrary", "arbitrary"), vmem_limit_bytes=VMEM_BIG),
        name=f"attn_lat_a_l{l}",
    )(*args)
    return outs if next_mod is not None else outs[0]


def _attn_lat_c_kernel(sink_ref, q_ref, k_ref, v_ref, kc_ref, vc_ref, wout_ref, o_ref, woutb_ref):
    woutb_ref[...] = wout_ref[...].astype(BF16)
    kv = pl.program_id(1)
    span = BQ_C + 2 * WINDOW
    kc = kc_ref[...].astype(BF16)
    vc = vc_ref[...].astype(BF16)
    g = C_HEADS // C_KV
    for sub in range(NQ_C):
        n = pl.program_id(2) * NQ_C + sub
        rows = slice(sub * BQ_C, (sub + 1) * BQ_C)
        start = pl.multiple_of(jnp.clip(n * BQ_C - WINDOW, 0, LAT_L - span), WINDOW)
        kw = k_ref[pl.ds(start, span), :]
        vw = v_ref[pl.ds(start, span), :]
        qpos = n * BQ_C + lax.broadcasted_iota(jnp.int32, (BQ_C, span), 0)
        kpos = start + lax.broadcasted_iota(jnp.int32, (BQ_C, span), 1)
        valid = jnp.abs(qpos - kpos) <= WINDOW
        for h in range(g):
            cols = slice(h * HD, (h + 1) * HD)
            q = q_ref[rows, cols]
            s1 = jnp.where(valid, _dot_nt(q, kw), NEG_INF)
            s2 = _dot_nt(q, kc)
            o_ref[rows, cols] = _softmax_pv([s1, s2], [vw, vc], sink_ref[kv * g + h]).astype(BF16)


def _attn_lat_c(qkvu, sink_l, cache_k4, cache_v4, w_out, l):
    gc = C_HEADS // C_KV * HD
    bq = BQ_C * NQ_C
    nq = LAT_L // bq
    row0 = T_CTX // bq
    lat_blk = T_CTX // LAT_L
    steps = LAT_B * C_KV * nq
    step = lambda b, k, q: (b * C_KV + k) * nq + q
    return pl.pallas_call(
        _attn_lat_c_kernel,
        grid=(LAT_B, C_KV, nq),
        in_specs=[
            pl.BlockSpec(memory_space=pltpu.SMEM),
            pl.BlockSpec((bq, gc), lambda b, k, q: (row0 + b * nq + q, COL_CQ // gc + k)),
            pl.BlockSpec((LAT_L, HD), lambda b, k, q: (lat_blk + b, COL_CK // HD + k)),
            pl.BlockSpec((LAT_L, HD), lambda b, k, q: (lat_blk + b, COL_CV // HD + k)),
            pl.BlockSpec((None, None, PAST, HD), lambda b, k, q: (b, l, 0, k)),
            pl.BlockSpec((None, None, PAST, HD), lambda b, k, q: (b, l, 0, k)),
            pl.BlockSpec((None, D_MODEL // steps, D_MODEL), lambda b, k, q: (l, step(b, k, q), 0)),
        ],
        out_specs=[
            pl.BlockSpec((bq, gc), lambda b, k, q: (b * nq + q, k)),
            pl.BlockSpec((D_MODEL // steps, D_MODEL), lambda b, k, q: (step(b, k, q), 0)),
        ],
        out_shape=[
            jax.ShapeDtypeStruct((T_LAT, C_HEADS * HD), BF16),
            jax.ShapeDtypeStruct((D_MODEL, D_MODEL), BF16),
        ],
        compiler_params=pltpu.CompilerParams(
            dimension_semantics=("arbitrary", "arbitrary", "arbitrary")),
        name=f"attn_lat_c_l{l}",
    )(sink_l, qkvu, qkvu, qkvu, cache_k4, cache_v4, w_out)


def _s5_param_kernel(rows_ref, bc_ref, c_ref, wmod_ref, bmod_ref, win_ref, w1_ref, w2_ref, a8_ref, mod_ref,
                     winb_ref):
    _mod_columns(c_ref, wmod_ref, bmod_ref, mod_ref)
    winb_ref[...] = win_ref[...].astype(BF16)
    a8_ref[...] = jnp.zeros_like(a8_ref)
    row_grp = lax.broadcasted_iota(jnp.int32, (128, S5_ST), 0) // S5_GC
    lane_grp = lax.broadcasted_iota(jnp.int32, (128, S5_ST), 1) // S5_P
    on_diag = row_grp == lane_grp

    def expand(a):
        return jnp.where(on_diag, jnp.concatenate([a] * S5_OG, axis=1), 0.0)

    taps = []
    for d in range(2):
        lr = rows_ref[0, d]
        li = rows_ref[1, d]
        dt = jnp.exp(rows_ref[2, d])
        mag = jnp.exp(lr * dt)
        ar = mag * jnp.cos(li * dt)
        ai = mag * jnp.sin(li * dt)
        den = lr * lr + li * li
        n_re = ar - 1.0
        f_re = (n_re * lr + ai * li) / den
        f_im = (ai * lr - n_re * li) / den
        pw = [(jnp.ones_like(ar), jnp.zeros_like(ar))]
        for _ in range(S5_T):
            pr, pi = pw[-1]
            pw.append((pr * ar - pi * ai, pr * ai + pi * ar))
        br, bi, cr, ci = (expand(bc_ref[k, d]) for k in range(4))
        bbr = f_re * br - f_im * bi
        bbi = f_re * bi + f_im * br
        xr_blocks, xi_blocks = [], []
        for t in range(S5_T):
            rows = slice(t * 128, (t + 1) * 128)
            pr, pi = pw[S5_T - 1 - t if d == 0 else t]
            xr = pr * bbr - pi * bbi
            xi = pr * bbi + pi * bbr
            c0 = S5_ROW + 2 * S5_ST * d
            w1_ref[rows, c0:c0 + S5_ST] = xr.astype(BF16)
            w1_ref[rows, c0 + S5_ST:c0 + 2 * S5_ST] = xi.astype(BF16)
            xr_blocks.append(xr)
            xi_blocks.append(xi)
            pr, pi = pw[t + 1 if d == 0 else S5_T - t]
            c0 = 2 * S5_ST * d
            w2_ref[rows, c0:c0 + S5_ST] = (cr * pr - ci * pi).astype(BF16)
            w2_ref[rows, c0 + S5_ST:c0 + 2 * S5_ST] = (-(cr * pi + ci * pr)).astype(BF16)
        a8_ref[2 * d:2 * d + 1, :] = pw[S5_T][0]
        a8_ref[2 * d + 1:2 * d + 2, :] = pw[S5_T][1]
        xr_all = jnp.concatenate(xr_blocks, axis=0)
        xi_all = jnp.concatenate(xi_blocks, axis=0)
        taps.append(_dot_nt_split(xr_all, cr) - _dot_nt_split(xi_all, ci))
    tf, tb = taps
    for t in range(S5_T):
        for t2 in range(S5_T):
            blk = None
            if t2 >= t:
                r0 = (S5_T - 1 - (t2 - t)) * 128
                blk = tf[r0:r0 + 128, :]
            if t2 <= t:
                r0 = (t - t2) * 128
                b2 = tb[r0:r0 + 128, :]
                blk = b2 if blk is None else blk + b2
            w1_ref[t * 128:(t + 1) * 128, t2 * 128:(t2 + 1) * 128] = blk.astype(BF16)


def _s5_params(s5_lam_re, s5_lam_im, s5_log_step, s5_b_re, s5_b_im, s5_c_re, s5_c_im, cvec8, w_mod, b_mod3, w_in):
    rows = jnp.stack([s5_lam_re, s5_lam_im, jnp.repeat(s5_log_step[..., None], S5_P, axis=-1)], axis=0)
    rows = rows.reshape(3, DEPTH, 2, S5_OCT, 1, S5_ST)
    bc = jnp.stack([jnp.swapaxes(s5_b_re, -1, -2), jnp.swapaxes(s5_b_im, -1, -2), s5_c_re, s5_c_im], axis=0)
    bc = bc.reshape(4, DEPTH, 2, S5_OCT, 128, S5_P)
    spec = lambda n, r, c: pl.BlockSpec((n, None, 2, None, r, c), lambda l, s: (0, l, 0, s, 0, 0))
    steps = DEPTH * S5_OCT
    step = lambda l, s: l * S5_OCT + s
    n_tail = MOD_MID - MOD_HEAD
    bn = n_tail // steps
    br = D_MODEL // steps
    return pl.pallas_call(
        _s5_param_kernel,
        grid=(DEPTH, S5_OCT),
        in_specs=[
            spec(3, 1, S5_ST), spec(4, 128, S5_P),
            pl.BlockSpec((8, D_MODEL), lambda l, s: (0, 0)),
            pl.BlockSpec((None, D_MODEL, bn), lambda l, s: (0, 0, MOD_HEAD // bn + step(l, s))),
            pl.BlockSpec((None, 1, bn), lambda l, s: (0, 0, MOD_HEAD // bn + step(l, s))),
            pl.BlockSpec((None, br, IN_WIDTH), lambda l, s: (0, step(l, s), 0)),
        ],
        out_specs=[
            pl.BlockSpec((None, None, S5_ROW, S5_W1), lambda l, s: (l, s, 0, 0)),
            pl.BlockSpec((None, None, S5_ROW, 4 * S5_ST), lambda l, s: (l, s, 0, 0)),
            pl.BlockSpec((None, None, 8, S5_ST), lambda l, s: (l, s, 0, 0)),
            pl.BlockSpec((8, bn), lambda l, s: (0, step(l, s))),
            pl.BlockSpec((br, IN_WIDTH), lambda l, s: (step(l, s), 0)),
        ],
        out_shape=[
            jax.ShapeDtypeStruct((DEPTH, S5_OCT, S5_ROW, S5_W1), BF16),
            jax.ShapeDtypeStruct((DEPTH, S5_OCT, S5_ROW, 4 * S5_ST), BF16),
            jax.ShapeDtypeStruct((DEPTH, S5_OCT, 8, S5_ST), F32),
            jax.ShapeDtypeStruct((8, n_tail), F32),
            jax.ShapeDtypeStruct((D_MODEL, IN_WIDTH), BF16),
        ],
        compiler_params=pltpu.CompilerParams(
            dimension_semantics=("arbitrary", "arbitrary"), vmem_limit_bytes=VMEM_BIG),
        name="s5_params",
    )(rows, bc, cvec8, w_mod, b_mod3, w_in)


def _s5_mix_kernel(u_ref, w1_ref, w2_ref, a8_ref, h0_ref, y_ref, hfin_ref, lhs_ref, a_ref, hp_ref, y8_ref):
    def stream(nb, nc, is_ctx):
        for b in range(nb):
            for t in range(S5_T):
                lhs_ref[t, pl.ds(b, nc, stride=nb), :] = u_ref[pl.ds(b * nc * S5_T + t, nc, stride=S5_T), :]
        lhs = jnp.concatenate([lhs_ref[t] for t in range(S5_T)], axis=1).astype(BF16)
        a_ref[...] = _dot(lhs, w1_ref[:, S5_ROW:])
        y_intra = _dot(lhs, w1_ref[:, 0:S5_ROW])

        coef = [a8_ref[k:k + 1, :] for k in range(4)]
        if is_ctx:
            init = tuple(jnp.zeros((nb, S5_ST), F32) for _ in range(4))
        else:
            init = tuple(h0_ref[k] for k in range(4))

        def step(c, st):
            new = []
            for d in range(2):
                cc = c if d == 0 else nc - 1 - c
                rows = slice(cc * nb, (cc + 1) * nb)
                hr, hi = st[2 * d], st[2 * d + 1]
                hp_ref[rows, 2 * S5_ST * d:2 * S5_ST * d + S5_ST] = hr
                hp_ref[rows, 2 * S5_ST * d + S5_ST:2 * S5_ST * (d + 1)] = hi
                c0 = 2 * S5_ST * d
                gr = a_ref[rows, c0:c0 + S5_ST]
                gi = a_ref[rows, c0 + S5_ST:c0 + 2 * S5_ST]
                ar, ai = coef[2 * d], coef[2 * d + 1]
                new += [ar * hr - ai * hi + gr, ar * hi + ai * hr + gi]
            return tuple(new)

        fin = init
        for c in range(nc):
            fin = step(c, fin)
        if is_ctx:
            for k in range(4):
                hfin_ref[k] = fin[k]

        y8 = y_intra + _dot_nt(hp_ref[...].astype(BF16), w2_ref[...])
        for t in range(S5_T):
            y8_ref[t] = y8[:, t * 128:(t + 1) * 128]
        for b in range(nb):
            for t in range(S5_T):
                y_ref[pl.ds(b * nc * S5_T + t, nc, stride=S5_T), :] = y8_ref[t, pl.ds(b, nc, stride=nb), :]

    pl.when(pl.program_id(1) == 0)(functools.partial(stream, CTX_B, CTX_NC, True))
    pl.when(pl.program_id(1) == 1)(functools.partial(stream, LAT_B, LAT_NC, False))


def _s5_mix(uf, w1, w2, a8, h0, l):
    return pl.pallas_call(
        _s5_mix_kernel,
        grid=(S5_OCT, 2),
        in_specs=[
            pl.BlockSpec((T_CTX, 128), lambda s, k: (k, s)),
            pl.BlockSpec((None, None, S5_ROW, S5_W1), lambda s, k: (l, s, 0, 0)),
            pl.BlockSpec((None, None, S5_ROW, 4 * S5_ST), lambda s, k: (l, s, 0, 0)),
            pl.BlockSpec((None, None, 8, S5_ST), lambda s, k: (l, s, 0, 0)),
            pl.BlockSpec((None, 4, LAT_B, S5_ST), lambda s, k: (s, 0, 0, 0)),
        ],
        out_specs=[
            pl.BlockSpec((T_CTX, 128), lambda s, k: (k, s)),
            pl.BlockSpec((None, 4, CTX_B, S5_ST), lambda s, k: (s, 0, 0, 0)),
        ],
        out_shape=[
            jax.ShapeDtypeStruct((T_ALL, S5_CH), F32),
            jax.ShapeDtypeStruct((S5_OCT, 4, CTX_B, S5_ST), F32),
        ],
        scratch_shapes=[
            pltpu.VMEM((S5_T, S5_ROWS, 128), F32),
            pltpu.VMEM((S5_ROWS, 4 * S5_ST), F32),
            pltpu.VMEM((S5_ROWS, 4 * S5_ST), F32),
            pltpu.VMEM((S5_T, S5_ROWS, 128), F32),
        ],
        compiler_params=pltpu.CompilerParams(
            dimension_semantics=("arbitrary", "arbitrary"), vmem_limit_bytes=VMEM_BIG),
        name=f"s5_mix_l{l}",
    )(uf, w1, w2, a8, h0)


def _s5_gate(y, u, d, w, b):
    y = y + d * u
    z = y * (0.5 * (1.0 + jnp.tanh(math.sqrt(2.0 / math.pi) * (y + 0.044715 * (y * y * y)))))
    t = _dot(z.astype(BF16), w) + b
    return (z * jax.nn.sigmoid(t)).astype(BF16)


def _outproj_kernel(x_ref, a_ctx_ref, a_lat_ref, c_ctx_ref, c_lat_ref, y_ref, u_ref, d_ref, wglu_ref, bglu_ref,
                    mod_ref, npost_ref, wb_ref, o_ref):
    na = A_HEADS * HD
    nc = na + C_HEADS * HD

    def body(a_ref, c_ref):
        gate = mod_ref[:, 5 * D_MODEL:6 * D_MODEL]
        wglu = wglu_ref[...].astype(BF16)
        for r in range(BM_OUT // RC_OUT):
            rows = slice(r * RC_OUT, (r + 1) * RC_OUT)
            s = _s5_gate(y_ref[rows, :], u_ref[rows, :], d_ref[...], wglu, bglu_ref[...])
            y = (_dot(a_ref[rows, :], wb_ref[0:na, :]) + _dot(c_ref[rows, :], wb_ref[na:nc, :])
                 + _dot(s, wb_ref[nc:, :]))
            o_ref[rows, :] = x_ref[rows, :] + gate * _rms(y, npost_ref[...])

    is_lat = pl.program_id(0) >= T_CTX // BM_OUT
    pl.when(jnp.logical_not(is_lat))(functools.partial(body, a_ctx_ref, c_ctx_ref))
    pl.when(is_lat)(functools.partial(body, a_lat_ref, c_lat_ref))


def _outproj(x, a_ctx, a_lat, c_ctx, c_lat, y_s5, uf, s5_d3, w_glu, b_glu3, mods3, norm_post4, w_out_bf, l):
    bm = BM_OUT
    nct = T_CTX // bm
    ctx_idx = lambda i: (jnp.minimum(i, nct - 1), 0)
    lat_idx = lambda i: (jnp.maximum(i - nct, 0), 0)
    return pl.pallas_call(
        _outproj_kernel,
        grid=(T_ALL // bm,),
        in_specs=[
            pl.BlockSpec((bm, D_MODEL), lambda i: (i, 0)),
            pl.BlockSpec((bm, A_HEADS * HD), ctx_idx),
            pl.BlockSpec((bm, A_HEADS * HD), lat_idx),
            pl.BlockSpec((bm, C_HEADS * HD), ctx_idx),
            pl.BlockSpec((bm, C_HEADS * HD), lat_idx),
            pl.BlockSpec((bm, S5_CH), lambda i: (i, 0)),
            pl.BlockSpec((bm, S5_CH), lambda i: (i, 0)),
            pl.BlockSpec((None, 1, S5_CH), lambda i: (l, 0, 0)),
            pl.BlockSpec((None, S5_CH, S5_CH), lambda i: (l, 0, 0)),
            pl.BlockSpec((None, 1, S5_CH), lambda i: (l, 0, 0)),
            pl.BlockSpec((None, 1, N_MOD * D_MODEL), lambda i: (_mod_index(i, bm), 0, 0)),
            pl.BlockSpec((None, None, 1, D_MODEL), lambda i: (l, 1, 0, 0)),
            pl.BlockSpec((D_MODEL, D_MODEL), lambda i: (0, 0), pipeline_mode=pl.Buffered(1)),
        ],
        out_specs=pl.BlockSpec((bm, D_MODEL), lambda i: (i, 0)),
        out_shape=jax.ShapeDtypeStruct((T_ALL, D_MODEL), F32),
        compiler_params=pltpu.CompilerParams(
            dimension_semantics=("arbitrary",), vmem_limit_bytes=VMEM_BIG),
        name=f"outproj_l{l}",
    )(x, a_ctx, a_lat, c_ctx, c_lat, y_s5, uf, s5_d3, w_glu, b_glu3, mods3, norm_post4, w_out_bf)


def _rope_tables():
    rows = LAT_L // GRID_W
    row = jnp.repeat(jnp.arange(rows, dtype=F32), GRID_W)
    col = jnp.tile(jnp.arange(GRID_W, dtype=F32), rows)
    axis_dim = HD // 2
    inv_freq = ROPE_BASE ** (-jnp.arange(0, axis_dim, 2, dtype=F32) / axis_dim)
    ang_row = row[:, None] * inv_freq
    ang_col = col[:, None] * inv_freq
    cr, sr = jnp.cos(ang_row), jnp.sin(ang_row)
    cc, sc = jnp.cos(ang_col), jnp.sin(ang_col)
    cos_t = jnp.concatenate([cr, cr, cc, cc], axis=-1)
    sin_t = jnp.concatenate([-sr, sr, -sc, sc], axis=-1)
    return cos_t, sin_t


def kernel(x_prompt, x_sample, cache_a_k, cache_a_v, cache_c_k, cache_c_v, state_ssm_re, state_ssm_im,
           c, c_ctx, w_mod, b_mod, norm_pre, norm_post, ffn_gate, ffn_up, ffn_down, w_in, w_out,
           q_norm, k_norm, sink, s5_lam_re, s5_lam_im, s5_log_step, s5_b_re, s5_b_im, s5_c_re, s5_c_im,
           s5_d, w_glu, b_glu):
    cvec8 = jnp.concatenate([c_ctx[None, :], c, jnp.zeros((8 - 1 - LAT_B, D_MODEL), F32)], axis=0)
    b_mod3 = b_mod.reshape(DEPTH, 1, N_MOD * D_MODEL)
    norm_pre4 = norm_pre.reshape(DEPTH, 3, 1, D_MODEL)
    norm_post4 = norm_post.reshape(DEPTH, 3, 1, D_MODEL)
    q_norm3 = q_norm.reshape(DEPTH, 1, HD)
    k_norm3 = k_norm.reshape(DEPTH, 1, HD)
    s5_d3 = s5_d.reshape(DEPTH, 1, S5_CH)
    b_glu3 = b_glu.reshape(DEPTH, 1, S5_CH)
    cos_t, sin_t = _rope_tables()
    mods_head = _modulation(cvec8, w_mod, b_mod3, 0, MOD_HEAD)
    w1, w2, a8, mods_mid, w_in_bf = _s5_params(s5_lam_re, s5_lam_im, s5_log_step, s5_b_re, s5_b_im,
                                               s5_c_re, s5_c_im, cvec8, w_mod, b_mod3, w_in)
    mods_tail = jnp.zeros((8, N_MOD * D_MODEL - MOD_MID), F32)
    table = lambda tail: jnp.concatenate([mods_head, mods_mid, tail], axis=1).reshape(8, 1, N_MOD * D_MODEL)
    mods3 = table(mods_tail)
    kv4 = lambda a: a.reshape(LAT_B, DEPTH, PAST, A_KV * HD)
    cak, cav, cck, ccv = kv4(cache_a_k), kv4(cache_a_v), kv4(cache_c_k), kv4(cache_c_v)
    h0_all = jnp.stack([state_ssm_re[:, :, 0], state_ssm_im[:, :, 0],
                        state_ssm_re[:, :, 1], state_ssm_im[:, :, 1]], axis=0)
    h0_all = h0_all.reshape(4, LAT_B, DEPTH, S5_OCT, S5_ST).transpose(2, 3, 0, 1, 4)

    ffn_w = (norm_pre4, norm_post4, ffn_gate, ffn_up, ffn_down)
    new_caches = ()
    new_state = []
    xs = [x_prompt.reshape(T_CTX, D_MODEL), x_sample.reshape(T_LAT, D_MODEL)]
    for l in range(DEPTH):
        x = _ffn(xs, mods3, *ffn_w, l, 0)

        if l == 0:
            qkvu, uf, *new_caches, mods_tail = _inproj(
                x, mods3, norm_pre4, q_norm3, k_norm3, cos_t, sin_t, w_in_bf, l, (),
                mod_args=(cvec8, w_mod, b_mod3), mod_cols=(MOD_MID, N_MOD * D_MODEL))
            mods3 = table(mods_tail)
        else:
            qkvu, uf, *new_caches = _inproj(x, mods3, norm_pre4, q_norm3, k_norm3, cos_t, sin_t, w_in_bf, l,
                                            new_caches)
        a_ctx, c_ctx = _attn_ctx(qkvu, sink[l], l)
        if l < DEPTH - 1:
            a_lat, mods_next, w_in_next = _attn_lat_a(qkvu, cak, cav, l, next_mod=(cvec8, w_mod, b_mod3, w_in))
        else:
            a_lat = _attn_lat_a(qkvu, cak, cav, l)
        c_lat, w_out_bf = _attn_lat_c(qkvu, sink[l], cck, ccv, w_out, l)
        y, hfin = _s5_mix(uf, w1, w2, a8, h0_all[l], l)
        x = _outproj(x, a_ctx, a_lat, c_ctx, c_lat, y, uf, s5_d3, w_glu, b_glu3, mods3, norm_post4, w_out_bf, l)
        if l < DEPTH - 1:
            xs = [_ffn([x], mods3, *ffn_w, l, 1)]
            mods3 = mods_next.reshape(8, 1, N_MOD * D_MODEL)
            w_in_bf = w_in_next
        else:
            y_prompt, y_sample = _ffn([x], mods3, *ffn_w, l, 1, split_out=True)

        hf = hfin.reshape(S5_OCT, 4, CTX_B, S5_OG, S5_P).transpose(1, 2, 0, 3, 4).reshape(4, CTX_B, S5_G, S5_P)
        new_state.append((jnp.stack([hf[0], hf[2]], axis=1), jnp.stack([hf[1], hf[3]], axis=1)))

    y_prompt = y_prompt.reshape(CTX_B, CTX_L, D_MODEL)
    y_sample = y_sample.reshape(LAT_B, LAT_L, D_MODEL)
    caches = list(new_caches)
    st_re = jnp.stack([new_state[l][0] for l in range(DEPTH)], axis=1)
    st_im = jnp.stack([new_state[l][1] for l in range(DEPTH)], axis=1)
    return (y_prompt, y_sample, caches[0], caches[1], caches[2], caches[3], st_re, st_im)
```

```python
import functools
import math

import jax
import jax.numpy as jnp
from jax import lax
from jax.experimental import pallas as pl
from jax.experimental.pallas import tpu as pltpu

F32 = jnp.float32
BF16 = jnp.bfloat16

D_MODEL = 2048
CTX_B, CTX_L = 16, 256
LAT_B, LAT_L = 2, 2048
DEPTH = 2
PAST = 512
GRID_W = 64
HD = 128
A_HEADS, A_KV = 8, 2
C_HEADS, C_KV = 4, 2
WINDOW = 128
S5_GC = 16
S5_CH = 512
S5_G = 32
S5_P = 64
D_FF = 5632
N_MOD = 9
IN_WIDTH = 3072
ROPE_BASE = 10000.0
EPS = 1e-6
HALF_STEP = 0.5
NEG_INF = -1e30
SCALE = HD ** -0.5
LOG2E = math.log2(math.e)

T_CTX = CTX_B * CTX_L
T_LAT = LAT_B * LAT_L
T_ALL = T_CTX + T_LAT

COL_AQ, COL_AK, COL_AV = 0, 1024, 1280
COL_CQ, COL_CK, COL_CV = 1536, 2048, 2304
COL_U = 2560

S5_T = 8
S5_OCT = S5_CH // 128
S5_OG = S5_G // S5_OCT
S5_ROW = S5_T * 128
S5_ST = S5_OG * S5_P
S5_W1 = S5_ROW + 4 * S5_ST
CTX_NC = CTX_L // S5_T
LAT_NC = LAT_L // S5_T
S5_ROWS = CTX_NC * CTX_B
assert S5_ROWS == LAT_NC * LAT_B and T_CTX == T_LAT

V7X_VMEM_BYTES = 64 * 1024 * 1024
VMEM_FFN = 62 * 1024 * 1024
VMEM_BIG = 56 * 1024 * 1024
VMEM_MID = 40 * 1024 * 1024

BM = 1024
BM_OUT = 512
RC_OUT = 256
BF = 512
RC = 512
RCX = 256
N_XC = BM // RCX
FFN_PREFETCH_STEP = 2
BN_MOD = 1024
MOD_HEAD = 3 * D_MODEL
MOD_MID = 5 * D_MODEL
BM_IN = 512
BQ_A = 512
BQ_C = 256
NQ_C = 8
NB_CTX = 8


def _dot(a, b):
    return jnp.dot(a, b, preferred_element_type=F32)


def _dot_nt(a, b, precision=None):
    return lax.dot_general(a, b, (((1,), (1,)), ((), ())), preferred_element_type=F32, precision=precision)


def _dot_nt_split(a, b):
    ah = a.astype(BF16)
    bh = b.astype(BF16)
    al = (a - ah.astype(F32)).astype(BF16)
    bl = (b - bh.astype(F32)).astype(BF16)
    return _dot_nt(ah, bh) + (_dot_nt(ah, bl) + _dot_nt(al, bh))


def _rms(x, g):
    return x * lax.rsqrt(jnp.mean(x * x, axis=-1, keepdims=True) + EPS) * g


def _mod_index(i, bm):
    nct = T_CTX // bm
    return jnp.where(i < nct, 0, 1 + (i - nct) // (LAT_L // bm))


def _mod_columns(c_ref, w_ref, b_ref, o_ref):
    c = c_ref[...]
    s = (c * jax.nn.sigmoid(c)).astype(BF16)
    o_ref[...] = _dot(s, w_ref[...].astype(BF16)) + b_ref[...]


def _modulation(cvec8, w_mod, b_mod3, l, n):
    return pl.pallas_call(
        _mod_columns,
        grid=(n // BN_MOD,),
        in_specs=[
            pl.BlockSpec((8, D_MODEL), lambda j: (0, 0)),
            pl.BlockSpec((None, D_MODEL, BN_MOD), lambda j: (l, 0, j)),
            pl.BlockSpec((None, 1, BN_MOD), lambda j: (l, 0, j)),
        ],
        out_specs=pl.BlockSpec((8, BN_MOD), lambda j: (0, j)),
        out_shape=jax.ShapeDtypeStruct((8, n), F32),
        compiler_params=pltpu.CompilerParams(
            dimension_semantics=("arbitrary",), vmem_limit_bytes=VMEM_MID),
        name=f"modulation_l{l}",
    )(cvec8, w_mod, b_mod3)


def _ffn_kernel(*refs, mo, n_x, n_out):
    x_hbms = refs[:n_x]
    mod_ref, npre_ref, npost_ref, wg_ref, wu_ref, wd_ref = refs[n_x:n_x + 6]
    out_hbms = refs[n_x + 6:n_x + 6 + n_out]
    acc_ref, xc_ref, h_ref, sem_x, sem_c, sem_o = refs[n_x + 6 + n_out:]
    i = pl.program_id(0)
    j = pl.program_id(1)
    n_tiles = pl.num_programs(0)
    last = pl.num_programs(1) - 1
    slot = i % 2
    nq = RC // RCX
    nct = T_CTX // BM

    def per_stream(arrays, tile, fn):
        if len(arrays) == 1:
            fn(arrays[0], pl.multiple_of(tile * BM, BM))
        else:
            pl.when(tile < nct)(lambda: fn(arrays[0], pl.multiple_of(tile * BM, BM)))
            pl.when(tile >= nct)(lambda: fn(arrays[1], pl.multiple_of((tile - nct) * BM, BM)))

    def x_tile_copy(arr, row0, sl):
        return pltpu.make_async_copy(arr.at[pl.ds(row0, BM), :], acc_ref.at[sl], sem_x)

    def x_chunk_copy(arr, row0, cs):
        return pltpu.make_async_copy(arr.at[pl.ds(row0, RCX), :], xc_ref.at[cs], sem_c.at[cs])

    def out_copy(arr, row0, sl):
        return pltpu.make_async_copy(acc_ref.at[sl], arr.at[pl.ds(row0, BM), :], sem_o.at[sl])

    wait_x_tile = lambda sl: x_tile_copy(x_hbms[0], 0, sl).wait()
    wait_x_chunk = lambda cs: x_chunk_copy(x_hbms[0], 0, cs).wait()
    wait_out = lambda sl: out_copy(out_hbms[0], 0, sl).wait()

    def start_x_chunk(q):
        per_stream(x_hbms, i, lambda arr, row0: x_chunk_copy(
            arr, pl.multiple_of(row0 + q * RCX, RCX), q).start())

    @pl.when(jnp.logical_and(i == 0, j == 0))
    def _():
        x_tile_copy(x_hbms[0], 0, 0).start()

    @pl.when(j == 0)
    def _():
        wait_x_tile(slot)

    @pl.when(j == FFN_PREFETCH_STEP)
    def _():
        pl.when(i >= 1)(lambda: wait_out(1 - slot))

        @pl.when(i + 1 < n_tiles)
        def _():
            per_stream(x_hbms, i + 1, lambda arr, row0: x_tile_copy(arr, row0, 1 - slot).start())

    @pl.when(j == last - 1)
    def _():
        for q in range(N_XC):
            start_x_chunk(q)

    def step(first, final):
        if final:
            for q in range(N_XC):
                wait_x_chunk(q)
        acc_slot = acc_ref.at[slot]
        wg = wg_ref[...].astype(BF16)
        wu = wu_ref[...].astype(BF16)
        wd = wd_ref[...].astype(BF16)
        for r in range(BM // RC):
            rows = slice(r * RC, (r + 1) * RC)
            if first:
                shift = mod_ref[:, mo * D_MODEL:(mo + 1) * D_MODEL]
                scale = mod_ref[:, (mo + 1) * D_MODEL:(mo + 2) * D_MODEL]
                hn = _rms(acc_slot[rows, :], npre_ref[...])
                h = (hn * (1.0 + scale) + shift).astype(BF16)
                h_ref[rows, :] = h
            else:
                h = h_ref[rows, :]
            g = _dot(h, wg)
            u = _dot(h, wu)
            a = (g * jax.nn.sigmoid(g) * u).astype(BF16)
            acc = _dot(a, wd)
            if not first:
                acc = acc_slot[rows, :] + acc
            if not final:
                acc_slot[rows, :] = acc
                continue
            gate = mod_ref[:, (mo + 2) * D_MODEL:(mo + 3) * D_MODEL]
            for qq in range(nq):
                sub = slice(qq * RCX, (qq + 1) * RCX)
                y = xc_ref[r * nq + qq] + (HALF_STEP * gate) * _rms(acc[sub, :], npost_ref[...])
                acc_slot[r * RC + qq * RCX:r * RC + (qq + 1) * RCX, :] = y
        if final:
            per_stream(out_hbms, i, lambda arr, row0: out_copy(arr, row0, slot).start())

    pl.when(j == 0)(functools.partial(step, True, False))
    pl.when(jnp.logical_and(j > 0, j < last))(functools.partial(step, False, False))
    pl.when(j == last)(functools.partial(step, False, True))

    @pl.when(jnp.logical_and(i == n_tiles - 1, j == last))
    def _():
        wait_out(slot)


def _ffn(xs, mods3, norm_pre4, norm_post4, ffn_gate, ffn_up, ffn_down, l, s, *, split_out=False):
    mo = 6 * s
    ni = 2 * s
    in_specs = [pl.BlockSpec(memory_space=pl.ANY)] * len(xs) + [
        pl.BlockSpec((None, 1, N_MOD * D_MODEL), lambda i, j: (_mod_index(i, BM), 0, 0)),
        pl.BlockSpec((None, None, 1, D_MODEL), lambda i, j: (l, ni, 0, 0)),
        pl.BlockSpec((None, None, 1, D_MODEL), lambda i, j: (l, ni, 0, 0)),
        pl.BlockSpec((None, None, D_MODEL, BF), lambda i, j: (l, s, 0, j)),
        pl.BlockSpec((None, None, D_MODEL, BF), lambda i, j: (l, s, 0, j)),
        pl.BlockSpec((None, None, BF, D_MODEL), lambda i, j: (l, s, j, 0)),
    ]
    out_rows = (T_CTX, T_LAT) if split_out else (T_ALL,)
    outs = pl.pallas_call(
        functools.partial(_ffn_kernel, mo=mo, n_x=len(xs), n_out=len(out_rows)),
        grid=(T_ALL // BM, D_FF // BF),
        in_specs=in_specs,
        out_specs=[pl.BlockSpec(memory_space=pl.ANY)] * len(out_rows),
        out_shape=[jax.ShapeDtypeStruct((r, D_MODEL), F32) for r in out_rows],
        scratch_shapes=[
            pltpu.VMEM((2, BM, D_MODEL), F32),
            pltpu.VMEM((N_XC, RCX, D_MODEL), F32),
            pltpu.VMEM((BM, D_MODEL), BF16),
            pltpu.SemaphoreType.DMA(()),
            pltpu.SemaphoreType.DMA((N_XC,)),
            pltpu.SemaphoreType.DMA((2,)),
        ],
        compiler_params=pltpu.CompilerParams(
            dimension_semantics=("arbitrary", "arbitrary"), vmem_limit_bytes=VMEM_FFN),
        name=f"ffn_l{l}_s{s}",
    )(*xs, mods3, norm_pre4, norm_post4, ffn_gate, ffn_up, ffn_down)
    return outs if split_out else outs[0]


def _rope(y, cos, sins):
    lane = lax.broadcasted_iota(jnp.int32, y.shape, 1)
    first = (lane & 63) < 32
    partner = jnp.where(first, pltpu.roll(y, 96, 1), pltpu.roll(y, 32, 1))
    return y * cos + partner * sins


_IN_SEGMENTS = (
    (COL_AQ, A_HEADS, "q", True, None),
    (COL_AK, A_KV, "k", True, 0),
    (COL_AV, A_KV, None, False, 1),
    (COL_CQ, C_HEADS, None, True, None),
    (COL_CK, C_KV, None, True, 2),
    (COL_CV, C_KV, None, False, 3),
)


def _inproj_kernel(x_ref, mod_ref, npre_ref, qn_ref, kn_ref, cos_ref, sin_ref, w_ref, *rest, first_layer):
    nb = BM_IN // CTX_L
    if first_layer:
        c_ref, wmod_ref, bmod_ref, qkvu_ref, uf_ref, *cache_full, modo_ref = rest
        cache_refs = [c.at[:, 0] for c in cache_full]
        _mod_columns(c_ref, wmod_ref, bmod_ref, modo_ref)
    else:
        qkvu_ref, uf_ref, *cache_refs = rest[4:]
        cache_full = ()

    def body(lat):
        shift = mod_ref[:, 3 * D_MODEL:4 * D_MODEL]
        scale = mod_ref[:, 4 * D_MODEL:5 * D_MODEL]
        h = (_rms(x_ref[...], npre_ref[...]) * (1.0 + scale) + shift).astype(BF16)
        for col0, heads, norm, rot, cache in _IN_SEGMENTS:
            p = _dot(h, w_ref[:, col0:col0 + heads * HD])
            for k in range(heads):
                y = p[:, k * HD:(k + 1) * HD]
                if norm == "q":
                    y = _rms(y, qn_ref[...])
                elif norm == "k":
                    y = _rms(y, kn_ref[...])
                if rot and lat:
                    y = _rope(y, cos_ref[...], sin_ref[...])
                qkvu_ref[:, col0 + k * HD:col0 + (k + 1) * HD] = y.astype(BF16)
                if cache is not None and not lat:
                    cache_refs[cache][:, :, k, :] = y.reshape(nb, CTX_L, HD)
        u = _dot(h, w_ref[:, COL_U:])
        uf_ref[...] = u
        qkvu_ref[:, COL_U:] = u.astype(BF16)
        if first_layer and not lat:
            for c in cache_full:
                c[:, 1:] = jnp.zeros((nb, DEPTH - 1, CTX_L, A_KV, HD), F32)

    is_lat = pl.program_id(0) >= T_CTX // BM_IN
    pl.when(is_lat)(functools.partial(body, True))
    pl.when(jnp.logical_not(is_lat))(functools.partial(body, False))


def _inproj(x, mods3, norm_pre4, q_norm3, k_norm3, cos_t, sin_t, w_in_bf, l, prev_caches, mod_args=None,
            mod_cols=None):
    bm = BM_IN
    nct = T_CTX // bm
    nb = bm // CTX_L
    tab_idx = lambda i: (jnp.maximum(i - nct, 0) % (LAT_L // bm), 0)
    cache_shape = jax.ShapeDtypeStruct((CTX_B, DEPTH, CTX_L, A_KV, HD), F32)
    first_layer = not prev_caches
    if first_layer:
        cache_spec = pl.BlockSpec((nb, DEPTH, CTX_L, A_KV, HD), lambda i: (jnp.minimum(i, nct - 1), 0, 0, 0, 0))
    else:
        cache_spec = pl.BlockSpec((nb, None, CTX_L, A_KV, HD), lambda i: (jnp.minimum(i, nct - 1), l, 0, 0, 0))
    n_in = 8
    extra_in, extra_out, extra_shape, extra_args = [], [], [], []
    if first_layer:
        lo, hi = mod_cols
        steps = T_ALL // bm
        bn = (hi - lo) // steps
        extra_in = [
            pl.BlockSpec((8, D_MODEL), lambda i: (0, 0)),
            pl.BlockSpec((None, D_MODEL, bn), lambda i: (l, 0, lo // bn + i)),
            pl.BlockSpec((None, 1, bn), lambda i: (l, 0, lo // bn + i)),
        ]
        extra_out = [pl.BlockSpec((8, bn), lambda i: (0, i))]
        extra_shape = [jax.ShapeDtypeStruct((8, hi - lo), F32)]
        extra_args = list(mod_args)
    return pl.pallas_call(
        functools.partial(_inproj_kernel, first_layer=first_layer),
        grid=(T_ALL // bm,),
        in_specs=[
            pl.BlockSpec((bm, D_MODEL), lambda i: (i, 0)),
            pl.BlockSpec((None, 1, N_MOD * D_MODEL), lambda i: (_mod_index(i, bm), 0, 0)),
            pl.BlockSpec((None, None, 1, D_MODEL), lambda i: (l, 1, 0, 0)),
            pl.BlockSpec((None, 1, HD), lambda i: (l, 0, 0)),
            pl.BlockSpec((None, 1, HD), lambda i: (l, 0, 0)),
            pl.BlockSpec((bm, HD), tab_idx),
            pl.BlockSpec((bm, HD), tab_idx),
            pl.BlockSpec((D_MODEL, IN_WIDTH), lambda i: (0, 0), pipeline_mode=pl.Buffered(1)),
        ] + extra_in + [pl.BlockSpec(memory_space=pl.ANY)] * len(prev_caches),
        out_specs=[
            pl.BlockSpec((bm, IN_WIDTH), lambda i: (i, 0)),
            pl.BlockSpec((bm, S5_CH), lambda i: (i, 0)),
        ] + [cache_spec] * 4 + extra_out,
        out_shape=[
            jax.ShapeDtypeStruct((T_ALL, IN_WIDTH), BF16),
            jax.ShapeDtypeStruct((T_ALL, S5_CH), F32),
        ] + [cache_shape] * 4 + extra_shape,
        input_output_aliases={n_in + k: 2 + k for k in range(len(prev_caches))},
        compiler_params=pltpu.CompilerParams(
            dimension_semantics=("arbitrary",), vmem_limit_bytes=VMEM_BIG),
        name=f"inproj_l{l}",
    )(x, mods3, norm_pre4, q_norm3, k_norm3, cos_t, sin_t, w_in_bf, *extra_args, *prev_caches)


def _softmax_pv(dots, values, sink=None):
    m = functools.reduce(jnp.maximum, [jnp.max(d, axis=-1, keepdims=True) for d in dots]) * SCALE
    if sink is not None:
        m = jnp.maximum(m, sink)
    m2 = m * LOG2E
    ps = [jnp.exp2(d * (SCALE * LOG2E) - m2) for d in dots]
    den = functools.reduce(jnp.add, [jnp.sum(p, axis=-1, keepdims=True) for p in ps])
    if sink is not None:
        den = den + jnp.exp2(sink * LOG2E - m2)
    o = functools.reduce(jnp.add, [_dot(p.astype(BF16), v) for p, v in zip(ps, values)])
    return o / den


def _attn_ctx_kernel(sink_ref, aq_ref, ak_ref, av_ref, cq_ref, ck_ref, cv_ref, ao_ref, co_ref):
    kv = pl.program_id(1)
    g = C_HEADS // C_KV
    for b in range(NB_CTX):
        rows = slice(b * CTX_L, (b + 1) * CTX_L)
        k = ak_ref[rows, :]
        v = av_ref[rows, :]
        for h in range(A_HEADS // A_KV):
            cols = slice(h * HD, (h + 1) * HD)
            s = _dot_nt(aq_ref[rows, cols], k)
            ao_ref[rows, cols] = _softmax_pv([s], [v]).astype(BF16)
        k = ck_ref[rows, :]
        v = cv_ref[rows, :]
        for h in range(g):
            cols = slice(h * HD, (h + 1) * HD)
            s = _dot_nt(cq_ref[rows, cols], k)
            co_ref[rows, cols] = _softmax_pv([s], [v], sink_ref[kv * g + h]).astype(BF16)


def _attn_ctx(qkvu, sink_l, l):
    ga = A_HEADS // A_KV * HD
    gc = C_HEADS // C_KV * HD
    rows = NB_CTX * CTX_L
    blk = lambda width, col0: pl.BlockSpec((rows, width), lambda b, k: (b, col0 // width + k))
    return pl.pallas_call(
        _attn_ctx_kernel,
        grid=(CTX_B // NB_CTX, A_KV),
        in_specs=[
            pl.BlockSpec(memory_space=pltpu.SMEM),
            blk(ga, COL_AQ), blk(HD, COL_AK), blk(HD, COL_AV),
            blk(gc, COL_CQ), blk(HD, COL_CK), blk(HD, COL_CV),
        ],
        out_specs=[
            pl.BlockSpec((rows, ga), lambda b, k: (b, k)),
            pl.BlockSpec((rows, gc), lambda b, k: (b, k)),
        ],
        out_shape=[
            jax.ShapeDtypeStruct((T_CTX, A_HEADS * HD), BF16),
            jax.ShapeDtypeStruct((T_CTX, C_HEADS * HD), BF16),
        ],
        compiler_params=pltpu.CompilerParams(dimension_semantics=("arbitrary", "arbitrary")),
        name=f"attn_ctx_l{l}",
    )(sink_l, qkvu, qkvu, qkvu, qkvu, qkvu, qkvu)


def _attn_lat_a_kernel(q_ref, k_ref, v_ref, kc_ref, vc_ref, *rest):
    o_ref = rest[0] if len(rest) == 1 else rest[4]
    k = k_ref[...]
    v = v_ref[...]
    kc = kc_ref[...].astype(BF16)
    vc = vc_ref[...].astype(BF16)
    for h in range(A_HEADS // A_KV):
        cols = slice(h * HD, (h + 1) * HD)
        q = q_ref[:, cols]
        s1 = _dot_nt(q, k)
        s2 = _dot_nt(q, kc)
        o_ref[:, cols] = _softmax_pv([s1, s2], [v, vc]).astype(BF16)
    if len(rest) > 1:
        c_ref, w_ref, b_ref, win_ref, _, mod_ref, winb_ref = rest
        _mod_columns(c_ref, w_ref, b_ref, mod_ref)
        winb_ref[...] = win_ref[...].astype(BF16)


def _attn_lat_a(qkvu, cache_k4, cache_v4, l, next_mod=None):
    ga = A_HEADS // A_KV * HD
    nq = LAT_L // BQ_A
    row0 = T_CTX // BQ_A
    lat_blk = T_CTX // LAT_L
    in_specs = [
        pl.BlockSpec((BQ_A, ga), lambda b, k, q: (row0 + b * nq + q, COL_AQ // ga + k)),
        pl.BlockSpec((LAT_L, HD), lambda b, k, q: (lat_blk + b, COL_AK // HD + k)),
        pl.BlockSpec((LAT_L, HD), lambda b, k, q: (lat_blk + b, COL_AV // HD + k)),
        pl.BlockSpec((None, None, PAST, HD), lambda b, k, q: (b, l, 0, k)),
        pl.BlockSpec((None, None, PAST, HD), lambda b, k, q: (b, l, 0, k)),
    ]
    out_specs = [pl.BlockSpec((BQ_A, ga), lambda b, k, q: (b * nq + q, k))]
    out_shape = [jax.ShapeDtypeStruct((T_LAT, A_HEADS * HD), BF16)]
    args = [qkvu, qkvu, qkvu, cache_k4, cache_v4]
    if next_mod is not None:
        n = N_MOD * D_MODEL
        steps = LAT_B * A_KV * nq
        bn = n // steps
        step = lambda b, k, q: (b * A_KV + k) * nq + q
        in_specs += [
            pl.BlockSpec((8, D_MODEL), lambda b, k, q: (0, 0)),
            pl.BlockSpec((None, D_MODEL, bn), lambda b, k, q: (l + 1, 0, step(b, k, q))),
            pl.BlockSpec((None, 1, bn), lambda b, k, q: (l + 1, 0, step(b, k, q))),
            pl.BlockSpec((None, D_MODEL // steps, IN_WIDTH), lambda b, k, q: (l + 1, step(b, k, q), 0)),
        ]
        out_specs += [
            pl.BlockSpec((8, bn), lambda b, k, q: (0, step(b, k, q))),
            pl.BlockSpec((D_MODEL // steps, IN_WIDTH), lambda b, k, q: (step(b, k, q), 0)),
        ]
        out_shape += [
            jax.ShapeDtypeStruct((8, n), F32),
            jax.ShapeDtypeStruct((D_MODEL, IN_WIDTH), BF16),
        ]
        args += list(next_mod)
    outs = pl.pallas_call(
        _attn_lat_a_kernel,
        grid=(LAT_B, A_KV, nq),
        in_specs=in_specs,
        out_specs=out_specs,
        out_shape=out_shape,
        compiler_params=pltpu.CompilerParams(
            dimension_semantics=("arbitrary", "arbitrary", "arbitrary"), vmem_limit_bytes=VMEM_BIG),
        name=f"attn_lat_a_l{l}",
    )(*args)
    return outs if next_mod is not None else outs[0]


def _attn_lat_c_kernel(sink_ref, q_ref, k_ref, v_ref, kc_ref, vc_ref, wout_ref, o_ref, woutb_ref):
    woutb_ref[...] = wout_ref[...].astype(BF16)
    kv = pl.program_id(1)
    span = BQ_C + 2 * WINDOW
    kc = kc_ref[...].astype(BF16)
    vc = vc_ref[...].astype(BF16)
    g = C_HEADS // C_KV
    for sub in range(NQ_C):
        n = pl.program_id(2) * NQ_C + sub
        rows = slice(sub * BQ_C, (sub + 1) * BQ_C)
        start = pl.multiple_of(jnp.clip(n * BQ_C - WINDOW, 0, LAT_L - span), WINDOW)
        kw = k_ref[pl.ds(start, span), :]
        vw = v_ref[pl.ds(start, span), :]
        qpos = n * BQ_C + lax.broadcasted_iota(jnp.int32, (BQ_C, span), 0)
        kpos = start + lax.broadcasted_iota(jnp.int32, (BQ_C, span), 1)
        valid = jnp.abs(qpos - kpos) <= WINDOW
        for h in range(g):
            cols = slice(h * HD, (h + 1) * HD)
            q = q_ref[rows, cols]
            s1 = jnp.where(valid, _dot_nt(q, kw), NEG_INF)
            s2 = _dot_nt(q, kc)
            o_ref[rows, cols] = _softmax_pv([s1, s2], [vw, vc], sink_ref[kv * g + h]).astype(BF16)


def _attn_lat_c(qkvu, sink_l, cache_k4, cache_v4, w_out, l):
    gc = C_HEADS // C_KV * HD
    bq = BQ_C * NQ_C
    nq = LAT_L // bq
    row0 = T_CTX // bq
    lat_blk = T_CTX // LAT_L
    steps = LAT_B * C_KV * nq
    step = lambda b, k, q: (b * C_KV + k) * nq + q
    return pl.pallas_call(
        _attn_lat_c_kernel,
        grid=(LAT_B, C_KV, nq),
        in_specs=[
            pl.BlockSpec(memory_space=pltpu.SMEM),
            pl.BlockSpec((bq, gc), lambda b, k, q: (row0 + b * nq + q, COL_CQ // gc + k)),
            pl.BlockSpec((LAT_L, HD), lambda b, k, q: (lat_blk + b, COL_CK // HD + k)),
            pl.BlockSpec((LAT_L, HD), lambda b, k, q: (lat_blk + b, COL_CV // HD + k)),
            pl.BlockSpec((None, None, PAST, HD), lambda b, k, q: (b, l, 0, k)),
            pl.BlockSpec((None, None, PAST, HD), lambda b, k, q: (b, l, 0, k)),
            pl.BlockSpec((None, D_MODEL // steps, D_MODEL), lambda b, k, q: (l, step(b, k, q), 0)),
        ],
        out_specs=[
            pl.BlockSpec((bq, gc), lambda b, k, q: (b * nq + q, k)),
            pl.BlockSpec((D_MODEL // steps, D_MODEL), lambda b, k, q: (step(b, k, q), 0)),
        ],
        out_shape=[
            jax.ShapeDtypeStruct((T_LAT, C_HEADS * HD), BF16),
            jax.ShapeDtypeStruct((D_MODEL, D_MODEL), BF16),
        ],
        compiler_params=pltpu.CompilerParams(
            dimension_semantics=("arbitrary", "arbitrary", "arbitrary")),
        name=f"attn_lat_c_l{l}",
    )(sink_l, qkvu, qkvu, qkvu, cache_k4, cache_v4, w_out)


def _s5_param_kernel(rows_ref, bc_ref, c_ref, wmod_ref, bmod_ref, win_ref, w1_ref, w2_ref, a8_ref, mod_ref,
                     winb_ref):
    _mod_columns(c_ref, wmod_ref, bmod_ref, mod_ref)
    winb_ref[...] = win_ref[...].astype(BF16)
    a8_ref[...] = jnp.zeros_like(a8_ref)
    row_grp = lax.broadcasted_iota(jnp.int32, (128, S5_ST), 0) // S5_GC
    lane_grp = lax.broadcasted_iota(jnp.int32, (128, S5_ST), 1) // S5_P
    on_diag = row_grp == lane_grp

    def expand(a):
        return jnp.where(on_diag, jnp.concatenate([a] * S5_OG, axis=1), 0.0)

    taps = []
    for d in range(2):
        lr = rows_ref[0, d]
        li = rows_ref[1, d]
        dt = jnp.exp(rows_ref[2, d])
        mag = jnp.exp(lr * dt)
        ar = mag * jnp.cos(li * dt)
        ai = mag * jnp.sin(li * dt)
        den = lr * lr + li * li
        n_re = ar - 1.0
        f_re = (n_re * lr + ai * li) / den
        f_im = (ai * lr - n_re * li) / den
        pw = [(jnp.ones_like(ar), jnp.zeros_like(ar))]
        for _ in range(S5_T):
            pr, pi = pw[-1]
            pw.append((pr * ar - pi * ai, pr * ai + pi * ar))
        br, bi, cr, ci = (expand(bc_ref[k, d]) for k in range(4))
        bbr = f_re * br - f_im * bi
        bbi = f_re * bi + f_im * br
        xr_blocks, xi_blocks = [], []
        for t in range(S5_T):
            rows = slice(t * 128, (t + 1) * 128)
            pr, pi = pw[S5_T - 1 - t if d == 0 else t]
            xr = pr * bbr - pi * bbi
            xi = pr * bbi + pi * bbr
            c0 = S5_ROW + 2 * S5_ST * d
            w1_ref[rows, c0:c0 + S5_ST] = xr.astype(BF16)
            w1_ref[rows, c0 + S5_ST:c0 + 2 * S5_ST] = xi.astype(BF16)
            xr_blocks.append(xr)
            xi_blocks.append(xi)
            pr, pi = pw[t + 1 if d == 0 else S5_T - t]
            c0 = 2 * S5_ST * d
            w2_ref[rows, c0:c0 + S5_ST] = (cr * pr - ci * pi).astype(BF16)
            w2_ref[rows, c0 + S5_ST:c0 + 2 * S5_ST] = (-(cr * pi + ci * pr)).astype(BF16)
        a8_ref[2 * d:2 * d + 1, :] = pw[S5_T][0]
        a8_ref[2 * d + 1:2 * d + 2, :] = pw[S5_T][1]
        xr_all = jnp.concatenate(xr_blocks, axis=0)
        xi_all = jnp.concatenate(xi_blocks, axis=0)
        taps.append(_dot_nt_split(xr_all, cr) - _dot_nt_split(xi_all, ci))
    tf, tb = taps
    for t in range(S5_T):
        for t2 in range(S5_T):
            blk = None
            if t2 >= t:
                r0 = (S5_T - 1 - (t2 - t)) * 128
                blk = tf[r0:r0 + 128, :]
            if t2 <= t:
                r0 = (t - t2) * 128
                b2 = tb[r0:r0 + 128, :]
                blk = b2 if blk is None else blk + b2
            w1_ref[t * 128:(t + 1) * 128, t2 * 128:(t2 + 1) * 128] = blk.astype(BF16)


def _s5_params(s5_lam_re, s5_lam_im, s5_log_step, s5_b_re, s5_b_im, s5_c_re, s5_c_im, cvec8, w_mod, b_mod3, w_in):
    rows = jnp.stack([s5_lam_re, s5_lam_im, jnp.repeat(s5_log_step[..., None], S5_P, axis=-1)], axis=0)
    rows = rows.reshape(3, DEPTH, 2, S5_OCT, 1, S5_ST)
    bc = jnp.stack([jnp.swapaxes(s5_b_re, -1, -2), jnp.swapaxes(s5_b_im, -1, -2), s5_c_re, s5_c_im], axis=0)
    bc = bc.reshape(4, DEPTH, 2, S5_OCT, 128, S5_P)
    spec = lambda n, r, c: pl.BlockSpec((n, None, 2, None, r, c), lambda l, s: (0, l, 0, s, 0, 0))
    steps = DEPTH * S5_OCT
    step = lambda l, s: l * S5_OCT + s
    n_tail = MOD_MID - MOD_HEAD
    bn = n_tail // steps
    br = D_MODEL // steps
    return pl.pallas_call(
        _s5_param_kernel,
        grid=(DEPTH, S5_OCT),
        in_specs=[
            spec(3, 1, S5_ST), spec(4, 128, S5_P),
            pl.BlockSpec((8, D_MODEL), lambda l, s: (0, 0)),
            pl.BlockSpec((None, D_MODEL, bn), lambda l, s: (0, 0, MOD_HEAD // bn + step(l, s))),
            pl.BlockSpec((None, 1, bn), lambda l, s: (0, 0, MOD_HEAD // bn + step(l, s))),
            pl.BlockSpec((None, br, IN_WIDTH), lambda l, s: (0, step(l, s), 0)),
        ],
        out_specs=[
            pl.BlockSpec((None, None, S5_ROW, S5_W1), lambda l, s: (l, s, 0, 0)),
            pl.BlockSpec((None, None, S5_ROW, 4 * S5_ST), lambda l, s: (l, s, 0, 0)),
            pl.BlockSpec((None, None, 8, S5_ST), lambda l, s: (l, s, 0, 0)),
            pl.BlockSpec((8, bn), lambda l, s: (0, step(l, s))),
            pl.BlockSpec((br, IN_WIDTH), lambda l, s: (step(l, s), 0)),
        ],
        out_shape=[
            jax.ShapeDtypeStruct((DEPTH, S5_OCT, S5_ROW, S5_W1), BF16),
            jax.ShapeDtypeStruct((DEPTH, S5_OCT, S5_ROW, 4 * S5_ST), BF16),
            jax.ShapeDtypeStruct((DEPTH, S5_OCT, 8, S5_ST), F32),
            jax.ShapeDtypeStruct((8, n_tail), F32),
            jax.ShapeDtypeStruct((D_MODEL, IN_WIDTH), BF16),
        ],
        compiler_params=pltpu.CompilerParams(
            dimension_semantics=("arbitrary", "arbitrary"), vmem_limit_bytes=VMEM_BIG),
        name="s5_params",
    )(rows, bc, cvec8, w_mod, b_mod3, w_in)


def _s5_mix_kernel(u_ref, w1_ref, w2_ref, a8_ref, h0_ref, y_ref, hfin_ref, lhs_ref, a_ref, hp_ref, y8_ref):
    def stream(nb, nc, is_ctx):
        for b in range(nb):
            for t in range(S5_T):
                lhs_ref[t, pl.ds(b, nc, stride=nb), :] = u_ref[pl.ds(b * nc * S5_T + t, nc, stride=S5_T), :]
        lhs = jnp.concatenate([lhs_ref[t] for t in range(S5_T)], axis=1).astype(BF16)
        a_ref[...] = _dot(lhs, w1_ref[:, S5_ROW:])
        y_intra = _dot(lhs, w1_ref[:, 0:S5_ROW])

        coef = [a8_ref[k:k + 1, :] for k in range(4)]
        if is_ctx:
            init = tuple(jnp.zeros((nb, S5_ST), F32) for _ in range(4))
        else:
            init = tuple(h0_ref[k] for k in range(4))

        def step(c, st):
            new = []
            for d in range(2):
                cc = c if d == 0 else nc - 1 - c
                rows = slice(cc * nb, (cc + 1) * nb)
                hr, hi = st[2 * d], st[2 * d + 1]
                hp_ref[rows, 2 * S5_ST * d:2 * S5_ST * d + S5_ST] = hr
                hp_ref[rows, 2 * S5_ST * d + S5_ST:2 * S5_ST * (d + 1)] = hi
                c0 = 2 * S5_ST * d
                gr = a_ref[rows, c0:c0 + S5_ST]
                gi = a_ref[rows, c0 + S5_ST:c0 + 2 * S5_ST]
                ar, ai = coef[2 * d], coef[2 * d + 1]
                new += [ar * hr - ai * hi + gr, ar * hi + ai * hr + gi]
            return tuple(new)

        fin = init
        for c in range(nc):
            fin = step(c, fin)
        if is_ctx:
            for k in range(4):
                hfin_ref[k] = fin[k]

        y8 = y_intra + _dot_nt(hp_ref[...].astype(BF16), w2_ref[...])
        for t in range(S5_T):
            y8_ref[t] = y8[:, t * 128:(t + 1) * 128]
        for b in range(nb):
            for t in range(S5_T):
                y_ref[pl.ds(b * nc * S5_T + t, nc, stride=S5_T), :] = y8_ref[t, pl.ds(b, nc, stride=nb), :]

    pl.when(pl.program_id(1) == 0)(functools.partial(stream, CTX_B, CTX_NC, True))
    pl.when(pl.program_id(1) == 1)(functools.partial(stream, LAT_B, LAT_NC, False))


def _s5_mix(uf, w1, w2, a8, h0, l):
    return pl.pallas_call(
        _s5_mix_kernel,
        grid=(S5_OCT, 2),
        in_specs=[
            pl.BlockSpec((T_CTX, 128), lambda s, k: (k, s)),
            pl.BlockSpec((None, None, S5_ROW, S5_W1), lambda s, k: (l, s, 0, 0)),
            pl.BlockSpec((None, None, S5_ROW, 4 * S5_ST), lambda s, k: (l, s, 0, 0)),
            pl.BlockSpec((None, None, 8, S5_ST), lambda s, k: (l, s, 0, 0)),
            pl.BlockSpec((None, 4, LAT_B, S5_ST), lambda s, k: (s, 0, 0, 0)),
        ],
        out_specs=[
            pl.BlockSpec((T_CTX, 128), lambda s, k: (k, s)),
            pl.BlockSpec((None, 4, CTX_B, S5_ST), lambda s, k: (s, 0, 0, 0)),
        ],
        out_shape=[
            jax.ShapeDtypeStruct((T_ALL, S5_CH), F32),
            jax.ShapeDtypeStruct((S5_OCT, 4, CTX_B, S5_ST), F32),
        ],
        scratch_shapes=[
            pltpu.VMEM((S5_T, S5_ROWS, 128), F32),
            pltpu.VMEM((S5_ROWS, 4 * S5_ST), F32),
            pltpu.VMEM((S5_ROWS, 4 * S5_ST), F32),
            pltpu.VMEM((S5_T, S5_ROWS, 128), F32),
        ],
        compiler_params=pltpu.CompilerParams(
            dimension_semantics=("arbitrary", "arbitrary"), vmem_limit_bytes=VMEM_BIG),
        name=f"s5_mix_l{l}",
    )(uf, w1, w2, a8, h0)


def _s5_gate(y, u, d, w, b):
    y = y + d * u
    z = y * (0.5 * (1.0 + jnp.tanh(math.sqrt(2.0 / math.pi) * (y + 0.044715 * (y * y * y)))))
    t = _dot(z.astype(BF16), w) + b
    return (z * jax.nn.sigmoid(t)).astype(BF16)


def _outproj_kernel(x_ref, a_ctx_ref, a_lat_ref, c_ctx_ref, c_lat_ref, y_ref, u_ref, d_ref, wglu_ref, bglu_ref,
                    mod_ref, npost_ref, wb_ref, o_ref):
    na = A_HEADS * HD
    nc = na + C_HEADS * HD

    def body(a_ref, c_ref):
        gate = mod_ref[:, 5 * D_MODEL:6 * D_MODEL]
        wglu = wglu_ref[...].astype(BF16)
        for r in range(BM_OUT // RC_OUT):
            rows = slice(r * RC_OUT, (r + 1) * RC_OUT)
            s = _s5_gate(y_ref[rows, :], u_ref[rows, :], d_ref[...], wglu, bglu_ref[...])
            y = (_dot(a_ref[rows, :], wb_ref[0:na, :]) + _dot(c_ref[rows, :], wb_ref[na:nc, :])
                 + _dot(s, wb_ref[nc:, :]))
            o_ref[rows, :] = x_ref[rows, :] + gate * _rms(y, npost_ref[...])

    is_lat = pl.program_id(0) >= T_CTX // BM_OUT
    pl.when(jnp.logical_not(is_lat))(functools.partial(body, a_ctx_ref, c_ctx_ref))
    pl.when(is_lat)(functools.partial(body, a_lat_ref, c_lat_ref))


def _outproj(x, a_ctx, a_lat, c_ctx, c_lat, y_s5, uf, s5_d3, w_glu, b_glu3, mods3, norm_post4, w_out_bf, l):
    bm = BM_OUT
    nct = T_CTX // bm
    ctx_idx = lambda i: (jnp.minimum(i, nct - 1), 0)
    lat_idx = lambda i: (jnp.maximum(i - nct, 0), 0)
    return pl.pallas_call(
        _outproj_kernel,
        grid=(T_ALL // bm,),
        in_specs=[
            pl.BlockSpec((bm, D_MODEL), lambda i: (i, 0)),
            pl.BlockSpec((bm, A_HEADS * HD), ctx_idx),
            pl.BlockSpec((bm, A_HEADS * HD), lat_idx),
            pl.BlockSpec((bm, C_HEADS * HD), ctx_idx),
            pl.BlockSpec((bm, C_HEADS * HD), lat_idx),
            pl.BlockSpec((bm, S5_CH), lambda i: (i, 0)),
            pl.BlockSpec((bm, S5_CH), lambda i: (i, 0)),
            pl.BlockSpec((None, 1, S5_CH), lambda i: (l, 0, 0)),
            pl.BlockSpec((None, S5_CH, S5_CH), lambda i: (l, 0, 0)),
            pl.BlockSpec((None, 1, S5_CH), lambda i: (l, 0, 0)),
            pl.BlockSpec((None, 1, N_MOD * D_MODEL), lambda i: (_mod_index(i, bm), 0, 0)),
            pl.BlockSpec((None, None, 1, D_MODEL), lambda i: (l, 1, 0, 0)),
            pl.BlockSpec((D_MODEL, D_MODEL), lambda i: (0, 0), pipeline_mode=pl.Buffered(1)),
        ],
        out_specs=pl.BlockSpec((bm, D_MODEL), lambda i: (i, 0)),
        out_shape=jax.ShapeDtypeStruct((T_ALL, D_MODEL), F32),
        compiler_params=pltpu.CompilerParams(
            dimension_semantics=("arbitrary",), vmem_limit_bytes=VMEM_BIG),
        name=f"outproj_l{l}",
    )(x, a_ctx, a_lat, c_ctx, c_lat, y_s5, uf, s5_d3, w_glu, b_glu3, mods3, norm_post4, w_out_bf)


def _rope_tables():
    rows = LAT_L // GRID_W
    row = jnp.repeat(jnp.arange(rows, dtype=F32), GRID_W)
    col = jnp.tile(jnp.arange(GRID_W, dtype=F32), rows)
    axis_dim = HD // 2
    inv_freq = ROPE_BASE ** (-jnp.arange(0, axis_dim, 2, dtype=F32) / axis_dim)
    ang_row = row[:, None] * inv_freq
    ang_col = col[:, None] * inv_freq
    cr, sr = jnp.cos(ang_row), jnp.sin(ang_row)
    cc, sc = jnp.cos(ang_col), jnp.sin(ang_col)
    cos_t = jnp.concatenate([cr, cr, cc, cc], axis=-1)
    sin_t = jnp.concatenate([-sr, sr, -sc, sc], axis=-1)
    return cos_t, sin_t


def kernel(x_prompt, x_sample, cache_a_k, cache_a_v, cache_c_k, cache_c_v, state_ssm_re, state_ssm_im,
           c, c_ctx, w_mod, b_mod, norm_pre, norm_post, ffn_gate, ffn_up, ffn_down, w_in, w_out,
           q_norm, k_norm, sink, s5_lam_re, s5_lam_im, s5_log_step, s5_b_re, s5_b_im, s5_c_re, s5_c_im,
           s5_d, w_glu, b_glu):
    cvec8 = jnp.concatenate([c_ctx[None, :], c, jnp.zeros((8 - 1 - LAT_B, D_MODEL), F32)], axis=0)
    b_mod3 = b_mod.reshape(DEPTH, 1, N_MOD * D_MODEL)
    norm_pre4 = norm_pre.reshape(DEPTH, 3, 1, D_MODEL)
    norm_post4 = norm_post.reshape(DEPTH, 3, 1, D_MODEL)
    q_norm3 = q_norm.reshape(DEPTH, 1, HD)
    k_norm3 = k_norm.reshape(DEPTH, 1, HD)
    s5_d3 = s5_d.reshape(DEPTH, 1, S5_CH)
    b_glu3 = b_glu.reshape(DEPTH, 1, S5_CH)
    cos_t, sin_t = _rope_tables()
    mods_head = _modulation(cvec8, w_mod, b_mod3, 0, MOD_HEAD)
    w1, w2, a8, mods_mid, w_in_bf = _s5_params(s5_lam_re, s5_lam_im, s5_log_step, s5_b_re, s5_b_im,
                                               s5_c_re, s5_c_im, cvec8, w_mod, b_mod3, w_in)
    mods_tail = jnp.zeros((8, N_MOD * D_MODEL - MOD_MID), F32)
    table = lambda tail: jnp.concatenate([mods_head, mods_mid, tail], axis=1).reshape(8, 1, N_MOD * D_MODEL)
    mods3 = table(mods_tail)
    kv4 = lambda a: a.reshape(LAT_B, DEPTH, PAST, A_KV * HD)
    cak, cav, cck, ccv = kv4(cache_a_k), kv4(cache_a_v), kv4(cache_c_k), kv4(cache_c_v)
    h0_all = jnp.stack([state_ssm_re[:, :, 0], state_ssm_im[:, :, 0],
                        state_ssm_re[:, :, 1], state_ssm_im[:, :, 1]], axis=0)
    h0_all = h0_all.reshape(4, LAT_B, DEPTH, S5_OCT, S5_ST).transpose(2, 3, 0, 1, 4)

    ffn_w = (norm_pre4, norm_post4, ffn_gate, ffn_up, ffn_down)
    new_caches = ()
    new_state = []
    xs = [x_prompt.reshape(T_CTX, D_MODEL), x_sample.reshape(T_LAT, D_MODEL)]
    for l in range(DEPTH):
        x = _ffn(xs, mods3, *ffn_w, l, 0)

        if l == 0:
            qkvu, uf, *new_caches, mods_tail = _inproj(
                x, mods3, norm_pre4, q_norm3, k_norm3, cos_t, sin_t, w_in_bf, l, (),
                mod_args=(cvec8, w_mod, b_mod3), mod_cols=(MOD_MID, N_MOD * D_MODEL))
            mods3 = table(mods_tail)
        else:
            qkvu, uf, *new_caches = _inproj(x, mods3, norm_pre4, q_norm3, k_norm3, cos_t, sin_t, w_in_bf, l,
                                            new_caches)
        a_ctx, c_ctx = _attn_ctx(qkvu, sink[l], l)
        if l < DEPTH - 1:
            a_lat, mods_next, w_in_next = _attn_lat_a(qkvu, cak, cav, l, next_mod=(cvec8, w_mod, b_mod3, w_in))
        else:
            a_lat = _attn_lat_a(qkvu, cak, cav, l)
        c_lat, w_out_bf = _attn_lat_c(qkvu, sink[l], cck, ccv, w_out, l)
        y, hfin = _s5_mix(uf, w1, w2, a8, h0_all[l], l)
        x = _outproj(x, a_ctx, a_lat, c_ctx, c_lat, y, uf, s5_d3, w_glu, b_glu3, mods3, norm_post4, w_out_bf, l)
        if l < DEPTH - 1:
            xs = [_ffn([x], mods3, *ffn_w, l, 1)]
            mods3 = mods_next.reshape(8, 1, N_MOD * D_MODEL)
            w_in_bf = w_in_next
        else:
            y_prompt, y_sample = _ffn([x], mods3, *ffn_w, l, 1, split_out=True)

        hf = hfin.reshape(S5_OCT, 4, CTX_B, S5_OG, S5_P).transpose(1, 2, 0, 3, 4).reshape(4, CTX_B, S5_G, S5_P)
        new_state.append((jnp.stack([hf[0], hf[2]], axis=1), jnp.stack([hf[1], hf[3]], axis=1)))

    y_prompt = y_prompt.reshape(CTX_B, CTX_L, D_MODEL)
    y_sample = y_sample.reshape(LAT_B, LAT_L, D_MODEL)
    caches = list(new_caches)
    st_re = jnp.stack([new_state[l][0] for l in range(DEPTH)], axis=1)
    st_im = jnp.stack([new_state[l][1] for l in range(DEPTH)], axis=1)
    return (y_prompt, y_sample, caches[0], caches[1], caches[2], caches[3], st_re, st_im)
```

```python
import functools
import math

import jax
import jax.numpy as jnp
from jax import lax
from jax.experimental import pallas as pl
from jax.experimental.pallas import tpu as pltpu

F32 = jnp.float32
BF16 = jnp.bfloat16

D_MODEL = 2048
CTX_B, CTX_L = 16, 256
LAT_B, LAT_L = 2, 2048
DEPTH = 2
PAST = 512
GRID_W = 64
HD = 128
A_HEADS, A_KV = 8, 2
C_HEADS, C_KV = 4, 2
WINDOW = 128
S5_GC = 16
S5_CH = 512
S5_G = 32
S5_P = 64
D_FF = 5632
N_MOD = 9
IN_WIDTH = 3072
ROPE_BASE = 10000.0
EPS = 1e-6
HALF_STEP = 0.5
NEG_INF = -1e30
SCALE = HD ** -0.5
LOG2E = math.log2(math.e)

T_CTX = CTX_B * CTX_L
T_LAT = LAT_B * LAT_L
T_ALL = T_CTX + T_LAT

COL_AQ, COL_AK, COL_AV = 0, 1024, 1280
COL_CQ, COL_CK, COL_CV = 1536, 2048, 2304
COL_U = 2560

S5_T = 8
S5_OCT = S5_CH // 128
S5_OG = S5_G // S5_OCT
S5_ROW = S5_T * 128
S5_ST = S5_OG * S5_P
S5_W1 = S5_ROW + 4 * S5_ST
CTX_NC = CTX_L // S5_T
LAT_NC = LAT_L // S5_T
S5_ROWS = CTX_NC * CTX_B
assert S5_ROWS == LAT_NC * LAT_B and T_CTX == T_LAT

V7X_VMEM_BYTES = 64 * 1024 * 1024
MIB = 1024 * 1024
VMEM_FFN = V7X_VMEM_BYTES - 2 * MIB
VMEM_BIG = V7X_VMEM_BYTES - 8 * MIB
VMEM_MID = V7X_VMEM_BYTES - 24 * MIB

BM = 1024
BM_OUT = 512
RC_OUT = 256
BF = 512
RC = 512
RCX = 256
N_XC = BM // RCX
FFN_PREFETCH_STEP = 2
BN_MOD = 1024
MOD_HEAD = 3 * D_MODEL
MOD_MID = 5 * D_MODEL
BM_IN = 512
BQ_A = 512
BQ_C = 256
NQ_C = 8
NB_CTX = 8


def _dot(a, b):
    return jnp.dot(a, b, preferred_element_type=F32)


def _dot_nt(a, b, precision=None):
    return lax.dot_general(a, b, (((1,), (1,)), ((), ())), preferred_element_type=F32, precision=precision)


def _dot_nt_split(a, b):
    ah = a.astype(BF16)
    bh = b.astype(BF16)
    al = (a - ah.astype(F32)).astype(BF16)
    bl = (b - bh.astype(F32)).astype(BF16)
    return _dot_nt(ah, bh) + (_dot_nt(ah, bl) + _dot_nt(al, bh))


def _rms(x, g):
    return x * lax.rsqrt(jnp.mean(x * x, axis=-1, keepdims=True) + EPS) * g


def _mod_index(i, bm):
    nct = T_CTX // bm
    return jnp.where(i < nct, 0, 1 + (i - nct) // (LAT_L // bm))


def _mod_columns(c_ref, w_ref, b_ref, o_ref):
    c = c_ref[...]
    s = (c * jax.nn.sigmoid(c)).astype(BF16)
    o_ref[...] = _dot(s, w_ref[...].astype(BF16)) + b_ref[...]


def _modulation(cvec8, w_mod, b_mod3, l, n):
    return pl.pallas_call(
        _mod_columns,
        grid=(n // BN_MOD,),
        in_specs=[
            pl.BlockSpec((8, D_MODEL), lambda j: (0, 0)),
            pl.BlockSpec((None, D_MODEL, BN_MOD), lambda j: (l, 0, j)),
            pl.BlockSpec((None, 1, BN_MOD), lambda j: (l, 0, j)),
        ],
        out_specs=pl.BlockSpec((8, BN_MOD), lambda j: (0, j)),
        out_shape=jax.ShapeDtypeStruct((8, n), F32),
        compiler_params=pltpu.CompilerParams(
            dimension_semantics=("arbitrary",), vmem_limit_bytes=VMEM_MID),
        name=f"modulation_l{l}",
    )(cvec8, w_mod, b_mod3)


def _ffn_kernel(*refs, mo, n_x, n_out):
    x_hbms = refs[:n_x]
    mod_ref, npre_ref, npost_ref, wg_ref, wu_ref, wd_ref = refs[n_x:n_x + 6]
    out_hbms = refs[n_x + 6:n_x + 6 + n_out]
    acc_ref, xc_ref, h_ref, sem_x, sem_c, sem_o = refs[n_x + 6 + n_out:]
    i = pl.program_id(0)
    j = pl.program_id(1)
    n_tiles = pl.num_programs(0)
    last = pl.num_programs(1) - 1
    slot = i % 2
    nq = RC // RCX
    nct = T_CTX // BM

    def per_stream(arrays, tile, fn):
        if len(arrays) == 1:
            fn(arrays[0], pl.multiple_of(tile * BM, BM))
        else:
            pl.when(tile < nct)(lambda: fn(arrays[0], pl.multiple_of(tile * BM, BM)))
            pl.when(tile >= nct)(lambda: fn(arrays[1], pl.multiple_of((tile - nct) * BM, BM)))

    def x_tile_copy(arr, row0, sl):
        return pltpu.make_async_copy(arr.at[pl.ds(row0, BM), :], acc_ref.at[sl], sem_x)

    def x_chunk_copy(arr, row0, cs):
        return pltpu.make_async_copy(arr.at[pl.ds(row0, RCX), :], xc_ref.at[cs], sem_c.at[cs])

    def out_copy(arr, row0, sl):
        return pltpu.make_async_copy(acc_ref.at[sl], arr.at[pl.ds(row0, BM), :], sem_o.at[sl])

    wait_x_tile = lambda sl: x_tile_copy(x_hbms[0], 0, sl).wait()
    wait_x_chunk = lambda cs: x_chunk_copy(x_hbms[0], 0, cs).wait()
    wait_out = lambda sl: out_copy(out_hbms[0], 0, sl).wait()

    def start_x_chunk(q):
        per_stream(x_hbms, i, lambda arr, row0: x_chunk_copy(
            arr, pl.multiple_of(row0 + q * RCX, RCX), q).start())

    @pl.when(jnp.logical_and(i == 0, j == 0))
    def _():
        x_tile_copy(x_hbms[0], 0, 0).start()

    @pl.when(j == 0)
    def _():
        wait_x_tile(slot)

    @pl.when(j == FFN_PREFETCH_STEP)
    def _():
        pl.when(i >= 1)(lambda: wait_out(1 - slot))

        @pl.when(i + 1 < n_tiles)
        def _():
            per_stream(x_hbms, i + 1, lambda arr, row0: x_tile_copy(arr, row0, 1 - slot).start())

    @pl.when(j == last - 1)
    def _():
        for q in range(N_XC):
            start_x_chunk(q)

    def step(first, final):
        if final:
            for q in range(N_XC):
                wait_x_chunk(q)
        acc_slot = acc_ref.at[slot]
        wg = wg_ref[...].astype(BF16)
        wu = wu_ref[...].astype(BF16)
        wd = wd_ref[...].astype(BF16)
        for r in range(BM // RC):
            rows = slice(r * RC, (r + 1) * RC)
            if first:
                shift = mod_ref[:, mo * D_MODEL:(mo + 1) * D_MODEL]
                scale = mod_ref[:, (mo + 1) * D_MODEL:(mo + 2) * D_MODEL]
                hn = _rms(acc_slot[rows, :], npre_ref[...])
                h = (hn * (1.0 + scale) + shift).astype(BF16)
                h_ref[rows, :] = h
            else:
                h = h_ref[rows, :]
            g = _dot(h, wg)
            u = _dot(h, wu)
            a = (g * jax.nn.sigmoid(g) * u).astype(BF16)
            acc = _dot(a, wd)
            if not first:
                acc = acc_slot[rows, :] + acc
            if not final:
                acc_slot[rows, :] = acc
                continue
            gate = mod_ref[:, (mo + 2) * D_MODEL:(mo + 3) * D_MODEL]
            for qq in range(nq):
                sub = slice(qq * RCX, (qq + 1) * RCX)
                y = xc_ref[r * nq + qq] + (HALF_STEP * gate) * _rms(acc[sub, :], npost_ref[...])
                acc_slot[r * RC + qq * RCX:r * RC + (qq + 1) * RCX, :] = y
        if final:
            per_stream(out_hbms, i, lambda arr, row0: out_copy(arr, row0, slot).start())

    pl.when(j == 0)(functools.partial(step, True, False))
    pl.when(jnp.logical_and(j > 0, j < last))(functools.partial(step, False, False))
    pl.when(j == last)(functools.partial(step, False, True))

    @pl.when(jnp.logical_and(i == n_tiles - 1, j == last))
    def _():
        wait_out(slot)


def _ffn(xs, mods3, norm_pre4, norm_post4, ffn_gate, ffn_up, ffn_down, l, s, *, split_out=False):
    mo = 6 * s
    ni = 2 * s
    in_specs = [pl.BlockSpec(memory_space=pl.ANY)] * len(xs) + [
        pl.BlockSpec((None, 1, N_MOD * D_MODEL), lambda i, j: (_mod_index(i, BM), 0, 0)),
        pl.BlockSpec((None, None, 1, D_MODEL), lambda i, j: (l, ni, 0, 0)),
        pl.BlockSpec((None, None, 1, D_MODEL), lambda i, j: (l, ni, 0, 0)),
        pl.BlockSpec((None, None, D_MODEL, BF), lambda i, j: (l, s, 0, j)),
        pl.BlockSpec((None, None, D_MODEL, BF), lambda i, j: (l, s, 0, j)),
        pl.BlockSpec((None, None, BF, D_MODEL), lambda i, j: (l, s, j, 0)),
    ]
    out_rows = (T_CTX, T_LAT) if split_out else (T_ALL,)
    outs = pl.pallas_call(
        functools.partial(_ffn_kernel, mo=mo, n_x=len(xs), n_out=len(out_rows)),
        grid=(T_ALL // BM, D_FF // BF),
        in_specs=in_specs,
        out_specs=[pl.BlockSpec(memory_space=pl.ANY)] * len(out_rows),
        out_shape=[jax.ShapeDtypeStruct((r, D_MODEL), F32) for r in out_rows],
        scratch_shapes=[
            pltpu.VMEM((2, BM, D_MODEL), F32),
            pltpu.VMEM((N_XC, RCX, D_MODEL), F32),
            pltpu.VMEM((BM, D_MODEL), BF16),
            pltpu.SemaphoreType.DMA(()),
            pltpu.SemaphoreType.DMA((N_XC,)),
            pltpu.SemaphoreType.DMA((2,)),
        ],
        compiler_params=pltpu.CompilerParams(
            dimension_semantics=("arbitrary", "arbitrary"), vmem_limit_bytes=VMEM_FFN),
        name=f"ffn_l{l}_s{s}",
    )(*xs, mods3, norm_pre4, norm_post4, ffn_gate, ffn_up, ffn_down)
    return outs if split_out else outs[0]


def _rope(y, cos, sins):
    lane = lax.broadcasted_iota(jnp.int32, y.shape, 1)
    first = (lane & 63) < 32
    partner = jnp.where(first, pltpu.roll(y, 96, 1), pltpu.roll(y, 32, 1))
    return y * cos + partner * sins


_IN_SEGMENTS = (
    (COL_AQ, A_HEADS, "q", True, None),
    (COL_AK, A_KV, "k", True, 0),
    (COL_AV, A_KV, None, False, 1),
    (COL_CQ, C_HEADS, None, True, None),
    (COL_CK, C_KV, None, True, 2),
    (COL_CV, C_KV, None, False, 3),
)


def _inproj_kernel(x_ref, mod_ref, npre_ref, qn_ref, kn_ref, cos_ref, sin_ref, w_ref, *rest, first_layer):
    nb = BM_IN // CTX_L
    if first_layer:
        c_ref, wmod_ref, bmod_ref, qkvu_ref, uf_ref, *cache_full, modo_ref = rest
        cache_refs = [c.at[:, 0] for c in cache_full]
        _mod_columns(c_ref, wmod_ref, bmod_ref, modo_ref)
    else:
        qkvu_ref, uf_ref, *cache_refs = rest[4:]
        cache_full = ()

    def body(lat):
        shift = mod_ref[:, 3 * D_MODEL:4 * D_MODEL]
        scale = mod_ref[:, 4 * D_MODEL:5 * D_MODEL]
        h = (_rms(x_ref[...], npre_ref[...]) * (1.0 + scale) + shift).astype(BF16)
        for col0, heads, norm, rot, cache in _IN_SEGMENTS:
            p = _dot(h, w_ref[:, col0:col0 + heads * HD])
            for k in range(heads):
                y = p[:, k * HD:(k + 1) * HD]
                if norm == "q":
                    y = _rms(y, qn_ref[...])
                elif norm == "k":
                    y = _rms(y, kn_ref[...])
                if rot and lat:
                    y = _rope(y, cos_ref[...], sin_ref[...])
                qkvu_ref[:, col0 + k * HD:col0 + (k + 1) * HD] = y.astype(BF16)
                if cache is not None and not lat:
                    cache_refs[cache][:, :, k, :] = y.reshape(nb, CTX_L, HD)
        u = _dot(h, w_ref[:, COL_U:])
        uf_ref[...] = u
        qkvu_ref[:, COL_U:] = u.astype(BF16)
        if first_layer and not lat:
            for c in cache_full:
                c[:, 1:] = jnp.zeros((nb, DEPTH - 1, CTX_L, A_KV, HD), F32)

    is_lat = pl.program_id(0) >= T_CTX // BM_IN
    pl.when(is_lat)(functools.partial(body, True))
    pl.when(jnp.logical_not(is_lat))(functools.partial(body, False))


def _inproj(x, mods3, norm_pre4, q_norm3, k_norm3, cos_t, sin_t, w_in_bf, l, prev_caches, mod_args=None,
            mod_cols=None):
    bm = BM_IN
    nct = T_CTX // bm
    nb = bm // CTX_L
    tab_idx = lambda i: (jnp.maximum(i - nct, 0) % (LAT_L // bm), 0)
    cache_shape = jax.ShapeDtypeStruct((CTX_B, DEPTH, CTX_L, A_KV, HD), F32)
    first_layer = not prev_caches
    if first_layer:
        cache_spec = pl.BlockSpec((nb, DEPTH, CTX_L, A_KV, HD), lambda i: (jnp.minimum(i, nct - 1), 0, 0, 0, 0))
    else:
        cache_spec = pl.BlockSpec((nb, None, CTX_L, A_KV, HD), lambda i: (jnp.minimum(i, nct - 1), l, 0, 0, 0))
    n_in = 8
    extra_in, extra_out, extra_shape, extra_args = [], [], [], []
    if first_layer:
        lo, hi = mod_cols
        steps = T_ALL // bm
        bn = (hi - lo) // steps
        extra_in = [
            pl.BlockSpec((8, D_MODEL), lambda i: (0, 0)),
            pl.BlockSpec((None, D_MODEL, bn), lambda i: (l, 0, lo // bn + i)),
            pl.BlockSpec((None, 1, bn), lambda i: (l, 0, lo // bn + i)),
        ]
        extra_out = [pl.BlockSpec((8, bn), lambda i: (0, i))]
        extra_shape = [jax.ShapeDtypeStruct((8, hi - lo), F32)]
        extra_args = list(mod_args)
    return pl.pallas_call(
        functools.partial(_inproj_kernel, first_layer=first_layer),
        grid=(T_ALL // bm,),
        in_specs=[
            pl.BlockSpec((bm, D_MODEL), lambda i: (i, 0)),
            pl.BlockSpec((None, 1, N_MOD * D_MODEL), lambda i: (_mod_index(i, bm), 0, 0)),
            pl.BlockSpec((None, None, 1, D_MODEL), lambda i: (l, 1, 0, 0)),
            pl.BlockSpec((None, 1, HD), lambda i: (l, 0, 0)),
            pl.BlockSpec((None, 1, HD), lambda i: (l, 0, 0)),
            pl.BlockSpec((bm, HD), tab_idx),
            pl.BlockSpec((bm, HD), tab_idx),
            pl.BlockSpec((D_MODEL, IN_WIDTH), lambda i: (0, 0), pipeline_mode=pl.Buffered(1)),
        ] + extra_in + [pl.BlockSpec(memory_space=pl.ANY)] * len(prev_caches),
        out_specs=[
            pl.BlockSpec((bm, IN_WIDTH), lambda i: (i, 0)),
            pl.BlockSpec((bm, S5_CH), lambda i: (i, 0)),
        ] + [cache_spec] * 4 + extra_out,
        out_shape=[
            jax.ShapeDtypeStruct((T_ALL, IN_WIDTH), BF16),
            jax.ShapeDtypeStruct((T_ALL, S5_CH), F32),
        ] + [cache_shape] * 4 + extra_shape,
        input_output_aliases={n_in + k: 2 + k for k in range(len(prev_caches))},
        compiler_params=pltpu.CompilerParams(
            dimension_semantics=("arbitrary",), vmem_limit_bytes=VMEM_BIG),
        name=f"inproj_l{l}",
    )(x, mods3, norm_pre4, q_norm3, k_norm3, cos_t, sin_t, w_in_bf, *extra_args, *prev_caches)


def _softmax_pv(dots, values, sink=None):
    m = functools.reduce(jnp.maximum, [jnp.max(d, axis=-1, keepdims=True) for d in dots]) * SCALE
    if sink is not None:
        m = jnp.maximum(m, sink)
    m2 = m * LOG2E
    ps = [jnp.exp2(d * (SCALE * LOG2E) - m2) for d in dots]
    den = functools.reduce(jnp.add, [jnp.sum(p, axis=-1, keepdims=True) for p in ps])
    if sink is not None:
        den = den + jnp.exp2(sink * LOG2E - m2)
    o = functools.reduce(jnp.add, [_dot(p.astype(BF16), v) for p, v in zip(ps, values)])
    return o / den


def _attn_ctx_kernel(sink_ref, aq_ref, ak_ref, av_ref, cq_ref, ck_ref, cv_ref, ao_ref, co_ref):
    kv = pl.program_id(1)
    g = C_HEADS // C_KV
    for b in range(NB_CTX):
        rows = slice(b * CTX_L, (b + 1) * CTX_L)
        k = ak_ref[rows, :]
        v = av_ref[rows, :]
        for h in range(A_HEADS // A_KV):
            cols = slice(h * HD, (h + 1) * HD)
            s = _dot_nt(aq_ref[rows, cols], k)
            ao_ref[rows, cols] = _softmax_pv([s], [v]).astype(BF16)
        k = ck_ref[rows, :]
        v = cv_ref[rows, :]
        for h in range(g):
            cols = slice(h * HD, (h + 1) * HD)
            s = _dot_nt(cq_ref[rows, cols], k)
            co_ref[rows, cols] = _softmax_pv([s], [v], sink_ref[kv * g + h]).astype(BF16)


def _attn_ctx(qkvu, sink_l, l):
    ga = A_HEADS // A_KV * HD
    gc = C_HEADS // C_KV * HD
    rows = NB_CTX * CTX_L
    blk = lambda width, col0: pl.BlockSpec((rows, width), lambda b, k: (b, col0 // width + k))
    return pl.pallas_call(
        _attn_ctx_kernel,
        grid=(CTX_B // NB_CTX, A_KV),
        in_specs=[
            pl.BlockSpec(memory_space=pltpu.SMEM),
            blk(ga, COL_AQ), blk(HD, COL_AK), blk(HD, COL_AV),
            blk(gc, COL_CQ), blk(HD, COL_CK), blk(HD, COL_CV),
        ],
        out_specs=[
            pl.BlockSpec((rows, ga), lambda b, k: (b, k)),
            pl.BlockSpec((rows, gc), lambda b, k: (b, k)),
        ],
        out_shape=[
            jax.ShapeDtypeStruct((T_CTX, A_HEADS * HD), BF16),
            jax.ShapeDtypeStruct((T_CTX, C_HEADS * HD), BF16),
        ],
        compiler_params=pltpu.CompilerParams(dimension_semantics=("arbitrary", "arbitrary")),
        name=f"attn_ctx_l{l}",
    )(sink_l, qkvu, qkvu, qkvu, qkvu, qkvu, qkvu)


def _attn_lat_a_kernel(q_ref, k_ref, v_ref, kc_ref, vc_ref, *rest):
    o_ref = rest[0] if len(rest) == 1 else rest[4]
    k = k_ref[...]
    v = v_ref[...]
    kc = kc_ref[...].astype(BF16)
    vc = vc_ref[...].astype(BF16)
    for h in range(A_HEADS // A_KV):
        cols = slice(h * HD, (h + 1) * HD)
        q = q_ref[:, cols]
        s1 = _dot_nt(q, k)
        s2 = _dot_nt(q, kc)
        o_ref[:, cols] = _softmax_pv([s1, s2], [v, vc]).astype(BF16)
    if len(rest) > 1:
        c_ref, w_ref, b_ref, win_ref, _, mod_ref, winb_ref = rest
        _mod_columns(c_ref, w_ref, b_ref, mod_ref)
        winb_ref[...] = win_ref[...].astype(BF16)


def _attn_lat_a(qkvu, cache_k4, cache_v4, l, next_mod=None):
    ga = A_HEADS // A_KV * HD
    nq = LAT_L // BQ_A
    row0 = T_CTX // BQ_A
    lat_blk = T_CTX // LAT_L
    in_specs = [
        pl.BlockSpec((BQ_A, ga), lambda b, k, q: (row0 + b * nq + q, COL_AQ // ga + k)),
        pl.BlockSpec((LAT_L, HD), lambda b, k, q: (lat_blk + b, COL_AK // HD + k)),
        pl.BlockSpec((LAT_L, HD), lambda b, k, q: (lat_blk + b, COL_AV // HD + k)),
        pl.BlockSpec((None, None, PAST, HD), lambda b, k, q: (b, l, 0, k)),
        pl.BlockSpec((None, None, PAST, HD), lambda b, k, q: (b, l, 0, k)),
    ]
    out_specs = [pl.BlockSpec((BQ_A, ga), lambda b, k, q: (b * nq + q, k))]
    out_shape = [jax.ShapeDtypeStruct((T_LAT, A_HEADS * HD), BF16)]
    args = [qkvu, qkvu, qkvu, cache_k4, cache_v4]
    if next_mod is not None:
        n = N_MOD * D_MODEL
        steps = LAT_B * A_KV * nq
        bn = n // steps
        step = lambda b, k, q: (b * A_KV + k) * nq + q
        in_specs += [
            pl.BlockSpec((8, D_MODEL), lambda b, k, q: (0, 0)),
            pl.BlockSpec((None, D_MODEL, bn), lambda b, k, q: (l + 1, 0, step(b, k, q))),
            pl.BlockSpec((None, 1, bn), lambda b, k, q: (l + 1, 0, step(b, k, q))),
            pl.BlockSpec((None, D_MODEL // steps, IN_WIDTH), lambda b, k, q: (l + 1, step(b, k, q), 0)),
        ]
        out_specs += [
            pl.BlockSpec((8, bn), lambda b, k, q: (0, step(b, k, q))),
            pl.BlockSpec((D_MODEL // steps, IN_WIDTH), lambda b, k, q: (step(b, k, q), 0)),
        ]
        out_shape += [
            jax.ShapeDtypeStruct((8, n), F32),
            jax.ShapeDtypeStruct((D_MODEL, IN_WIDTH), BF16),
        ]
        args += list(next_mod)
    outs = pl.pallas_call(
        _attn_lat_a_kernel,
        grid=(LAT_B, A_KV, nq),
        in_specs=in_specs,
        out_specs=out_specs,
        out_shape=out_shape,
        compiler_params=pltpu.CompilerParams(
            dimension_semantics=("arbitrary", "arbitrary", "arbitrary"), vmem_limit_bytes=VMEM_BIG),
        name=f"attn_lat_a_l{l}",
    )(*args)
    return outs if next_mod is not None else outs[0]


def _attn_lat_c_kernel(sink_ref, q_ref, k_ref, v_ref, kc_ref, vc_ref, wout_ref, o_ref, woutb_ref):
    woutb_ref[...] = wout_ref[...].astype(BF16)
    kv = pl.program_id(1)
    span = BQ_C + 2 * WINDOW
    kc = kc_ref[...].astype(BF16)
    vc = vc_ref[...].astype(BF16)
    g = C_HEADS // C_KV
    for sub in range(NQ_C):
        n = pl.program_id(2) * NQ_C + sub
        rows = slice(sub * BQ_C, (sub + 1) * BQ_C)
        start = pl.multiple_of(jnp.clip(n * BQ_C - WINDOW, 0, LAT_L - span), WINDOW)
        kw = k_ref[pl.ds(start, span), :]
        vw = v_ref[pl.ds(start, span), :]
        qpos = n * BQ_C + lax.broadcasted_iota(jnp.int32, (BQ_C, span), 0)
        kpos = start + lax.broadcasted_iota(jnp.int32, (BQ_C, span), 1)
        valid = jnp.abs(qpos - kpos) <= WINDOW
        for h in range(g):
            cols = slice(h * HD, (h + 1) * HD)
            q = q_ref[rows, cols]
            s1 = jnp.where(valid, _dot_nt(q, kw), NEG_INF)
            s2 = _dot_nt(q, kc)
            o_ref[rows, cols] = _softmax_pv([s1, s2], [vw, vc], sink_ref[kv * g + h]).astype(BF16)


def _attn_lat_c(qkvu, sink_l, cache_k4, cache_v4, w_out, l):
    gc = C_HEADS // C_KV * HD
    bq = BQ_C * NQ_C
    nq = LAT_L // bq
    row0 = T_CTX // bq
    lat_blk = T_CTX // LAT_L
    steps = LAT_B * C_KV * nq
    step = lambda b, k, q: (b * C_KV + k) * nq + q
    return pl.pallas_call(
        _attn_lat_c_kernel,
        grid=(LAT_B, C_KV, nq),
        in_specs=[
            pl.BlockSpec(memory_space=pltpu.SMEM),
            pl.BlockSpec((bq, gc), lambda b, k, q: (row0 + b * nq + q, COL_CQ // gc + k)),
            pl.BlockSpec((LAT_L, HD), lambda b, k, q: (lat_blk + b, COL_CK // HD + k)),
            pl.BlockSpec((LAT_L, HD), lambda b, k, q: (lat_blk + b, COL_CV // HD + k)),
            pl.BlockSpec((None, None, PAST, HD), lambda b, k, q: (b, l, 0, k)),
            pl.BlockSpec((None, None, PAST, HD), lambda b, k, q: (b, l, 0, k)),
            pl.BlockSpec((None, D_MODEL // steps, D_MODEL), lambda b, k, q: (l, step(b, k, q), 0)),
        ],
        out_specs=[
            pl.BlockSpec((bq, gc), lambda b, k, q: (b * nq + q, k)),
            pl.BlockSpec((D_MODEL // steps, D_MODEL), lambda b, k, q: (step(b, k, q), 0)),
        ],
        out_shape=[
            jax.ShapeDtypeStruct((T_LAT, C_HEADS * HD), BF16),
            jax.ShapeDtypeStruct((D_MODEL, D_MODEL), BF16),
        ],
        compiler_params=pltpu.CompilerParams(
            dimension_semantics=("arbitrary", "arbitrary", "arbitrary")),
        name=f"attn_lat_c_l{l}",
    )(sink_l, qkvu, qkvu, qkvu, cache_k4, cache_v4, w_out)


def _s5_param_kernel(rows_ref, bc_ref, c_ref, wmod_ref, bmod_ref, win_ref, w1_ref, w2_ref, a8_ref, mod_ref,
                     winb_ref):
    _mod_columns(c_ref, wmod_ref, bmod_ref, mod_ref)
    winb_ref[...] = win_ref[...].astype(BF16)
    a8_ref[...] = jnp.zeros_like(a8_ref)
    row_grp = lax.broadcasted_iota(jnp.int32, (128, S5_ST), 0) // S5_GC
    lane_grp = lax.broadcasted_iota(jnp.int32, (128, S5_ST), 1) // S5_P
    on_diag = row_grp == lane_grp

    def expand(a):
        return jnp.where(on_diag, jnp.concatenate([a] * S5_OG, axis=1), 0.0)

    taps = []
    for d in range(2):
        lr = rows_ref[0, d]
        li = rows_ref[1, d]
        dt = jnp.exp(rows_ref[2, d])
        mag = jnp.exp(lr * dt)
        ar = mag * jnp.cos(li * dt)
        ai = mag * jnp.sin(li * dt)
        den = lr * lr + li * li
        n_re = ar - 1.0
        f_re = (n_re * lr + ai * li) / den
        f_im = (ai * lr - n_re * li) / den
        pw = [(jnp.ones_like(ar), jnp.zeros_like(ar))]
        for _ in range(S5_T):
            pr, pi = pw[-1]
            pw.append((pr * ar - pi * ai, pr * ai + pi * ar))
        br, bi, cr, ci = (expand(bc_ref[k, d]) for k in range(4))
        bbr = f_re * br - f_im * bi
        bbi = f_re * bi + f_im * br
        xr_blocks, xi_blocks = [], []
        for t in range(S5_T):
            rows = slice(t * 128, (t + 1) * 128)
            pr, pi = pw[S5_T - 1 - t if d == 0 else t]
            xr = pr * bbr - pi * bbi
            xi = pr * bbi + pi * bbr
            c0 = S5_ROW + 2 * S5_ST * d
            w1_ref[rows, c0:c0 + S5_ST] = xr.astype(BF16)
            w1_ref[rows, c0 + S5_ST:c0 + 2 * S5_ST] = xi.astype(BF16)
            xr_blocks.append(xr)
            xi_blocks.append(xi)
            pr, pi = pw[t + 1 if d == 0 else S5_T - t]
            c0 = 2 * S5_ST * d
            w2_ref[rows, c0:c0 + S5_ST] = (cr * pr - ci * pi).astype(BF16)
            w2_ref[rows, c0 + S5_ST:c0 + 2 * S5_ST] = (-(cr * pi + ci * pr)).astype(BF16)
        a8_ref[2 * d:2 * d + 1, :] = pw[S5_T][0]
        a8_ref[2 * d + 1:2 * d + 2, :] = pw[S5_T][1]
        xr_all = jnp.concatenate(xr_blocks, axis=0)
        xi_all = jnp.concatenate(xi_blocks, axis=0)
        taps.append(_dot_nt_split(xr_all, cr) - _dot_nt_split(xi_all, ci))
    tf, tb = taps
    for t in range(S5_T):
        for t2 in range(S5_T):
            blk = None
            if t2 >= t:
                r0 = (S5_T - 1 - (t2 - t)) * 128
                blk = tf[r0:r0 + 128, :]
            if t2 <= t:
                r0 = (t - t2) * 128
                b2 = tb[r0:r0 + 128, :]
                blk = b2 if blk is None else blk + b2
            w1_ref[t * 128:(t + 1) * 128, t2 * 128:(t2 + 1) * 128] = blk.astype(BF16)


def _s5_params(s5_lam_re, s5_lam_im, s5_log_step, s5_b_re, s5_b_im, s5_c_re, s5_c_im, cvec8, w_mod, b_mod3, w_in):
    rows = jnp.stack([s5_lam_re, s5_lam_im, jnp.repeat(s5_log_step[..., None], S5_P, axis=-1)], axis=0)
    rows = rows.reshape(3, DEPTH, 2, S5_OCT, 1, S5_ST)
    bc = jnp.stack([jnp.swapaxes(s5_b_re, -1, -2), jnp.swapaxes(s5_b_im, -1, -2), s5_c_re, s5_c_im], axis=0)
    bc = bc.reshape(4, DEPTH, 2, S5_OCT, 128, S5_P)
    spec = lambda n, r, c: pl.BlockSpec((n, None, 2, None, r, c), lambda l, s: (0, l, 0, s, 0, 0))
    steps = DEPTH * S5_OCT
    step = lambda l, s: l * S5_OCT + s
    n_tail = MOD_MID - MOD_HEAD
    bn = n_tail // steps
    br = D_MODEL // steps
    return pl.pallas_call(
        _s5_param_kernel,
        grid=(DEPTH, S5_OCT),
        in_specs=[
            spec(3, 1, S5_ST), spec(4, 128, S5_P),
            pl.BlockSpec((8, D_MODEL), lambda l, s: (0, 0)),
            pl.BlockSpec((None, D_MODEL, bn), lambda l, s: (0, 0, MOD_HEAD // bn + step(l, s))),
            pl.BlockSpec((None, 1, bn), lambda l, s: (0, 0, MOD_HEAD // bn + step(l, s))),
            pl.BlockSpec((None, br, IN_WIDTH), lambda l, s: (0, step(l, s), 0)),
        ],
        out_specs=[
            pl.BlockSpec((None, None, S5_ROW, S5_W1), lambda l, s: (l, s, 0, 0)),
            pl.BlockSpec((None, None, S5_ROW, 4 * S5_ST), lambda l, s: (l, s, 0, 0)),
            pl.BlockSpec((None, None, 8, S5_ST), lambda l, s: (l, s, 0, 0)),
            pl.BlockSpec((8, bn), lambda l, s: (0, step(l, s))),
            pl.BlockSpec((br, IN_WIDTH), lambda l, s: (step(l, s), 0)),
        ],
        out_shape=[
            jax.ShapeDtypeStruct((DEPTH, S5_OCT, S5_ROW, S5_W1), BF16),
            jax.ShapeDtypeStruct((DEPTH, S5_OCT, S5_ROW, 4 * S5_ST), BF16),
            jax.ShapeDtypeStruct((DEPTH, S5_OCT, 8, S5_ST), F32),
            jax.ShapeDtypeStruct((8, n_tail), F32),
            jax.ShapeDtypeStruct((D_MODEL, IN_WIDTH), BF16),
        ],
        compiler_params=pltpu.CompilerParams(
            dimension_semantics=("arbitrary", "arbitrary"), vmem_limit_bytes=VMEM_BIG),
        name="s5_params",
    )(rows, bc, cvec8, w_mod, b_mod3, w_in)


def _s5_mix_kernel(u_ref, w1_ref, w2_ref, a8_ref, h0_ref, y_ref, hfin_ref, lhs_ref, a_ref, hp_ref, y8_ref):
    def stream(nb, nc, is_ctx):
        for b in range(nb):
            for t in range(S5_T):
                lhs_ref[t, pl.ds(b, nc, stride=nb), :] = u_ref[pl.ds(b * nc * S5_T + t, nc, stride=S5_T), :]
        lhs = jnp.concatenate([lhs_ref[t] for t in range(S5_T)], axis=1).astype(BF16)
        a_ref[...] = _dot(lhs, w1_ref[:, S5_ROW:])
        y_intra = _dot(lhs, w1_ref[:, 0:S5_ROW])

        coef = [a8_ref[k:k + 1, :] for k in range(4)]
        if is_ctx:
            init = tuple(jnp.zeros((nb, S5_ST), F32) for _ in range(4))
        else:
            init = tuple(h0_ref[k] for k in range(4))

        def step(c, st):
            new = []
            for d in range(2):
                cc = c if d == 0 else nc - 1 - c
                rows = slice(cc * nb, (cc + 1) * nb)
                hr, hi = st[2 * d], st[2 * d + 1]
                hp_ref[rows, 2 * S5_ST * d:2 * S5_ST * d + S5_ST] = hr
                hp_ref[rows, 2 * S5_ST * d + S5_ST:2 * S5_ST * (d + 1)] = hi
                c0 = 2 * S5_ST * d
                gr = a_ref[rows, c0:c0 + S5_ST]
                gi = a_ref[rows, c0 + S5_ST:c0 + 2 * S5_ST]
                ar, ai = coef[2 * d], coef[2 * d + 1]
                new += [ar * hr - ai * hi + gr, ar * hi + ai * hr + gi]
            return tuple(new)

        fin = init
        for c in range(nc):
            fin = step(c, fin)
        if is_ctx:
            for k in range(4):
                hfin_ref[k] = fin[k]

        y8 = y_intra + _dot_nt(hp_ref[...].astype(BF16), w2_ref[...])
        for t in range(S5_T):
            y8_ref[t] = y8[:, t * 128:(t + 1) * 128]
        for b in range(nb):
            for t in range(S5_T):
                y_ref[pl.ds(b * nc * S5_T + t, nc, stride=S5_T), :] = y8_ref[t, pl.ds(b, nc, stride=nb), :]

    pl.when(pl.program_id(1) == 0)(functools.partial(stream, CTX_B, CTX_NC, True))
    pl.when(pl.program_id(1) == 1)(functools.partial(stream, LAT_B, LAT_NC, False))


def _s5_mix(uf, w1, w2, a8, h0, l):
    return pl.pallas_call(
        _s5_mix_kernel,
        grid=(S5_OCT, 2),
        in_specs=[
            pl.BlockSpec((T_CTX, 128), lambda s, k: (k, s)),
            pl.BlockSpec((None, None, S5_ROW, S5_W1), lambda s, k: (l, s, 0, 0)),
            pl.BlockSpec((None, None, S5_ROW, 4 * S5_ST), lambda s, k: (l, s, 0, 0)),
            pl.BlockSpec((None, None, 8, S5_ST), lambda s, k: (l, s, 0, 0)),
            pl.BlockSpec((None, 4, LAT_B, S5_ST), lambda s, k: (s, 0, 0, 0)),
        ],
        out_specs=[
            pl.BlockSpec((T_CTX, 128), lambda s, k: (k, s)),
            pl.BlockSpec((None, 4, CTX_B, S5_ST), lambda s, k: (s, 0, 0, 0)),
        ],
        out_shape=[
            jax.ShapeDtypeStruct((T_ALL, S5_CH), F32),
            jax.ShapeDtypeStruct((S5_OCT, 4, CTX_B, S5_ST), F32),
        ],
        scratch_shapes=[
            pltpu.VMEM((S5_T, S5_ROWS, 128), F32),
            pltpu.VMEM((S5_ROWS, 4 * S5_ST), F32),
            pltpu.VMEM((S5_ROWS, 4 * S5_ST), F32),
            pltpu.VMEM((S5_T, S5_ROWS, 128), F32),
        ],
        compiler_params=pltpu.CompilerParams(
            dimension_semantics=("arbitrary", "arbitrary"), vmem_limit_bytes=VMEM_BIG),
        name=f"s5_mix_l{l}",
    )(uf, w1, w2, a8, h0)


def _s5_gate(y, u, d, w, b):
    y = y + d * u
    z = y * (0.5 * (1.0 + jnp.tanh(math.sqrt(2.0 / math.pi) * (y + 0.044715 * (y * y * y)))))
    t = _dot(z.astype(BF16), w) + b
    return (z * jax.nn.sigmoid(t)).astype(BF16)


def _outproj_kernel(x_ref, a_ctx_ref, a_lat_ref, c_ctx_ref, c_lat_ref, y_ref, u_ref, d_ref, wglu_ref, bglu_ref,
                    mod_ref, npost_ref, wb_ref, o_ref):
    na = A_HEADS * HD
    nc = na + C_HEADS * HD

    def body(a_ref, c_ref):
        gate = mod_ref[:, 5 * D_MODEL:6 * D_MODEL]
        wglu = wglu_ref[...].astype(BF16)
        for r in range(BM_OUT // RC_OUT):
            rows = slice(r * RC_OUT, (r + 1) * RC_OUT)
            s = _s5_gate(y_ref[rows, :], u_ref[rows, :], d_ref[...], wglu, bglu_ref[...])
            y = (_dot(a_ref[rows, :], wb_ref[0:na, :]) + _dot(c_ref[rows, :], wb_ref[na:nc, :])
                 + _dot(s, wb_ref[nc:, :]))
            o_ref[rows, :] = x_ref[rows, :] + gate * _rms(y, npost_ref[...])

    is_lat = pl.program_id(0) >= T_CTX // BM_OUT
    pl.when(jnp.logical_not(is_lat))(functools.partial(body, a_ctx_ref, c_ctx_ref))
    pl.when(is_lat)(functools.partial(body, a_lat_ref, c_lat_ref))


def _outproj(x, a_ctx, a_lat, c_ctx, c_lat, y_s5, uf, s5_d3, w_glu, b_glu3, mods3, norm_post4, w_out_bf, l):
    bm = BM_OUT
    nct = T_CTX // bm
    ctx_idx = lambda i: (jnp.minimum(i, nct - 1), 0)
    lat_idx = lambda i: (jnp.maximum(i - nct, 0), 0)
    return pl.pallas_call(
        _outproj_kernel,
        grid=(T_ALL // bm,),
        in_specs=[
            pl.BlockSpec((bm, D_MODEL), lambda i: (i, 0)),
            pl.BlockSpec((bm, A_HEADS * HD), ctx_idx),
            pl.BlockSpec((bm, A_HEADS * HD), lat_idx),
            pl.BlockSpec((bm, C_HEADS * HD), ctx_idx),
            pl.BlockSpec((bm, C_HEADS * HD), lat_idx),
            pl.BlockSpec((bm, S5_CH), lambda i: (i, 0)),
            pl.BlockSpec((bm, S5_CH), lambda i: (i, 0)),
            pl.BlockSpec((None, 1, S5_CH), lambda i: (l, 0, 0)),
            pl.BlockSpec((None, S5_CH, S5_CH), lambda i: (l, 0, 0)),
            pl.BlockSpec((None, 1, S5_CH), lambda i: (l, 0, 0)),
            pl.BlockSpec((None, 1, N_MOD * D_MODEL), lambda i: (_mod_index(i, bm), 0, 0)),
            pl.BlockSpec((None, None, 1, D_MODEL), lambda i: (l, 1, 0, 0)),
            pl.BlockSpec((D_MODEL, D_MODEL), lambda i: (0, 0), pipeline_mode=pl.Buffered(1)),
        ],
        out_specs=pl.BlockSpec((bm, D_MODEL), lambda i: (i, 0)),
        out_shape=jax.ShapeDtypeStruct((T_ALL, D_MODEL), F32),
        compiler_params=pltpu.CompilerParams(
            dimension_semantics=("arbitrary",), vmem_limit_bytes=VMEM_BIG),
        name=f"outproj_l{l}",
    )(x, a_ctx, a_lat, c_ctx, c_lat, y_s5, uf, s5_d3, w_glu, b_glu3, mods3, norm_post4, w_out_bf)


def _rope_tables():
    rows = LAT_L // GRID_W
    row = jnp.repeat(jnp.arange(rows, dtype=F32), GRID_W)
    col = jnp.tile(jnp.arange(GRID_W, dtype=F32), rows)
    axis_dim = HD // 2
    inv_freq = ROPE_BASE ** (-jnp.arange(0, axis_dim, 2, dtype=F32) / axis_dim)
    ang_row = row[:, None] * inv_freq
    ang_col = col[:, None] * inv_freq
    cr, sr = jnp.cos(ang_row), jnp.sin(ang_row)
    cc, sc = jnp.cos(ang_col), jnp.sin(ang_col)
    cos_t = jnp.concatenate([cr, cr, cc, cc], axis=-1)
    sin_t = jnp.concatenate([-sr, sr, -sc, sc], axis=-1)
    return cos_t, sin_t


def kernel(x_prompt, x_sample, cache_a_k, cache_a_v, cache_c_k, cache_c_v, state_ssm_re, state_ssm_im,
           c, c_ctx, w_mod, b_mod, norm_pre, norm_post, ffn_gate, ffn_up, ffn_down, w_in, w_out,
           q_norm, k_norm, sink, s5_lam_re, s5_lam_im, s5_log_step, s5_b_re, s5_b_im, s5_c_re, s5_c_im,
           s5_d, w_glu, b_glu):
    cvec8 = jnp.concatenate([c_ctx[None, :], c, jnp.zeros((8 - 1 - LAT_B, D_MODEL), F32)], axis=0)
    b_mod3 = b_mod.reshape(DEPTH, 1, N_MOD * D_MODEL)
    norm_pre4 = norm_pre.reshape(DEPTH, 3, 1, D_MODEL)
    norm_post4 = norm_post.reshape(DEPTH, 3, 1, D_MODEL)
    q_norm3 = q_norm.reshape(DEPTH, 1, HD)
    k_norm3 = k_norm.reshape(DEPTH, 1, HD)
    s5_d3 = s5_d.reshape(DEPTH, 1, S5_CH)
    b_glu3 = b_glu.reshape(DEPTH, 1, S5_CH)
    cos_t, sin_t = _rope_tables()
    mods_head = _modulation(cvec8, w_mod, b_mod3, 0, MOD_HEAD)
    w1, w2, a8, mods_mid, w_in_bf = _s5_params(s5_lam_re, s5_lam_im, s5_log_step, s5_b_re, s5_b_im,
                                               s5_c_re, s5_c_im, cvec8, w_mod, b_mod3, w_in)
    mods_tail = jnp.zeros((8, N_MOD * D_MODEL - MOD_MID), F32)
    table = lambda tail: jnp.concatenate([mods_head, mods_mid, tail], axis=1).reshape(8, 1, N_MOD * D_MODEL)
    mods3 = table(mods_tail)
    kv4 = lambda a: a.reshape(LAT_B, DEPTH, PAST, A_KV * HD)
    cak, cav, cck, ccv = kv4(cache_a_k), kv4(cache_a_v), kv4(cache_c_k), kv4(cache_c_v)
    h0_all = jnp.stack([state_ssm_re[:, :, 0], state_ssm_im[:, :, 0],
                        state_ssm_re[:, :, 1], state_ssm_im[:, :, 1]], axis=0)
    h0_all = h0_all.reshape(4, LAT_B, DEPTH, S5_OCT, S5_ST).transpose(2, 3, 0, 1, 4)

    ffn_w = (norm_pre4, norm_post4, ffn_gate, ffn_up, ffn_down)
    new_caches = ()
    new_state = []
    xs = [x_prompt.reshape(T_CTX, D_MODEL), x_sample.reshape(T_LAT, D_MODEL)]
    for l in range(DEPTH):
        x = _ffn(xs, mods3, *ffn_w, l, 0)

        if l == 0:
            qkvu, uf, *new_caches, mods_tail = _inproj(
                x, mods3, norm_pre4, q_norm3, k_norm3, cos_t, sin_t, w_in_bf, l, (),
                mod_args=(cvec8, w_mod, b_mod3), mod_cols=(MOD_MID, N_MOD * D_MODEL))
            mods3 = table(mods_tail)
        else:
            qkvu, uf, *new_caches = _inproj(x, mods3, norm_pre4, q_norm3, k_norm3, cos_t, sin_t, w_in_bf, l,
                                            new_caches)
        a_ctx, c_ctx = _attn_ctx(qkvu, sink[l], l)
        if l < DEPTH - 1:
            a_lat, mods_next, w_in_next = _attn_lat_a(qkvu, cak, cav, l, next_mod=(cvec8, w_mod, b_mod3, w_in))
        else:
            a_lat = _attn_lat_a(qkvu, cak, cav, l)
        c_lat, w_out_bf = _attn_lat_c(qkvu, sink[l], cck, ccv, w_out, l)
        y, hfin = _s5_mix(uf, w1, w2, a8, h0_all[l], l)
        x = _outproj(x, a_ctx, a_lat, c_ctx, c_lat, y, uf, s5_d3, w_glu, b_glu3, mods3, norm_post4, w_out_bf, l)
        if l < DEPTH - 1:
            xs = [_ffn([x], mods3, *ffn_w, l, 1)]
            mods3 = mods_next.reshape(8, 1, N_MOD * D_MODEL)
            w_in_bf = w_in_next
        else:
            y_prompt, y_sample = _ffn([x], mods3, *ffn_w, l, 1, split_out=True)

        hf = hfin.reshape(S5_OCT, 4, CTX_B, S5_OG, S5_P).transpose(1, 2, 0, 3, 4).reshape(4, CTX_B, S5_G, S5_P)
        new_state.append((jnp.stack([hf[0], hf[2]], axis=1), jnp.stack([hf[1], hf[3]], axis=1)))

    y_prompt = y_prompt.reshape(CTX_B, CTX_L, D_MODEL)
    y_sample = y_sample.reshape(LAT_B, LAT_L, D_MODEL)
    caches = list(new_caches)
    st_re = jnp.stack([new_state[l][0] for l in range(DEPTH)], axis=1)
    st_im = jnp.stack([new_state[l][1] for l in range(DEPTH)], axis=1)
    return (y_prompt, y_sample, caches[0], caches[1], caches[2], caches[3], st_re, st_im)
```

```python
import functools
import math

import jax
import jax.numpy as jnp
from jax import lax
from jax.experimental import pallas as pl
from jax.experimental.pallas import tpu as pltpu

F32 = jnp.float32
BF16 = jnp.bfloat16

D_MODEL = 2048
CTX_B, CTX_L = 16, 256
LAT_B, LAT_L = 2, 2048
DEPTH = 2
PAST = 512
GRID_W = 64
HD = 128
A_HEADS, A_KV = 8, 2
C_HEADS, C_KV = 4, 2
WINDOW = 128
S5_GC = 16
S5_CH = 512
S5_G = 32
S5_P = 64
D_FF = 5632
N_MOD = 9
IN_WIDTH = 3072
ROPE_BASE = 10000.0
EPS = 1e-6
HALF_STEP = 0.5
NEG_INF = -1e30
SCALE = HD ** -0.5
LOG2E = math.log2(math.e)

T_CTX = CTX_B * CTX_L
T_LAT = LAT_B * LAT_L
T_ALL = T_CTX + T_LAT

COL_AQ, COL_AK, COL_AV = 0, 1024, 1280
COL_CQ, COL_CK, COL_CV = 1536, 2048, 2304
COL_U = 2560

S5_T = 8
S5_OCT = S5_CH // 128
S5_OG = S5_G // S5_OCT
S5_ROW = S5_T * 128
S5_ST = S5_OG * S5_P
S5_W1 = S5_ROW + 4 * S5_ST
CTX_NC = CTX_L // S5_T
LAT_NC = LAT_L // S5_T
S5_ROWS = CTX_NC * CTX_B
assert S5_ROWS == LAT_NC * LAT_B and T_CTX == T_LAT

V7X_VMEM_BYTES = 64 * 1024 * 1024
MIB = 1024 * 1024
VMEM_FFN = V7X_VMEM_BYTES - 2 * MIB
VMEM_BIG = V7X_VMEM_BYTES - 8 * MIB
VMEM_MID = V7X_VMEM_BYTES - 24 * MIB

BM = 1024
BM_OUT = 512
RC_OUT = 256
BF = 512
RC = 512
RCX = 256
N_XC = BM // RCX
FFN_PREFETCH_STEP = 2
BN_MOD = 1024
MOD_HEAD = 3 * D_MODEL
MOD_MID = 5 * D_MODEL
BM_IN = 512
BQ_A = 512
BQ_C = 256
NQ_C = 8
NB_CTX = 8


def _dot(a, b):
    return jnp.dot(a, b, preferred_element_type=F32)


def _dot_nt(a, b, precision=None):
    return lax.dot_general(a, b, (((1,), (1,)), ((), ())), preferred_element_type=F32, precision=precision)


def _dot_nt_split(a, b):
    ah = a.astype(BF16)
    bh = b.astype(BF16)
    al = (a - ah.astype(F32)).astype(BF16)
    bl = (b - bh.astype(F32)).astype(BF16)
    return _dot_nt(ah, bh) + (_dot_nt(ah, bl) + _dot_nt(al, bh))


def _rms(x, g):
    return x * lax.rsqrt(jnp.mean(x * x, axis=-1, keepdims=True) + EPS) * g


def _mod_index(i, bm):
    nct = T_CTX // bm
    return jnp.where(i < nct, 0, 1 + (i - nct) // (LAT_L // bm))


def _mod_columns(c_ref, w_ref, b_ref, o_ref):
    c = c_ref[...]
    s = (c * jax.nn.sigmoid(c)).astype(BF16)
    o_ref[...] = _dot(s, w_ref[...].astype(BF16)) + b_ref[...]


def _modulation(cvec8, w_mod, b_mod3, l, n):
    return pl.pallas_call(
        _mod_columns,
        grid=(n // BN_MOD,),
        in_specs=[
            pl.BlockSpec((8, D_MODEL), lambda j: (0, 0)),
            pl.BlockSpec((None, D_MODEL, BN_MOD), lambda j: (l, 0, j)),
            pl.BlockSpec((None, 1, BN_MOD), lambda j: (l, 0, j)),
        ],
        out_specs=pl.BlockSpec((8, BN_MOD), lambda j: (0, j)),
        out_shape=jax.ShapeDtypeStruct((8, n), F32),
        compiler_params=pltpu.CompilerParams(
            dimension_semantics=("arbitrary",), vmem_limit_bytes=VMEM_MID),
        name=f"modulation_l{l}",
    )(cvec8, w_mod, b_mod3)


def _ffn_kernel(*refs, mo, n_x, n_out):
    x_hbms = refs[:n_x]
    mod_ref, npre_ref, npost_ref, wg_ref, wu_ref, wd_ref = refs[n_x:n_x + 6]
    out_hbms = refs[n_x + 6:n_x + 6 + n_out]
    acc_ref, xc_ref, h_ref, sem_x, sem_c, sem_o = refs[n_x + 6 + n_out:]
    i = pl.program_id(0)
    j = pl.program_id(1)
    n_tiles = T_ALL // BM
    last = D_FF // BF - 1
    slot = i % 2
    nq = RC // RCX
    nct = T_CTX // BM

    def per_stream(arrays, tile, fn):
        if len(arrays) == 1:
            fn(arrays[0], pl.multiple_of(tile * BM, BM))
        else:
            pl.when(tile < nct)(lambda: fn(arrays[0], pl.multiple_of(tile * BM, BM)))
            pl.when(tile >= nct)(lambda: fn(arrays[1], pl.multiple_of((tile - nct) * BM, BM)))

    def x_tile_copy(arr, row0, sl):
        return pltpu.make_async_copy(arr.at[pl.ds(row0, BM), :], acc_ref.at[sl], sem_x)

    def x_chunk_copy(arr, row0, cs):
        return pltpu.make_async_copy(arr.at[pl.ds(row0, RCX), :], xc_ref.at[cs], sem_c.at[cs])

    def out_copy(arr, row0, sl):
        return pltpu.make_async_copy(acc_ref.at[sl], arr.at[pl.ds(row0, BM), :], sem_o.at[sl])

    wait_x_tile = lambda sl: x_tile_copy(x_hbms[0], 0, sl).wait()
    wait_x_chunk = lambda cs: x_chunk_copy(x_hbms[0], 0, cs).wait()
    wait_out = lambda sl: out_copy(out_hbms[0], 0, sl).wait()

    def start_x_chunk(q):
        per_stream(x_hbms, i, lambda arr, row0: x_chunk_copy(
            arr, pl.multiple_of(row0 + q * RCX, RCX), q).start())

    @pl.when(jnp.logical_and(i == 0, j == 0))
    def _():
        x_tile_copy(x_hbms[0], 0, 0).start()

    @pl.when(j == 0)
    def _():
        wait_x_tile(slot)

    @pl.when(j == FFN_PREFETCH_STEP)
    def _():
        pl.when(i >= 1)(lambda: wait_out(1 - slot))

        @pl.when(i + 1 < n_tiles)
        def _():
            per_stream(x_hbms, i + 1, lambda arr, row0: x_tile_copy(arr, row0, 1 - slot).start())

    @pl.when(j == last - 1)
    def _():
        for q in range(N_XC):
            start_x_chunk(q)

    def step(first, final):
        if final:
            for q in range(N_XC):
                wait_x_chunk(q)
        acc_slot = acc_ref.at[slot]
        wg = wg_ref[...].astype(BF16)
        wu = wu_ref[...].astype(BF16)
        wd = wd_ref[...].astype(BF16)
        for r in range(BM // RC):
            rows = slice(r * RC, (r + 1) * RC)
            if first:
                shift = mod_ref[:, mo * D_MODEL:(mo + 1) * D_MODEL]
                scale = mod_ref[:, (mo + 1) * D_MODEL:(mo + 2) * D_MODEL]
                hn = _rms(acc_slot[rows, :], npre_ref[...])
                h = (hn * (1.0 + scale) + shift).astype(BF16)
                h_ref[rows, :] = h
            else:
                h = h_ref[rows, :]
            g = _dot(h, wg)
            u = _dot(h, wu)
            a = (g * jax.nn.sigmoid(g) * u).astype(BF16)
            acc = _dot(a, wd)
            if not first:
                acc = acc_slot[rows, :] + acc
            if not final:
                acc_slot[rows, :] = acc
                continue
            gate = mod_ref[:, (mo + 2) * D_MODEL:(mo + 3) * D_MODEL]
            for qq in range(nq):
                sub = slice(qq * RCX, (qq + 1) * RCX)
                y = xc_ref[r * nq + qq] + (HALF_STEP * gate) * _rms(acc[sub, :], npost_ref[...])
                acc_slot[r * RC + qq * RCX:r * RC + (qq + 1) * RCX, :] = y
        if final:
            per_stream(out_hbms, i, lambda arr, row0: out_copy(arr, row0, slot).start())

    pl.when(j == 0)(functools.partial(step, True, False))
    pl.when(jnp.logical_and(j > 0, j < last))(functools.partial(step, False, False))
    pl.when(j == last)(functools.partial(step, False, True))

    @pl.when(jnp.logical_and(i == n_tiles - 1, j == last))
    def _():
        wait_out(slot)


def _ffn(xs, mods3, norm_pre4, norm_post4, ffn_gate, ffn_up, ffn_down, l, s, *, split_out=False):
    mo = 6 * s
    ni = 2 * s
    inner_specs = [
        pl.BlockSpec((None, 1, N_MOD * D_MODEL), lambda i, j: (_mod_index(i, BM), 0, 0)),
        pl.BlockSpec((None, None, 1, D_MODEL), lambda i, j: (l, ni, 0, 0)),
        pl.BlockSpec((None, None, 1, D_MODEL), lambda i, j: (l, ni, 0, 0)),
        pl.BlockSpec((None, None, D_MODEL, BF), lambda i, j: (l, s, 0, j)),
        pl.BlockSpec((None, None, D_MODEL, BF), lambda i, j: (l, s, 0, j)),
        pl.BlockSpec((None, None, BF, D_MODEL), lambda i, j: (l, s, j, 0)),
    ]
    out_rows = (T_CTX, T_LAT) if split_out else (T_ALL,)
    n_x, n_out = len(xs), len(out_rows)

    def outer(*refs):
        x_hbms = refs[:n_x]
        streamed = refs[n_x:n_x + 6]
        rest = refs[n_x + 6:]

        def body(*blocks):
            _ffn_kernel(*x_hbms, *blocks, *rest, mo=mo, n_x=n_x, n_out=n_out)

        pltpu.emit_pipeline(body, grid=(T_ALL // BM, D_FF // BF), in_specs=inner_specs)(*streamed)

    outs = pl.pallas_call(
        outer,
        in_specs=[pl.BlockSpec(memory_space=pl.ANY)] * (n_x + 6),
        out_specs=[pl.BlockSpec(memory_space=pl.ANY)] * len(out_rows),
        out_shape=[jax.ShapeDtypeStruct((r, D_MODEL), F32) for r in out_rows],
        scratch_shapes=[
            pltpu.VMEM((2, BM, D_MODEL), F32),
            pltpu.VMEM((N_XC, RCX, D_MODEL), F32),
            pltpu.VMEM((BM, D_MODEL), BF16),
            pltpu.SemaphoreType.DMA(()),
            pltpu.SemaphoreType.DMA((N_XC,)),
            pltpu.SemaphoreType.DMA((2,)),
        ],
        compiler_params=pltpu.CompilerParams(vmem_limit_bytes=VMEM_FFN),
        name=f"ffn_l{l}_s{s}",
    )(*xs, mods3, norm_pre4, norm_post4, ffn_gate, ffn_up, ffn_down)
    return outs if split_out else outs[0]


def _rope(y, cos, sins):
    lane = lax.broadcasted_iota(jnp.int32, y.shape, 1)
    first = (lane & 63) < 32
    partner = jnp.where(first, pltpu.roll(y, 96, 1), pltpu.roll(y, 32, 1))
    return y * cos + partner * sins


_IN_SEGMENTS = (
    (COL_AQ, A_HEADS, "q", True, None),
    (COL_AK, A_KV, "k", True, 0),
    (COL_AV, A_KV, None, False, 1),
    (COL_CQ, C_HEADS, None, True, None),
    (COL_CK, C_KV, None, True, 2),
    (COL_CV, C_KV, None, False, 3),
)


def _inproj_kernel(x_ref, mod_ref, npre_ref, qn_ref, kn_ref, cos_ref, sin_ref, w_ref, *rest, first_layer):
    nb = BM_IN // CTX_L
    if first_layer:
        c_ref, wmod_ref, bmod_ref, qkvu_ref, uf_ref, *cache_full, modo_ref = rest
        cache_refs = [c.at[:, 0] for c in cache_full]
        _mod_columns(c_ref, wmod_ref, bmod_ref, modo_ref)
    else:
        qkvu_ref, uf_ref, *cache_refs = rest[4:]
        cache_full = ()

    def body(lat):
        shift = mod_ref[:, 3 * D_MODEL:4 * D_MODEL]
        scale = mod_ref[:, 4 * D_MODEL:5 * D_MODEL]
        h = (_rms(x_ref[...], npre_ref[...]) * (1.0 + scale) + shift).astype(BF16)
        for col0, heads, norm, rot, cache in _IN_SEGMENTS:
            p = _dot(h, w_ref[:, col0:col0 + heads * HD])
            for k in range(heads):
                y = p[:, k * HD:(k + 1) * HD]
                if norm == "q":
                    y = _rms(y, qn_ref[...])
                elif norm == "k":
                    y = _rms(y, kn_ref[...])
                if rot and lat:
                    y = _rope(y, cos_ref[...], sin_ref[...])
                qkvu_ref[:, col0 + k * HD:col0 + (k + 1) * HD] = y.astype(BF16)
                if cache is not None and not lat:
                    cache_refs[cache][:, :, k, :] = y.reshape(nb, CTX_L, HD)
        u = _dot(h, w_ref[:, COL_U:])
        uf_ref[...] = u
        qkvu_ref[:, COL_U:] = u.astype(BF16)
        if first_layer and not lat:
            for c in cache_full:
                c[:, 1:] = jnp.zeros((nb, DEPTH - 1, CTX_L, A_KV, HD), F32)

    is_lat = pl.program_id(0) >= T_CTX // BM_IN
    pl.when(is_lat)(functools.partial(body, True))
    pl.when(jnp.logical_not(is_lat))(functools.partial(body, False))


def _inproj(x, mods3, norm_pre4, q_norm3, k_norm3, cos_t, sin_t, w_in_bf, l, prev_caches, mod_args=None,
            mod_cols=None):
    bm = BM_IN
    nct = T_CTX // bm
    nb = bm // CTX_L
    tab_idx = lambda i: (jnp.maximum(i - nct, 0) % (LAT_L // bm), 0)
    cache_shape = jax.ShapeDtypeStruct((CTX_B, DEPTH, CTX_L, A_KV, HD), F32)
    first_layer = not prev_caches
    if first_layer:
        cache_spec = pl.BlockSpec((nb, DEPTH, CTX_L, A_KV, HD), lambda i: (jnp.minimum(i, nct - 1), 0, 0, 0, 0))
    else:
        cache_spec = pl.BlockSpec((nb, None, CTX_L, A_KV, HD), lambda i: (jnp.minimum(i, nct - 1), l, 0, 0, 0))
    n_in = 8
    extra_in, extra_out, extra_shape, extra_args = [], [], [], []
    if first_layer:
        lo, hi = mod_cols
        steps = T_ALL // bm
        bn = (hi - lo) // steps
        extra_in = [
            pl.BlockSpec((8, D_MODEL), lambda i: (0, 0)),
            pl.BlockSpec((None, D_MODEL, bn), lambda i: (l, 0, lo // bn + i)),
            pl.BlockSpec((None, 1, bn), lambda i: (l, 0, lo // bn + i)),
        ]
        extra_out = [pl.BlockSpec((8, bn), lambda i: (0, i))]
        extra_shape = [jax.ShapeDtypeStruct((8, hi - lo), F32)]
        extra_args = list(mod_args)
    return pl.pallas_call(
        functools.partial(_inproj_kernel, first_layer=first_layer),
        grid=(T_ALL // bm,),
        in_specs=[
            pl.BlockSpec((bm, D_MODEL), lambda i: (i, 0)),
            pl.BlockSpec((None, 1, N_MOD * D_MODEL), lambda i: (_mod_index(i, bm), 0, 0)),
            pl.BlockSpec((None, None, 1, D_MODEL), lambda i: (l, 1, 0, 0)),
            pl.BlockSpec((None, 1, HD), lambda i: (l, 0, 0)),
            pl.BlockSpec((None, 1, HD), lambda i: (l, 0, 0)),
            pl.BlockSpec((bm, HD), tab_idx),
            pl.BlockSpec((bm, HD), tab_idx),
            pl.BlockSpec((D_MODEL, IN_WIDTH), lambda i: (0, 0), pipeline_mode=pl.Buffered(1)),
        ] + extra_in + [pl.BlockSpec(memory_space=pl.ANY)] * len(prev_caches),
        out_specs=[
            pl.BlockSpec((bm, IN_WIDTH), lambda i: (i, 0)),
            pl.BlockSpec((bm, S5_CH), lambda i: (i, 0)),
        ] + [cache_spec] * 4 + extra_out,
        out_shape=[
            jax.ShapeDtypeStruct((T_ALL, IN_WIDTH), BF16),
            jax.ShapeDtypeStruct((T_ALL, S5_CH), F32),
        ] + [cache_shape] * 4 + extra_shape,
        input_output_aliases={n_in + k: 2 + k for k in range(len(prev_caches))},
        compiler_params=pltpu.CompilerParams(
            dimension_semantics=("arbitrary",), vmem_limit_bytes=VMEM_BIG),
        name=f"inproj_l{l}",
    )(x, mods3, norm_pre4, q_norm3, k_norm3, cos_t, sin_t, w_in_bf, *extra_args, *prev_caches)


def _softmax_pv(dots, values, sink=None):
    m = functools.reduce(jnp.maximum, [jnp.max(d, axis=-1, keepdims=True) for d in dots]) * SCALE
    if sink is not None:
        m = jnp.maximum(m, sink)
    m2 = m * LOG2E
    ps = [jnp.exp2(d * (SCALE * LOG2E) - m2) for d in dots]
    den = functools.reduce(jnp.add, [jnp.sum(p, axis=-1, keepdims=True) for p in ps])
    if sink is not None:
        den = den + jnp.exp2(sink * LOG2E - m2)
    o = functools.reduce(jnp.add, [_dot(p.astype(BF16), v) for p, v in zip(ps, values)])
    return o / den


def _attn_ctx_kernel(sink_ref, aq_ref, ak_ref, av_ref, cq_ref, ck_ref, cv_ref, ao_ref, co_ref):
    kv = pl.program_id(1)
    g = C_HEADS // C_KV
    for b in range(NB_CTX):
        rows = slice(b * CTX_L, (b + 1) * CTX_L)
        k = ak_ref[rows, :]
        v = av_ref[rows, :]
        for h in range(A_HEADS // A_KV):
            cols = slice(h * HD, (h + 1) * HD)
            s = _dot_nt(aq_ref[rows, cols], k)
            ao_ref[rows, cols] = _softmax_pv([s], [v]).astype(BF16)
        k = ck_ref[rows, :]
        v = cv_ref[rows, :]
        for h in range(g):
            cols = slice(h * HD, (h + 1) * HD)
            s = _dot_nt(cq_ref[rows, cols], k)
            co_ref[rows, cols] = _softmax_pv([s], [v], sink_ref[kv * g + h]).astype(BF16)


def _attn_ctx(qkvu, sink_l, l):
    ga = A_HEADS // A_KV * HD
    gc = C_HEADS // C_KV * HD
    rows = NB_CTX * CTX_L
    blk = lambda width, col0: pl.BlockSpec((rows, width), lambda b, k: (b, col0 // width + k))
    return pl.pallas_call(
        _attn_ctx_kernel,
        grid=(CTX_B // NB_CTX, A_KV),
        in_specs=[
            pl.BlockSpec(memory_space=pltpu.SMEM),
            blk(ga, COL_AQ), blk(HD, COL_AK), blk(HD, COL_AV),
            blk(gc, COL_CQ), blk(HD, COL_CK), blk(HD, COL_CV),
        ],
        out_specs=[
            pl.BlockSpec((rows, ga), lambda b, k: (b, k)),
            pl.BlockSpec((rows, gc), lambda b, k: (b, k)),
        ],
        out_shape=[
            jax.ShapeDtypeStruct((T_CTX, A_HEADS * HD), BF16),
            jax.ShapeDtypeStruct((T_CTX, C_HEADS * HD), BF16),
        ],
        compiler_params=pltpu.CompilerParams(dimension_semantics=("arbitrary", "arbitrary")),
        name=f"attn_ctx_l{l}",
    )(sink_l, qkvu, qkvu, qkvu, qkvu, qkvu, qkvu)


def _attn_lat_a_kernel(q_ref, k_ref, v_ref, kc_ref, vc_ref, *rest):
    o_ref = rest[0] if len(rest) == 1 else rest[4]
    k = k_ref[...]
    v = v_ref[...]
    kc = kc_ref[...].astype(BF16)
    vc = vc_ref[...].astype(BF16)
    for h in range(A_HEADS // A_KV):
        cols = slice(h * HD, (h + 1) * HD)
        q = q_ref[:, cols]
        s1 = _dot_nt(q, k)
        s2 = _dot_nt(q, kc)
        o_ref[:, cols] = _softmax_pv([s1, s2], [v, vc]).astype(BF16)
    if len(rest) > 1:
        c_ref, w_ref, b_ref, win_ref, _, mod_ref, winb_ref = rest
        _mod_columns(c_ref, w_ref, b_ref, mod_ref)
        winb_ref[...] = win_ref[...].astype(BF16)


def _attn_lat_a(qkvu, cache_k4, cache_v4, l, next_mod=None):
    ga = A_HEADS // A_KV * HD
    nq = LAT_L // BQ_A
    row0 = T_CTX // BQ_A
    lat_blk = T_CTX // LAT_L
    in_specs = [
        pl.BlockSpec((BQ_A, ga), lambda b, k, q: (row0 + b * nq + q, COL_AQ // ga + k)),
        pl.BlockSpec((LAT_L, HD), lambda b, k, q: (lat_blk + b, COL_AK // HD + k)),
        pl.BlockSpec((LAT_L, HD), lambda b, k, q: (lat_blk + b, COL_AV // HD + k)),
        pl.BlockSpec((None, None, PAST, HD), lambda b, k, q: (b, l, 0, k)),
        pl.BlockSpec((None, None, PAST, HD), lambda b, k, q: (b, l, 0, k)),
    ]
    out_specs = [pl.BlockSpec((BQ_A, ga), lambda b, k, q: (b * nq + q, k))]
    out_shape = [jax.ShapeDtypeStruct((T_LAT, A_HEADS * HD), BF16)]
    args = [qkvu, qkvu, qkvu, cache_k4, cache_v4]
    if next_mod is not None:
        n = N_MOD * D_MODEL
        steps = LAT_B * A_KV * nq
        bn = n // steps
        step = lambda b, k, q: (b * A_KV + k) * nq + q
        in_specs += [
            pl.BlockSpec((8, D_MODEL), lambda b, k, q: (0, 0)),
            pl.BlockSpec((None, D_MODEL, bn), lambda b, k, q: (l + 1, 0, step(b, k, q))),
            pl.BlockSpec((None, 1, bn), lambda b, k, q: (l + 1, 0, step(b, k, q))),
            pl.BlockSpec((None, D_MODEL // steps, IN_WIDTH), lambda b, k, q: (l + 1, step(b, k, q), 0)),
        ]
        out_specs += [
            pl.BlockSpec((8, bn), lambda b, k, q: (0, step(b, k, q))),
            pl.BlockSpec((D_MODEL // steps, IN_WIDTH), lambda b, k, q: (step(b, k, q), 0)),
        ]
        out_shape += [
            jax.ShapeDtypeStruct((8, n), F32),
            jax.ShapeDtypeStruct((D_MODEL, IN_WIDTH), BF16),
        ]
        args += list(next_mod)
    outs = pl.pallas_call(
        _attn_lat_a_kernel,
        grid=(LAT_B, A_KV, nq),
        in_specs=in_specs,
        out_specs=out_specs,
        out_shape=out_shape,
        compiler_params=pltpu.CompilerParams(
            dimension_semantics=("arbitrary", "arbitrary", "arbitrary"), vmem_limit_bytes=VMEM_BIG),
        name=f"attn_lat_a_l{l}",
    )(*args)
    return outs if next_mod is not None else outs[0]


def _attn_lat_c_kernel(sink_ref, q_ref, k_ref, v_ref, kc_ref, vc_ref, wout_ref, o_ref, woutb_ref):
    woutb_ref[...] = wout_ref[...].astype(BF16)
    kv = pl.program_id(1)
    span = BQ_C + 2 * WINDOW
    kc = kc_ref[...].astype(BF16)
    vc = vc_ref[...].astype(BF16)
    g = C_HEADS // C_KV
    for sub in range(NQ_C):
        n = pl.program_id(2) * NQ_C + sub
        rows = slice(sub * BQ_C, (sub + 1) * BQ_C)
        start = pl.multiple_of(jnp.clip(n * BQ_C - WINDOW, 0, LAT_L - span), WINDOW)
        kw = k_ref[pl.ds(start, span), :]
        vw = v_ref[pl.ds(start, span), :]
        qpos = n * BQ_C + lax.broadcasted_iota(jnp.int32, (BQ_C, span), 0)
        kpos = start + lax.broadcasted_iota(jnp.int32, (BQ_C, span), 1)
        valid = jnp.abs(qpos - kpos) <= WINDOW
        for h in range(g):
            cols = slice(h * HD, (h + 1) * HD)
            q = q_ref[rows, cols]
            s1 = jnp.where(valid, _dot_nt(q, kw), NEG_INF)
            s2 = _dot_nt(q, kc)
            o_ref[rows, cols] = _softmax_pv([s1, s2], [vw, vc], sink_ref[kv * g + h]).astype(BF16)


def _attn_lat_c(qkvu, sink_l, cache_k4, cache_v4, w_out, l):
    gc = C_HEADS // C_KV * HD
    bq = BQ_C * NQ_C
    nq = LAT_L // bq
    row0 = T_CTX // bq
    lat_blk = T_CTX // LAT_L
    steps = LAT_B * C_KV * nq
    step = lambda b, k, q: (b * C_KV + k) * nq + q
    return pl.pallas_call(
        _attn_lat_c_kernel,
        grid=(LAT_B, C_KV, nq),
        in_specs=[
            pl.BlockSpec(memory_space=pltpu.SMEM),
            pl.BlockSpec((bq, gc), lambda b, k, q: (row0 + b * nq + q, COL_CQ // gc + k)),
            pl.BlockSpec((LAT_L, HD), lambda b, k, q: (lat_blk + b, COL_CK // HD + k)),
            pl.BlockSpec((LAT_L, HD), lambda b, k, q: (lat_blk + b, COL_CV // HD + k)),
            pl.BlockSpec((None, None, PAST, HD), lambda b, k, q: (b, l, 0, k)),
            pl.BlockSpec((None, None, PAST, HD), lambda b, k, q: (b, l, 0, k)),
            pl.BlockSpec((None, D_MODEL // steps, D_MODEL), lambda b, k, q: (l, step(b, k, q), 0)),
        ],
        out_specs=[
            pl.BlockSpec((bq, gc), lambda b, k, q: (b * nq + q, k)),
            pl.BlockSpec((D_MODEL // steps, D_MODEL), lambda b, k, q: (step(b, k, q), 0)),
        ],
        out_shape=[
            jax.ShapeDtypeStruct((T_LAT, C_HEADS * HD), BF16),
            jax.ShapeDtypeStruct((D_MODEL, D_MODEL), BF16),
        ],
        compiler_params=pltpu.CompilerParams(
            dimension_semantics=("arbitrary", "arbitrary", "arbitrary")),
        name=f"attn_lat_c_l{l}",
    )(sink_l, qkvu, qkvu, qkvu, cache_k4, cache_v4, w_out)


def _s5_param_kernel(rows_ref, bc_ref, c_ref, wmod_ref, bmod_ref, win_ref, w1_ref, w2_ref, a8_ref, mod_ref,
                     winb_ref):
    _mod_columns(c_ref, wmod_ref, bmod_ref, mod_ref)
    winb_ref[...] = win_ref[...].astype(BF16)
    a8_ref[...] = jnp.zeros_like(a8_ref)
    row_grp = lax.broadcasted_iota(jnp.int32, (128, S5_ST), 0) // S5_GC
    lane_grp = lax.broadcasted_iota(jnp.int32, (128, S5_ST), 1) // S5_P
    on_diag = row_grp == lane_grp

    def expand(a):
        return jnp.where(on_diag, jnp.concatenate([a] * S5_OG, axis=1), 0.0)

    taps = []
    for d in range(2):
        lr = rows_ref[0, d]
        li = rows_ref[1, d]
        dt = jnp.exp(rows_ref[2, d])
        mag = jnp.exp(lr * dt)
        ar = mag * jnp.cos(li * dt)
        ai = mag * jnp.sin(li * dt)
        den = lr * lr + li * li
        n_re = ar - 1.0
        f_re = (n_re * lr + ai * li) / den
        f_im = (ai * lr - n_re * li) / den
        pw = [(jnp.ones_like(ar), jnp.zeros_like(ar))]
        for _ in range(S5_T):
            pr, pi = pw[-1]
            pw.append((pr * ar - pi * ai, pr * ai + pi * ar))
        br, bi, cr, ci = (expand(bc_ref[k, d]) for k in range(4))
        bbr = f_re * br - f_im * bi
        bbi = f_re * bi + f_im * br
        xr_blocks, xi_blocks = [], []
        for t in range(S5_T):
            rows = slice(t * 128, (t + 1) * 128)
            pr, pi = pw[S5_T - 1 - t if d == 0 else t]
            xr = pr * bbr - pi * bbi
            xi = pr * bbi + pi * bbr
            c0 = S5_ROW + 2 * S5_ST * d
            w1_ref[rows, c0:c0 + S5_ST] = xr.astype(BF16)
            w1_ref[rows, c0 + S5_ST:c0 + 2 * S5_ST] = xi.astype(BF16)
            xr_blocks.append(xr)
            xi_blocks.append(xi)
            pr, pi = pw[t + 1 if d == 0 else S5_T - t]
            c0 = 2 * S5_ST * d
            w2_ref[rows, c0:c0 + S5_ST] = (cr * pr - ci * pi).astype(BF16)
            w2_ref[rows, c0 + S5_ST:c0 + 2 * S5_ST] = (-(cr * pi + ci * pr)).astype(BF16)
        a8_ref[2 * d:2 * d + 1, :] = pw[S5_T][0]
        a8_ref[2 * d + 1:2 * d + 2, :] = pw[S5_T][1]
        xr_all = jnp.concatenate(xr_blocks, axis=0)
        xi_all = jnp.concatenate(xi_blocks, axis=0)
        taps.append(_dot_nt_split(xr_all, cr) - _dot_nt_split(xi_all, ci))
    tf, tb = taps
    for t in range(S5_T):
        for t2 in range(S5_T):
            blk = None
            if t2 >= t:
                r0 = (S5_T - 1 - (t2 - t)) * 128
                blk = tf[r0:r0 + 128, :]
            if t2 <= t:
                r0 = (t - t2) * 128
                b2 = tb[r0:r0 + 128, :]
                blk = b2 if blk is None else blk + b2
            w1_ref[t * 128:(t + 1) * 128, t2 * 128:(t2 + 1) * 128] = blk.astype(BF16)


def _s5_params(s5_lam_re, s5_lam_im, s5_log_step, s5_b_re, s5_b_im, s5_c_re, s5_c_im, cvec8, w_mod, b_mod3, w_in):
    rows = jnp.stack([s5_lam_re, s5_lam_im, jnp.repeat(s5_log_step[..., None], S5_P, axis=-1)], axis=0)
    rows = rows.reshape(3, DEPTH, 2, S5_OCT, 1, S5_ST)
    bc = jnp.stack([jnp.swapaxes(s5_b_re, -1, -2), jnp.swapaxes(s5_b_im, -1, -2), s5_c_re, s5_c_im], axis=0)
    bc = bc.reshape(4, DEPTH, 2, S5_OCT, 128, S5_P)
    spec = lambda n, r, c: pl.BlockSpec((n, None, 2, None, r, c), lambda l, s: (0, l, 0, s, 0, 0))
    steps = DEPTH * S5_OCT
    step = lambda l, s: l * S5_OCT + s
    n_tail = MOD_MID - MOD_HEAD
    bn = n_tail // steps
    br = D_MODEL // steps
    return pl.pallas_call(
        _s5_param_kernel,
        grid=(DEPTH, S5_OCT),
        in_specs=[
            spec(3, 1, S5_ST), spec(4, 128, S5_P),
            pl.BlockSpec((8, D_MODEL), lambda l, s: (0, 0)),
            pl.BlockSpec((None, D_MODEL, bn), lambda l, s: (0, 0, MOD_HEAD // bn + step(l, s))),
            pl.BlockSpec((None, 1, bn), lambda l, s: (0, 0, MOD_HEAD // bn + step(l, s))),
            pl.BlockSpec((None, br, IN_WIDTH), lambda l, s: (0, step(l, s), 0)),
        ],
        out_specs=[
            pl.BlockSpec((None, None, S5_ROW, S5_W1), lambda l, s: (l, s, 0, 0)),
            pl.BlockSpec((None, None, S5_ROW, 4 * S5_ST), lambda l, s: (l, s, 0, 0)),
            pl.BlockSpec((None, None, 8, S5_ST), lambda l, s: (l, s, 0, 0)),
            pl.BlockSpec((8, bn), lambda l, s: (0, step(l, s))),
            pl.BlockSpec((br, IN_WIDTH), lambda l, s: (step(l, s), 0)),
        ],
        out_shape=[
            jax.ShapeDtypeStruct((DEPTH, S5_OCT, S5_ROW, S5_W1), BF16),
            jax.ShapeDtypeStruct((DEPTH, S5_OCT, S5_ROW, 4 * S5_ST), BF16),
            jax.ShapeDtypeStruct((DEPTH, S5_OCT, 8, S5_ST), F32),
            jax.ShapeDtypeStruct((8, n_tail), F32),
            jax.ShapeDtypeStruct((D_MODEL, IN_WIDTH), BF16),
        ],
        compiler_params=pltpu.CompilerParams(
            dimension_semantics=("arbitrary", "arbitrary"), vmem_limit_bytes=VMEM_BIG),
        name="s5_params",
    )(rows, bc, cvec8, w_mod, b_mod3, w_in)


def _s5_mix_kernel(u_ref, w1_ref, w2_ref, a8_ref, h0_ref, y_ref, hfin_ref, lhs_ref, a_ref, hp_ref, y8_ref):
    def stream(nb, nc, is_ctx):
        for b in range(nb):
            for t in range(S5_T):
                lhs_ref[t, pl.ds(b, nc, stride=nb), :] = u_ref[pl.ds(b * nc * S5_T + t, nc, stride=S5_T), :]
        lhs = jnp.concatenate([lhs_ref[t] for t in range(S5_T)], axis=1).astype(BF16)
        a_ref[...] = _dot(lhs, w1_ref[:, S5_ROW:])
        y_intra = _dot(lhs, w1_ref[:, 0:S5_ROW])

        coef = [a8_ref[k:k + 1, :] for k in range(4)]
        if is_ctx:
            init = tuple(jnp.zeros((nb, S5_ST), F32) for _ in range(4))
        else:
            init = tuple(h0_ref[k] for k in range(4))

        def step(c, st):
            new = []
            for d in range(2):
                cc = c if d == 0 else nc - 1 - c
                rows = slice(cc * nb, (cc + 1) * nb)
                hr, hi = st[2 * d], st[2 * d + 1]
                hp_ref[rows, 2 * S5_ST * d:2 * S5_ST * d + S5_ST] = hr
                hp_ref[rows, 2 * S5_ST * d + S5_ST:2 * S5_ST * (d + 1)] = hi
                c0 = 2 * S5_ST * d
                gr = a_ref[rows, c0:c0 + S5_ST]
                gi = a_ref[rows, c0 + S5_ST:c0 + 2 * S5_ST]
                ar, ai = coef[2 * d], coef[2 * d + 1]
                new += [ar * hr - ai * hi + gr, ar * hi + ai * hr + gi]
            return tuple(new)

        fin = init
        for c in range(nc):
            fin = step(c, fin)
        if is_ctx:
            for k in range(4):
                hfin_ref[k] = fin[k]

        y8 = y_intra + _dot_nt(hp_ref[...].astype(BF16), w2_ref[...])
        for t in range(S5_T):
            y8_ref[t] = y8[:, t * 128:(t + 1) * 128]
        for b in range(nb):
            for t in range(S5_T):
                y_ref[pl.ds(b * nc * S5_T + t, nc, stride=S5_T), :] = y8_ref[t, pl.ds(b, nc, stride=nb), :]

    pl.when(pl.program_id(1) == 0)(functools.partial(stream, CTX_B, CTX_NC, True))
    pl.when(pl.program_id(1) == 1)(functools.partial(stream, LAT_B, LAT_NC, False))


def _s5_mix(uf, w1, w2, a8, h0, l):
    return pl.pallas_call(
        _s5_mix_kernel,
        grid=(S5_OCT, 2),
        in_specs=[
            pl.BlockSpec((T_CTX, 128), lambda s, k: (k, s)),
            pl.BlockSpec((None, None, S5_ROW, S5_W1), lambda s, k: (l, s, 0, 0)),
            pl.BlockSpec((None, None, S5_ROW, 4 * S5_ST), lambda s, k: (l, s, 0, 0)),
            pl.BlockSpec((None, None, 8, S5_ST), lambda s, k: (l, s, 0, 0)),
            pl.BlockSpec((None, 4, LAT_B, S5_ST), lambda s, k: (s, 0, 0, 0)),
        ],
        out_specs=[
            pl.BlockSpec((T_CTX, 128), lambda s, k: (k, s)),
            pl.BlockSpec((None, 4, CTX_B, S5_ST), lambda s, k: (s, 0, 0, 0)),
        ],
        out_shape=[
            jax.ShapeDtypeStruct((T_ALL, S5_CH), F32),
            jax.ShapeDtypeStruct((S5_OCT, 4, CTX_B, S5_ST), F32),
        ],
        scratch_shapes=[
            pltpu.VMEM((S5_T, S5_ROWS, 128), F32),
            pltpu.VMEM((S5_ROWS, 4 * S5_ST), F32),
            pltpu.VMEM((S5_ROWS, 4 * S5_ST), F32),
            pltpu.VMEM((S5_T, S5_ROWS, 128), F32),
        ],
        compiler_params=pltpu.CompilerParams(
            dimension_semantics=("arbitrary", "arbitrary"), vmem_limit_bytes=VMEM_BIG),
        name=f"s5_mix_l{l}",
    )(uf, w1, w2, a8, h0)


def _s5_gate(y, u, d, w, b):
    y = y + d * u
    z = y * (0.5 * (1.0 + jnp.tanh(math.sqrt(2.0 / math.pi) * (y + 0.044715 * (y * y * y)))))
    t = _dot(z.astype(BF16), w) + b
    return (z * jax.nn.sigmoid(t)).astype(BF16)


def _outproj_kernel(x_ref, a_ctx_ref, a_lat_ref, c_ctx_ref, c_lat_ref, y_ref, u_ref, d_ref, wglu_ref, bglu_ref,
                    mod_ref, npost_ref, wb_ref, o_ref):
    na = A_HEADS * HD
    nc = na + C_HEADS * HD

    def body(a_ref, c_ref):
        gate = mod_ref[:, 5 * D_MODEL:6 * D_MODEL]
        wglu = wglu_ref[...].astype(BF16)
        for r in range(BM_OUT // RC_OUT):
            rows = slice(r * RC_OUT, (r + 1) * RC_OUT)
            s = _s5_gate(y_ref[rows, :], u_ref[rows, :], d_ref[...], wglu, bglu_ref[...])
            y = (_dot(a_ref[rows, :], wb_ref[0:na, :]) + _dot(c_ref[rows, :], wb_ref[na:nc, :])
                 + _dot(s, wb_ref[nc:, :]))
            o_ref[rows, :] = x_ref[rows, :] + gate * _rms(y, npost_ref[...])

    is_lat = pl.program_id(0) >= T_CTX // BM_OUT
    pl.when(jnp.logical_not(is_lat))(functools.partial(body, a_ctx_ref, c_ctx_ref))
    pl.when(is_lat)(functools.partial(body, a_lat_ref, c_lat_ref))


def _outproj(x, a_ctx, a_lat, c_ctx, c_lat, y_s5, uf, s5_d3, w_glu, b_glu3, mods3, norm_post4, w_out_bf, l):
    bm = BM_OUT
    nct = T_CTX // bm
    ctx_idx = lambda i: (jnp.minimum(i, nct - 1), 0)
    lat_idx = lambda i: (jnp.maximum(i - nct, 0), 0)
    return pl.pallas_call(
        _outproj_kernel,
        grid=(T_ALL // bm,),
        in_specs=[
            pl.BlockSpec((bm, D_MODEL), lambda i: (i, 0)),
            pl.BlockSpec((bm, A_HEADS * HD), ctx_idx),
            pl.BlockSpec((bm, A_HEADS * HD), lat_idx),
            pl.BlockSpec((bm, C_HEADS * HD), ctx_idx),
            pl.BlockSpec((bm, C_HEADS * HD), lat_idx),
            pl.BlockSpec((bm, S5_CH), lambda i: (i, 0)),
            pl.BlockSpec((bm, S5_CH), lambda i: (i, 0)),
            pl.BlockSpec((None, 1, S5_CH), lambda i: (l, 0, 0)),
            pl.BlockSpec((None, S5_CH, S5_CH), lambda i: (l, 0, 0)),
            pl.BlockSpec((None, 1, S5_CH), lambda i: (l, 0, 0)),
            pl.BlockSpec((None, 1, N_MOD * D_MODEL), lambda i: (_mod_index(i, bm), 0, 0)),
            pl.BlockSpec((None, None, 1, D_MODEL), lambda i: (l, 1, 0, 0)),
            pl.BlockSpec((D_MODEL, D_MODEL), lambda i: (0, 0), pipeline_mode=pl.Buffered(1)),
        ],
        out_specs=pl.BlockSpec((bm, D_MODEL), lambda i: (i, 0)),
        out_shape=jax.ShapeDtypeStruct((T_ALL, D_MODEL), F32),
        compiler_params=pltpu.CompilerParams(
            dimension_semantics=("arbitrary",), vmem_limit_bytes=VMEM_BIG),
        name=f"outproj_l{l}",
    )(x, a_ctx, a_lat, c_ctx, c_lat, y_s5, uf, s5_d3, w_glu, b_glu3, mods3, norm_post4, w_out_bf)


def _rope_tables():
    rows = LAT_L // GRID_W
    row = jnp.repeat(jnp.arange(rows, dtype=F32), GRID_W)
    col = jnp.tile(jnp.arange(GRID_W, dtype=F32), rows)
    axis_dim = HD // 2
    inv_freq = ROPE_BASE ** (-jnp.arange(0, axis_dim, 2, dtype=F32) / axis_dim)
    ang_row = row[:, None] * inv_freq
    ang_col = col[:, None] * inv_freq
    cr, sr = jnp.cos(ang_row), jnp.sin(ang_row)
    cc, sc = jnp.cos(ang_col), jnp.sin(ang_col)
    cos_t = jnp.concatenate([cr, cr, cc, cc], axis=-1)
    sin_t = jnp.concatenate([-sr, sr, -sc, sc], axis=-1)
    return cos_t, sin_t


def kernel(x_prompt, x_sample, cache_a_k, cache_a_v, cache_c_k, cache_c_v, state_ssm_re, state_ssm_im,
           c, c_ctx, w_mod, b_mod, norm_pre, norm_post, ffn_gate, ffn_up, ffn_down, w_in, w_out,
           q_norm, k_norm, sink, s5_lam_re, s5_lam_im, s5_log_step, s5_b_re, s5_b_im, s5_c_re, s5_c_im,
           s5_d, w_glu, b_glu):
    cvec8 = jnp.concatenate([c_ctx[None, :], c, jnp.zeros((8 - 1 - LAT_B, D_MODEL), F32)], axis=0)
    b_mod3 = b_mod.reshape(DEPTH, 1, N_MOD * D_MODEL)
    norm_pre4 = norm_pre.reshape(DEPTH, 3, 1, D_MODEL)
    norm_post4 = norm_post.reshape(DEPTH, 3, 1, D_MODEL)
    q_norm3 = q_norm.reshape(DEPTH, 1, HD)
    k_norm3 = k_norm.reshape(DEPTH, 1, HD)
    s5_d3 = s5_d.reshape(DEPTH, 1, S5_CH)
    b_glu3 = b_glu.reshape(DEPTH, 1, S5_CH)
    cos_t, sin_t = _rope_tables()
    mods_head = _modulation(cvec8, w_mod, b_mod3, 0, MOD_HEAD)
    w1, w2, a8, mods_mid, w_in_bf = _s5_params(s5_lam_re, s5_lam_im, s5_log_step, s5_b_re, s5_b_im,
                                               s5_c_re, s5_c_im, cvec8, w_mod, b_mod3, w_in)
    mods_tail = jnp.zeros((8, N_MOD * D_MODEL - MOD_MID), F32)
    table = lambda tail: jnp.concatenate([mods_head, mods_mid, tail], axis=1).reshape(8, 1, N_MOD * D_MODEL)
    mods3 = table(mods_tail)
    kv4 = lambda a: a.reshape(LAT_B, DEPTH, PAST, A_KV * HD)
    cak, cav, cck, ccv = kv4(cache_a_k), kv4(cache_a_v), kv4(cache_c_k), kv4(cache_c_v)
    h0_all = jnp.stack([state_ssm_re[:, :, 0], state_ssm_im[:, :, 0],
                        state_ssm_re[:, :, 1], state_ssm_im[:, :, 1]], axis=0)
    h0_all = h0_all.reshape(4, LAT_B, DEPTH, S5_OCT, S5_ST).transpose(2, 3, 0, 1, 4)

    ffn_w = (norm_pre4, norm_post4, ffn_gate, ffn_up, ffn_down)
    new_caches = ()
    new_state = []
    xs = [x_prompt.reshape(T_CTX, D_MODEL), x_sample.reshape(T_LAT, D_MODEL)]
    for l in range(DEPTH):
        x = _ffn(xs, mods3, *ffn_w, l, 0)

        if l == 0:
            qkvu, uf, *new_caches, mods_tail = _inproj(
                x, mods3, norm_pre4, q_norm3, k_norm3, cos_t, sin_t, w_in_bf, l, (),
                mod_args=(cvec8, w_mod, b_mod3), mod_cols=(MOD_MID, N_MOD * D_MODEL))
            mods3 = table(mods_tail)
        else:
            qkvu, uf, *new_caches = _inproj(x, mods3, norm_pre4, q_norm3, k_norm3, cos_t, sin_t, w_in_bf, l,
                                            new_caches)
        a_ctx, c_ctx = _attn_ctx(qkvu, sink[l], l)
        if l < DEPTH - 1:
            a_lat, mods_next, w_in_next = _attn_lat_a(qkvu, cak, cav, l, next_mod=(cvec8, w_mod, b_mod3, w_in))
        else:
            a_lat = _attn_lat_a(qkvu, cak, cav, l)
        c_lat, w_out_bf = _attn_lat_c(qkvu, sink[l], cck, ccv, w_out, l)
        y, hfin = _s5_mix(uf, w1, w2, a8, h0_all[l], l)
        x = _outproj(x, a_ctx, a_lat, c_ctx, c_lat, y, uf, s5_d3, w_glu, b_glu3, mods3, norm_post4, w_out_bf, l)
        if l < DEPTH - 1:
            xs = [_ffn([x], mods3, *ffn_w, l, 1)]
            mods3 = mods_next.reshape(8, 1, N_MOD * D_MODEL)
            w_in_bf = w_in_next
        else:
            y_prompt, y_sample = _ffn([x], mods3, *ffn_w, l, 1, split_out=True)

        hf = hfin.reshape(S5_OCT, 4, CTX_B, S5_OG, S5_P).transpose(1, 2, 0, 3, 4).reshape(4, CTX_B, S5_G, S5_P)
        new_state.append((jnp.stack([hf[0], hf[2]], axis=1), jnp.stack([hf[1], hf[3]], axis=1)))

    y_prompt = y_prompt.reshape(CTX_B, CTX_L, D_MODEL)
    y_sample = y_sample.reshape(LAT_B, LAT_L, D_MODEL)
    caches = list(new_caches)
    st_re = jnp.stack([new_state[l][0] for l in range(DEPTH)], axis=1)
    st_im = jnp.stack([new_state[l][1] for l in range(DEPTH)], axis=1)
    return (y_prompt, y_sample, caches[0], caches[1], caches[2], caches[3], st_re, st_im)
```

```python
import functools
import math

import jax
import jax.numpy as jnp
from jax import lax
from jax.experimental import pallas as pl
from jax.experimental.pallas import tpu as pltpu

F32 = jnp.float32
BF16 = jnp.bfloat16

D_MODEL = 2048
CTX_B, CTX_L = 16, 256
LAT_B, LAT_L = 2, 2048
DEPTH = 2
PAST = 512
GRID_W = 64
HD = 128
A_HEADS, A_KV = 8, 2
C_HEADS, C_KV = 4, 2
WINDOW = 128
S5_GC = 16
S5_CH = 512
S5_G = 32
S5_P = 64
D_FF = 5632
N_MOD = 9
IN_WIDTH = 3072
ROPE_BASE = 10000.0
EPS = 1e-6
HALF_STEP = 0.5
NEG_INF = -1e30
SCALE = HD ** -0.5
LOG2E = math.log2(math.e)

T_CTX = CTX_B * CTX_L
T_LAT = LAT_B * LAT_L
T_ALL = T_CTX + T_LAT

COL_AQ, COL_AK, COL_AV = 0, 1024, 1280
COL_CQ, COL_CK, COL_CV = 1536, 2048, 2304
COL_U = 2560

S5_T = 8
S5_OCT = S5_CH // 128
S5_OG = S5_G // S5_OCT
S5_ROW = S5_T * 128
S5_ST = S5_OG * S5_P
S5_W1 = S5_ROW + 4 * S5_ST
CTX_NC = CTX_L // S5_T
LAT_NC = LAT_L // S5_T
S5_ROWS = CTX_NC * CTX_B
assert S5_ROWS == LAT_NC * LAT_B and T_CTX == T_LAT

V7X_VMEM_BYTES = 64 * 1024 * 1024
MIB = 1024 * 1024
VMEM_FFN = V7X_VMEM_BYTES - 2 * MIB
VMEM_BIG = V7X_VMEM_BYTES - 8 * MIB
VMEM_MID = V7X_VMEM_BYTES - 24 * MIB

BM = 1024
BM_OUT = 512
RC_OUT = 256
BF = 512
RC = 512
RCX = 256
N_XC = BM // RCX
FFN_PREFETCH_STEP = 2
BN_MOD = 1024
MOD_HEAD = 3 * D_MODEL
MOD_MID = 5 * D_MODEL
BM_IN = 512
BQ_A = 512
BQ_C = 256
NQ_C = 8
NB_CTX = 8


def _dot(a, b):
    return jnp.dot(a, b, preferred_element_type=F32)


def _dot_nt(a, b, precision=None):
    return lax.dot_general(a, b, (((1,), (1,)), ((), ())), preferred_element_type=F32, precision=precision)


def _dot_nt_split(a, b):
    ah = a.astype(BF16)
    bh = b.astype(BF16)
    al = (a - ah.astype(F32)).astype(BF16)
    bl = (b - bh.astype(F32)).astype(BF16)
    return _dot_nt(ah, bh) + (_dot_nt(ah, bl) + _dot_nt(al, bh))


def _rms(x, g):
    return x * lax.rsqrt(jnp.mean(x * x, axis=-1, keepdims=True) + EPS) * g


def _mod_index(i, bm):
    nct = T_CTX // bm
    return jnp.where(i < nct, 0, 1 + (i - nct) // (LAT_L // bm))


def _mod_columns(c_ref, w_ref, b_ref, o_ref):
    c = c_ref[...]
    s = (c * jax.nn.sigmoid(c)).astype(BF16)
    o_ref[...] = _dot(s, w_ref[...].astype(BF16)) + b_ref[...]


def _modulation(cvec8, w_mod, b_mod3, l, n):
    return pl.pallas_call(
        _mod_columns,
        grid=(n // BN_MOD,),
        in_specs=[
            pl.BlockSpec((8, D_MODEL), lambda j: (0, 0)),
            pl.BlockSpec((None, D_MODEL, BN_MOD), lambda j: (l, 0, j)),
            pl.BlockSpec((None, 1, BN_MOD), lambda j: (l, 0, j)),
        ],
        out_specs=pl.BlockSpec((8, BN_MOD), lambda j: (0, j)),
        out_shape=jax.ShapeDtypeStruct((8, n), F32),
        compiler_params=pltpu.CompilerParams(
            dimension_semantics=("arbitrary",), vmem_limit_bytes=VMEM_MID),
        name=f"modulation_l{l}",
    )(cvec8, w_mod, b_mod3)


def _ffn_kernel(*refs, mo, n_x, n_out):
    x_hbms = refs[:n_x]
    mod_ref, npre_ref, npost_ref, wg_ref, wu_ref, wd_ref = refs[n_x:n_x + 6]
    out_hbms = refs[n_x + 6:n_x + 6 + n_out]
    acc_ref, xc_ref, h_ref, sem_x, sem_c, sem_o = refs[n_x + 6 + n_out:]
    i = pl.program_id(0)
    j = pl.program_id(1)
    n_tiles = T_ALL // BM
    last = D_FF // BF - 1
    slot = i % 2
    nq = RC // RCX
    nct = T_CTX // BM

    def per_stream(arrays, tile, fn):
        if len(arrays) == 1:
            fn(arrays[0], pl.multiple_of(tile * BM, BM))
        else:
            pl.when(tile < nct)(lambda: fn(arrays[0], pl.multiple_of(tile * BM, BM)))
            pl.when(tile >= nct)(lambda: fn(arrays[1], pl.multiple_of((tile - nct) * BM, BM)))

    def x_tile_copy(arr, row0, sl):
        return pltpu.make_async_copy(arr.at[pl.ds(row0, BM), :], acc_ref.at[sl], sem_x)

    def x_chunk_copy(arr, row0, cs):
        return pltpu.make_async_copy(arr.at[pl.ds(row0, RCX), :], xc_ref.at[cs], sem_c.at[cs])

    def out_copy(arr, row0, sl):
        return pltpu.make_async_copy(acc_ref.at[sl], arr.at[pl.ds(row0, BM), :], sem_o.at[sl])

    wait_x_tile = lambda sl: x_tile_copy(x_hbms[0], 0, sl).wait()
    wait_x_chunk = lambda cs: x_chunk_copy(x_hbms[0], 0, cs).wait()
    wait_out = lambda sl: out_copy(out_hbms[0], 0, sl).wait()

    def start_x_chunk(q):
        per_stream(x_hbms, i, lambda arr, row0: x_chunk_copy(
            arr, pl.multiple_of(row0 + q * RCX, RCX), q).start())

    @pl.when(jnp.logical_and(i == 0, j == 0))
    def _():
        x_tile_copy(x_hbms[0], 0, 0).start()

    @pl.when(j == 0)
    def _():
        wait_x_tile(slot)

    @pl.when(j == FFN_PREFETCH_STEP)
    def _():
        pl.when(i >= 1)(lambda: wait_out(1 - slot))

        @pl.when(i + 1 < n_tiles)
        def _():
            per_stream(x_hbms, i + 1, lambda arr, row0: x_tile_copy(arr, row0, 1 - slot).start())

    @pl.when(j == last - 1)
    def _():
        for q in range(N_XC):
            start_x_chunk(q)

    def step(first, final):
        if final:
            for q in range(N_XC):
                wait_x_chunk(q)
        acc_slot = acc_ref.at[slot]
        wg = wg_ref[...].astype(BF16)
        wu = wu_ref[...].astype(BF16)
        wd = wd_ref[...].astype(BF16)
        for r in range(BM // RC):
            rows = slice(r * RC, (r + 1) * RC)
            if first:
                shift = mod_ref[:, mo * D_MODEL:(mo + 1) * D_MODEL]
                scale = mod_ref[:, (mo + 1) * D_MODEL:(mo + 2) * D_MODEL]
                hn = _rms(acc_slot[rows, :], npre_ref[...])
                h = (hn * (1.0 + scale) + shift).astype(BF16)
                h_ref[rows, :] = h
            else:
                h = h_ref[rows, :]
            g = _dot(h, wg)
            u = _dot(h, wu)
            a = (g * jax.nn.sigmoid(g) * u).astype(BF16)
            acc = _dot(a, wd)
            if not first:
                acc = acc_slot[rows, :] + acc
            if not final:
                acc_slot[rows, :] = acc
                continue
            gate = mod_ref[:, (mo + 2) * D_MODEL:(mo + 3) * D_MODEL]
            for qq in range(nq):
                sub = slice(qq * RCX, (qq + 1) * RCX)
                y = xc_ref[r * nq + qq] + (HALF_STEP * gate) * _rms(acc[sub, :], npost_ref[...])
                acc_slot[r * RC + qq * RCX:r * RC + (qq + 1) * RCX, :] = y
        if final:
            per_stream(out_hbms, i, lambda arr, row0: out_copy(arr, row0, slot).start())

    pl.when(j == 0)(functools.partial(step, True, False))
    pl.when(jnp.logical_and(j > 0, j < last))(functools.partial(step, False, False))
    pl.when(j == last)(functools.partial(step, False, True))

    @pl.when(jnp.logical_and(i == n_tiles - 1, j == last))
    def _():
        wait_out(slot)


def _ffn(xs, mods3, norm_pre4, norm_post4, ffn_gate, ffn_up, ffn_down, l, s, *, split_out=False):
    mo = 6 * s
    ni = 2 * s
    inner_specs = [
        pl.BlockSpec((None, 1, N_MOD * D_MODEL), lambda i, j: (_mod_index(i, BM), 0, 0)),
        pl.BlockSpec((None, None, 1, D_MODEL), lambda i, j: (l, ni, 0, 0)),
        pl.BlockSpec((None, None, 1, D_MODEL), lambda i, j: (l, ni, 0, 0)),
        pl.BlockSpec((None, None, D_MODEL, BF), lambda i, j: (l, s, 0, j)),
        pl.BlockSpec((None, None, D_MODEL, BF), lambda i, j: (l, s, 0, j)),
        pl.BlockSpec((None, None, BF, D_MODEL), lambda i, j: (l, s, j, 0)),
    ]
    out_rows = (T_CTX, T_LAT) if split_out else (T_ALL,)
    n_x, n_out = len(xs), len(out_rows)

    def outer(*refs):
        x_hbms = refs[:n_x]
        streamed = refs[n_x:n_x + 6]
        rest = refs[n_x + 6:]

        def body(*blocks):
            _ffn_kernel(*x_hbms, *blocks, *rest, mo=mo, n_x=n_x, n_out=n_out)

        pltpu.emit_pipeline(body, grid=(T_ALL // BM, D_FF // BF), in_specs=inner_specs)(*streamed)

    outs = pl.pallas_call(
        outer,
        in_specs=[pl.BlockSpec(memory_space=pl.ANY)] * (n_x + 6),
        out_specs=[pl.BlockSpec(memory_space=pl.ANY)] * len(out_rows),
        out_shape=[jax.ShapeDtypeStruct((r, D_MODEL), F32) for r in out_rows],
        scratch_shapes=[
            pltpu.VMEM((2, BM, D_MODEL), F32),
            pltpu.VMEM((N_XC, RCX, D_MODEL), F32),
            pltpu.VMEM((BM, D_MODEL), BF16),
            pltpu.SemaphoreType.DMA(()),
            pltpu.SemaphoreType.DMA((N_XC,)),
            pltpu.SemaphoreType.DMA((2,)),
        ],
        compiler_params=pltpu.CompilerParams(vmem_limit_bytes=VMEM_FFN),
        name=f"ffn_l{l}_s{s}",
    )(*xs, mods3, norm_pre4, norm_post4, ffn_gate, ffn_up, ffn_down)
    return outs if split_out else outs[0]


def _rope(y, cos, sins):
    lane = lax.broadcasted_iota(jnp.int32, y.shape, 1)
    first = (lane & 63) < 32
    partner = jnp.where(first, pltpu.roll(y, 96, 1), pltpu.roll(y, 32, 1))
    return y * cos + partner * sins


_IN_SEGMENTS = (
    (COL_AQ, A_HEADS, "q", True, None),
    (COL_AK, A_KV, "k", True, 0),
    (COL_AV, A_KV, None, False, 1),
    (COL_CQ, C_HEADS, None, True, None),
    (COL_CK, C_KV, None, True, 2),
    (COL_CV, C_KV, None, False, 3),
)


def _inproj_kernel(x_ref, mod_ref, npre_ref, qn_ref, kn_ref, cos_ref, sin_ref, w_ref, *rest, first_layer):
    nb = BM_IN // CTX_L
    if first_layer:
        c_ref, wmod_ref, bmod_ref, qkvu_ref, uf_ref, *cache_full, modo_ref = rest
        cache_refs = [c.at[:, 0] for c in cache_full]
        _mod_columns(c_ref, wmod_ref, bmod_ref, modo_ref)
    else:
        qkvu_ref, uf_ref, *cache_refs = rest[4:]
        cache_full = ()

    def body(lat):
        shift = mod_ref[:, 3 * D_MODEL:4 * D_MODEL]
        scale = mod_ref[:, 4 * D_MODEL:5 * D_MODEL]
        h = (_rms(x_ref[...], npre_ref[...]) * (1.0 + scale) + shift).astype(BF16)
        for col0, heads, norm, rot, cache in _IN_SEGMENTS:
            p = _dot(h, w_ref[:, col0:col0 + heads * HD])
            for k in range(heads):
                y = p[:, k * HD:(k + 1) * HD]
                if norm == "q":
                    y = _rms(y, qn_ref[...])
                elif norm == "k":
                    y = _rms(y, kn_ref[...])
                if rot and lat:
                    y = _rope(y, cos_ref[...], sin_ref[...])
                qkvu_ref[:, col0 + k * HD:col0 + (k + 1) * HD] = y.astype(BF16)
                if cache is not None and not lat:
                    cache_refs[cache][:, :, k, :] = y.reshape(nb, CTX_L, HD)
        u = _dot(h, w_ref[:, COL_U:])
        uf_ref[...] = u
        qkvu_ref[:, COL_U:] = u.astype(BF16)
        if first_layer and not lat:
            for c in cache_full:
                c[:, 1:] = jnp.zeros((nb, DEPTH - 1, CTX_L, A_KV, HD), F32)

    is_lat = pl.program_id(0) >= T_CTX // BM_IN
    pl.when(is_lat)(functools.partial(body, True))
    pl.when(jnp.logical_not(is_lat))(functools.partial(body, False))


def _inproj(x, mods3, norm_pre4, q_norm3, k_norm3, cos_t, sin_t, w_in_bf, l, prev_caches, mod_args=None,
            mod_cols=None):
    bm = BM_IN
    nct = T_CTX // bm
    nb = bm // CTX_L
    tab_idx = lambda i: (jnp.maximum(i - nct, 0) % (LAT_L // bm), 0)
    cache_shape = jax.ShapeDtypeStruct((CTX_B, DEPTH, CTX_L, A_KV, HD), F32)
    first_layer = not prev_caches
    if first_layer:
        cache_spec = pl.BlockSpec((nb, DEPTH, CTX_L, A_KV, HD), lambda i: (jnp.minimum(i, nct - 1), 0, 0, 0, 0))
    else:
        cache_spec = pl.BlockSpec((nb, None, CTX_L, A_KV, HD), lambda i: (jnp.minimum(i, nct - 1), l, 0, 0, 0))
    n_in = 8
    extra_in, extra_out, extra_shape, extra_args = [], [], [], []
    if first_layer:
        lo, hi = mod_cols
        steps = T_ALL // bm
        bn = (hi - lo) // steps
        extra_in = [
            pl.BlockSpec((8, D_MODEL), lambda i: (0, 0)),
            pl.BlockSpec((None, D_MODEL, bn), lambda i: (l, 0, lo // bn + i)),
            pl.BlockSpec((None, 1, bn), lambda i: (l, 0, lo // bn + i)),
        ]
        extra_out = [pl.BlockSpec((8, bn), lambda i: (0, i))]
        extra_shape = [jax.ShapeDtypeStruct((8, hi - lo), F32)]
        extra_args = list(mod_args)
    return pl.pallas_call(
        functools.partial(_inproj_kernel, first_layer=first_layer),
        grid=(T_ALL // bm,),
        in_specs=[
            pl.BlockSpec((bm, D_MODEL), lambda i: (i, 0)),
            pl.BlockSpec((None, 1, N_MOD * D_MODEL), lambda i: (_mod_index(i, bm), 0, 0)),
            pl.BlockSpec((None, None, 1, D_MODEL), lambda i: (l, 1, 0, 0)),
            pl.BlockSpec((None, 1, HD), lambda i: (l, 0, 0)),
            pl.BlockSpec((None, 1, HD), lambda i: (l, 0, 0)),
            pl.BlockSpec((bm, HD), tab_idx),
            pl.BlockSpec((bm, HD), tab_idx),
            pl.BlockSpec((D_MODEL, IN_WIDTH), lambda i: (0, 0), pipeline_mode=pl.Buffered(1)),
        ] + extra_in + [pl.BlockSpec(memory_space=pl.ANY)] * len(prev_caches),
        out_specs=[
            pl.BlockSpec((bm, IN_WIDTH), lambda i: (i, 0)),
            pl.BlockSpec((bm, S5_CH), lambda i: (i, 0)),
        ] + [cache_spec] * 4 + extra_out,
        out_shape=[
            jax.ShapeDtypeStruct((T_ALL, IN_WIDTH), BF16),
            jax.ShapeDtypeStruct((T_ALL, S5_CH), F32),
        ] + [cache_shape] * 4 + extra_shape,
        input_output_aliases={n_in + k: 2 + k for k in range(len(prev_caches))},
        compiler_params=pltpu.CompilerParams(
            dimension_semantics=("arbitrary",), vmem_limit_bytes=VMEM_BIG),
        name=f"inproj_l{l}",
    )(x, mods3, norm_pre4, q_norm3, k_norm3, cos_t, sin_t, w_in_bf, *extra_args, *prev_caches)


def _softmax_pv(dots, values, sink=None):
    m = functools.reduce(jnp.maximum, [jnp.max(d, axis=-1, keepdims=True) for d in dots]) * SCALE
    if sink is not None:
        m = jnp.maximum(m, sink)
    m2 = m * LOG2E
    ps = [jnp.exp2(d * (SCALE * LOG2E) - m2) for d in dots]
    den = functools.reduce(jnp.add, [jnp.sum(p, axis=-1, keepdims=True) for p in ps])
    if sink is not None:
        den = den + jnp.exp2(sink * LOG2E - m2)
    o = functools.reduce(jnp.add, [_dot(p.astype(BF16), v) for p, v in zip(ps, values)])
    return o / den


def _attn_ctx_kernel(sink_ref, aq_ref, ak_ref, av_ref, cq_ref, ck_ref, cv_ref, ao_ref, co_ref):
    kv = pl.program_id(1)
    g = C_HEADS // C_KV
    for b in range(NB_CTX):
        rows = slice(b * CTX_L, (b + 1) * CTX_L)
        k = ak_ref[rows, :]
        v = av_ref[rows, :]
        for h in range(A_HEADS // A_KV):
            cols = slice(h * HD, (h + 1) * HD)
            s = _dot_nt(aq_ref[rows, cols], k)
            ao_ref[rows, cols] = _softmax_pv([s], [v]).astype(BF16)
        k = ck_ref[rows, :]
        v = cv_ref[rows, :]
        for h in range(g):
            cols = slice(h * HD, (h + 1) * HD)
            s = _dot_nt(cq_ref[rows, cols], k)
            co_ref[rows, cols] = _softmax_pv([s], [v], sink_ref[kv * g + h]).astype(BF16)


def _attn_ctx(qkvu, sink_l, l):
    ga = A_HEADS // A_KV * HD
    gc = C_HEADS // C_KV * HD
    rows = NB_CTX * CTX_L
    blk = lambda width, col0: pl.BlockSpec((rows, width), lambda b, k: (b, col0 // width + k))
    return pl.pallas_call(
        _attn_ctx_kernel,
        grid=(CTX_B // NB_CTX, A_KV),
        in_specs=[
            pl.BlockSpec(memory_space=pltpu.SMEM),
            blk(ga, COL_AQ), blk(HD, COL_AK), blk(HD, COL_AV),
            blk(gc, COL_CQ), blk(HD, COL_CK), blk(HD, COL_CV),
        ],
        out_specs=[
            pl.BlockSpec((rows, ga), lambda b, k: (b, k)),
            pl.BlockSpec((rows, gc), lambda b, k: (b, k)),
        ],
        out_shape=[
            jax.ShapeDtypeStruct((T_CTX, A_HEADS * HD), BF16),
            jax.ShapeDtypeStruct((T_CTX, C_HEADS * HD), BF16),
        ],
        compiler_params=pltpu.CompilerParams(dimension_semantics=("arbitrary", "arbitrary")),
        name=f"attn_ctx_l{l}",
    )(sink_l, qkvu, qkvu, qkvu, qkvu, qkvu, qkvu)


def _attn_lat_a_kernel(q_ref, k_ref, v_ref, kc_ref, vc_ref, *rest):
    o_ref = rest[0] if len(rest) == 1 else rest[4]
    k = k_ref[...]
    v = v_ref[...]
    kc = kc_ref[...].astype(BF16)
    vc = vc_ref[...].astype(BF16)
    for h in range(A_HEADS // A_KV):
        cols = slice(h * HD, (h + 1) * HD)
        q = q_ref[:, cols]
        s1 = _dot_nt(q, k)
        s2 = _dot_nt(q, kc)
        o_ref[:, cols] = _softmax_pv([s1, s2], [v, vc]).astype(BF16)
    if len(rest) > 1:
        c_ref, w_ref, b_ref, win_ref, _, mod_ref, winb_ref = rest
        _mod_columns(c_ref, w_ref, b_ref, mod_ref)
        winb_ref[...] = win_ref[...].astype(BF16)


def _attn_lat_a(qkvu, cache_k4, cache_v4, l, next_mod=None):
    ga = A_HEADS // A_KV * HD
    nq = LAT_L // BQ_A
    row0 = T_CTX // BQ_A
    lat_blk = T_CTX // LAT_L
    in_specs = [
        pl.BlockSpec((BQ_A, ga), lambda b, k, q: (row0 + b * nq + q, COL_AQ // ga + k)),
        pl.BlockSpec((LAT_L, HD), lambda b, k, q: (lat_blk + b, COL_AK // HD + k)),
        pl.BlockSpec((LAT_L, HD), lambda b, k, q: (lat_blk + b, COL_AV // HD + k)),
        pl.BlockSpec((None, None, PAST, HD), lambda b, k, q: (b, l, 0, k)),
        pl.BlockSpec((None, None, PAST, HD), lambda b, k, q: (b, l, 0, k)),
    ]
    out_specs = [pl.BlockSpec((BQ_A, ga), lambda b, k, q: (b * nq + q, k))]
    out_shape = [jax.ShapeDtypeStruct((T_LAT, A_HEADS * HD), BF16)]
    args = [qkvu, qkvu, qkvu, cache_k4, cache_v4]
    if next_mod is not None:
        n = N_MOD * D_MODEL
        steps = LAT_B * A_KV * nq
        bn = n // steps
        step = lambda b, k, q: (b * A_KV + k) * nq + q
        in_specs += [
            pl.BlockSpec((8, D_MODEL), lambda b, k, q: (0, 0)),
            pl.BlockSpec((None, D_MODEL, bn), lambda b, k, q: (l + 1, 0, step(b, k, q))),
            pl.BlockSpec((None, 1, bn), lambda b, k, q: (l + 1, 0, step(b, k, q))),
            pl.BlockSpec((None, D_MODEL // steps, IN_WIDTH), lambda b, k, q: (l + 1, step(b, k, q), 0)),
        ]
        out_specs += [
            pl.BlockSpec((8, bn), lambda b, k, q: (0, step(b, k, q))),
            pl.BlockSpec((D_MODEL // steps, IN_WIDTH), lambda b, k, q: (step(b, k, q), 0)),
        ]
        out_shape += [
            jax.ShapeDtypeStruct((8, n), F32),
            jax.ShapeDtypeStruct((D_MODEL, IN_WIDTH), BF16),
        ]
        args += list(next_mod)
    outs = pl.pallas_call(
        _attn_lat_a_kernel,
        grid=(LAT_B, A_KV, nq),
        in_specs=in_specs,
        out_specs=out_specs,
        out_shape=out_shape,
        compiler_params=pltpu.CompilerParams(
            dimension_semantics=("arbitrary", "arbitrary", "arbitrary"), vmem_limit_bytes=VMEM_BIG),
        name=f"attn_lat_a_l{l}",
    )(*args)
    return outs if next_mod is not None else outs[0]


def _attn_lat_c_kernel(sink_ref, q_ref, k_ref, v_ref, kc_ref, vc_ref, wout_ref, o_ref, woutb_ref):
    woutb_ref[...] = wout_ref[...].astype(BF16)
    kv = pl.program_id(1)
    span = BQ_C + 2 * WINDOW
    kc = kc_ref[...].astype(BF16)
    vc = vc_ref[...].astype(BF16)
    g = C_HEADS // C_KV
    for sub in range(NQ_C):
        n = pl.program_id(2) * NQ_C + sub
        rows = slice(sub * BQ_C, (sub + 1) * BQ_C)
        start = pl.multiple_of(jnp.clip(n * BQ_C - WINDOW, 0, LAT_L - span), WINDOW)
        kw = k_ref[pl.ds(start, span), :]
        vw = v_ref[pl.ds(start, span), :]
        qpos = n * BQ_C + lax.broadcasted_iota(jnp.int32, (BQ_C, span), 0)
        kpos = start + lax.broadcasted_iota(jnp.int32, (BQ_C, span), 1)
        valid = jnp.abs(qpos - kpos) <= WINDOW
        for h in range(g):
            cols = slice(h * HD, (h + 1) * HD)
            q = q_ref[rows, cols]
            s1 = jnp.where(valid, _dot_nt(q, kw), NEG_INF)
            s2 = _dot_nt(q, kc)
            o_ref[rows, cols] = _softmax_pv([s1, s2], [vw, vc], sink_ref[kv * g + h]).astype(BF16)


def _attn_lat_c(qkvu, sink_l, cache_k4, cache_v4, w_out, l):
    gc = C_HEADS // C_KV * HD
    bq = BQ_C * NQ_C
    nq = LAT_L // bq
    row0 = T_CTX // bq
    lat_blk = T_CTX // LAT_L
    steps = LAT_B * C_KV * nq
    step = lambda b, k, q: (b * C_KV + k) * nq + q
    return pl.pallas_call(
        _attn_lat_c_kernel,
        grid=(LAT_B, C_KV, nq),
        in_specs=[
            pl.BlockSpec(memory_space=pltpu.SMEM),
            pl.BlockSpec((bq, gc), lambda b, k, q: (row0 + b * nq + q, COL_CQ // gc + k)),
            pl.BlockSpec((LAT_L, HD), lambda b, k, q: (lat_blk + b, COL_CK // HD + k)),
            pl.BlockSpec((LAT_L, HD), lambda b, k, q: (lat_blk + b, COL_CV // HD + k)),
            pl.BlockSpec((None, None, PAST, HD), lambda b, k, q: (b, l, 0, k)),
            pl.BlockSpec((None, None, PAST, HD), lambda b, k, q: (b, l, 0, k)),
            pl.BlockSpec((None, D_MODEL // steps, D_MODEL), lambda b, k, q: (l, step(b, k, q), 0)),
        ],
        out_specs=[
            pl.BlockSpec((bq, gc), lambda b, k, q: (b * nq + q, k)),
            pl.BlockSpec((D_MODEL // steps, D_MODEL), lambda b, k, q: (step(b, k, q), 0)),
        ],
        out_shape=[
            jax.ShapeDtypeStruct((T_LAT, C_HEADS * HD), BF16),
            jax.ShapeDtypeStruct((D_MODEL, D_MODEL), BF16),
        ],
        compiler_params=pltpu.CompilerParams(
            dimension_semantics=("arbitrary", "arbitrary", "arbitrary")),
        name=f"attn_lat_c_l{l}",
    )(sink_l, qkvu, qkvu, qkvu, cache_k4, cache_v4, w_out)


def _s5_param_kernel(rows_ref, bc_ref, c_ref, wmod_ref, bmod_ref, win_ref, w1_ref, w2_ref, a8_ref, mod_ref,
                     winb_ref):
    _mod_columns(c_ref, wmod_ref, bmod_ref, mod_ref)
    winb_ref[...] = win_ref[...].astype(BF16)
    a8_ref[...] = jnp.zeros_like(a8_ref)
    row_grp = lax.broadcasted_iota(jnp.int32, (128, S5_ST), 0) // S5_GC
    lane_grp = lax.broadcasted_iota(jnp.int32, (128, S5_ST), 1) // S5_P
    on_diag = row_grp == lane_grp

    def expand(a):
        return jnp.where(on_diag, jnp.concatenate([a] * S5_OG, axis=1), 0.0)

    taps = []
    for d in range(2):
        lr = rows_ref[0, d]
        li = rows_ref[1, d]
        dt = jnp.exp(rows_ref[2, d])
        mag = jnp.exp(lr * dt)
        ar = mag * jnp.cos(li * dt)
        ai = mag * jnp.sin(li * dt)
        den = lr * lr + li * li
        n_re = ar - 1.0
        f_re = (n_re * lr + ai * li) / den
        f_im = (ai * lr - n_re * li) / den
        pw = [(jnp.ones_like(ar), jnp.zeros_like(ar))]
        for _ in range(S5_T):
            pr, pi = pw[-1]
            pw.append((pr * ar - pi * ai, pr * ai + pi * ar))
        br, bi, cr, ci = (expand(bc_ref[k, d]) for k in range(4))
        bbr = f_re * br - f_im * bi
        bbi = f_re * bi + f_im * br
        xr_blocks, xi_blocks = [], []
        for t in range(S5_T):
            rows = slice(t * 128, (t + 1) * 128)
            pr, pi = pw[S5_T - 1 - t if d == 0 else t]
            xr = pr * bbr - pi * bbi
            xi = pr * bbi + pi * bbr
            c0 = S5_ROW + 2 * S5_ST * d
            w1_ref[rows, c0:c0 + S5_ST] = xr.astype(BF16)
            w1_ref[rows, c0 + S5_ST:c0 + 2 * S5_ST] = xi.astype(BF16)
            xr_blocks.append(xr)
            xi_blocks.append(xi)
            pr, pi = pw[t + 1 if d == 0 else S5_T - t]
            c0 = 2 * S5_ST * d
            w2_ref[rows, c0:c0 + S5_ST] = (cr * pr - ci * pi).astype(BF16)
            w2_ref[rows, c0 + S5_ST:c0 + 2 * S5_ST] = (-(cr * pi + ci * pr)).astype(BF16)
        a8_ref[2 * d:2 * d + 1, :] = pw[S5_T][0]
        a8_ref[2 * d + 1:2 * d + 2, :] = pw[S5_T][1]
        xr_all = jnp.concatenate(xr_blocks, axis=0)
        xi_all = jnp.concatenate(xi_blocks, axis=0)
        taps.append(_dot_nt_split(xr_all, cr) - _dot_nt_split(xi_all, ci))
    tf, tb = taps
    for t in range(S5_T):
        for t2 in range(S5_T):
            blk = None
            if t2 >= t:
                r0 = (S5_T - 1 - (t2 - t)) * 128
                blk = tf[r0:r0 + 128, :]
            if t2 <= t:
                r0 = (t - t2) * 128
                b2 = tb[r0:r0 + 128, :]
                blk = b2 if blk is None else blk + b2
            w1_ref[t * 128:(t + 1) * 128, t2 * 128:(t2 + 1) * 128] = blk.astype(BF16)


def _s5_params(s5_lam_re, s5_lam_im, s5_log_step, s5_b_re, s5_b_im, s5_c_re, s5_c_im, cvec8, w_mod, b_mod3, w_in):
    rows = jnp.stack([s5_lam_re, s5_lam_im, jnp.repeat(s5_log_step[..., None], S5_P, axis=-1)], axis=0)
    rows = rows.reshape(3, DEPTH, 2, S5_OCT, 1, S5_ST)
    bc = jnp.stack([jnp.swapaxes(s5_b_re, -1, -2), jnp.swapaxes(s5_b_im, -1, -2), s5_c_re, s5_c_im], axis=0)
    bc = bc.reshape(4, DEPTH, 2, S5_OCT, 128, S5_P)
    spec = lambda n, r, c: pl.BlockSpec((n, None, 2, None, r, c), lambda l, s: (0, l, 0, s, 0, 0))
    steps = DEPTH * S5_OCT
    step = lambda l, s: l * S5_OCT + s
    n_tail = MOD_MID - MOD_HEAD
    bn = n_tail // steps
    br = D_MODEL // steps
    return pl.pallas_call(
        _s5_param_kernel,
        grid=(DEPTH, S5_OCT),
        in_specs=[
            spec(3, 1, S5_ST), spec(4, 128, S5_P),
            pl.BlockSpec((8, D_MODEL), lambda l, s: (0, 0)),
            pl.BlockSpec((None, D_MODEL, bn), lambda l, s: (0, 0, MOD_HEAD // bn + step(l, s))),
            pl.BlockSpec((None, 1, bn), lambda l, s: (0, 0, MOD_HEAD // bn + step(l, s))),
            pl.BlockSpec((None, br, IN_WIDTH), lambda l, s: (0, step(l, s), 0)),
        ],
        out_specs=[
            pl.BlockSpec((None, None, S5_ROW, S5_W1), lambda l, s: (l, s, 0, 0)),
            pl.BlockSpec((None, None, S5_ROW, 4 * S5_ST), lambda l, s: (l, s, 0, 0)),
            pl.BlockSpec((None, None, 8, S5_ST), lambda l, s: (l, s, 0, 0)),
            pl.BlockSpec((8, bn), lambda l, s: (0, step(l, s))),
            pl.BlockSpec((br, IN_WIDTH), lambda l, s: (step(l, s), 0)),
        ],
        out_shape=[
            jax.ShapeDtypeStruct((DEPTH, S5_OCT, S5_ROW, S5_W1), BF16),
            jax.ShapeDtypeStruct((DEPTH, S5_OCT, S5_ROW, 4 * S5_ST), BF16),
            jax.ShapeDtypeStruct((DEPTH, S5_OCT, 8, S5_ST), F32),
            jax.ShapeDtypeStruct((8, n_tail), F32),
            jax.ShapeDtypeStruct((D_MODEL, IN_WIDTH), BF16),
        ],
        compiler_params=pltpu.CompilerParams(
            dimension_semantics=("arbitrary", "arbitrary"), vmem_limit_bytes=VMEM_BIG),
        name="s5_params",
    )(rows, bc, cvec8, w_mod, b_mod3, w_in)


def _s5_mix_kernel(u_ref, w1_ref, w2_ref, a8_ref, h0_ref, y_ref, hfin_ref, lhs_ref, a_ref, hp_ref, y8_ref):
    def stream(nb, nc, is_ctx):
        for b in range(nb):
            for t in range(S5_T):
                lhs_ref[t, pl.ds(b, nc, stride=nb), :] = u_ref[pl.ds(b * nc * S5_T + t, nc, stride=S5_T), :]
        lhs = jnp.concatenate([lhs_ref[t] for t in range(S5_T)], axis=1).astype(BF16)
        a_ref[...] = _dot(lhs, w1_ref[:, S5_ROW:])
        y_intra = _dot(lhs, w1_ref[:, 0:S5_ROW])

        coef = [a8_ref[k:k + 1, :] for k in range(4)]
        if is_ctx:
            init = tuple(jnp.zeros((nb, S5_ST), F32) for _ in range(4))
        else:
            init = tuple(h0_ref[k] for k in range(4))

        def step(c, st):
            new = []
            for d in range(2):
                cc = c if d == 0 else nc - 1 - c
                rows = slice(cc * nb, (cc + 1) * nb)
                hr, hi = st[2 * d], st[2 * d + 1]
                hp_ref[rows, 2 * S5_ST * d:2 * S5_ST * d + S5_ST] = hr
                hp_ref[rows, 2 * S5_ST * d + S5_ST:2 * S5_ST * (d + 1)] = hi
                c0 = 2 * S5_ST * d
                gr = a_ref[rows, c0:c0 + S5_ST]
                gi = a_ref[rows, c0 + S5_ST:c0 + 2 * S5_ST]
                ar, ai = coef[2 * d], coef[2 * d + 1]
                new += [ar * hr - ai * hi + gr, ar * hi + ai * hr + gi]
            return tuple(new)

        fin = init
        for c in range(nc):
            fin = step(c, fin)
        if is_ctx:
            for k in range(4):
                hfin_ref[k] = fin[k]

        y8 = y_intra + _dot_nt(hp_ref[...].astype(BF16), w2_ref[...])
        for t in range(S5_T):
            y8_ref[t] = y8[:, t * 128:(t + 1) * 128]
        for b in range(nb):
            for t in range(S5_T):
                y_ref[pl.ds(b * nc * S5_T + t, nc, stride=S5_T), :] = y8_ref[t, pl.ds(b, nc, stride=nb), :]

    pl.when(pl.program_id(1) == 0)(functools.partial(stream, CTX_B, CTX_NC, True))
    pl.when(pl.program_id(1) == 1)(functools.partial(stream, LAT_B, LAT_NC, False))


def _s5_mix(uf, w1, w2, a8, h0, l):
    return pl.pallas_call(
        _s5_mix_kernel,
        grid=(S5_OCT, 2),
        in_specs=[
            pl.BlockSpec((T_CTX, 128), lambda s, k: (k, s)),
            pl.BlockSpec((None, None, S5_ROW, S5_W1), lambda s, k: (l, s, 0, 0)),
            pl.BlockSpec((None, None, S5_ROW, 4 * S5_ST), lambda s, k: (l, s, 0, 0)),
            pl.BlockSpec((None, None, 8, S5_ST), lambda s, k: (l, s, 0, 0)),
            pl.BlockSpec((None, 4, LAT_B, S5_ST), lambda s, k: (s, 0, 0, 0)),
        ],
        out_specs=[
            pl.BlockSpec((T_CTX, 128), lambda s, k: (k, s)),
            pl.BlockSpec((None, 4, CTX_B, S5_ST), lambda s, k: (s, 0, 0, 0)),
        ],
        out_shape=[
            jax.ShapeDtypeStruct((T_ALL, S5_CH), F32),
            jax.ShapeDtypeStruct((S5_OCT, 4, CTX_B, S5_ST), F32),
        ],
        scratch_shapes=[
            pltpu.VMEM((S5_T, S5_ROWS, 128), F32),
            pltpu.VMEM((S5_ROWS, 4 * S5_ST), F32),
            pltpu.VMEM((S5_ROWS, 4 * S5_ST), F32),
            pltpu.VMEM((S5_T, S5_ROWS, 128), F32),
        ],
        compiler_params=pltpu.CompilerParams(
            dimension_semantics=("arbitrary", "arbitrary"), vmem_limit_bytes=VMEM_BIG),
        name=f"s5_mix_l{l}",
    )(uf, w1, w2, a8, h0)


def _s5_gate(y, u, d, w, b):
    y = y + d * u
    z = y * (0.5 * (1.0 + jnp.tanh(math.sqrt(2.0 / math.pi) * (y + 0.044715 * (y * y * y)))))
    t = _dot(z.astype(BF16), w) + b
    return (z * jax.nn.sigmoid(t)).astype(BF16)


def _outproj_kernel(x_ref, a_ctx_ref, a_lat_ref, c_ctx_ref, c_lat_ref, y_ref, u_ref, d_ref, wglu_ref, bglu_ref,
                    mod_ref, npost_ref, wb_ref, o_ref):
    na = A_HEADS * HD
    nc = na + C_HEADS * HD

    def body(a_ref, c_ref):
        gate = mod_ref[:, 5 * D_MODEL:6 * D_MODEL]
        wglu = wglu_ref[...].astype(BF16)
        for r in range(BM_OUT // RC_OUT):
            rows = slice(r * RC_OUT, (r + 1) * RC_OUT)
            s = _s5_gate(y_ref[rows, :], u_ref[rows, :], d_ref[...], wglu, bglu_ref[...])
            y = (_dot(a_ref[rows, :], wb_ref[0:na, :]) + _dot(c_ref[rows, :], wb_ref[na:nc, :])
                 + _dot(s, wb_ref[nc:, :]))
            o_ref[rows, :] = x_ref[rows, :] + gate * _rms(y, npost_ref[...])

    is_lat = pl.program_id(0) >= T_CTX // BM_OUT
    pl.when(jnp.logical_not(is_lat))(functools.partial(body, a_ctx_ref, c_ctx_ref))
    pl.when(is_lat)(functools.partial(body, a_lat_ref, c_lat_ref))


def _outproj(x, a_ctx, a_lat, c_ctx, c_lat, y_s5, uf, s5_d3, w_glu, b_glu3, mods3, norm_post4, w_out_bf, l):
    bm = BM_OUT
    nct = T_CTX // bm
    ctx_idx = lambda i: (jnp.minimum(i, nct - 1), 0)
    lat_idx = lambda i: (jnp.maximum(i - nct, 0), 0)
    inner_specs = [
            pl.BlockSpec((bm, D_MODEL), lambda i: (i, 0)),
            pl.BlockSpec((bm, A_HEADS * HD), ctx_idx),
            pl.BlockSpec((bm, A_HEADS * HD), lat_idx),
            pl.BlockSpec((bm, C_HEADS * HD), ctx_idx),
            pl.BlockSpec((bm, C_HEADS * HD), lat_idx),
            pl.BlockSpec((bm, S5_CH), lambda i: (i, 0)),
            pl.BlockSpec((bm, S5_CH), lambda i: (i, 0)),
            pl.BlockSpec((None, 1, S5_CH), lambda i: (l, 0, 0)),
            pl.BlockSpec((None, S5_CH, S5_CH), lambda i: (l, 0, 0)),
            pl.BlockSpec((None, 1, S5_CH), lambda i: (l, 0, 0)),
            pl.BlockSpec((None, 1, N_MOD * D_MODEL), lambda i: (_mod_index(i, bm), 0, 0)),
            pl.BlockSpec((None, None, 1, D_MODEL), lambda i: (l, 1, 0, 0)),
            pl.BlockSpec((D_MODEL, D_MODEL), lambda i: (0, 0)),
    ]

    def outer(*refs):
        pltpu.emit_pipeline(
            _outproj_kernel, grid=(T_ALL // bm,), in_specs=inner_specs,
            out_specs=[pl.BlockSpec((bm, D_MODEL), lambda i: (i, 0))])(*refs)

    return pl.pallas_call(
        outer,
        in_specs=[pl.BlockSpec(memory_space=pl.ANY)] * len(inner_specs),
        out_specs=pl.BlockSpec(memory_space=pl.ANY),
        out_shape=jax.ShapeDtypeStruct((T_ALL, D_MODEL), F32),
        compiler_params=pltpu.CompilerParams(vmem_limit_bytes=VMEM_BIG),
        name=f"outproj_l{l}",
    )(x, a_ctx, a_lat, c_ctx, c_lat, y_s5, uf, s5_d3, w_glu, b_glu3, mods3, norm_post4, w_out_bf)


def _rope_tables():
    rows = LAT_L // GRID_W
    row = jnp.repeat(jnp.arange(rows, dtype=F32), GRID_W)
    col = jnp.tile(jnp.arange(GRID_W, dtype=F32), rows)
    axis_dim = HD // 2
    inv_freq = ROPE_BASE ** (-jnp.arange(0, axis_dim, 2, dtype=F32) / axis_dim)
    ang_row = row[:, None] * inv_freq
    ang_col = col[:, None] * inv_freq
    cr, sr = jnp.cos(ang_row), jnp.sin(ang_row)
    cc, sc = jnp.cos(ang_col), jnp.sin(ang_col)
    cos_t = jnp.concatenate([cr, cr, cc, cc], axis=-1)
    sin_t = jnp.concatenate([-sr, sr, -sc, sc], axis=-1)
    return cos_t, sin_t


def kernel(x_prompt, x_sample, cache_a_k, cache_a_v, cache_c_k, cache_c_v, state_ssm_re, state_ssm_im,
           c, c_ctx, w_mod, b_mod, norm_pre, norm_post, ffn_gate, ffn_up, ffn_down, w_in, w_out,
           q_norm, k_norm, sink, s5_lam_re, s5_lam_im, s5_log_step, s5_b_re, s5_b_im, s5_c_re, s5_c_im,
           s5_d, w_glu, b_glu):
    cvec8 = jnp.concatenate([c_ctx[None, :], c, jnp.zeros((8 - 1 - LAT_B, D_MODEL), F32)], axis=0)
    b_mod3 = b_mod.reshape(DEPTH, 1, N_MOD * D_MODEL)
    norm_pre4 = norm_pre.reshape(DEPTH, 3, 1, D_MODEL)
    norm_post4 = norm_post.reshape(DEPTH, 3, 1, D_MODEL)
    q_norm3 = q_norm.reshape(DEPTH, 1, HD)
    k_norm3 = k_norm.reshape(DEPTH, 1, HD)
    s5_d3 = s5_d.reshape(DEPTH, 1, S5_CH)
    b_glu3 = b_glu.reshape(DEPTH, 1, S5_CH)
    cos_t, sin_t = _rope_tables()
    mods_head = _modulation(cvec8, w_mod, b_mod3, 0, MOD_HEAD)
    w1, w2, a8, mods_mid, w_in_bf = _s5_params(s5_lam_re, s5_lam_im, s5_log_step, s5_b_re, s5_b_im,
                                               s5_c_re, s5_c_im, cvec8, w_mod, b_mod3, w_in)
    mods_tail = jnp.zeros((8, N_MOD * D_MODEL - MOD_MID), F32)
    table = lambda tail: jnp.concatenate([mods_head, mods_mid, tail], axis=1).reshape(8, 1, N_MOD * D_MODEL)
    mods3 = table(mods_tail)
    kv4 = lambda a: a.reshape(LAT_B, DEPTH, PAST, A_KV * HD)
    cak, cav, cck, ccv = kv4(cache_a_k), kv4(cache_a_v), kv4(cache_c_k), kv4(cache_c_v)
    h0_all = jnp.stack([state_ssm_re[:, :, 0], state_ssm_im[:, :, 0],
                        state_ssm_re[:, :, 1], state_ssm_im[:, :, 1]], axis=0)
    h0_all = h0_all.reshape(4, LAT_B, DEPTH, S5_OCT, S5_ST).transpose(2, 3, 0, 1, 4)

    ffn_w = (norm_pre4, norm_post4, ffn_gate, ffn_up, ffn_down)
    new_caches = ()
    new_state = []
    xs = [x_prompt.reshape(T_CTX, D_MODEL), x_sample.reshape(T_LAT, D_MODEL)]
    for l in range(DEPTH):
        x = _ffn(xs, mods3, *ffn_w, l, 0)

        if l == 0:
            qkvu, uf, *new_caches, mods_tail = _inproj(
                x, mods3, norm_pre4, q_norm3, k_norm3, cos_t, sin_t, w_in_bf, l, (),
                mod_args=(cvec8, w_mod, b_mod3), mod_cols=(MOD_MID, N_MOD * D_MODEL))
            mods3 = table(mods_tail)
        else:
            qkvu, uf, *new_caches = _inproj(x, mods3, norm_pre4, q_norm3, k_norm3, cos_t, sin_t, w_in_bf, l,
                                            new_caches)
        a_ctx, c_ctx = _attn_ctx(qkvu, sink[l], l)
        if l < DEPTH - 1:
            a_lat, mods_next, w_in_next = _attn_lat_a(qkvu, cak, cav, l, next_mod=(cvec8, w_mod, b_mod3, w_in))
        else:
            a_lat = _attn_lat_a(qkvu, cak, cav, l)
        c_lat, w_out_bf = _attn_lat_c(qkvu, sink[l], cck, ccv, w_out, l)
        y, hfin = _s5_mix(uf, w1, w2, a8, h0_all[l], l)
        x = _outproj(x, a_ctx, a_lat, c_ctx, c_lat, y, uf, s5_d3, w_glu, b_glu3, mods3, norm_post4, w_out_bf, l)
        if l < DEPTH - 1:
            xs = [_ffn([x], mods3, *ffn_w, l, 1)]
            mods3 = mods_next.reshape(8, 1, N_MOD * D_MODEL)
            w_in_bf = w_in_next
        else:
            y_prompt, y_sample = _ffn([x], mods3, *ffn_w, l, 1, split_out=True)

        hf = hfin.reshape(S5_OCT, 4, CTX_B, S5_OG, S5_P).transpose(1, 2, 0, 3, 4).reshape(4, CTX_B, S5_G, S5_P)
        new_state.append((jnp.stack([hf[0], hf[2]], axis=1), jnp.stack([hf[1], hf[3]], axis=1)))

    y_prompt = y_prompt.reshape(CTX_B, CTX_L, D_MODEL)
    y_sample = y_sample.reshape(LAT_B, LAT_L, D_MODEL)
    caches = list(new_caches)
    st_re = jnp.stack([new_state[l][0] for l in range(DEPTH)], axis=1)
    st_im = jnp.stack([new_state[l][1] for l in range(DEPTH)], axis=1)
    return (y_prompt, y_sample, caches[0], caches[1], caches[2], caches[3], st_re, st_im)
```

```python
import functools
import math

import jax
import jax.numpy as jnp
from jax import lax
from jax.experimental import pallas as pl
from jax.experimental.pallas import tpu as pltpu

F32 = jnp.float32
BF16 = jnp.bfloat16

D_MODEL = 2048
CTX_B, CTX_L = 16, 256
LAT_B, LAT_L = 2, 2048
DEPTH = 2
PAST = 512
GRID_W = 64
HD = 128
A_HEADS, A_KV = 8, 2
C_HEADS, C_KV = 4, 2
WINDOW = 128
S5_GC = 16
S5_CH = 512
S5_G = 32
S5_P = 64
D_FF = 5632
N_MOD = 9
IN_WIDTH = 3072
ROPE_BASE = 10000.0
EPS = 1e-6
HALF_STEP = 0.5
NEG_INF = -1e30
SCALE = HD ** -0.5
LOG2E = math.log2(math.e)

T_CTX = CTX_B * CTX_L
T_LAT = LAT_B * LAT_L
T_ALL = T_CTX + T_LAT

COL_AQ, COL_AK, COL_AV = 0, 1024, 1280
COL_CQ, COL_CK, COL_CV = 1536, 2048, 2304
COL_U = 2560

S5_T = 8
S5_OCT = S5_CH // 128
S5_OG = S5_G // S5_OCT
S5_ROW = S5_T * 128
S5_ST = S5_OG * S5_P
S5_W1 = S5_ROW + 4 * S5_ST
CTX_NC = CTX_L // S5_T
LAT_NC = LAT_L // S5_T
S5_ROWS = CTX_NC * CTX_B
assert S5_ROWS == LAT_NC * LAT_B and T_CTX == T_LAT

V7X_VMEM_BYTES = 64 * 1024 * 1024
MIB = 1024 * 1024
VMEM_FFN = V7X_VMEM_BYTES - 2 * MIB
VMEM_BIG = V7X_VMEM_BYTES - 8 * MIB
VMEM_MID = V7X_VMEM_BYTES - 24 * MIB

BM = 1024
BM_OUT = 512
RC_OUT = 256
BF = 512
RC = 512
RCX = 256
N_XC = BM // RCX
FFN_PREFETCH_STEP = 2
BN_MOD = 1024
MOD_HEAD = 3 * D_MODEL
MOD_MID = 5 * D_MODEL
BM_IN = 512
BQ_A = 512
BQ_C = 256
NQ_C = 8
NB_CTX = 8


def _dot(a, b):
    return jnp.dot(a, b, preferred_element_type=F32)


def _dot_nt(a, b, precision=None):
    return lax.dot_general(a, b, (((1,), (1,)), ((), ())), preferred_element_type=F32, precision=precision)


def _dot_nt_split(a, b):
    ah = a.astype(BF16)
    bh = b.astype(BF16)
    al = (a - ah.astype(F32)).astype(BF16)
    bl = (b - bh.astype(F32)).astype(BF16)
    return _dot_nt(ah, bh) + (_dot_nt(ah, bl) + _dot_nt(al, bh))


def _rms(x, g):
    return x * lax.rsqrt(jnp.mean(x * x, axis=-1, keepdims=True) + EPS) * g


def _mod_index(i, bm):
    nct = T_CTX // bm
    return jnp.where(i < nct, 0, 1 + (i - nct) // (LAT_L // bm))


def _mod_columns(c_ref, w_ref, b_ref, o_ref):
    c = c_ref[...]
    s = (c * jax.nn.sigmoid(c)).astype(BF16)
    o_ref[...] = _dot(s, w_ref[...].astype(BF16)) + b_ref[...]


def _modulation(cvec8, w_mod, b_mod3, l, n):
    return pl.pallas_call(
        _mod_columns,
        grid=(n // BN_MOD,),
        in_specs=[
            pl.BlockSpec((8, D_MODEL), lambda j: (0, 0)),
            pl.BlockSpec((None, D_MODEL, BN_MOD), lambda j: (l, 0, j)),
            pl.BlockSpec((None, 1, BN_MOD), lambda j: (l, 0, j)),
        ],
        out_specs=pl.BlockSpec((8, BN_MOD), lambda j: (0, j)),
        out_shape=jax.ShapeDtypeStruct((8, n), F32),
        compiler_params=pltpu.CompilerParams(
            dimension_semantics=("arbitrary",), vmem_limit_bytes=VMEM_MID),
        name=f"modulation_l{l}",
    )(cvec8, w_mod, b_mod3)


def _ffn_kernel(*refs, mo, n_x, n_out):
    x_hbms = refs[:n_x]
    mod_ref, npre_ref, npost_ref, wg_ref, wu_ref, wd_ref = refs[n_x:n_x + 6]
    out_hbms = refs[n_x + 6:n_x + 6 + n_out]
    acc_ref, xc_ref, h_ref, sem_x, sem_c, sem_o = refs[n_x + 6 + n_out:]
    i = pl.program_id(0)
    j = pl.program_id(1)
    n_tiles = T_ALL // BM
    last = D_FF // BF - 1
    slot = i % 2
    nq = RC // RCX
    nct = T_CTX // BM

    def per_stream(arrays, tile, fn):
        if len(arrays) == 1:
            fn(arrays[0], pl.multiple_of(tile * BM, BM))
        else:
            pl.when(tile < nct)(lambda: fn(arrays[0], pl.multiple_of(tile * BM, BM)))
            pl.when(tile >= nct)(lambda: fn(arrays[1], pl.multiple_of((tile - nct) * BM, BM)))

    def x_tile_copy(arr, row0, sl):
        return pltpu.make_async_copy(arr.at[pl.ds(row0, BM), :], acc_ref.at[sl], sem_x)

    def x_chunk_copy(arr, row0, cs):
        return pltpu.make_async_copy(arr.at[pl.ds(row0, RCX), :], xc_ref.at[cs], sem_c.at[cs])

    def out_copy(arr, row0, sl):
        return pltpu.make_async_copy(acc_ref.at[sl], arr.at[pl.ds(row0, BM), :], sem_o.at[sl])

    wait_x_tile = lambda sl: x_tile_copy(x_hbms[0], 0, sl).wait()
    wait_x_chunk = lambda cs: x_chunk_copy(x_hbms[0], 0, cs).wait()
    wait_out = lambda sl: out_copy(out_hbms[0], 0, sl).wait()

    def start_x_chunk(q):
        per_stream(x_hbms, i, lambda arr, row0: x_chunk_copy(
            arr, pl.multiple_of(row0 + q * RCX, RCX), q).start())

    @pl.when(jnp.logical_and(i == 0, j == 0))
    def _():
        x_tile_copy(x_hbms[0], 0, 0).start()

    @pl.when(j == 0)
    def _():
        wait_x_tile(slot)

    @pl.when(j == FFN_PREFETCH_STEP)
    def _():
        pl.when(i >= 1)(lambda: wait_out(1 - slot))

        @pl.when(i + 1 < n_tiles)
        def _():
            per_stream(x_hbms, i + 1, lambda arr, row0: x_tile_copy(arr, row0, 1 - slot).start())

    @pl.when(j == last - 1)
    def _():
        for q in range(N_XC):
            start_x_chunk(q)

    def step(first, final):
        if final:
            for q in range(N_XC):
                wait_x_chunk(q)
        acc_slot = acc_ref.at[slot]
        wg = wg_ref[...].astype(BF16)
        wu = wu_ref[...].astype(BF16)
        wd = wd_ref[...].astype(BF16)
        for r in range(BM // RC):
            rows = slice(r * RC, (r + 1) * RC)
            if first:
                shift = mod_ref[:, mo * D_MODEL:(mo + 1) * D_MODEL]
                scale = mod_ref[:, (mo + 1) * D_MODEL:(mo + 2) * D_MODEL]
                hn = _rms(acc_slot[rows, :], npre_ref[...])
                h = (hn * (1.0 + scale) + shift).astype(BF16)
                h_ref[rows, :] = h
            else:
                h = h_ref[rows, :]
            g = _dot(h, wg)
            u = _dot(h, wu)
            a = (g * jax.nn.sigmoid(g) * u).astype(BF16)
            acc = _dot(a, wd)
            if not first:
                acc = acc_slot[rows, :] + acc
            if not final:
                acc_slot[rows, :] = acc
                continue
            gate = mod_ref[:, (mo + 2) * D_MODEL:(mo + 3) * D_MODEL]
            for qq in range(nq):
                sub = slice(qq * RCX, (qq + 1) * RCX)
                y = xc_ref[r * nq + qq] + (HALF_STEP * gate) * _rms(acc[sub, :], npost_ref[...])
                acc_slot[r * RC + qq * RCX:r * RC + (qq + 1) * RCX, :] = y
        if final:
            per_stream(out_hbms, i, lambda arr, row0: out_copy(arr, row0, slot).start())

    pl.when(j == 0)(functools.partial(step, True, False))
    pl.when(jnp.logical_and(j > 0, j < last))(functools.partial(step, False, False))
    pl.when(j == last)(functools.partial(step, False, True))

    @pl.when(jnp.logical_and(i == n_tiles - 1, j == last))
    def _():
        wait_out(slot)


def _ffn(xs, mods3, norm_pre4, norm_post4, ffn_gate, ffn_up, ffn_down, l, s, *, split_out=False):
    mo = 6 * s
    ni = 2 * s
    inner_specs = [
        pl.BlockSpec((None, 1, N_MOD * D_MODEL), lambda i, j: (_mod_index(i, BM), 0, 0)),
        pl.BlockSpec((None, None, 1, D_MODEL), lambda i, j: (l, ni, 0, 0)),
        pl.BlockSpec((None, None, 1, D_MODEL), lambda i, j: (l, ni, 0, 0)),
        pl.BlockSpec((None, None, D_MODEL, BF), lambda i, j: (l, s, 0, j)),
        pl.BlockSpec((None, None, D_MODEL, BF), lambda i, j: (l, s, 0, j)),
        pl.BlockSpec((None, None, BF, D_MODEL), lambda i, j: (l, s, j, 0)),
    ]
    out_rows = (T_CTX, T_LAT) if split_out else (T_ALL,)
    n_x, n_out = len(xs), len(out_rows)

    def outer(*refs):
        x_hbms = refs[:n_x]
        streamed = refs[n_x:n_x + 6]
        rest = refs[n_x + 6:]

        def body(*blocks):
            _ffn_kernel(*x_hbms, *blocks, *rest, mo=mo, n_x=n_x, n_out=n_out)

        pltpu.emit_pipeline(body, grid=(T_ALL // BM, D_FF // BF), in_specs=inner_specs)(*streamed)

    outs = pl.pallas_call(
        outer,
        in_specs=[pl.BlockSpec(memory_space=pl.ANY)] * (n_x + 6),
        out_specs=[pl.BlockSpec(memory_space=pl.ANY)] * len(out_rows),
        out_shape=[jax.ShapeDtypeStruct((r, D_MODEL), F32) for r in out_rows],
        scratch_shapes=[
            pltpu.VMEM((2, BM, D_MODEL), F32),
            pltpu.VMEM((N_XC, RCX, D_MODEL), F32),
            pltpu.VMEM((BM, D_MODEL), BF16),
            pltpu.SemaphoreType.DMA(()),
            pltpu.SemaphoreType.DMA((N_XC,)),
            pltpu.SemaphoreType.DMA((2,)),
        ],
        compiler_params=pltpu.CompilerParams(vmem_limit_bytes=VMEM_FFN),
        name=f"ffn_l{l}_s{s}",
    )(*xs, mods3, norm_pre4, norm_post4, ffn_gate, ffn_up, ffn_down)
    return outs if split_out else outs[0]


def _rope(y, cos, sins):
    lane = lax.broadcasted_iota(jnp.int32, y.shape, 1)
    first = (lane & 63) < 32
    partner = jnp.where(first, pltpu.roll(y, 96, 1), pltpu.roll(y, 32, 1))
    return y * cos + partner * sins


_IN_SEGMENTS = (
    (COL_AQ, A_HEADS, "q", True, None),
    (COL_AK, A_KV, "k", True, 0),
    (COL_AV, A_KV, None, False, 1),
    (COL_CQ, C_HEADS, None, True, None),
    (COL_CK, C_KV, None, True, 2),
    (COL_CV, C_KV, None, False, 3),
)


def _inproj_kernel(x_ref, mod_ref, npre_ref, qn_ref, kn_ref, cos_ref, sin_ref, w_ref, *rest, first_layer):
    nb = BM_IN // CTX_L
    if first_layer:
        c_ref, wmod_ref, bmod_ref, qkvu_ref, uf_ref, *cache_full, modo_ref = rest
        cache_refs = [c.at[:, 0] for c in cache_full]
        _mod_columns(c_ref, wmod_ref, bmod_ref, modo_ref)
    else:
        qkvu_ref, uf_ref, *cache_refs = rest[4:]
        cache_full = ()

    def body(lat):
        shift = mod_ref[:, 3 * D_MODEL:4 * D_MODEL]
        scale = mod_ref[:, 4 * D_MODEL:5 * D_MODEL]
        h = (_rms(x_ref[...], npre_ref[...]) * (1.0 + scale) + shift).astype(BF16)
        for col0, heads, norm, rot, cache in _IN_SEGMENTS:
            p = _dot(h, w_ref[:, col0:col0 + heads * HD])
            for k in range(heads):
                y = p[:, k * HD:(k + 1) * HD]
                if norm == "q":
                    y = _rms(y, qn_ref[...])
                elif norm == "k":
                    y = _rms(y, kn_ref[...])
                if rot and lat:
                    y = _rope(y, cos_ref[...], sin_ref[...])
                qkvu_ref[:, col0 + k * HD:col0 + (k + 1) * HD] = y.astype(BF16)
                if cache is not None and not lat:
                    cache_refs[cache][:, :, k, :] = y.reshape(nb, CTX_L, HD)
        u = _dot(h, w_ref[:, COL_U:])
        uf_ref[...] = u
        qkvu_ref[:, COL_U:] = u.astype(BF16)
        if first_layer and not lat:
            for c in cache_full:
                c[:, 1:] = jnp.zeros((nb, DEPTH - 1, CTX_L, A_KV, HD), F32)

    is_lat = pl.program_id(0) >= T_CTX // BM_IN
    pl.when(is_lat)(functools.partial(body, True))
    pl.when(jnp.logical_not(is_lat))(functools.partial(body, False))


def _inproj(x, mods3, norm_pre4, q_norm3, k_norm3, cos_t, sin_t, w_in_bf, l, prev_caches, mod_args=None,
            mod_cols=None):
    bm = BM_IN
    nct = T_CTX // bm
    nb = bm // CTX_L
    tab_idx = lambda i: (jnp.maximum(i - nct, 0) % (LAT_L // bm), 0)
    cache_shape = jax.ShapeDtypeStruct((CTX_B, DEPTH, CTX_L, A_KV, HD), F32)
    first_layer = not prev_caches
    if first_layer:
        cache_spec = pl.BlockSpec((nb, DEPTH, CTX_L, A_KV, HD), lambda i: (jnp.minimum(i, nct - 1), 0, 0, 0, 0))
    else:
        cache_spec = pl.BlockSpec((nb, None, CTX_L, A_KV, HD), lambda i: (jnp.minimum(i, nct - 1), l, 0, 0, 0))
    n_in = 8
    extra_in, extra_out, extra_shape, extra_args = [], [], [], []
    if first_layer:
        lo, hi = mod_cols
        steps = T_ALL // bm
        bn = (hi - lo) // steps
        extra_in = [
            pl.BlockSpec((8, D_MODEL), lambda i: (0, 0)),
            pl.BlockSpec((None, D_MODEL, bn), lambda i: (l, 0, lo // bn + i)),
            pl.BlockSpec((None, 1, bn), lambda i: (l, 0, lo // bn + i)),
        ]
        extra_out = [pl.BlockSpec((8, bn), lambda i: (0, i))]
        extra_shape = [jax.ShapeDtypeStruct((8, hi - lo), F32)]
        extra_args = list(mod_args)
    return pl.pallas_call(
        functools.partial(_inproj_kernel, first_layer=first_layer),
        grid=(T_ALL // bm,),
        in_specs=[
            pl.BlockSpec((bm, D_MODEL), lambda i: (i, 0)),
            pl.BlockSpec((None, 1, N_MOD * D_MODEL), lambda i: (_mod_index(i, bm), 0, 0)),
            pl.BlockSpec((None, None, 1, D_MODEL), lambda i: (l, 1, 0, 0)),
            pl.BlockSpec((None, 1, HD), lambda i: (l, 0, 0)),
            pl.BlockSpec((None, 1, HD), lambda i: (l, 0, 0)),
            pl.BlockSpec((bm, HD), tab_idx),
            pl.BlockSpec((bm, HD), tab_idx),
            pl.BlockSpec((D_MODEL, IN_WIDTH), lambda i: (0, 0), pipeline_mode=pl.Buffered(1)),
        ] + extra_in + [pl.BlockSpec(memory_space=pl.ANY)] * len(prev_caches),
        out_specs=[
            pl.BlockSpec((bm, IN_WIDTH), lambda i: (i, 0)),
            pl.BlockSpec((bm, S5_CH), lambda i: (i, 0)),
        ] + [cache_spec] * 4 + extra_out,
        out_shape=[
            jax.ShapeDtypeStruct((T_ALL, IN_WIDTH), BF16),
            jax.ShapeDtypeStruct((T_ALL, S5_CH), F32),
        ] + [cache_shape] * 4 + extra_shape,
        input_output_aliases={n_in + k: 2 + k for k in range(len(prev_caches))},
        compiler_params=pltpu.CompilerParams(
            dimension_semantics=("arbitrary",), vmem_limit_bytes=VMEM_BIG),
        name=f"inproj_l{l}",
    )(x, mods3, norm_pre4, q_norm3, k_norm3, cos_t, sin_t, w_in_bf, *extra_args, *prev_caches)


def _softmax_pv(dots, values, sink=None):
    m = functools.reduce(jnp.maximum, [jnp.max(d, axis=-1, keepdims=True) for d in dots]) * SCALE
    if sink is not None:
        m = jnp.maximum(m, sink)
    m2 = m * LOG2E
    ps = [jnp.exp2(d * (SCALE * LOG2E) - m2) for d in dots]
    den = functools.reduce(jnp.add, [jnp.sum(p, axis=-1, keepdims=True) for p in ps])
    if sink is not None:
        den = den + jnp.exp2(sink * LOG2E - m2)
    o = functools.reduce(jnp.add, [_dot(p.astype(BF16), v) for p, v in zip(ps, values)])
    return o / den


def _attn_ctx_kernel(sink_ref, aq_ref, ak_ref, av_ref, cq_ref, ck_ref, cv_ref, ao_ref, co_ref):
    kv = pl.program_id(1)
    g = C_HEADS // C_KV
    for b in range(NB_CTX):
        rows = slice(b * CTX_L, (b + 1) * CTX_L)
        k = ak_ref[rows, :]
        v = av_ref[rows, :]
        for h in range(A_HEADS // A_KV):
            cols = slice(h * HD, (h + 1) * HD)
            s = _dot_nt(aq_ref[rows, cols], k)
            ao_ref[rows, cols] = _softmax_pv([s], [v]).astype(BF16)
        k = ck_ref[rows, :]
        v = cv_ref[rows, :]
        for h in range(g):
            cols = slice(h * HD, (h + 1) * HD)
            s = _dot_nt(cq_ref[rows, cols], k)
            co_ref[rows, cols] = _softmax_pv([s], [v], sink_ref[kv * g + h]).astype(BF16)


def _attn_ctx(qkvu, sink_l, l):
    ga = A_HEADS // A_KV * HD
    gc = C_HEADS // C_KV * HD
    rows = NB_CTX * CTX_L
    blk = lambda width, col0: pl.BlockSpec((rows, width), lambda b, k: (b, col0 // width + k))
    return pl.pallas_call(
        _attn_ctx_kernel,
        grid=(CTX_B // NB_CTX, A_KV),
        in_specs=[
            pl.BlockSpec(memory_space=pltpu.SMEM),
            blk(ga, COL_AQ), blk(HD, COL_AK), blk(HD, COL_AV),
            blk(gc, COL_CQ), blk(HD, COL_CK), blk(HD, COL_CV),
        ],
        out_specs=[
            pl.BlockSpec((rows, ga), lambda b, k: (b, k)),
            pl.BlockSpec((rows, gc), lambda b, k: (b, k)),
        ],
        out_shape=[
            jax.ShapeDtypeStruct((T_CTX, A_HEADS * HD), BF16),
            jax.ShapeDtypeStruct((T_CTX, C_HEADS * HD), BF16),
        ],
        compiler_params=pltpu.CompilerParams(dimension_semantics=("arbitrary", "arbitrary")),
        name=f"attn_ctx_l{l}",
    )(sink_l, qkvu, qkvu, qkvu, qkvu, qkvu, qkvu)


def _attn_lat_a_kernel(q_ref, k_ref, v_ref, kc_ref, vc_ref, *rest):
    o_ref = rest[0] if len(rest) == 1 else rest[4]
    k = k_ref[...]
    v = v_ref[...]
    kc = kc_ref[...].astype(BF16)
    vc = vc_ref[...].astype(BF16)
    for h in range(A_HEADS // A_KV):
        cols = slice(h * HD, (h + 1) * HD)
        q = q_ref[:, cols]
        s1 = _dot_nt(q, k)
        s2 = _dot_nt(q, kc)
        o_ref[:, cols] = _softmax_pv([s1, s2], [v, vc]).astype(BF16)
    if len(rest) > 1:
        c_ref, w_ref, b_ref, win_ref, _, mod_ref, winb_ref = rest
        _mod_columns(c_ref, w_ref, b_ref, mod_ref)
        winb_ref[...] = win_ref[...].astype(BF16)


def _attn_lat_a(qkvu, cache_k4, cache_v4, l, next_mod=None):
    ga = A_HEADS // A_KV * HD
    nq = LAT_L // BQ_A
    row0 = T_CTX // BQ_A
    lat_blk = T_CTX // LAT_L
    in_specs = [
        pl.BlockSpec((BQ_A, ga), lambda b, k, q: (row0 + b * nq + q, COL_AQ // ga + k)),
        pl.BlockSpec((LAT_L, HD), lambda b, k, q: (lat_blk + b, COL_AK // HD + k)),
        pl.BlockSpec((LAT_L, HD), lambda b, k, q: (lat_blk + b, COL_AV // HD + k)),
        pl.BlockSpec((None, None, PAST, HD), lambda b, k, q: (b, l, 0, k)),
        pl.BlockSpec((None, None, PAST, HD), lambda b, k, q: (b, l, 0, k)),
    ]
    out_specs = [pl.BlockSpec((BQ_A, ga), lambda b, k, q: (b * nq + q, k))]
    out_shape = [jax.ShapeDtypeStruct((T_LAT, A_HEADS * HD), BF16)]
    args = [qkvu, qkvu, qkvu, cache_k4, cache_v4]
    if next_mod is not None:
        n = N_MOD * D_MODEL
        steps = LAT_B * A_KV * nq
        bn = n // steps
        step = lambda b, k, q: (b * A_KV + k) * nq + q
        in_specs += [
            pl.BlockSpec((8, D_MODEL), lambda b, k, q: (0, 0)),
            pl.BlockSpec((None, D_MODEL, bn), lambda b, k, q: (l + 1, 0, step(b, k, q))),
            pl.BlockSpec((None, 1, bn), lambda b, k, q: (l + 1, 0, step(b, k, q))),
            pl.BlockSpec((None, D_MODEL // steps, IN_WIDTH), lambda b, k, q: (l + 1, step(b, k, q), 0)),
        ]
        out_specs += [
            pl.BlockSpec((8, bn), lambda b, k, q: (0, step(b, k, q))),
            pl.BlockSpec((D_MODEL // steps, IN_WIDTH), lambda b, k, q: (step(b, k, q), 0)),
        ]
        out_shape += [
            jax.ShapeDtypeStruct((8, n), F32),
            jax.ShapeDtypeStruct((D_MODEL, IN_WIDTH), BF16),
        ]
        args += list(next_mod)
    def outer(*refs):
        pltpu.emit_pipeline(
            _attn_lat_a_kernel, grid=(LAT_B, A_KV, nq), in_specs=in_specs, out_specs=out_specs)(*refs)

    outs = pl.pallas_call(
        outer,
        in_specs=[pl.BlockSpec(memory_space=pl.ANY)] * len(in_specs),
        out_specs=[pl.BlockSpec(memory_space=pl.ANY)] * len(out_specs),
        out_shape=out_shape,
        compiler_params=pltpu.CompilerParams(vmem_limit_bytes=VMEM_BIG),
        name=f"attn_lat_a_l{l}",
    )(*args)
    return outs if next_mod is not None else outs[0]


def _attn_lat_c_kernel(sink_ref, q_ref, k_ref, v_ref, kc_ref, vc_ref, wout_ref, o_ref, woutb_ref):
    woutb_ref[...] = wout_ref[...].astype(BF16)
    kv = pl.program_id(1)
    span = BQ_C + 2 * WINDOW
    kc = kc_ref[...].astype(BF16)
    vc = vc_ref[...].astype(BF16)
    g = C_HEADS // C_KV
    for sub in range(NQ_C):
        n = pl.program_id(2) * NQ_C + sub
        rows = slice(sub * BQ_C, (sub + 1) * BQ_C)
        start = pl.multiple_of(jnp.clip(n * BQ_C - WINDOW, 0, LAT_L - span), WINDOW)
        kw = k_ref[pl.ds(start, span), :]
        vw = v_ref[pl.ds(start, span), :]
        qpos = n * BQ_C + lax.broadcasted_iota(jnp.int32, (BQ_C, span), 0)
        kpos = start + lax.broadcasted_iota(jnp.int32, (BQ_C, span), 1)
        valid = jnp.abs(qpos - kpos) <= WINDOW
        for h in range(g):
            cols = slice(h * HD, (h + 1) * HD)
            q = q_ref[rows, cols]
            s1 = jnp.where(valid, _dot_nt(q, kw), NEG_INF)
            s2 = _dot_nt(q, kc)
            o_ref[rows, cols] = _softmax_pv([s1, s2], [vw, vc], sink_ref[kv * g + h]).astype(BF16)


def _attn_lat_c(qkvu, sink_l, cache_k4, cache_v4, w_out, l):
    gc = C_HEADS // C_KV * HD
    bq = BQ_C * NQ_C
    nq = LAT_L // bq
    row0 = T_CTX // bq
    lat_blk = T_CTX // LAT_L
    steps = LAT_B * C_KV * nq
    step = lambda b, k, q: (b * C_KV + k) * nq + q
    return pl.pallas_call(
        _attn_lat_c_kernel,
        grid=(LAT_B, C_KV, nq),
        in_specs=[
            pl.BlockSpec(memory_space=pltpu.SMEM),
            pl.BlockSpec((bq, gc), lambda b, k, q: (row0 + b * nq + q, COL_CQ // gc + k)),
            pl.BlockSpec((LAT_L, HD), lambda b, k, q: (lat_blk + b, COL_CK // HD + k)),
            pl.BlockSpec((LAT_L, HD), lambda b, k, q: (lat_blk + b, COL_CV // HD + k)),
            pl.BlockSpec((None, None, PAST, HD), lambda b, k, q: (b, l, 0, k)),
            pl.BlockSpec((None, None, PAST, HD), lambda b, k, q: (b, l, 0, k)),
            pl.BlockSpec((None, D_MODEL // steps, D_MODEL), lambda b, k, q: (l, step(b, k, q), 0)),
        ],
        out_specs=[
            pl.BlockSpec((bq, gc), lambda b, k, q: (b * nq + q, k)),
            pl.BlockSpec((D_MODEL // steps, D_MODEL), lambda b, k, q: (step(b, k, q), 0)),
        ],
        out_shape=[
            jax.ShapeDtypeStruct((T_LAT, C_HEADS * HD), BF16),
            jax.ShapeDtypeStruct((D_MODEL, D_MODEL), BF16),
        ],
        compiler_params=pltpu.CompilerParams(
            dimension_semantics=("arbitrary", "arbitrary", "arbitrary")),
        name=f"attn_lat_c_l{l}",
    )(sink_l, qkvu, qkvu, qkvu, cache_k4, cache_v4, w_out)


def _s5_param_kernel(rows_ref, bc_ref, c_ref, wmod_ref, bmod_ref, win_ref, w1_ref, w2_ref, a8_ref, mod_ref,
                     winb_ref):
    _mod_columns(c_ref, wmod_ref, bmod_ref, mod_ref)
    winb_ref[...] = win_ref[...].astype(BF16)
    a8_ref[...] = jnp.zeros_like(a8_ref)
    row_grp = lax.broadcasted_iota(jnp.int32, (128, S5_ST), 0) // S5_GC
    lane_grp = lax.broadcasted_iota(jnp.int32, (128, S5_ST), 1) // S5_P
    on_diag = row_grp == lane_grp

    def expand(a):
        return jnp.where(on_diag, jnp.concatenate([a] * S5_OG, axis=1), 0.0)

    taps = []
    for d in range(2):
        lr = rows_ref[0, d]
        li = rows_ref[1, d]
        dt = jnp.exp(rows_ref[2, d])
        mag = jnp.exp(lr * dt)
        ar = mag * jnp.cos(li * dt)
        ai = mag * jnp.sin(li * dt)
        den = lr * lr + li * li
        n_re = ar - 1.0
        f_re = (n_re * lr + ai * li) / den
        f_im = (ai * lr - n_re * li) / den
        pw = [(jnp.ones_like(ar), jnp.zeros_like(ar))]
        for _ in range(S5_T):
            pr, pi = pw[-1]
            pw.append((pr * ar - pi * ai, pr * ai + pi * ar))
        br, bi, cr, ci = (expand(bc_ref[k, d]) for k in range(4))
        bbr = f_re * br - f_im * bi
        bbi = f_re * bi + f_im * br
        xr_blocks, xi_blocks = [], []
        for t in range(S5_T):
            rows = slice(t * 128, (t + 1) * 128)
            pr, pi = pw[S5_T - 1 - t if d == 0 else t]
            xr = pr * bbr - pi * bbi
            xi = pr * bbi + pi * bbr
            c0 = S5_ROW + 2 * S5_ST * d
            w1_ref[rows, c0:c0 + S5_ST] = xr.astype(BF16)
            w1_ref[rows, c0 + S5_ST:c0 + 2 * S5_ST] = xi.astype(BF16)
            xr_blocks.append(xr)
            xi_blocks.append(xi)
            pr, pi = pw[t + 1 if d == 0 else S5_T - t]
            c0 = 2 * S5_ST * d
            w2_ref[rows, c0:c0 + S5_ST] = (cr * pr - ci * pi).astype(BF16)
            w2_ref[rows, c0 + S5_ST:c0 + 2 * S5_ST] = (-(cr * pi + ci * pr)).astype(BF16)
        a8_ref[2 * d:2 * d + 1, :] = pw[S5_T][0]
        a8_ref[2 * d + 1:2 * d + 2, :] = pw[S5_T][1]
        xr_all = jnp.concatenate(xr_blocks, axis=0)
        xi_all = jnp.concatenate(xi_blocks, axis=0)
        taps.append(_dot_nt_split(xr_all, cr) - _dot_nt_split(xi_all, ci))
    tf, tb = taps
    for t in range(S5_T):
        for t2 in range(S5_T):
            blk = None
            if t2 >= t:
                r0 = (S5_T - 1 - (t2 - t)) * 128
                blk = tf[r0:r0 + 128, :]
            if t2 <= t:
                r0 = (t - t2) * 128
                b2 = tb[r0:r0 + 128, :]
                blk = b2 if blk is None else blk + b2
            w1_ref[t * 128:(t + 1) * 128, t2 * 128:(t2 + 1) * 128] = blk.astype(BF16)


def _s5_params(s5_lam_re, s5_lam_im, s5_log_step, s5_b_re, s5_b_im, s5_c_re, s5_c_im, cvec8, w_mod, b_mod3, w_in):
    rows = jnp.stack([s5_lam_re, s5_lam_im, jnp.repeat(s5_log_step[..., None], S5_P, axis=-1)], axis=0)
    rows = rows.reshape(3, DEPTH, 2, S5_OCT, 1, S5_ST)
    bc = jnp.stack([jnp.swapaxes(s5_b_re, -1, -2), jnp.swapaxes(s5_b_im, -1, -2), s5_c_re, s5_c_im], axis=0)
    bc = bc.reshape(4, DEPTH, 2, S5_OCT, 128, S5_P)
    spec = lambda n, r, c: pl.BlockSpec((n, None, 2, None, r, c), lambda l, s: (0, l, 0, s, 0, 0))
    steps = DEPTH * S5_OCT
    step = lambda l, s: l * S5_OCT + s
    n_tail = MOD_MID - MOD_HEAD
    bn = n_tail // steps
    br = D_MODEL // steps
    return pl.pallas_call(
        _s5_param_kernel,
        grid=(DEPTH, S5_OCT),
        in_specs=[
            spec(3, 1, S5_ST), spec(4, 128, S5_P),
            pl.BlockSpec((8, D_MODEL), lambda l, s: (0, 0)),
            pl.BlockSpec((None, D_MODEL, bn), lambda l, s: (0, 0, MOD_HEAD // bn + step(l, s))),
            pl.BlockSpec((None, 1, bn), lambda l, s: (0, 0, MOD_HEAD // bn + step(l, s))),
            pl.BlockSpec((None, br, IN_WIDTH), lambda l, s: (0, step(l, s), 0)),
        ],
        out_specs=[
            pl.BlockSpec((None, None, S5_ROW, S5_W1), lambda l, s: (l, s, 0, 0)),
            pl.BlockSpec((None, None, S5_ROW, 4 * S5_ST), lambda l, s: (l, s, 0, 0)),
            pl.BlockSpec((None, None, 8, S5_ST), lambda l, s: (l, s, 0, 0)),
            pl.BlockSpec((8, bn), lambda l, s: (0, step(l, s))),
            pl.BlockSpec((br, IN_WIDTH), lambda l, s: (step(l, s), 0)),
        ],
        out_shape=[
            jax.ShapeDtypeStruct((DEPTH, S5_OCT, S5_ROW, S5_W1), BF16),
            jax.ShapeDtypeStruct((DEPTH, S5_OCT, S5_ROW, 4 * S5_ST), BF16),
            jax.ShapeDtypeStruct((DEPTH, S5_OCT, 8, S5_ST), F32),
            jax.ShapeDtypeStruct((8, n_tail), F32),
            jax.ShapeDtypeStruct((D_MODEL, IN_WIDTH), BF16),
        ],
        compiler_params=pltpu.CompilerParams(
            dimension_semantics=("arbitrary", "arbitrary"), vmem_limit_bytes=VMEM_BIG),
        name="s5_params",
    )(rows, bc, cvec8, w_mod, b_mod3, w_in)


def _s5_mix_kernel(u_ref, w1_ref, w2_ref, a8_ref, h0_ref, y_ref, hfin_ref, lhs_ref, a_ref, hp_ref, y8_ref):
    def stream(nb, nc, is_ctx):
        for b in range(nb):
            for t in range(S5_T):
                lhs_ref[t, pl.ds(b, nc, stride=nb), :] = u_ref[pl.ds(b * nc * S5_T + t, nc, stride=S5_T), :]
        lhs = jnp.concatenate([lhs_ref[t] for t in range(S5_T)], axis=1).astype(BF16)
        a_ref[...] = _dot(lhs, w1_ref[:, S5_ROW:])
        y_intra = _dot(lhs, w1_ref[:, 0:S5_ROW])

        coef = [a8_ref[k:k + 1, :] for k in range(4)]
        if is_ctx:
            init = tuple(jnp.zeros((nb, S5_ST), F32) for _ in range(4))
        else:
            init = tuple(h0_ref[k] for k in range(4))

        def step(c, st):
            new = []
            for d in range(2):
                cc = c if d == 0 else nc - 1 - c
                rows = slice(cc * nb, (cc + 1) * nb)
                hr, hi = st[2 * d], st[2 * d + 1]
                hp_ref[rows, 2 * S5_ST * d:2 * S5_ST * d + S5_ST] = hr
                hp_ref[rows, 2 * S5_ST * d + S5_ST:2 * S5_ST * (d + 1)] = hi
                c0 = 2 * S5_ST * d
                gr = a_ref[rows, c0:c0 + S5_ST]
                gi = a_ref[rows, c0 + S5_ST:c0 + 2 * S5_ST]
                ar, ai = coef[2 * d], coef[2 * d + 1]
                new += [ar * hr - ai * hi + gr, ar * hi + ai * hr + gi]
            return tuple(new)

        fin = init
        for c in range(nc):
            fin = step(c, fin)
        if is_ctx:
            for k in range(4):
                hfin_ref[k] = fin[k]

        y8 = y_intra + _dot_nt(hp_ref[...].astype(BF16), w2_ref[...])
        for t in range(S5_T):
            y8_ref[t] = y8[:, t * 128:(t + 1) * 128]
        for b in range(nb):
            for t in range(S5_T):
                y_ref[pl.ds(b * nc * S5_T + t, nc, stride=S5_T), :] = y8_ref[t, pl.ds(b, nc, stride=nb), :]

    pl.when(pl.program_id(1) == 0)(functools.partial(stream, CTX_B, CTX_NC, True))
    pl.when(pl.program_id(1) == 1)(functools.partial(stream, LAT_B, LAT_NC, False))


def _s5_mix(uf, w1, w2, a8, h0, l):
    return pl.pallas_call(
        _s5_mix_kernel,
        grid=(S5_OCT, 2),
        in_specs=[
            pl.BlockSpec((T_CTX, 128), lambda s, k: (k, s)),
            pl.BlockSpec((None, None, S5_ROW, S5_W1), lambda s, k: (l, s, 0, 0)),
            pl.BlockSpec((None, None, S5_ROW, 4 * S5_ST), lambda s, k: (l, s, 0, 0)),
            pl.BlockSpec((None, None, 8, S5_ST), lambda s, k: (l, s, 0, 0)),
            pl.BlockSpec((None, 4, LAT_B, S5_ST), lambda s, k: (s, 0, 0, 0)),
        ],
        out_specs=[
            pl.BlockSpec((T_CTX, 128), lambda s, k: (k, s)),
            pl.BlockSpec((None, 4, CTX_B, S5_ST), lambda s, k: (s, 0, 0, 0)),
        ],
        out_shape=[
            jax.ShapeDtypeStruct((T_ALL, S5_CH), F32),
            jax.ShapeDtypeStruct((S5_OCT, 4, CTX_B, S5_ST), F32),
        ],
        scratch_shapes=[
            pltpu.VMEM((S5_T, S5_ROWS, 128), F32),
            pltpu.VMEM((S5_ROWS, 4 * S5_ST), F32),
            pltpu.VMEM((S5_ROWS, 4 * S5_ST), F32),
            pltpu.VMEM((S5_T, S5_ROWS, 128), F32),
        ],
        compiler_params=pltpu.CompilerParams(
            dimension_semantics=("arbitrary", "arbitrary"), vmem_limit_bytes=VMEM_BIG),
        name=f"s5_mix_l{l}",
    )(uf, w1, w2, a8, h0)


def _s5_gate(y, u, d, w, b):
    y = y + d * u
    z = y * (0.5 * (1.0 + jnp.tanh(math.sqrt(2.0 / math.pi) * (y + 0.044715 * (y * y * y)))))
    t = _dot(z.astype(BF16), w) + b
    return (z * jax.nn.sigmoid(t)).astype(BF16)


def _outproj_kernel(x_ref, a_ctx_ref, a_lat_ref, c_ctx_ref, c_lat_ref, y_ref, u_ref, d_ref, wglu_ref, bglu_ref,
                    mod_ref, npost_ref, wb_ref, o_ref):
    na = A_HEADS * HD
    nc = na + C_HEADS * HD

    def body(a_ref, c_ref):
        gate = mod_ref[:, 5 * D_MODEL:6 * D_MODEL]
        wglu = wglu_ref[...].astype(BF16)
        for r in range(BM_OUT // RC_OUT):
            rows = slice(r * RC_OUT, (r + 1) * RC_OUT)
            s = _s5_gate(y_ref[rows, :], u_ref[rows, :], d_ref[...], wglu, bglu_ref[...])
            y = (_dot(a_ref[rows, :], wb_ref[0:na, :]) + _dot(c_ref[rows, :], wb_ref[na:nc, :])
                 + _dot(s, wb_ref[nc:, :]))
            o_ref[rows, :] = x_ref[rows, :] + gate * _rms(y, npost_ref[...])

    is_lat = pl.program_id(0) >= T_CTX // BM_OUT
    pl.when(jnp.logical_not(is_lat))(functools.partial(body, a_ctx_ref, c_ctx_ref))
    pl.when(is_lat)(functools.partial(body, a_lat_ref, c_lat_ref))


def _outproj(x, a_ctx, a_lat, c_ctx, c_lat, y_s5, uf, s5_d3, w_glu, b_glu3, mods3, norm_post4, w_out_bf, l):
    bm = BM_OUT
    nct = T_CTX // bm
    ctx_idx = lambda i: (jnp.minimum(i, nct - 1), 0)
    lat_idx = lambda i: (jnp.maximum(i - nct, 0), 0)
    inner_specs = [
            pl.BlockSpec((bm, D_MODEL), lambda i: (i, 0)),
            pl.BlockSpec((bm, A_HEADS * HD), ctx_idx),
            pl.BlockSpec((bm, A_HEADS * HD), lat_idx),
            pl.BlockSpec((bm, C_HEADS * HD), ctx_idx),
            pl.BlockSpec((bm, C_HEADS * HD), lat_idx),
            pl.BlockSpec((bm, S5_CH), lambda i: (i, 0)),
            pl.BlockSpec((bm, S5_CH), lambda i: (i, 0)),
            pl.BlockSpec((None, 1, S5_CH), lambda i: (l, 0, 0)),
            pl.BlockSpec((None, S5_CH, S5_CH), lambda i: (l, 0, 0)),
            pl.BlockSpec((None, 1, S5_CH), lambda i: (l, 0, 0)),
            pl.BlockSpec((None, 1, N_MOD * D_MODEL), lambda i: (_mod_index(i, bm), 0, 0)),
            pl.BlockSpec((None, None, 1, D_MODEL), lambda i: (l, 1, 0, 0)),
            pl.BlockSpec((D_MODEL, D_MODEL), lambda i: (0, 0)),
    ]

    def outer(*refs):
        pltpu.emit_pipeline(
            _outproj_kernel, grid=(T_ALL // bm,), in_specs=inner_specs,
            out_specs=[pl.BlockSpec((bm, D_MODEL), lambda i: (i, 0))])(*refs)

    return pl.pallas_call(
        outer,
        in_specs=[pl.BlockSpec(memory_space=pl.ANY)] * len(inner_specs),
        out_specs=pl.BlockSpec(memory_space=pl.ANY),
        out_shape=jax.ShapeDtypeStruct((T_ALL, D_MODEL), F32),
        compiler_params=pltpu.CompilerParams(vmem_limit_bytes=VMEM_BIG),
        name=f"outproj_l{l}",
    )(x, a_ctx, a_lat, c_ctx, c_lat, y_s5, uf, s5_d3, w_glu, b_glu3, mods3, norm_post4, w_out_bf)


def _rope_tables():
    rows = LAT_L // GRID_W
    row = jnp.repeat(jnp.arange(rows, dtype=F32), GRID_W)
    col = jnp.tile(jnp.arange(GRID_W, dtype=F32), rows)
    axis_dim = HD // 2
    inv_freq = ROPE_BASE ** (-jnp.arange(0, axis_dim, 2, dtype=F32) / axis_dim)
    ang_row = row[:, None] * inv_freq
    ang_col = col[:, None] * inv_freq
    cr, sr = jnp.cos(ang_row), jnp.sin(ang_row)
    cc, sc = jnp.cos(ang_col), jnp.sin(ang_col)
    cos_t = jnp.concatenate([cr, cr, cc, cc], axis=-1)
    sin_t = jnp.concatenate([-sr, sr, -sc, sc], axis=-1)
    return cos_t, sin_t


def kernel(x_prompt, x_sample, cache_a_k, cache_a_v, cache_c_k, cache_c_v, state_ssm_re, state_ssm_im,
           c, c_ctx, w_mod, b_mod, norm_pre, norm_post, ffn_gate, ffn_up, ffn_down, w_in, w_out,
           q_norm, k_norm, sink, s5_lam_re, s5_lam_im, s5_log_step, s5_b_re, s5_b_im, s5_c_re, s5_c_im,
           s5_d, w_glu, b_glu):
    cvec8 = jnp.concatenate([c_ctx[None, :], c, jnp.zeros((8 - 1 - LAT_B, D_MODEL), F32)], axis=0)
    b_mod3 = b_mod.reshape(DEPTH, 1, N_MOD * D_MODEL)
    norm_pre4 = norm_pre.reshape(DEPTH, 3, 1, D_MODEL)
    norm_post4 = norm_post.reshape(DEPTH, 3, 1, D_MODEL)
    q_norm3 = q_norm.reshape(DEPTH, 1, HD)
    k_norm3 = k_norm.reshape(DEPTH, 1, HD)
    s5_d3 = s5_d.reshape(DEPTH, 1, S5_CH)
    b_glu3 = b_glu.reshape(DEPTH, 1, S5_CH)
    cos_t, sin_t = _rope_tables()
    mods_head = _modulation(cvec8, w_mod, b_mod3, 0, MOD_HEAD)
    w1, w2, a8, mods_mid, w_in_bf = _s5_params(s5_lam_re, s5_lam_im, s5_log_step, s5_b_re, s5_b_im,
                                               s5_c_re, s5_c_im, cvec8, w_mod, b_mod3, w_in)
    mods_tail = jnp.zeros((8, N_MOD * D_MODEL - MOD_MID), F32)
    table = lambda tail: jnp.concatenate([mods_head, mods_mid, tail], axis=1).reshape(8, 1, N_MOD * D_MODEL)
    mods3 = table(mods_tail)
    kv4 = lambda a: a.reshape(LAT_B, DEPTH, PAST, A_KV * HD)
    cak, cav, cck, ccv = kv4(cache_a_k), kv4(cache_a_v), kv4(cache_c_k), kv4(cache_c_v)
    h0_all = jnp.stack([state_ssm_re[:, :, 0], state_ssm_im[:, :, 0],
                        state_ssm_re[:, :, 1], state_ssm_im[:, :, 1]], axis=0)
    h0_all = h0_all.reshape(4, LAT_B, DEPTH, S5_OCT, S5_ST).transpose(2, 3, 0, 1, 4)

    ffn_w = (norm_pre4, norm_post4, ffn_gate, ffn_up, ffn_down)
    new_caches = ()
    new_state = []
    xs = [x_prompt.reshape(T_CTX, D_MODEL), x_sample.reshape(T_LAT, D_MODEL)]
    for l in range(DEPTH):
        x = _ffn(xs, mods3, *ffn_w, l, 0)

        if l == 0:
            qkvu, uf, *new_caches, mods_tail = _inproj(
                x, mods3, norm_pre4, q_norm3, k_norm3, cos_t, sin_t, w_in_bf, l, (),
                mod_args=(cvec8, w_mod, b_mod3), mod_cols=(MOD_MID, N_MOD * D_MODEL))
            mods3 = table(mods_tail)
        else:
            qkvu, uf, *new_caches = _inproj(x, mods3, norm_pre4, q_norm3, k_norm3, cos_t, sin_t, w_in_bf, l,
                                            new_caches)
        a_ctx, c_ctx = _attn_ctx(qkvu, sink[l], l)
        if l < DEPTH - 1:
            a_lat, mods_next, w_in_next = _attn_lat_a(qkvu, cak, cav, l, next_mod=(cvec8, w_mod, b_mod3, w_in))
        else:
            a_lat = _attn_lat_a(qkvu, cak, cav, l)
        c_lat, w_out_bf = _attn_lat_c(qkvu, sink[l], cck, ccv, w_out, l)
        y, hfin = _s5_mix(uf, w1, w2, a8, h0_all[l], l)
        x = _outproj(x, a_ctx, a_lat, c_ctx, c_lat, y, uf, s5_d3, w_glu, b_glu3, mods3, norm_post4, w_out_bf, l)
        if l < DEPTH - 1:
            xs = [_ffn([x], mods3, *ffn_w, l, 1)]
            mods3 = mods_next.reshape(8, 1, N_MOD * D_MODEL)
            w_in_bf = w_in_next
        else:
            y_prompt, y_sample = _ffn([x], mods3, *ffn_w, l, 1, split_out=True)

        hf = hfin.reshape(S5_OCT, 4, CTX_B, S5_OG, S5_P).transpose(1, 2, 0, 3, 4).reshape(4, CTX_B, S5_G, S5_P)
        new_state.append((jnp.stack([hf[0], hf[2]], axis=1), jnp.stack([hf[1], hf[3]], axis=1)))

    y_prompt = y_prompt.reshape(CTX_B, CTX_L, D_MODEL)
    y_sample = y_sample.reshape(LAT_B, LAT_L, D_MODEL)
    caches = list(new_caches)
    st_re = jnp.stack([new_state[l][0] for l in range(DEPTH)], axis=1)
    st_im = jnp.stack([new_state[l][1] for l in range(DEPTH)], axis=1)
    return (y_prompt, y_sample, caches[0], caches[1], caches[2], caches[3], st_re, st_im)
```
